```python
import jax
import jax.numpy as jnp
from jax import lax
import numpy as np

D_MODEL = 1024
BATCH = 4
SEQ = 4096
DEPTH = 1

GRID_W = 64
CTX_LEN = 256
D_CONV = 1024
CONV_WIDTH = 31
D_RNN = 1024
RNN_HEADS = 4
RNN_BLOCK = D_RNN // RNN_HEADS
SHORT_CONV = 4
LRU_C = 8.0
N_EXPERTS = 64
TOP_K = 8
N_GROUPS = 8
TOPK_GROUPS = 4
D_EXPERT = 256
D_SHARED = 256
ROUTED_SCALE = 2.5
EPS = 1e-6
N_MOD = 6 * D_MODEL
OFF_GLU = 0
OFF_RNN_X = 2 * D_CONV
OFF_RNN_G = OFF_RNN_X + D_RNN
OFF_MERGE = OFF_RNN_G + D_RNN
N_IN = OFF_MERGE + 2 * D_MODEL

kernel_name = 'hybrid_conv_rglru_moe_dit_block'


def rms_norm(x, g):
    x32 = x.astype(jnp.float32)
    y = x32 * lax.rsqrt(jnp.mean(x32 * x32, axis=-1, keepdims=True) + EPS)
    return (y * g.astype(jnp.float32)).astype(x.dtype)


def layer_norm(x, g, b):
    x32 = x.astype(jnp.float32)
    xc = x32 - jnp.mean(x32, axis=-1, keepdims=True)
    y = xc * lax.rsqrt(jnp.mean(xc * xc, axis=-1, keepdims=True) + EPS)
    return (y * g.astype(jnp.float32) + b.astype(jnp.float32)).astype(x.dtype)


def modulate(x, shift, scale):
    return x * (1.0 + scale[:, None, :]) + shift[:, None, :]


def depthwise_conv(u, w, b, pad):
    out = lax.conv_general_dilated(u, w[:, None, :].astype(u.dtype), window_strides=(1,), padding=[pad], dimension_numbers=('NWC', 'WIO', 'NWC'), feature_group_count=u.shape[-1])
    return out + b.astype(u.dtype)


def conformer_conv(glu_in, p, rows):
    a, g = jnp.split(glu_in, 2, axis=-1)
    u = a * jax.nn.sigmoid(g)
    half = CONV_WIDTH // 2
    if rows is None:
        v = depthwise_conv(u, p['w_dw'], p['b_dw'], (half, half))
    else:
        bsz, length, ch = u.shape
        v = depthwise_conv(u.reshape(bsz * rows, GRID_W, ch), p['w_dw'], p['b_dw'], (half, half)).reshape(bsz, length, ch)
    v = jax.nn.silu(layer_norm(v, p['ln_conv_g'], p['ln_conv_b']))
    return v @ p['w_conv_out']


def _lin_combine(e1, e2):
    a1, b1 = e1
    a2, b2 = e2
    return a1 * a2, a2 * b1 + b2


def rglru_direction(u, p, d, h0, reverse):
    pad = (0, SHORT_CONV - 1) if reverse else (SHORT_CONV - 1, 0)
    v = depthwise_conv(u, p['w_sc'][d], p['b_sc'][d], pad)
    vb = v.reshape(v.shape[:-1] + (RNN_HEADS, RNN_BLOCK))
    r = jax.nn.sigmoid(jnp.einsum('blhi,hij->blhj', vb, p['w_rg_a'][d]).reshape(v.shape) + p['b_rg_a'][d])
    i = jax.nn.sigmoid(jnp.einsum('blhi,hij->blhj', vb, p['w_rg_x'][d]).reshape(v.shape) + p['b_rg_x'][d])
    log_a = LRU_C * r.astype(jnp.float32) * jax.nn.log_sigmoid(p['lru_lambda'][d].astype(jnp.float32))
    a = jnp.exp(log_a)
    b = jnp.sqrt(-jnp.expm1(2.0 * log_a)) * (i * v).astype(jnp.float32)
    a_cum, b_cum = lax.associative_scan(_lin_combine, (a, b), reverse=reverse, axis=1)
    return a_cum * h0[:, None, :] + b_cum


def rglru_bidir(u, p, h0_f, h0_b):
    h_f = rglru_direction(u, p, 0, h0_f, False)
    h_b = rglru_direction(u, p, 1, h0_b, True)
    return h_f, h_b


def mixer(h, p, rows, h0_f, h0_b):
    proj = h @ p['w_in']
    y_a = conformer_conv(proj[..., OFF_GLU:OFF_RNN_X], p, rows)
    h_f, h_b = rglru_bidir(proj[..., OFF_RNN_X:OFF_RNN_G], p, h0_f, h0_b)
    gate_branch = jax.nn.gelu(proj[..., OFF_RNN_G:OFF_MERGE])
    y_b = (gate_branch * (h_f + h_b).astype(h.dtype)) @ p['w_rnn_out']
    g_a, g_b = jnp.split(jax.nn.sigmoid(proj[..., OFF_MERGE:N_IN]), 2, axis=-1)
    out = (g_a * y_a + g_b * y_b) @ p['w_out']
    return out, h_f, h_b


def swiglu(t, wg, wu, wd):
    return (jax.nn.silu(t @ wg) * (t @ wu)) @ wd


def moe(h, p):
    bsz, length, dim = h.shape
    t = h.reshape(-1, dim)
    scores = jax.nn.sigmoid((t @ p['w_router']).astype(jnp.float32))
    sel = scores + p['router_bias'].astype(jnp.float32)
    grouped = sel.reshape(-1, N_GROUPS, N_EXPERTS // N_GROUPS)
    group_score = jnp.sum(lax.top_k(grouped, 2)[0], axis=-1)
    _, top_groups = lax.top_k(group_score, TOPK_GROUPS)
    group_mask = jnp.sum(jax.nn.one_hot(top_groups, N_GROUPS, dtype=jnp.float32), axis=1) > 0
    expert_mask = jnp.repeat(group_mask, N_EXPERTS // N_GROUPS, axis=1)
    _, idx = lax.top_k(jnp.where(expert_mask, sel, -jnp.inf), TOP_K)
    w = jnp.take_along_axis(scores, idx, axis=1)
    w = ROUTED_SCALE * w / jnp.sum(w, axis=-1, keepdims=True)
    gates = jnp.sum(jax.nn.one_hot(idx, N_EXPERTS, dtype=jnp.float32) * w[..., None], axis=1)

    def expert_step(acc, xs):
        wg, wu, wd, g = xs
        return acc + g[:, None].astype(t.dtype) * swiglu(t, wg, wu, wd), None

    routed, _ = lax.scan(expert_step, jnp.zeros_like(t), (p['w_e_gate'], p['w_e_up'], p['w_e_down'], gates.T))
    shared = swiglu(t, p['w_s_gate'], p['w_s_up'], p['w_s_down'])
    return (routed + shared).reshape(bsz, length, dim)


def setup_inputs(seed: int = 0) -> dict:
    key = jax.random.key(seed)
    ks = jax.random.split(key, 40)
    f32 = jnp.float32

    def nrm(k, shape, scale):
        return scale * jax.random.normal(k, shape, f32)

    def gain(k, shape):
        return 1.0 + 0.1 * jax.random.normal(k, shape, f32)

    lam_u = jax.random.uniform(ks[22], (DEPTH, 2, D_RNN), f32, 0.9, 0.999)
    lam_s = lam_u ** (1.0 / LRU_C)
    lru_lambda = jnp.log(lam_s) - jnp.log1p(-lam_s)
    return {
        'x': nrm(ks[0], (BATCH, SEQ, D_MODEL), 1.0),
        'c': nrm(ks[1], (BATCH, D_MODEL), 1.0),
        'ctx': nrm(ks[2], (BATCH, CTX_LEN, D_MODEL), 1.0),
        'c_ctx': nrm(ks[3], (D_MODEL,), 1.0),
        'w_mod': nrm(ks[4], (DEPTH, D_MODEL, N_MOD), 0.5 * D_MODEL ** -0.5),
        'b_mod': nrm(ks[5], (DEPTH, N_MOD), 0.02),
        'pre1_g': gain(ks[6], (DEPTH, D_MODEL)),
        'post1_g': gain(ks[7], (DEPTH, D_MODEL)),
        'pre2_g': gain(ks[8], (DEPTH, D_MODEL)),
        'post2_g': gain(ks[9], (DEPTH, D_MODEL)),
        'w_in': nrm(ks[10], (DEPTH, D_MODEL, N_IN), D_MODEL ** -0.5),
        'w_dw': nrm(ks[11], (DEPTH, CONV_WIDTH, D_CONV), CONV_WIDTH ** -0.5),
        'b_dw': nrm(ks[12], (DEPTH, D_CONV), 0.02),
        'ln_conv_g': gain(ks[13], (DEPTH, D_CONV)),
        'ln_conv_b': nrm(ks[14], (DEPTH, D_CONV), 0.02),
        'w_conv_out': nrm(ks[15], (DEPTH, D_CONV, D_MODEL), D_CONV ** -0.5),
        'w_sc': nrm(ks[16], (DEPTH, 2, SHORT_CONV, D_RNN), SHORT_CONV ** -0.5),
        'b_sc': nrm(ks[17], (DEPTH, 2, D_RNN), 0.02),
        'w_rg_a': nrm(ks[18], (DEPTH, 2, RNN_HEADS, RNN_BLOCK, RNN_BLOCK), RNN_BLOCK ** -0.5),
        'b_rg_a': nrm(ks[19], (DEPTH, 2, D_RNN), 0.02),
        'w_rg_x': nrm(ks[20], (DEPTH, 2, RNN_HEADS, RNN_BLOCK, RNN_BLOCK), RNN_BLOCK ** -0.5),
        'b_rg_x': nrm(ks[21], (DEPTH, 2, D_RNN), 0.02),
        'lru_lambda': lru_lambda,
        'w_rnn_out': nrm(ks[23], (DEPTH, D_RNN, D_MODEL), D_RNN ** -0.5),
        'w_out': nrm(ks[24], (DEPTH, D_MODEL, D_MODEL), D_MODEL ** -0.5),
        'w_router': nrm(ks[25], (DEPTH, D_MODEL, N_EXPERTS), D_MODEL ** -0.5),
        'router_bias': nrm(ks[26], (DEPTH, N_EXPERTS), 0.01),
        'w_e_gate': nrm(ks[27], (DEPTH, N_EXPERTS, D_MODEL, D_EXPERT), D_MODEL ** -0.5),
        'w_e_up': nrm(ks[28], (DEPTH, N_EXPERTS, D_MODEL, D_EXPERT), D_MODEL ** -0.5),
        'w_e_down': nrm(ks[29], (DEPTH, N_EXPERTS, D_EXPERT, D_MODEL), D_EXPERT ** -0.5),
        'w_s_gate': nrm(ks[30], (DEPTH, D_MODEL, D_SHARED), D_MODEL ** -0.5),
        'w_s_up': nrm(ks[31], (DEPTH, D_MODEL, D_SHARED), D_MODEL ** -0.5),
        'w_s_down': nrm(ks[32], (DEPTH, D_SHARED, D_MODEL), D_SHARED ** -0.5),
    }


def reference(x, c, ctx, c_ctx, w_mod, b_mod, pre1_g, post1_g, pre2_g, post2_g, w_in, w_dw, b_dw, ln_conv_g, ln_conv_b, w_conv_out, w_sc, b_sc, w_rg_a, b_rg_a, w_rg_x, b_rg_x, lru_lambda, w_rnn_out, w_out, w_router, router_bias, w_e_gate, w_e_up, w_e_down, w_s_gate, w_s_up, w_s_down):
    rows = x.shape[1] // GRID_W
    zero_state = jnp.zeros((x.shape[0], D_RNN), jnp.float32)
    cx = ctx
    for l in range(DEPTH):
        p = {
            'w_in': w_in[l], 'w_dw': w_dw[l], 'b_dw': b_dw[l], 'ln_conv_g': ln_conv_g[l], 'ln_conv_b': ln_conv_b[l],
            'w_conv_out': w_conv_out[l], 'w_sc': w_sc[l], 'b_sc': b_sc[l], 'w_rg_a': w_rg_a[l], 'b_rg_a': b_rg_a[l],
            'w_rg_x': w_rg_x[l], 'b_rg_x': b_rg_x[l], 'lru_lambda': lru_lambda[l], 'w_rnn_out': w_rnn_out[l],
            'w_out': w_out[l], 'w_router': w_router[l], 'router_bias': router_bias[l], 'w_e_gate': w_e_gate[l],
            'w_e_up': w_e_up[l], 'w_e_down': w_e_down[l], 'w_s_gate': w_s_gate[l], 'w_s_up': w_s_up[l],
            'w_s_down': w_s_down[l],
        }
        last = l == DEPTH - 1
        mod_x = jax.nn.silu(c) @ w_mod[l] + b_mod[l]
        mod_c = (jax.nn.silu(c_ctx) @ w_mod[l] + b_mod[l])[None, :]
        sh1x, sc1x, g1x, sh2x, sc2x, g2x = jnp.split(mod_x, 6, axis=-1)
        sh1c, sc1c, g1c, sh2c, sc2c, g2c = jnp.split(mod_c, 6, axis=-1)

        hc = modulate(rms_norm(cx, pre1_g[l]), sh1c, sc1c)
        if last:
            hcf, hcb = rglru_bidir(hc @ w_in[l][:, OFF_RNN_X:OFF_RNN_G], p, zero_state, zero_state)
        else:
            out_c, hcf, hcb = mixer(hc, p, None, zero_state, zero_state)
            cx = cx + g1c[:, None, :] * rms_norm(out_c, post1_g[l])
            hc2 = modulate(rms_norm(cx, pre2_g[l]), sh2c, sc2c)
            cx = cx + g2c[:, None, :] * rms_norm(moe(hc2, p), post2_g[l])
        h0_f = hcf[:, -1]
        h0_b = hcb[:, 0]

        hx = modulate(rms_norm(x, pre1_g[l]), sh1x, sc1x)
        out_x, _, _ = mixer(hx, p, rows, h0_f, h0_b)
        x = x + g1x[:, None, :] * rms_norm(out_x, post1_g[l])
        hx2 = modulate(rms_norm(x, pre2_g[l]), sh2x, sc2x)
        x = x + g2x[:, None, :] * rms_norm(moe(hx2, p), post2_g[l])
    return x
```

```python
import functools

import jax
import jax.numpy as jnp
from jax import lax
from jax.experimental import pallas as pl
from jax.experimental.pallas import tpu as pltpu

F32 = jnp.float32
BF16 = jnp.bfloat16

D_MODEL = 1024
GRID_W = 64
CONV_WIDTH = 31
CONV_HALF = CONV_WIDTH // 2
SHORT_CONV = 4
RNN_HEADS = 4
RNN_BLOCK = D_MODEL // RNN_HEADS
LRU_C = 8.0
N_EXPERTS = 64
N_GROUPS = 8
GROUP_SIZE = N_EXPERTS // N_GROUPS
TOPK_GROUPS = 4
TOP_K = 8
D_EXPERT = 256
ROUTED_SCALE = 2.5
EPS = 1e-6

SUBLANES = 8
TM = 512
ROWS_PER_TILE = TM // GRID_W
PAD = 16
ROW_STRIDE = GRID_W + PAD
CONV_LANES = 256
TMOE = 1024
VMEM_LIMIT = 58 * 1024 * 1024


def _sigmoid(x):
    return 0.5 * (jnp.tanh(0.5 * x) + 1.0)


def _silu(x):
    return x * _sigmoid(x)


def _gelu_tanh(x):
    return 0.5 * x * (1.0 + jnp.tanh(0.7978845608028654 * (x + 0.044715 * (x * x * x))))


def _rms_norm(x, g):
    return x * lax.rsqrt(jnp.mean(x * x, axis=-1, keepdims=True) + EPS) * g


def _dot(a, b):
    return jnp.dot(a, b, preferred_element_type=F32)


def _log_decay(lam):
    return LRU_C * (jnp.minimum(lam, 0.0) - jnp.log1p(jnp.exp(-jnp.abs(lam))))


def _rglru_coeffs(v, wa_ref, ba, wx_ref, bx, c_lam, a_ref, b_ref):
    vb = v.astype(BF16)
    for h in range(RNN_HEADS):
        cs = slice(h * RNN_BLOCK, (h + 1) * RNN_BLOCK)
        vh = vb[:, cs]
        r = _sigmoid(_dot(vh, wa_ref[h]) + ba[:, cs])
        i = _sigmoid(_dot(vh, wx_ref[h]) + bx[:, cs])
        log_a = c_lam[:, cs] * r
        a = jnp.exp(log_a)
        a_ref[:, cs] = a
        b_ref[:, cs] = jnp.sqrt(jnp.tanh(-log_a) * (1.0 + a * a)) * (i * v[:, cs])


def _scan_tile(a_ref, b_ref, carry, n_rows, reverse):
    row = lax.broadcasted_iota(jnp.int32, (SUBLANES, D_MODEL), 0)
    n_groups = n_rows // SUBLANES

    def body(j, carry):
        g = (n_groups - 1 - j) if reverse else j
        off = pl.multiple_of(g * SUBLANES, SUBLANES)
        a = a_ref[pl.ds(off, SUBLANES), :]
        b = b_ref[pl.ds(off, SUBLANES), :]
        for s in (1, 2, 4):
            if reverse:
                keep = row < SUBLANES - s
                shift = SUBLANES - s
            else:
                keep = row >= s
                shift = s
            a_sh = jnp.where(keep, pltpu.roll(a, shift, 0), 1.0)
            b_sh = jnp.where(keep, pltpu.roll(b, shift, 0), 0.0)
            b = a * b_sh + b
            a = a * a_sh
        h = a * carry + b
        b_ref[pl.ds(off, SUBLANES), :] = h
        last = h[0:1, :] if reverse else h[SUBLANES - 1:SUBLANES, :]
        return jnp.broadcast_to(last, (SUBLANES, D_MODEL))

    return lax.fori_loop(0, n_groups, body, carry)


def _mod_kernel(c_ref, w_ref, b_ref, o_ref):
    o_ref[...] = jnp.dot(_silu(c_ref[...]), w_ref[...], preferred_element_type=F32,
                         precision=lax.Precision.HIGHEST) + b_ref[...]


def _mod_call(cc, w_mod, b_mod):
    n_mod = w_mod.shape[1]
    return pl.pallas_call(
        _mod_kernel,
        grid=(n_mod // D_MODEL,),
        in_specs=[
            pl.BlockSpec((SUBLANES, D_MODEL), lambda j: (0, 0)),
            pl.BlockSpec((D_MODEL, D_MODEL), lambda j: (0, j)),
            pl.BlockSpec((1, D_MODEL), lambda j: (0, j)),
        ],
        out_specs=pl.BlockSpec((SUBLANES, D_MODEL), lambda j: (0, j)),
        out_shape=jax.ShapeDtypeStruct((SUBLANES, n_mod), F32),
        name="mod",
    )(cc, w_mod, b_mod)


def _ctx_kernel(ctx_ref, mod_ref, g_ref, w_ref, wsc_ref, bsc_ref, wa_ref, ba_ref, wx_ref, bx_ref,
                lam_ref, o_ref, uext_ref, a_ref, b_ref):
    n = ctx_ref.shape[0]
    hc = _rms_norm(ctx_ref[...], g_ref[...]) * (1.0 + mod_ref[1:2, :]) + mod_ref[0:1, :]
    u = _dot(hc.astype(BF16), w_ref[...])
    zeros8 = jnp.zeros((SUBLANES, D_MODEL), F32)
    uext_ref[0:SUBLANES, :] = zeros8
    uext_ref[SUBLANES:SUBLANES + n, :] = u
    uext_ref[SUBLANES + n:2 * SUBLANES + n, :] = zeros8
    for d in range(2):
        v = jnp.broadcast_to(bsc_ref[d:d + 1, :], (n, D_MODEL))
        for k in range(SHORT_CONV):
            start = SUBLANES + k - (SHORT_CONV - 1) * (1 - d)
            v = v + wsc_ref[d, k:k + 1, :] * uext_ref[start:start + n, :]
        _rglru_coeffs(v, wa_ref.at[d], ba_ref[d:d + 1, :], wx_ref.at[d], bx_ref[d:d + 1, :],
                      _log_decay(lam_ref[d:d + 1, :]), a_ref, b_ref)
        final = _scan_tile(a_ref, b_ref, zeros8, n, reverse=(d == 1))
        o_ref[d:d + 1, :] = final[0:1, :]


def _ctx_call(ctx, mod_c, pre1_g, w_in_bf, w_sc, b_sc, wa_bf, b_rg_a, wx_bf, b_rg_x, lam):
    bsz, n, _ = ctx.shape
    const2 = lambda b: (0, 0)
    const3 = lambda b: (0, 0, 0)
    const4 = lambda b: (0, 0, 0, 0)
    return pl.pallas_call(
        _ctx_kernel,
        grid=(bsz,),
        in_specs=[
            pl.BlockSpec((None, n, D_MODEL), lambda b: (b, 0, 0)),
            pl.BlockSpec((6, D_MODEL), const2),
            pl.BlockSpec((1, D_MODEL), const2),
            pl.BlockSpec((D_MODEL, D_MODEL), lambda b: (0, 2)),
            pl.BlockSpec((2, SHORT_CONV, D_MODEL), const3),
            pl.BlockSpec((2, D_MODEL), const2),
            pl.BlockSpec((2, RNN_HEADS, RNN_BLOCK, RNN_BLOCK), const4),
            pl.BlockSpec((2, D_MODEL), const2),
            pl.BlockSpec((2, RNN_HEADS, RNN_BLOCK, RNN_BLOCK), const4),
            pl.BlockSpec((2, D_MODEL), const2),
            pl.BlockSpec((2, D_MODEL), const2),
        ],
        out_specs=pl.BlockSpec((None, 2, D_MODEL), lambda b: (b, 0, 0)),
        out_shape=jax.ShapeDtypeStruct((bsz, 2, D_MODEL), F32),
        scratch_shapes=[
            pltpu.VMEM((n + 2 * SUBLANES, D_MODEL), F32),
            pltpu.VMEM((n, D_MODEL), F32),
            pltpu.VMEM((n, D_MODEL), F32),
        ],
        name="ctx",
    )(ctx, mod_c, pre1_g, w_in_bf, w_sc, b_sc, wa_bf, b_rg_a, wx_bf, b_rg_x, lam)


def _mixer_fwd_kernel(x_ref, mod_ref, g_ref, win_ref, wdw_ref, bdw_ref, lng_ref, lnb_ref, wco_ref,
                      wsc_ref, bsc_ref, wa_ref, ba_ref, wx_ref, bx_ref, lam_ref, h0_ref,
                      gaya_ref, gb_ref, gbr_ref, urnn_ref, hf_ref,
                      upad_ref, cv_ref, uext_ref, a_ref, b_ref, carry_ref):
    j = pl.program_id(1)
    zeros8 = jnp.zeros((SUBLANES, D_MODEL), F32)

    @pl.when(j == 0)
    def _():
        carry_ref[...] = jnp.broadcast_to(h0_ref[0:1, :], (SUBLANES, D_MODEL))
        uext_ref[0:SUBLANES, :] = zeros8
        zeros_pad = jnp.zeros((PAD, D_MODEL), F32)
        for r in range(ROWS_PER_TILE + 1):
            upad_ref[r * ROW_STRIDE:r * ROW_STRIDE + PAD, :] = zeros_pad

    hx = (_rms_norm(x_ref[...], g_ref[...]) * (1.0 + mod_ref[1:2, :]) + mod_ref[0:1, :]).astype(BF16)

    u = _dot(hx, win_ref[:, 0:D_MODEL]) * _sigmoid(_dot(hx, win_ref[:, D_MODEL:2 * D_MODEL]))
    for r in range(ROWS_PER_TILE):
        upad_ref[PAD + r * ROW_STRIDE:PAD + r * ROW_STRIDE + GRID_W, :] = u[r * GRID_W:(r + 1) * GRID_W, :]
    for r in range(ROWS_PER_TILE):
        for c in range(D_MODEL // CONV_LANES):
            cs = slice(c * CONV_LANES, (c + 1) * CONV_LANES)
            acc = jnp.broadcast_to(bdw_ref[:, cs], (GRID_W, CONV_LANES))
            for k in range(CONV_WIDTH):
                start = r * ROW_STRIDE + PAD - CONV_HALF + k
                acc = acc + wdw_ref[k:k + 1, cs] * upad_ref[start:start + GRID_W, cs]
            cv_ref[r * GRID_W:(r + 1) * GRID_W, cs] = acc
    cv = cv_ref[...]
    cvc = cv - jnp.mean(cv, axis=-1, keepdims=True)
    cvn = cvc * lax.rsqrt(jnp.mean(cvc * cvc, axis=-1, keepdims=True) + EPS) * lng_ref[...] + lnb_ref[...]
    y_a = _dot(_silu(cvn).astype(BF16), wco_ref[...])

    g_a = _sigmoid(_dot(hx, win_ref[:, 4 * D_MODEL:5 * D_MODEL]))
    gaya_ref[...] = (g_a * y_a).astype(BF16)
    gb_ref[...] = _sigmoid(_dot(hx, win_ref[:, 5 * D_MODEL:6 * D_MODEL])).astype(BF16)
    gbr_ref[...] = _gelu_tanh(_dot(hx, win_ref[:, 3 * D_MODEL:4 * D_MODEL])).astype(BF16)

    ur = _dot(hx, win_ref[:, 2 * D_MODEL:3 * D_MODEL])
    urnn_ref[...] = ur.astype(BF16)
    uext_ref[SUBLANES:SUBLANES + TM, :] = ur
    v = jnp.broadcast_to(bsc_ref[...], (TM, D_MODEL))
    for k in range(SHORT_CONV):
        start = SUBLANES + k - (SHORT_CONV - 1)
        v = v + wsc_ref[k:k + 1, :] * uext_ref[start:start + TM, :]
    uext_ref[0:SUBLANES, :] = uext_ref[TM:TM + SUBLANES, :]
    _rglru_coeffs(v, wa_ref, ba_ref[...], wx_ref, bx_ref[...], _log_decay(lam_ref[...]), a_ref, b_ref)
    carry_ref[...] = _scan_tile(a_ref, b_ref, carry_ref[...], TM, reverse=False)
    hf_ref[...] = b_ref[...].astype(BF16)


def _resident(shape):
    nd = len(shape)
    return pl.BlockSpec(shape, lambda b, j: (0,) * nd, pipeline_mode=pl.Buffered(1))


def _mixer_fwd_call(x, mod_x, pre1_g, w_in_bf, w_dw, b_dw, ln_g, ln_b, wco_bf,
                    w_sc, b_sc, wa_bf, b_rg_a, wx_bf, b_rg_x, lam, h0):
    bsz, seq, _ = x.shape
    nt = seq // TM
    tile = pl.BlockSpec((None, TM, D_MODEL), lambda b, j: (b, j, 0))
    act = jax.ShapeDtypeStruct((bsz, seq, D_MODEL), BF16)
    head_w = pl.BlockSpec((None, RNN_HEADS, RNN_BLOCK, RNN_BLOCK), lambda b, j: (0, 0, 0, 0),
                          pipeline_mode=pl.Buffered(1))
    dir_row = pl.BlockSpec((None, 1, D_MODEL), lambda b, j: (0, 0, 0), pipeline_mode=pl.Buffered(1))
    return pl.pallas_call(
        _mixer_fwd_kernel,
        grid=(bsz, nt),
        in_specs=[
            tile,
            pl.BlockSpec((None, 6, D_MODEL), lambda b, j: (b, 0, 0)),
            _resident((1, D_MODEL)),
            _resident((D_MODEL, 6 * D_MODEL)),
            _resident((CONV_WIDTH, D_MODEL)),
            _resident((1, D_MODEL)),
            _resident((1, D_MODEL)),
            _resident((1, D_MODEL)),
            _resident((D_MODEL, D_MODEL)),
            pl.BlockSpec((None, SHORT_CONV, D_MODEL), lambda b, j: (0, 0, 0), pipeline_mode=pl.Buffered(1)),
            dir_row, head_w, dir_row, head_w, dir_row, dir_row,
            pl.BlockSpec((None, 2, D_MODEL), lambda b, j: (b, 0, 0)),
        ],
        out_specs=[tile] * 5,
        out_shape=[act] * 5,
        scratch_shapes=[
            pltpu.VMEM((ROWS_PER_TILE * ROW_STRIDE + PAD, D_MODEL), F32),
            pltpu.VMEM((TM, D_MODEL), F32),
            pltpu.VMEM((TM + SUBLANES, D_MODEL), F32),
            pltpu.VMEM((TM, D_MODEL), F32),
            pltpu.VMEM((TM, D_MODEL), F32),
            pltpu.VMEM((SUBLANES, D_MODEL), F32),
        ],
        compiler_params=pltpu.CompilerParams(
            dimension_semantics=("arbitrary", "arbitrary"), vmem_limit_bytes=VMEM_LIMIT),
        name="mixer_fwd",
    )(x, mod_x, pre1_g, w_in_bf, w_dw, b_dw, ln_g, ln_b, wco_bf,
      w_sc, b_sc.reshape(2, 1, D_MODEL), wa_bf, b_rg_a.reshape(2, 1, D_MODEL), wx_bf,
      b_rg_x.reshape(2, 1, D_MODEL), lam.reshape(2, 1, D_MODEL), h0)


def _route(logits_t, bias):
    t = logits_t.shape[1]
    scores = _sigmoid(logits_t)
    sel = scores + bias
    neg_inf = jnp.float32(-jnp.inf)

    sel3 = sel.reshape(N_GROUPS, GROUP_SIZE, t)
    within = lax.broadcasted_iota(jnp.int32, sel3.shape, 1)
    m1 = jnp.max(sel3, axis=1, keepdims=True)
    first = jnp.min(jnp.where(sel3 == m1, within, GROUP_SIZE), axis=1, keepdims=True)
    m2 = jnp.max(jnp.where(within == first, neg_inf, sel3), axis=1, keepdims=True)
    gscore = (m1 + m2).reshape(N_GROUPS, t)

    gidx = lax.broadcasted_iota(jnp.int32, gscore.shape, 0)
    rank = jnp.zeros(gscore.shape, jnp.int32)
    for g in range(N_GROUPS):
        other = gscore[g:g + 1, :]
        beats = jnp.where(other > gscore, 1, jnp.where((other == gscore) & (gidx > g), 1, 0))
        rank = rank + beats
    gkeep = (rank < TOPK_GROUPS).reshape(N_GROUPS, 1, t)
    masked = jnp.where(gkeep, sel3, neg_inf).reshape(N_EXPERTS, t)

    eidx = lax.broadcasted_iota(jnp.int32, masked.shape, 0)
    chosen = jnp.zeros(masked.shape, jnp.bool_)
    for _ in range(TOP_K):
        m = jnp.max(masked, axis=0, keepdims=True)
        pick = eidx == jnp.min(jnp.where(masked == m, eidx, N_EXPERTS), axis=0, keepdims=True)
        chosen = chosen | pick
        masked = jnp.where(pick, neg_inf, masked)
    w = jnp.where(chosen, scores, 0.0)
    return ROUTED_SCALE * w / jnp.sum(w, axis=0, keepdims=True)


def _mixer_bwd_kernel(urnn_ref, hf_ref, gbr_ref, gaya_ref, gb_ref, x_ref, mod_ref,
                      wsc_ref, bsc_ref, wa_ref, ba_ref, wx_ref, bx_ref, lam_ref, h0_ref,
                      wro_ref, wout_ref, post1_ref, pre2_ref, wrt_ref, rbias_ref,
                      x1_ref, hx2_ref, gates_ref,
                      uext_ref, a_ref, b_ref, carry_ref):
    j = pl.program_id(1)
    zeros8 = jnp.zeros((SUBLANES, D_MODEL), F32)

    @pl.when(j == 0)
    def _():
        carry_ref[...] = jnp.broadcast_to(h0_ref[1:2, :], (SUBLANES, D_MODEL))
        uext_ref[TM:TM + SUBLANES, :] = zeros8

    uext_ref[0:TM, :] = urnn_ref[...].astype(F32)
    v = jnp.broadcast_to(bsc_ref[...], (TM, D_MODEL))
    for k in range(SHORT_CONV):
        v = v + wsc_ref[k:k + 1, :] * uext_ref[k:k + TM, :]
    uext_ref[TM:TM + SUBLANES, :] = uext_ref[0:SUBLANES, :]
    _rglru_coeffs(v, wa_ref, ba_ref[...], wx_ref, bx_ref[...], _log_decay(lam_ref[...]), a_ref, b_ref)
    carry_ref[...] = _scan_tile(a_ref, b_ref, carry_ref[...], TM, reverse=True)

    h_sum = hf_ref[...].astype(F32) + b_ref[...]
    y_b = _dot((gbr_ref[...].astype(F32) * h_sum).astype(BF16), wro_ref[...])
    mix = gaya_ref[...].astype(F32) + gb_ref[...].astype(F32) * y_b
    out = _dot(mix.astype(BF16), wout_ref[...])
    x1 = x_ref[...] + mod_ref[2:3, :] * _rms_norm(out, post1_ref[...])
    x1_ref[...] = x1

    hx2 = _rms_norm(x1, pre2_ref[...]) * (1.0 + mod_ref[4:5, :]) + mod_ref[3:4, :]
    hx2_ref[...] = hx2.astype(BF16)
    logits_t = lax.dot_general(wrt_ref[...], hx2, (((1,), (1,)), ((), ())),
                               preferred_element_type=F32, precision=lax.Precision.HIGHEST)
    gates_ref[...] = _route(logits_t, rbias_ref[...])


def _mixer_bwd_call(urnn, hf, gbr, gaya, gb, x, mod_x, w_sc, b_sc, wa_bf, b_rg_a, wx_bf, b_rg_x, lam, h0,
                    wro_bf, wout_bf, post1_g, pre2_g, w_router_t, router_bias):
    bsz, seq, _ = x.shape
    nt = seq // TM
    rev = lambda b, j: (b, nt - 1 - j, 0)
    tile = pl.BlockSpec((None, TM, D_MODEL), rev)
    head_w = pl.BlockSpec((None, RNN_HEADS, RNN_BLOCK, RNN_BLOCK), lambda b, j: (1, 0, 0, 0),
                          pipeline_mode=pl.Buffered(1))
    dir_row = pl.BlockSpec((None, 1, D_MODEL), lambda b, j: (1, 0, 0), pipeline_mode=pl.Buffered(1))
    return pl.pallas_call(
        _mixer_bwd_kernel,
        grid=(bsz, nt),
        in_specs=[
            tile, tile, tile, tile, tile, tile,
            pl.BlockSpec((None, 6, D_MODEL), lambda b, j: (b, 0, 0)),
            pl.BlockSpec((None, SHORT_CONV, D_MODEL), lambda b, j: (1, 0, 0), pipeline_mode=pl.Buffered(1)),
            dir_row, head_w, dir_row, head_w, dir_row, dir_row,
            pl.BlockSpec((None, 2, D_MODEL), lambda b, j: (b, 0, 0)),
            _resident((D_MODEL, D_MODEL)),
            _resident((D_MODEL, D_MODEL)),
            _resident((1, D_MODEL)),
            _resident((1, D_MODEL)),
            _resident((N_EXPERTS, D_MODEL)),
            _resident((N_EXPERTS, 1)),
        ],
        out_specs=[
            tile, tile,
            pl.BlockSpec((N_EXPERTS, TM), lambda b, j: (0, b * nt + nt - 1 - j)),
        ],
        out_shape=[
            jax.ShapeDtypeStruct((bsz, seq, D_MODEL), F32),
            jax.ShapeDtypeStruct((bsz, seq, D_MODEL), BF16),
            jax.ShapeDtypeStruct((N_EXPERTS, bsz * seq), F32),
        ],
        scratch_shapes=[
            pltpu.VMEM((TM + SUBLANES, D_MODEL), F32),
            pltpu.VMEM((TM, D_MODEL), F32),
            pltpu.VMEM((TM, D_MODEL), F32),
            pltpu.VMEM((SUBLANES, D_MODEL), F32),
        ],
        compiler_params=pltpu.CompilerParams(
            dimension_semantics=("arbitrary", "arbitrary"), vmem_limit_bytes=VMEM_LIMIT),
        name="mixer_bwd",
    )(urnn, hf, gbr, gaya, gb, x, mod_x, w_sc, b_sc.reshape(2, 1, D_MODEL), wa_bf,
      b_rg_a.reshape(2, 1, D_MODEL), wx_bf, b_rg_x.reshape(2, 1, D_MODEL), lam.reshape(2, 1, D_MODEL), h0,
      wro_bf, wout_bf, post1_g, pre2_g, w_router_t, router_bias)


def _swiglu(t, wg, wu):
    return _silu(_dot(t, wg)) * _dot(t, wu)


def _moe_kernel(t_ref, gates_ref, x1_ref, mod_ref, post2_ref, wg_ref, wu_ref, wd_ref,
                wsg_ref, wsu_ref, wsd_ref, o_ref):
    e = pl.program_id(1)
    t = t_ref[...]

    @pl.when(e == 0)
    def _():
        hs = _swiglu(t, wsg_ref[...].astype(BF16), wsu_ref[...].astype(BF16))
        o_ref[...] = _dot(hs.astype(BF16), wsd_ref[...].astype(BF16))

    lane = lax.broadcasted_iota(jnp.int32, (TMOE, N_EXPERTS), 1)
    g = jnp.sum(jnp.where(lane == e, gates_ref[...], 0.0), axis=-1, keepdims=True)
    hr = g * _swiglu(t, wg_ref[...].astype(BF16), wu_ref[...].astype(BF16))
    o_ref[...] += _dot(hr.astype(BF16), wd_ref[...].astype(BF16))

    @pl.when(e == N_EXPERTS - 1)
    def _():
        o_ref[...] = x1_ref[...] + mod_ref[5:6, :] * _rms_norm(o_ref[...], post2_ref[...])


def _moe_call(t, gates, x1, mod_x, post2_g, w_e_gate, w_e_up, w_e_down, w_s_gate, w_s_up, w_s_down, seq):
    n = t.shape[0]
    tiles_per_seq = seq // TMOE
    tile = pl.BlockSpec((TMOE, D_MODEL), lambda i, e: (i, 0))
    const = lambda i, e: (0, 0)
    return pl.pallas_call(
        _moe_kernel,
        grid=(n // TMOE, N_EXPERTS),
        in_specs=[
            tile,
            pl.BlockSpec((TMOE, N_EXPERTS), lambda i, e: (i, 0)),
            tile,
            pl.BlockSpec((None, 6, D_MODEL), lambda i, e: (i // tiles_per_seq, 0, 0)),
            pl.BlockSpec((1, D_MODEL), const),
            pl.BlockSpec((None, D_MODEL, D_EXPERT), lambda i, e: (e, 0, 0)),
            pl.BlockSpec((None, D_MODEL, D_EXPERT), lambda i, e: (e, 0, 0)),
            pl.BlockSpec((None, D_EXPERT, D_MODEL), lambda i, e: (e, 0, 0)),
            pl.BlockSpec((D_MODEL, D_EXPERT), const),
            pl.BlockSpec((D_MODEL, D_EXPERT), const),
            pl.BlockSpec((D_EXPERT, D_MODEL), const),
        ],
        out_specs=tile,
        out_shape=jax.ShapeDtypeStruct((n, D_MODEL), F32),
        compiler_params=pltpu.CompilerParams(
            dimension_semantics=("arbitrary", "arbitrary"), vmem_limit_bytes=VMEM_LIMIT),
        name="moe",
    )(t, gates, x1, mod_x, post2_g, w_e_gate, w_e_up, w_e_down, w_s_gate, w_s_up, w_s_down)


def kernel(x, c, ctx, c_ctx, w_mod, b_mod, pre1_g, post1_g, pre2_g, post2_g, w_in, w_dw, b_dw, ln_conv_g, ln_conv_b, w_conv_out, w_sc, b_sc, w_rg_a, b_rg_a, w_rg_x, b_rg_x, lru_lambda, w_rnn_out, w_out, w_router, router_bias, w_e_gate, w_e_up, w_e_down, w_s_gate, w_s_up, w_s_down):
    assert w_mod.shape[0] == 1, "single-layer block"
    bsz, seq, d = x.shape
    assert d == D_MODEL and seq % TM == 0 and seq % TMOE == 0 and bsz + 1 <= SUBLANES

    cc = jnp.zeros((SUBLANES, D_MODEL), F32).at[:bsz].set(c).at[bsz].set(c_ctx)
    mod = _mod_call(cc, w_mod[0], b_mod)
    mod_x = mod[:bsz].reshape(bsz, 6, D_MODEL)
    mod_c = mod[bsz].reshape(6, D_MODEL)

    w_in_bf = w_in[0].astype(BF16)
    wa_bf = w_rg_a[0].astype(BF16)
    wx_bf = w_rg_x[0].astype(BF16)

    h0 = _ctx_call(ctx, mod_c, pre1_g, w_in_bf, w_sc[0], b_sc[0], wa_bf, b_rg_a[0], wx_bf, b_rg_x[0],
                   lru_lambda[0])

    gaya, gb, gbr, urnn, hf = _mixer_fwd_call(
        x, mod_x, pre1_g, w_in_bf, w_dw[0], b_dw, ln_conv_g, ln_conv_b, w_conv_out[0].astype(BF16),
        w_sc[0], b_sc[0], wa_bf, b_rg_a[0], wx_bf, b_rg_x[0], lru_lambda[0], h0)

    x1, hx2, gates_t = _mixer_bwd_call(
        urnn, hf, gbr, gaya, gb, x, mod_x, w_sc[0], b_sc[0], wa_bf, b_rg_a[0], wx_bf, b_rg_x[0],
        lru_lambda[0], h0, w_rnn_out[0].astype(BF16), w_out[0].astype(BF16), post1_g, pre2_g,
        w_router[0].T, router_bias.reshape(N_EXPERTS, 1))

    out = _moe_call(hx2.reshape(bsz * seq, D_MODEL), gates_t.T, x1.reshape(bsz * seq, D_MODEL), mod_x,
                    post2_g, w_e_gate[0], w_e_up[0], w_e_down[0], w_s_gate[0], w_s_up[0], w_s_down[0], seq)
    return out.reshape(bsz, seq, D_MODEL)
```

```python
import jax
import jax.numpy as jnp
from jax import lax
from jax.experimental import pallas as pl
from jax.experimental.pallas import tpu as pltpu
from jax.experimental.pallas import tpu_sc as plsc

F32 = jnp.float32
BF16 = jnp.bfloat16

D_MODEL = 1024
GRID_W = 64
CONV_WIDTH = 31
CONV_HALF = CONV_WIDTH // 2
SHORT_CONV = 4
RNN_HEADS = 4
RNN_BLOCK = D_MODEL // RNN_HEADS
LRU_C = 8.0
N_EXPERTS = 64
N_GROUPS = 8
GROUP_SIZE = N_EXPERTS // N_GROUPS
TOPK_GROUPS = 4
TOP_K = 8
D_EXPERT = 256
ROUTED_SCALE = 2.5
EPS = 1e-6

SUBLANES = 8
TM = 512
ROWS_PER_TILE = TM // GRID_W
PAD = 16
ROW_STRIDE = GRID_W + PAD
CONV_LANES = 256
LANES = 128
ROW_WORDS = D_MODEL // 2
ROW_TILE = 512
POS_TILE = 512
TOUT = 256
V7X_SC_CORES = 2
V7X_SC_SUBCORES = 16
SC_WORKERS = V7X_SC_CORES * V7X_SC_SUBCORES
SC_CHUNK = 64
SC_BUFFERS = 2
VMEM_LIMIT = 58 * 1024 * 1024


def _sigmoid(x):
    return 0.5 * (jnp.tanh(0.5 * x) + 1.0)


def _silu(x):
    return x * _sigmoid(x)


def _gelu_tanh(x):
    return 0.5 * x * (1.0 + jnp.tanh(0.7978845608028654 * (x + 0.044715 * (x * x * x))))


def _rms_norm(x, g):
    return x * lax.rsqrt(jnp.mean(x * x, axis=-1, keepdims=True) + EPS) * g


def _dot(a, b):
    return jnp.dot(a, b, preferred_element_type=F32)


def _pack_bf16_pairs(x):
    half = x.shape[-1] // 2
    lo = lax.bitcast_convert_type(x[:, :half].astype(BF16).astype(F32), jnp.uint32)
    hi = lax.bitcast_convert_type(x[:, half:].astype(BF16).astype(F32), jnp.uint32)
    return lax.bitcast_convert_type(hi | (lo >> 16), jnp.int32)


def _unpack_bf16_pairs(words):
    u = lax.bitcast_convert_type(words, jnp.uint32)
    return (lax.bitcast_convert_type(u << 16, F32),
            lax.bitcast_convert_type(u & jnp.uint32(0xFFFF0000), F32))


def _log_decay(lam):
    return LRU_C * (jnp.minimum(lam, 0.0) - jnp.log1p(jnp.exp(-jnp.abs(lam))))


def _rglru_coeffs(v, wa_ref, ba, wx_ref, bx, c_lam, a_ref, b_ref):
    vb = v.astype(BF16)
    for h in range(RNN_HEADS):
        cs = slice(h * RNN_BLOCK, (h + 1) * RNN_BLOCK)
        vh = vb[:, cs]
        r = _sigmoid(_dot(vh, wa_ref[h]) + ba[:, cs])
        i = _sigmoid(_dot(vh, wx_ref[h]) + bx[:, cs])
        log_a = c_lam[:, cs] * r
        a = jnp.exp(log_a)
        a_ref[:, cs] = a
        b_ref[:, cs] = jnp.sqrt(jnp.tanh(-log_a) * (1.0 + a * a)) * (i * v[:, cs])


def _scan_tile(a_ref, b_ref, carry, n_rows, reverse):
    row = lax.broadcasted_iota(jnp.int32, (SUBLANES, D_MODEL), 0)
    n_groups = n_rows // SUBLANES

    def body(j, carry):
        g = (n_groups - 1 - j) if reverse else j
        off = pl.multiple_of(g * SUBLANES, SUBLANES)
        a = a_ref[pl.ds(off, SUBLANES), :]
        b = b_ref[pl.ds(off, SUBLANES), :]
        for s in (1, 2, 4):
            if reverse:
                keep = row < SUBLANES - s
                shift = SUBLANES - s
            else:
                keep = row >= s
                shift = s
            a_sh = jnp.where(keep, pltpu.roll(a, shift, 0), 1.0)
            b_sh = jnp.where(keep, pltpu.roll(b, shift, 0), 0.0)
            b = a * b_sh + b
            a = a * a_sh
        h = a * carry + b
        b_ref[pl.ds(off, SUBLANES), :] = h
        last = h[0:1, :] if reverse else h[SUBLANES - 1:SUBLANES, :]
        return jnp.broadcast_to(last, (SUBLANES, D_MODEL))

    return lax.fori_loop(0, n_groups, body, carry)


def _mod_kernel(c_ref, w_ref, b_ref, o_ref):
    o_ref[...] = jnp.dot(_silu(c_ref[...]), w_ref[...], preferred_element_type=F32,
                         precision=lax.Precision.HIGHEST) + b_ref[...]


def _mod_call(cc, w_mod, b_mod):
    n_mod = w_mod.shape[1]
    return pl.pallas_call(
        _mod_kernel,
        grid=(n_mod // D_MODEL,),
        in_specs=[
            pl.BlockSpec((SUBLANES, D_MODEL), lambda j: (0, 0)),
            pl.BlockSpec((D_MODEL, D_MODEL), lambda j: (0, j)),
            pl.BlockSpec((1, D_MODEL), lambda j: (0, j)),
        ],
        out_specs=pl.BlockSpec((SUBLANES, D_MODEL), lambda j: (0, j)),
        out_shape=jax.ShapeDtypeStruct((SUBLANES, n_mod), F32),
        name="mod",
    )(cc, w_mod, b_mod)


def _ctx_kernel(ctx_ref, mod_ref, g_ref, w_ref, wsc_ref, bsc_ref, wa_ref, ba_ref, wx_ref, bx_ref,
                lam_ref, o_ref, uext_ref, a_ref, b_ref):
    n = ctx_ref.shape[0]
    hc = _rms_norm(ctx_ref[...], g_ref[...]) * (1.0 + mod_ref[1:2, :]) + mod_ref[0:1, :]
    u = _dot(hc.astype(BF16), w_ref[...])
    zeros8 = jnp.zeros((SUBLANES, D_MODEL), F32)
    uext_ref[0:SUBLANES, :] = zeros8
    uext_ref[SUBLANES:SUBLANES + n, :] = u
    uext_ref[SUBLANES + n:2 * SUBLANES + n, :] = zeros8
    for d in range(2):
        v = jnp.broadcast_to(bsc_ref[d:d + 1, :], (n, D_MODEL))
        for k in range(SHORT_CONV):
            start = SUBLANES + k - (SHORT_CONV - 1) * (1 - d)
            v = v + wsc_ref[d, k:k + 1, :] * uext_ref[start:start + n, :]
        _rglru_coeffs(v, wa_ref.at[d], ba_ref[d:d + 1, :], wx_ref.at[d], bx_ref[d:d + 1, :],
                      _log_decay(lam_ref[d:d + 1, :]), a_ref, b_ref)
        final = _scan_tile(a_ref, b_ref, zeros8, n, reverse=(d == 1))
        o_ref[d:d + 1, :] = final[0:1, :]


def _ctx_call(ctx, mod_c, pre1_g, w_in_bf, w_sc, b_sc, wa_bf, b_rg_a, wx_bf, b_rg_x, lam):
    bsz, n, _ = ctx.shape
    const2 = lambda b: (0, 0)
    const3 = lambda b: (0, 0, 0)
    const4 = lambda b: (0, 0, 0, 0)
    return pl.pallas_call(
        _ctx_kernel,
        grid=(bsz,),
        in_specs=[
            pl.BlockSpec((None, n, D_MODEL), lambda b: (b, 0, 0)),
            pl.BlockSpec((6, D_MODEL), const2),
            pl.BlockSpec((1, D_MODEL), const2),
            pl.BlockSpec((D_MODEL, D_MODEL), lambda b: (0, 2)),
            pl.BlockSpec((2, SHORT_CONV, D_MODEL), const3),
            pl.BlockSpec((2, D_MODEL), const2),
            pl.BlockSpec((2, RNN_HEADS, RNN_BLOCK, RNN_BLOCK), const4),
            pl.BlockSpec((2, D_MODEL), const2),
            pl.BlockSpec((2, RNN_HEADS, RNN_BLOCK, RNN_BLOCK), const4),
            pl.BlockSpec((2, D_MODEL), const2),
            pl.BlockSpec((2, D_MODEL), const2),
        ],
        out_specs=pl.BlockSpec((None, 2, D_MODEL), lambda b: (b, 0, 0)),
        out_shape=jax.ShapeDtypeStruct((bsz, 2, D_MODEL), F32),
        scratch_shapes=[
            pltpu.VMEM((n + 2 * SUBLANES, D_MODEL), F32),
            pltpu.VMEM((n, D_MODEL), F32),
            pltpu.VMEM((n, D_MODEL), F32),
        ],
        name="ctx",
    )(ctx, mod_c, pre1_g, w_in_bf, w_sc, b_sc, wa_bf, b_rg_a, wx_bf, b_rg_x, lam)


def _mixer_fwd_kernel(x_ref, mod_ref, g_ref, win_ref, wdw_ref, bdw_ref, lng_ref, lnb_ref, wco_ref,
                      wsc_ref, bsc_ref, wa_ref, ba_ref, wx_ref, bx_ref, lam_ref, h0_ref,
                      gaya_ref, gb_ref, gbr_ref, urnn_ref, hf_ref,
                      upad_ref, cv_ref, uext_ref, a_ref, b_ref, carry_ref):
    j = pl.program_id(1)
    zeros8 = jnp.zeros((SUBLANES, D_MODEL), F32)

    @pl.when(j == 0)
    def _():
        carry_ref[...] = jnp.broadcast_to(h0_ref[0:1, :], (SUBLANES, D_MODEL))
        uext_ref[0:SUBLANES, :] = zeros8
        zeros_pad = jnp.zeros((PAD, D_MODEL), F32)
        for r in range(ROWS_PER_TILE + 1):
            upad_ref[r * ROW_STRIDE:r * ROW_STRIDE + PAD, :] = zeros_pad

    hx = (_rms_norm(x_ref[...], g_ref[...]) * (1.0 + mod_ref[1:2, :]) + mod_ref[0:1, :]).astype(BF16)

    u = _dot(hx, win_ref[:, 0:D_MODEL]) * _sigmoid(_dot(hx, win_ref[:, D_MODEL:2 * D_MODEL]))
    for r in range(ROWS_PER_TILE):
        upad_ref[PAD + r * ROW_STRIDE:PAD + r * ROW_STRIDE + GRID_W, :] = u[r * GRID_W:(r + 1) * GRID_W, :]
    for r in range(ROWS_PER_TILE):
        for c in range(D_MODEL // CONV_LANES):
            cs = slice(c * CONV_LANES, (c + 1) * CONV_LANES)
            acc = jnp.broadcast_to(bdw_ref[:, cs], (GRID_W, CONV_LANES))
            for k in range(CONV_WIDTH):
                start = r * ROW_STRIDE + PAD - CONV_HALF + k
                acc = acc + wdw_ref[k:k + 1, cs] * upad_ref[start:start + GRID_W, cs]
            cv_ref[r * GRID_W:(r + 1) * GRID_W, cs] = acc
    cv = cv_ref[...]
    cvc = cv - jnp.mean(cv, axis=-1, keepdims=True)
    cvn = cvc * lax.rsqrt(jnp.mean(cvc * cvc, axis=-1, keepdims=True) + EPS) * lng_ref[...] + lnb_ref[...]
    y_a = _dot(_silu(cvn).astype(BF16), wco_ref[...])

    g_a = _sigmoid(_dot(hx, win_ref[:, 4 * D_MODEL:5 * D_MODEL]))
    gaya_ref[...] = (g_a * y_a).astype(BF16)
    gb_ref[...] = _sigmoid(_dot(hx, win_ref[:, 5 * D_MODEL:6 * D_MODEL])).astype(BF16)
    gbr_ref[...] = _gelu_tanh(_dot(hx, win_ref[:, 3 * D_MODEL:4 * D_MODEL])).astype(BF16)

    ur = _dot(hx, win_ref[:, 2 * D_MODEL:3 * D_MODEL])
    urnn_ref[...] = ur.astype(BF16)
    uext_ref[SUBLANES:SUBLANES + TM, :] = ur
    v = jnp.broadcast_to(bsc_ref[...], (TM, D_MODEL))
    for k in range(SHORT_CONV):
        start = SUBLANES + k - (SHORT_CONV - 1)
        v = v + wsc_ref[k:k + 1, :] * uext_ref[start:start + TM, :]
    uext_ref[0:SUBLANES, :] = uext_ref[TM:TM + SUBLANES, :]
    _rglru_coeffs(v, wa_ref, ba_ref[...], wx_ref, bx_ref[...], _log_decay(lam_ref[...]), a_ref, b_ref)
    carry_ref[...] = _scan_tile(a_ref, b_ref, carry_ref[...], TM, reverse=False)
    hf_ref[...] = b_ref[...].astype(BF16)


def _resident(shape):
    nd = len(shape)
    return pl.BlockSpec(shape, lambda b, j: (0,) * nd, pipeline_mode=pl.Buffered(1))


def _mixer_fwd_call(x, mod_x, pre1_g, w_in_bf, w_dw, b_dw, ln_g, ln_b, wco_bf,
                    w_sc, b_sc, wa_bf, b_rg_a, wx_bf, b_rg_x, lam, h0):
    bsz, seq, _ = x.shape
    nt = seq // TM
    tile = pl.BlockSpec((None, TM, D_MODEL), lambda b, j: (b, j, 0))
    act = jax.ShapeDtypeStruct((bsz, seq, D_MODEL), BF16)
    head_w = pl.BlockSpec((None, RNN_HEADS, RNN_BLOCK, RNN_BLOCK), lambda b, j: (0, 0, 0, 0),
                          pipeline_mode=pl.Buffered(1))
    dir_row = pl.BlockSpec((None, 1, D_MODEL), lambda b, j: (0, 0, 0), pipeline_mode=pl.Buffered(1))
    return pl.pallas_call(
        _mixer_fwd_kernel,
        grid=(bsz, nt),
        in_specs=[
            tile,
            pl.BlockSpec((None, 6, D_MODEL), lambda b, j: (b, 0, 0)),
            _resident((1, D_MODEL)),
            _resident((D_MODEL, 6 * D_MODEL)),
            _resident((CONV_WIDTH, D_MODEL)),
            _resident((1, D_MODEL)),
            _resident((1, D_MODEL)),
            _resident((1, D_MODEL)),
            _resident((D_MODEL, D_MODEL)),
            pl.BlockSpec((None, SHORT_CONV, D_MODEL), lambda b, j: (0, 0, 0), pipeline_mode=pl.Buffered(1)),
            dir_row, head_w, dir_row, head_w, dir_row, dir_row,
            pl.BlockSpec((None, 2, D_MODEL), lambda b, j: (b, 0, 0)),
        ],
        out_specs=[tile] * 5,
        out_shape=[act] * 5,
        scratch_shapes=[
            pltpu.VMEM((ROWS_PER_TILE * ROW_STRIDE + PAD, D_MODEL), F32),
            pltpu.VMEM((TM, D_MODEL), F32),
            pltpu.VMEM((TM + SUBLANES, D_MODEL), F32),
            pltpu.VMEM((TM, D_MODEL), F32),
            pltpu.VMEM((TM, D_MODEL), F32),
            pltpu.VMEM((SUBLANES, D_MODEL), F32),
        ],
        compiler_params=pltpu.CompilerParams(
            dimension_semantics=("arbitrary", "arbitrary"), vmem_limit_bytes=VMEM_LIMIT),
        name="mixer_fwd",
    )(x, mod_x, pre1_g, w_in_bf, w_dw, b_dw, ln_g, ln_b, wco_bf,
      w_sc, b_sc.reshape(2, 1, D_MODEL), wa_bf, b_rg_a.reshape(2, 1, D_MODEL), wx_bf,
      b_rg_x.reshape(2, 1, D_MODEL), lam.reshape(2, 1, D_MODEL), h0)


def _route(logits_t, bias):
    t = logits_t.shape[1]
    scores = _sigmoid(logits_t)
    sel = scores + bias
    neg_inf = jnp.float32(-jnp.inf)

    sel3 = sel.reshape(N_GROUPS, GROUP_SIZE, t)
    within = lax.broadcasted_iota(jnp.int32, sel3.shape, 1)
    m1 = jnp.max(sel3, axis=1, keepdims=True)
    first = jnp.min(jnp.where(sel3 == m1, within, GROUP_SIZE), axis=1, keepdims=True)
    m2 = jnp.max(jnp.where(within == first, neg_inf, sel3), axis=1, keepdims=True)
    gscore = (m1 + m2).reshape(N_GROUPS, t)

    gidx = lax.broadcasted_iota(jnp.int32, gscore.shape, 0)
    rank = jnp.zeros(gscore.shape, jnp.int32)
    for g in range(N_GROUPS):
        other = gscore[g:g + 1, :]
        beats = jnp.where(other > gscore, 1, jnp.where((other == gscore) & (gidx > g), 1, 0))
        rank = rank + beats
    gkeep = (rank < TOPK_GROUPS).reshape(N_GROUPS, 1, t)
    masked = jnp.where(gkeep, sel3, neg_inf).reshape(N_EXPERTS, t)

    eidx = lax.broadcasted_iota(jnp.int32, masked.shape, 0)
    picks, weights = [], []
    for _ in range(TOP_K):
        m = jnp.max(masked, axis=0, keepdims=True)
        first = jnp.min(jnp.where(masked == m, eidx, N_EXPERTS), axis=0, keepdims=True)
        pick = eidx == first
        picks.append(first)
        weights.append(jnp.sum(jnp.where(pick, scores, 0.0), axis=0, keepdims=True))
        masked = jnp.where(pick, neg_inf, masked)
    idx = jnp.concatenate(picks, axis=0)
    w = jnp.concatenate(weights, axis=0)
    return idx, ROUTED_SCALE * w / jnp.sum(w, axis=0, keepdims=True)


def _mixer_bwd_kernel(urnn_ref, hf_ref, gbr_ref, gaya_ref, gb_ref, x_ref, mod_ref,
                      wsc_ref, bsc_ref, wa_ref, ba_ref, wx_ref, bx_ref, lam_ref, h0_ref,
                      wro_ref, wout_ref, post1_ref, pre2_ref, wrt_ref, rbias_ref,
                      x1_ref, hx2w_ref, idx_ref, w_ref,
                      uext_ref, a_ref, b_ref, carry_ref):
    j = pl.program_id(1)
    zeros8 = jnp.zeros((SUBLANES, D_MODEL), F32)

    @pl.when(j == 0)
    def _():
        carry_ref[...] = jnp.broadcast_to(h0_ref[1:2, :], (SUBLANES, D_MODEL))
        uext_ref[TM:TM + SUBLANES, :] = zeros8

    uext_ref[0:TM, :] = urnn_ref[...].astype(F32)
    v = jnp.broadcast_to(bsc_ref[...], (TM, D_MODEL))
    for k in range(SHORT_CONV):
        v = v + wsc_ref[k:k + 1, :] * uext_ref[k:k + TM, :]
    uext_ref[TM:TM + SUBLANES, :] = uext_ref[0:SUBLANES, :]
    _rglru_coeffs(v, wa_ref, ba_ref[...], wx_ref, bx_ref[...], _log_decay(lam_ref[...]), a_ref, b_ref)
    carry_ref[...] = _scan_tile(a_ref, b_ref, carry_ref[...], TM, reverse=True)

    h_sum = hf_ref[...].astype(F32) + b_ref[...]
    y_b = _dot((gbr_ref[...].astype(F32) * h_sum).astype(BF16), wro_ref[...])
    mix = gaya_ref[...].astype(F32) + gb_ref[...].astype(F32) * y_b
    out = _dot(mix.astype(BF16), wout_ref[...])
    x1 = x_ref[...] + mod_ref[2:3, :] * _rms_norm(out, post1_ref[...])
    x1_ref[...] = x1

    hx2 = _rms_norm(x1, pre2_ref[...]) * (1.0 + mod_ref[4:5, :]) + mod_ref[3:4, :]
    hx2w_ref[...] = _pack_bf16_pairs(hx2)
    logits_t = lax.dot_general(wrt_ref[...], hx2, (((1,), (1,)), ((), ())),
                               preferred_element_type=F32, precision=lax.Precision.HIGHEST)
    idx, w = _route(logits_t, rbias_ref[...])
    idx_ref[...] = idx
    w_ref[...] = w


def _mixer_bwd_call(urnn, hf, gbr, gaya, gb, x, mod_x, w_sc, b_sc, wa_bf, b_rg_a, wx_bf, b_rg_x, lam, h0,
                    wro_bf, wout_bf, post1_g, pre2_g, w_router_t, router_bias):
    bsz, seq, _ = x.shape
    nt = seq // TM
    rev = lambda b, j: (b, nt - 1 - j, 0)
    tile = pl.BlockSpec((None, TM, D_MODEL), rev)
    head_w = pl.BlockSpec((None, RNN_HEADS, RNN_BLOCK, RNN_BLOCK), lambda b, j: (1, 0, 0, 0),
                          pipeline_mode=pl.Buffered(1))
    dir_row = pl.BlockSpec((None, 1, D_MODEL), lambda b, j: (1, 0, 0), pipeline_mode=pl.Buffered(1))
    return pl.pallas_call(
        _mixer_bwd_kernel,
        grid=(bsz, nt),
        in_specs=[
            tile, tile, tile, tile, tile, tile,
            pl.BlockSpec((None, 6, D_MODEL), lambda b, j: (b, 0, 0)),
            pl.BlockSpec((None, SHORT_CONV, D_MODEL), lambda b, j: (1, 0, 0), pipeline_mode=pl.Buffered(1)),
            dir_row, head_w, dir_row, head_w, dir_row, dir_row,
            pl.BlockSpec((None, 2, D_MODEL), lambda b, j: (b, 0, 0)),
            _resident((D_MODEL, D_MODEL)),
            _resident((D_MODEL, D_MODEL)),
            _resident((1, D_MODEL)),
            _resident((1, D_MODEL)),
            _resident((N_EXPERTS, D_MODEL)),
            _resident((N_EXPERTS, 1)),
        ],
        out_specs=[
            tile,
            pl.BlockSpec((None, TM, ROW_WORDS), rev),
            pl.BlockSpec((TOP_K, TM), lambda b, j: (0, b * nt + nt - 1 - j)),
            pl.BlockSpec((TOP_K, TM), lambda b, j: (0, b * nt + nt - 1 - j)),
        ],
        out_shape=[
            jax.ShapeDtypeStruct((bsz, seq, D_MODEL), F32),
            jax.ShapeDtypeStruct((bsz, seq, ROW_WORDS), jnp.int32),
            jax.ShapeDtypeStruct((TOP_K, bsz * seq), jnp.int32),
            jax.ShapeDtypeStruct((TOP_K, bsz * seq), F32),
        ],
        scratch_shapes=[
            pltpu.VMEM((TM + SUBLANES, D_MODEL), F32),
            pltpu.VMEM((TM, D_MODEL), F32),
            pltpu.VMEM((TM, D_MODEL), F32),
            pltpu.VMEM((SUBLANES, D_MODEL), F32),
        ],
        compiler_params=pltpu.CompilerParams(
            dimension_semantics=("arbitrary", "arbitrary"), vmem_limit_bytes=VMEM_LIMIT),
        name="mixer_bwd",
    )(urnn, hf, gbr, gaya, gb, x, mod_x, w_sc, b_sc.reshape(2, 1, D_MODEL), wa_bf,
      b_rg_a.reshape(2, 1, D_MODEL), wx_bf, b_rg_x.reshape(2, 1, D_MODEL), lam.reshape(2, 1, D_MODEL), h0,
      wro_bf, wout_bf, post1_g, pre2_g, w_router_t, router_bias)


def _positions_kernel(idx_ref, pos_ref, te_ref, nused_ref):
    n = idx_ref.shape[1]
    n_tiles = n // POS_TILE
    eidx = lax.broadcasted_iota(jnp.int32, (N_EXPERTS, POS_TILE), 0)

    def chosen(t):
        idx = idx_ref[:, pl.ds(pl.multiple_of(t * POS_TILE, POS_TILE), POS_TILE)]
        ch = jnp.zeros((N_EXPERTS, POS_TILE), F32)
        for k in range(TOP_K):
            ch = ch + jnp.where(eidx == idx[k:k + 1, :], 1.0, 0.0)
        return idx, ch

    def count_body(t, cnt):
        return cnt + jnp.sum(chosen(t)[1], axis=1, keepdims=True)

    cnt = lax.fori_loop(0, n_tiles, count_body, jnp.zeros((N_EXPERTS, 1), F32))
    padded = jnp.ceil(cnt * (1.0 / ROW_TILE)) * ROW_TILE
    r = lax.broadcasted_iota(jnp.int32, (N_EXPERTS, N_EXPERTS), 0)
    c = lax.broadcasted_iota(jnp.int32, (N_EXPERTS, N_EXPERTS), 1)
    off = jnp.dot(jnp.where(c < r, 1.0, 0.0), jnp.broadcast_to(padded, (N_EXPERTS, LANES)),
                  preferred_element_type=F32, precision=lax.Precision.HIGHEST)[:, 0:1]
    end = off + padded

    n_map = te_ref.shape[1]
    tstart = lax.broadcasted_iota(jnp.int32, (N_EXPERTS, n_map), 1).astype(F32) * ROW_TILE
    te_ref[...] = jnp.minimum(jnp.sum(jnp.where(end <= tstart, 1, 0), axis=0, keepdims=True), N_EXPERTS - 1)
    total = jnp.sum(padded, axis=0, keepdims=True)
    nused_ref[...] = jnp.broadcast_to(total * (1.0 / ROW_TILE), nused_ref.shape).astype(jnp.int32)

    row = lax.broadcasted_iota(jnp.int32, (POS_TILE, POS_TILE), 0)
    col = lax.broadcasted_iota(jnp.int32, (POS_TILE, POS_TILE), 1)
    before = jnp.where(row < col, 1.0, 0.0).astype(BF16)

    def pos_body(t, carry):
        idx, ch = chosen(t)
        base = _dot(ch.astype(BF16), before) + (carry + off)
        rows = [jnp.sum(jnp.where(eidx == idx[k:k + 1, :], base, 0.0), axis=0, keepdims=True)
                for k in range(TOP_K)]
        pos_ref[:, pl.ds(pl.multiple_of(t * POS_TILE, POS_TILE), POS_TILE)] = (
            jnp.concatenate(rows, axis=0).astype(jnp.int32))
        return carry + jnp.sum(ch, axis=1, keepdims=True)

    lax.fori_loop(0, n_tiles, pos_body, jnp.zeros((N_EXPERTS, 1), F32))


def _positions_call(idx_t, n_row_tiles):
    n = idx_t.shape[1]
    n_map = -(-n_row_tiles // LANES) * LANES
    return pl.pallas_call(
        _positions_kernel,
        out_shape=[
            jax.ShapeDtypeStruct((TOP_K, n), jnp.int32),
            jax.ShapeDtypeStruct((1, n_map), jnp.int32),
            jax.ShapeDtypeStruct((1, LANES), jnp.int32),
        ],
        name="moe_positions",
    )(idx_t)


def _sc_mesh():
    return plsc.VectorSubcoreMesh(core_axis_name="c", subcore_axis_name="s",
                                  num_cores=V7X_SC_CORES, num_subcores=V7X_SC_SUBCORES)


def _sc_worker():
    return lax.axis_index("s") * V7X_SC_CORES + lax.axis_index("c")


def _dispatch_call(rows, pos, n_slots):
    n = rows.shape[0]
    tok_w = n // SC_WORKERS
    n_items = tok_w // SC_CHUNK

    def body(rows_hbm, pos_hbm, xs_hbm, idx_v, rows_v, lsem, ssem):
        wid = _sc_worker()
        pltpu.sync_copy(pos_hbm.at[wid], idx_v)
        base = wid * tok_w

        def load(i):
            b = i % SC_BUFFERS
            return pltpu.async_copy(rows_hbm.at[pl.ds(base + i * SC_CHUNK, SC_CHUNK)], rows_v.at[b], lsem.at[b])

        def scatter(i):
            b = i % SC_BUFFERS
            return [pltpu.async_copy(rows_v.at[b], xs_hbm.at[idx_v.at[i, k]], ssem.at[b]) for k in range(TOP_K)]

        loads = {i: load(i) for i in range(SC_BUFFERS - 1)}
        scat = {}
        for i in range(n_items):
            loads[i].wait()
            scat[i] = scatter(i)
            if i >= 1:
                for cp in scat[i - 1]:
                    cp.wait()
            if i + SC_BUFFERS - 1 < n_items:
                loads[i + SC_BUFFERS - 1] = load(i + SC_BUFFERS - 1)
        for cp in scat[n_items - 1]:
            cp.wait()

    return pl.kernel(
        body, mesh=_sc_mesh(),
        out_type=jax.ShapeDtypeStruct((n_slots, ROW_WORDS), jnp.int32),
        scratch_types=[pltpu.VMEM((n_items, TOP_K, SC_CHUNK), jnp.int32),
                       pltpu.VMEM((SC_BUFFERS, SC_CHUNK, ROW_WORDS), jnp.int32),
                       pltpu.SemaphoreType.DMA((SC_BUFFERS,)), pltpu.SemaphoreType.DMA((SC_BUFFERS,))],
        compiler_params=pltpu.CompilerParams(use_tc_tiling_on_sc=True),
        name="moe_dispatch",
    )(rows, pos)


def _collect_call(ys, pos, n):
    tok_w = n // SC_WORKERS
    n_chunks = tok_w // SC_CHUNK
    items = [(c, k) for c in range(n_chunks) for k in range(TOP_K)]

    def body(ys_hbm, pos_hbm, yt_hbm, idx_v, rows_v, gsem, wsem):
        wid = _sc_worker()
        pltpu.sync_copy(pos_hbm.at[wid], idx_v)
        base = wid * tok_w

        def gather(j):
            c, k = items[j]
            b = j % SC_BUFFERS
            return pltpu.async_copy(ys_hbm.at[idx_v.at[c, k]], rows_v.at[b], gsem.at[b])

        def write(j):
            c, k = items[j]
            b = j % SC_BUFFERS
            return pltpu.async_copy(rows_v.at[b], yt_hbm.at[k, pl.ds(base + c * SC_CHUNK, SC_CHUNK)], wsem.at[b])

        g = {j: gather(j) for j in range(SC_BUFFERS - 1)}
        w = {}
        for j in range(len(items)):
            g[j].wait()
            w[j] = write(j)
            if j >= 1:
                w[j - 1].wait()
            if j + SC_BUFFERS - 1 < len(items):
                g[j + SC_BUFFERS - 1] = gather(j + SC_BUFFERS - 1)
        w[len(items) - 1].wait()

    return pl.kernel(
        body, mesh=_sc_mesh(),
        out_type=jax.ShapeDtypeStruct((TOP_K, n, ROW_WORDS), jnp.int32),
        scratch_types=[pltpu.VMEM((n_chunks, TOP_K, SC_CHUNK), jnp.int32),
                       pltpu.VMEM((SC_BUFFERS, SC_CHUNK, ROW_WORDS), jnp.int32),
                       pltpu.SemaphoreType.DMA((SC_BUFFERS,)), pltpu.SemaphoreType.DMA((SC_BUFFERS,))],
        compiler_params=pltpu.CompilerParams(use_tc_tiling_on_sc=True),
        name="moe_collect",
    )(ys, pos)


def _expert_gemm_kernel(te_ref, nused_ref, xs_ref, wg_ref, wu_ref, wd_ref, ys_ref, wgu_scr, wd_scr):
    i = pl.program_id(0)
    e = te_ref[i]
    prev = te_ref[jnp.maximum(i - 1, 0)]

    @pl.when((i == 0) | (e != prev))
    def _():
        wgu_scr[:, 0:D_EXPERT] = wg_ref[...].astype(BF16)
        wgu_scr[:, D_EXPERT:2 * D_EXPERT] = wu_ref[...].astype(BF16)
        wd_scr[...] = wd_ref[...].astype(BF16)

    @pl.when(i < nused_ref[0])
    def _():
        lo, hi = _unpack_bf16_pairs(xs_ref[...])
        gu = _dot(lo.astype(BF16), wgu_scr[0:ROW_WORDS, :]) + _dot(hi.astype(BF16), wgu_scr[ROW_WORDS:D_MODEL, :])
        h = _silu(gu[:, 0:D_EXPERT]) * gu[:, D_EXPERT:2 * D_EXPERT]
        ys_ref[...] = _pack_bf16_pairs(_dot(h.astype(BF16), wd_scr[...]))


def _expert_gemm_call(te, nused, xs, w_e_gate, w_e_up, w_e_down):
    n_slots = xs.shape[0]
    n_row_tiles = n_slots // ROW_TILE
    rows = pl.BlockSpec((ROW_TILE, ROW_WORDS), lambda i, te, nused: (jnp.minimum(i, nused[0] - 1), 0))
    return pl.pallas_call(
        _expert_gemm_kernel,
        grid_spec=pltpu.PrefetchScalarGridSpec(
            num_scalar_prefetch=2,
            grid=(n_row_tiles,),
            in_specs=[
                rows,
                pl.BlockSpec((None, D_MODEL, D_EXPERT), lambda i, te, nused: (te[i], 0, 0)),
                pl.BlockSpec((None, D_MODEL, D_EXPERT), lambda i, te, nused: (te[i], 0, 0)),
                pl.BlockSpec((None, D_EXPERT, D_MODEL), lambda i, te, nused: (te[i], 0, 0)),
            ],
            out_specs=rows,
            scratch_shapes=[pltpu.VMEM((D_MODEL, 2 * D_EXPERT), BF16), pltpu.VMEM((D_EXPERT, D_MODEL), BF16)],
        ),
        out_shape=jax.ShapeDtypeStruct((n_slots, ROW_WORDS), jnp.int32),
        compiler_params=pltpu.CompilerParams(dimension_semantics=("arbitrary",), vmem_limit_bytes=VMEM_LIMIT),
        name="moe_experts",
    )(te, nused, xs, w_e_gate, w_e_up, w_e_down)


def _moe_out_kernel(yt_ref, w_ref, t_ref, x1_ref, mod_ref, post2_ref, wsg_ref, wsu_ref, wsd_ref, o_ref):
    lo, hi = _unpack_bf16_pairs(t_ref[...])
    lo = lo.astype(BF16)
    hi = hi.astype(BF16)
    g = _dot(lo, wsg_ref[0:ROW_WORDS, :]) + _dot(hi, wsg_ref[ROW_WORDS:D_MODEL, :])
    u = _dot(lo, wsu_ref[0:ROW_WORDS, :]) + _dot(hi, wsu_ref[ROW_WORDS:D_MODEL, :])
    shared = _dot((_silu(g) * u).astype(BF16), wsd_ref[...])
    acc_lo = shared[:, 0:ROW_WORDS]
    acc_hi = shared[:, ROW_WORDS:D_MODEL]
    for k in range(TOP_K):
        y_lo, y_hi = _unpack_bf16_pairs(yt_ref[k])
        wk = w_ref[:, k:k + 1]
        acc_lo = acc_lo + wk * y_lo
        acc_hi = acc_hi + wk * y_hi
    moe = jnp.concatenate([acc_lo, acc_hi], axis=-1)
    o_ref[...] = x1_ref[...] + mod_ref[5:6, :] * _rms_norm(moe, post2_ref[...])


def _moe_out_call(yt, w, t, x1, mod_x, post2_g, wsg_bf, wsu_bf, wsd_bf, seq):
    n = t.shape[0]
    tiles_per_seq = seq // TOUT
    const = lambda i: (0, 0)
    return pl.pallas_call(
        _moe_out_kernel,
        grid=(n // TOUT,),
        in_specs=[
            pl.BlockSpec((TOP_K, TOUT, ROW_WORDS), lambda i: (0, i, 0)),
            pl.BlockSpec((TOUT, TOP_K), lambda i: (i, 0)),
            pl.BlockSpec((TOUT, ROW_WORDS), lambda i: (i, 0)),
            pl.BlockSpec((TOUT, D_MODEL), lambda i: (i, 0)),
            pl.BlockSpec((None, 6, D_MODEL), lambda i: (i // tiles_per_seq, 0, 0)),
            pl.BlockSpec((1, D_MODEL), const),
            pl.BlockSpec((D_MODEL, D_EXPERT), const),
            pl.BlockSpec((D_MODEL, D_EXPERT), const),
            pl.BlockSpec((D_EXPERT, D_MODEL), const),
        ],
        out_specs=pl.BlockSpec((TOUT, D_MODEL), lambda i: (i, 0)),
        out_shape=jax.ShapeDtypeStruct((n, D_MODEL), F32),
        compiler_params=pltpu.CompilerParams(dimension_semantics=("arbitrary",), vmem_limit_bytes=VMEM_LIMIT),
        name="moe_out",
    )(yt, w, t, x1, mod_x, post2_g, wsg_bf, wsu_bf, wsd_bf)


def _moe(hx2w, idx_t, w_t, x1, mod_x, post2_g, w_e_gate, w_e_up, w_e_down, w_s_gate, w_s_up, w_s_down, seq):
    n = hx2w.shape[0]
    n_slots = n * TOP_K + N_EXPERTS * ROW_TILE
    n_row_tiles = n_slots // ROW_TILE
    pos_t, te, nused = _positions_call(idx_t, n_row_tiles)
    pos = pos_t.reshape(TOP_K, SC_WORKERS, n // (SC_WORKERS * SC_CHUNK), SC_CHUNK).transpose(1, 2, 0, 3)
    xs = _dispatch_call(hx2w, pos, n_slots)
    ys = _expert_gemm_call(te[0, :n_row_tiles], nused[0, :1], xs, w_e_gate, w_e_up, w_e_down)
    yt = _collect_call(ys, pos, n)
    return _moe_out_call(yt, w_t.T, hx2w, x1, mod_x, post2_g, w_s_gate.astype(BF16), w_s_up.astype(BF16),
                         w_s_down.astype(BF16), seq)


def kernel(x, c, ctx, c_ctx, w_mod, b_mod, pre1_g, post1_g, pre2_g, post2_g, w_in, w_dw, b_dw, ln_conv_g, ln_conv_b, w_conv_out, w_sc, b_sc, w_rg_a, b_rg_a, w_rg_x, b_rg_x, lru_lambda, w_rnn_out, w_out, w_router, router_bias, w_e_gate, w_e_up, w_e_down, w_s_gate, w_s_up, w_s_down):
    assert w_mod.shape[0] == 1, "single-layer block"
    bsz, seq, d = x.shape
    n = bsz * seq
    assert d == D_MODEL and seq % TM == 0 and seq % TOUT == 0 and bsz + 1 <= SUBLANES
    assert n % (SC_WORKERS * SC_CHUNK) == 0 and n % POS_TILE == 0

    cc = jnp.zeros((SUBLANES, D_MODEL), F32).at[:bsz].set(c).at[bsz].set(c_ctx)
    mod = _mod_call(cc, w_mod[0], b_mod)
    mod_x = mod[:bsz].reshape(bsz, 6, D_MODEL)
    mod_c = mod[bsz].reshape(6, D_MODEL)

    w_in_bf = w_in[0].astype(BF16)
    wa_bf = w_rg_a[0].astype(BF16)
    wx_bf = w_rg_x[0].astype(BF16)

    h0 = _ctx_call(ctx, mod_c, pre1_g, w_in_bf, w_sc[0], b_sc[0], wa_bf, b_rg_a[0], wx_bf, b_rg_x[0],
                   lru_lambda[0])

    gaya, gb, gbr, urnn, hf = _mixer_fwd_call(
        x, mod_x, pre1_g, w_in_bf, w_dw[0], b_dw, ln_conv_g, ln_conv_b, w_conv_out[0].astype(BF16),
        w_sc[0], b_sc[0], wa_bf, b_rg_a[0], wx_bf, b_rg_x[0], lru_lambda[0], h0)

    x1, hx2w, idx_t, w_t = _mixer_bwd_call(
        urnn, hf, gbr, gaya, gb, x, mod_x, w_sc[0], b_sc[0], wa_bf, b_rg_a[0], wx_bf, b_rg_x[0],
        lru_lambda[0], h0, w_rnn_out[0].astype(BF16), w_out[0].astype(BF16), post1_g, pre2_g,
        w_router[0].T, router_bias.reshape(N_EXPERTS, 1))

    out = _moe(hx2w.reshape(n, ROW_WORDS), idx_t, w_t, x1.reshape(n, D_MODEL), mod_x, post2_g,
               w_e_gate[0], w_e_up[0], w_e_down[0], w_s_gate[0], w_s_up[0], w_s_down[0], seq)
    return out.reshape(bsz, seq, D_MODEL)
```

```python
import jax
import jax.numpy as jnp
from jax import lax
from jax.experimental import pallas as pl
from jax.experimental.pallas import tpu as pltpu
from jax.experimental.pallas import tpu_sc as plsc

F32 = jnp.float32
BF16 = jnp.bfloat16

D_MODEL = 1024
GRID_W = 64
CONV_WIDTH = 31
CONV_HALF = CONV_WIDTH // 2
SHORT_CONV = 4
RNN_HEADS = 4
RNN_BLOCK = D_MODEL // RNN_HEADS
LRU_C = 8.0
N_EXPERTS = 64
N_GROUPS = 8
GROUP_SIZE = N_EXPERTS // N_GROUPS
TOPK_GROUPS = 4
TOP_K = 8
D_EXPERT = 256
ROUTED_SCALE = 2.5
EPS = 1e-6

SUBLANES = 8
TM = 512
ROWS_PER_TILE = TM // GRID_W
PAD = 16
ROW_STRIDE = GRID_W + PAD
UPAD_ROWS = ROWS_PER_TILE * ROW_STRIDE + PAD
CONV_LANES = 256
LANES = 128
ROW_WORDS = D_MODEL // 2
ROW_TILE = 512
POS_TILE = 512
TOUT = 256
V7X_SC_CORES = 2
V7X_SC_SUBCORES = 16
SC_WORKERS = V7X_SC_CORES * V7X_SC_SUBCORES
SC_CHUNK = 64
SC_BUFFERS = 2
VMEM_LIMIT = 58 * 1024 * 1024


def _sigmoid(x):
    return 0.5 * (jnp.tanh(0.5 * x) + 1.0)


def _silu(x):
    return x * _sigmoid(x)


def _gelu_tanh(x):
    return 0.5 * x * (1.0 + jnp.tanh(0.7978845608028654 * (x + 0.044715 * (x * x * x))))


def _rms_norm(x, g):
    return x * lax.rsqrt(jnp.mean(x * x, axis=-1, keepdims=True) + EPS) * g


def _dot(a, b):
    return jnp.dot(a, b, preferred_element_type=F32)


def _pack_bf16_pairs(x):
    half = x.shape[-1] // 2
    lo = lax.bitcast_convert_type(x[:, :half].astype(BF16).astype(F32), jnp.uint32)
    hi = lax.bitcast_convert_type(x[:, half:].astype(BF16).astype(F32), jnp.uint32)
    return lax.bitcast_convert_type(hi | (lo >> 16), jnp.int32)


def _unpack_bf16_pairs(words):
    u = lax.bitcast_convert_type(words, jnp.uint32)
    return (lax.bitcast_convert_type(u << 16, F32),
            lax.bitcast_convert_type(u & jnp.uint32(0xFFFF0000), F32))


def _log_decay(lam):
    return LRU_C * (jnp.minimum(lam, 0.0) - jnp.log1p(jnp.exp(-jnp.abs(lam))))


def _rglru_coeffs(v, wa_ref, ba, wx_ref, bx, c_lam, a_ref, b_ref):
    vb = v.astype(BF16)
    for h in range(RNN_HEADS):
        cs = slice(h * RNN_BLOCK, (h + 1) * RNN_BLOCK)
        vh = vb[:, cs]
        r = _sigmoid(_dot(vh, wa_ref[h]) + ba[:, cs])
        i = _sigmoid(_dot(vh, wx_ref[h]) + bx[:, cs])
        log_a = c_lam[:, cs] * r
        a = jnp.exp(log_a)
        a_ref[:, cs] = a
        b_ref[:, cs] = jnp.sqrt(jnp.tanh(-log_a) * (1.0 + a * a)) * (i * v[:, cs])


def _scan_tile(a_ref, b_ref, carry, n_rows, reverse):
    row = lax.broadcasted_iota(jnp.int32, (SUBLANES, D_MODEL), 0)
    n_groups = n_rows // SUBLANES

    def body(j, carry):
        g = (n_groups - 1 - j) if reverse else j
        off = pl.multiple_of(g * SUBLANES, SUBLANES)
        a = a_ref[pl.ds(off, SUBLANES), :]
        b = b_ref[pl.ds(off, SUBLANES), :]
        for s in (1, 2, 4):
            keep = (row < SUBLANES - s) if reverse else (row >= s)
            shift = (SUBLANES - s) if reverse else s
            a_sh = jnp.where(keep, pltpu.roll(a, shift, 0), 1.0)
            b_sh = jnp.where(keep, pltpu.roll(b, shift, 0), 0.0)
            b = a * b_sh + b
            a = a * a_sh
        h = a * carry + b
        b_ref[pl.ds(off, SUBLANES), :] = h
        last = h[0:1, :] if reverse else h[SUBLANES - 1:SUBLANES, :]
        return jnp.broadcast_to(last, (SUBLANES, D_MODEL))

    return lax.fori_loop(0, n_groups, body, carry, unroll=2)


def _mod_kernel(c_ref, w_ref, b_ref, o_ref):
    o_ref[...] = jnp.dot(_silu(c_ref[...]), w_ref[...], preferred_element_type=F32,
                         precision=lax.Precision.HIGHEST) + b_ref[...]


def _mod_call(cc, w_mod, b_mod):
    n_mod = w_mod.shape[1]
    return pl.pallas_call(
        _mod_kernel,
        grid=(n_mod // D_MODEL,),
        in_specs=[
            pl.BlockSpec((SUBLANES, D_MODEL), lambda j: (0, 0)),
            pl.BlockSpec((D_MODEL, D_MODEL), lambda j: (0, j)),
            pl.BlockSpec((1, D_MODEL), lambda j: (0, j)),
        ],
        out_specs=pl.BlockSpec((SUBLANES, D_MODEL), lambda j: (0, j)),
        out_shape=jax.ShapeDtypeStruct((SUBLANES, n_mod), F32),
        name="mod",
    )(cc, w_mod, b_mod)


def _ctx_kernel(ctx_ref, mod_ref, g_ref, w_ref, wsc_ref, bsc_ref, wa_ref, ba_ref, wx_ref, bx_ref,
                lam_ref, o_ref, uext_ref, a_ref, b_ref):
    n = ctx_ref.shape[0]
    hc = _rms_norm(ctx_ref[...], g_ref[...]) * (1.0 + mod_ref[1:2, :]) + mod_ref[0:1, :]
    u = _dot(hc.astype(BF16), w_ref[...])
    zeros8 = jnp.zeros((SUBLANES, D_MODEL), F32)
    uext_ref[0:SUBLANES, :] = zeros8
    uext_ref[SUBLANES:SUBLANES + n, :] = u
    uext_ref[SUBLANES + n:2 * SUBLANES + n, :] = zeros8
    for d in range(2):
        v = jnp.broadcast_to(bsc_ref[d:d + 1, :], (n, D_MODEL))
        for k in range(SHORT_CONV):
            start = SUBLANES + k - (SHORT_CONV - 1) * (1 - d)
            v = v + wsc_ref[d, k:k + 1, :] * uext_ref[start:start + n, :]
        _rglru_coeffs(v, wa_ref.at[d], ba_ref[d:d + 1, :], wx_ref.at[d], bx_ref[d:d + 1, :],
                      _log_decay(lam_ref[d:d + 1, :]), a_ref, b_ref)
        final = _scan_tile(a_ref, b_ref, zeros8, n, reverse=(d == 1))
        o_ref[d:d + 1, :] = final[0:1, :]


def _ctx_call(ctx, mod_c, pre1_g, w_in_bf, w_sc, b_sc, wa_bf, b_rg_a, wx_bf, b_rg_x, lam):
    bsz, n, _ = ctx.shape
    const2 = lambda b: (0, 0)
    const3 = lambda b: (0, 0, 0)
    const4 = lambda b: (0, 0, 0, 0)
    return pl.pallas_call(
        _ctx_kernel,
        grid=(bsz,),
        in_specs=[
            pl.BlockSpec((None, n, D_MODEL), lambda b: (b, 0, 0)),
            pl.BlockSpec((6, D_MODEL), const2),
            pl.BlockSpec((1, D_MODEL), const2),
            pl.BlockSpec((D_MODEL, D_MODEL), lambda b: (0, 2)),
            pl.BlockSpec((2, SHORT_CONV, D_MODEL), const3),
            pl.BlockSpec((2, D_MODEL), const2),
            pl.BlockSpec((2, RNN_HEADS, RNN_BLOCK, RNN_BLOCK), const4),
            pl.BlockSpec((2, D_MODEL), const2),
            pl.BlockSpec((2, RNN_HEADS, RNN_BLOCK, RNN_BLOCK), const4),
            pl.BlockSpec((2, D_MODEL), const2),
            pl.BlockSpec((2, D_MODEL), const2),
        ],
        out_specs=pl.BlockSpec((None, 2, D_MODEL), lambda b: (b, 0, 0)),
        out_shape=jax.ShapeDtypeStruct((bsz, 2, D_MODEL), F32),
        scratch_shapes=[
            pltpu.VMEM((n + 2 * SUBLANES, D_MODEL), F32),
            pltpu.VMEM((n, D_MODEL), F32),
            pltpu.VMEM((n, D_MODEL), F32),
        ],
        name="ctx",
    )(ctx, mod_c, pre1_g, w_in_bf, w_sc, b_sc, wa_bf, b_rg_a, wx_bf, b_rg_x, lam)


def _mixer_fwd_kernel(x_ref, mod_ref, g_ref, win_ref, wdw_ref, bdw_ref, lng_ref, lnb_ref, wco_ref,
                      wsc_ref, bsc_ref, wa_ref, ba_ref, wx_ref, bx_ref, lam_ref, h0_ref,
                      gaya_ref, gb_ref, gbr_ref, urnn_ref, hf_ref,
                      upad_ref, ush_ref, wb_ref, cv_ref, uext_ref, a_ref, b_ref, carry_ref):
    j = pl.program_id(1)
    zeros8 = jnp.zeros((SUBLANES, D_MODEL), F32)

    @pl.when(j == 0)
    def _():
        carry_ref[...] = jnp.broadcast_to(h0_ref[0:1, :], (SUBLANES, D_MODEL))
        uext_ref[0:SUBLANES, :] = zeros8
        zeros_pad = jnp.zeros((PAD, D_MODEL), F32)
        for r in range(ROWS_PER_TILE + 1):
            upad_ref[r * ROW_STRIDE:r * ROW_STRIDE + PAD, :] = zeros_pad
        for k in range(CONV_WIDTH):
            wb_ref[k] = jnp.broadcast_to(wdw_ref[k:k + 1, :], (SUBLANES, D_MODEL))

    hx = (_rms_norm(x_ref[...], g_ref[...]) * (1.0 + mod_ref[1:2, :]) + mod_ref[0:1, :]).astype(BF16)

    u = _dot(hx, win_ref[:, 0:D_MODEL]) * _sigmoid(_dot(hx, win_ref[:, D_MODEL:2 * D_MODEL]))
    for r in range(ROWS_PER_TILE):
        upad_ref[PAD + r * ROW_STRIDE:PAD + r * ROW_STRIDE + GRID_W, :] = u[r * GRID_W:(r + 1) * GRID_W, :]
    vregs_per_row = GRID_W // SUBLANES
    for c in range(D_MODEL // CONV_LANES):
        cs = slice(c * CONV_LANES, (c + 1) * CONV_LANES)
        xpad = upad_ref[:, cs]
        for s in range(1, SUBLANES):
            ush_ref[s - 1] = pltpu.roll(xpad, UPAD_ROWS - s, 0)
        for r in range(ROWS_PER_TILE):
            acc = jnp.broadcast_to(bdw_ref[:, cs].reshape(1, 1, CONV_LANES), (vregs_per_row, SUBLANES, CONV_LANES))
            for k in range(CONV_WIDTH):
                q, s = divmod(r * ROW_STRIDE + PAD - CONV_HALF + k, SUBLANES)
                rows = slice(q * SUBLANES, q * SUBLANES + GRID_W)
                win = upad_ref[rows, cs] if s == 0 else ush_ref[s - 1, rows, :]
                acc = acc + wb_ref[k, :, cs] * win.reshape(vregs_per_row, SUBLANES, CONV_LANES)
            cv_ref[r * GRID_W:(r + 1) * GRID_W, cs] = acc.reshape(GRID_W, CONV_LANES)
    cv = cv_ref[...]
    cvc = cv - jnp.mean(cv, axis=-1, keepdims=True)
    cvn = cvc * lax.rsqrt(jnp.mean(cvc * cvc, axis=-1, keepdims=True) + EPS) * lng_ref[...] + lnb_ref[...]
    y_a = _dot(_silu(cvn).astype(BF16), wco_ref[...])

    g_a = _sigmoid(_dot(hx, win_ref[:, 4 * D_MODEL:5 * D_MODEL]))
    gaya_ref[...] = (g_a * y_a).astype(BF16)
    gb_ref[...] = _sigmoid(_dot(hx, win_ref[:, 5 * D_MODEL:6 * D_MODEL])).astype(BF16)
    gbr_ref[...] = _gelu_tanh(_dot(hx, win_ref[:, 3 * D_MODEL:4 * D_MODEL])).astype(BF16)

    ur = _dot(hx, win_ref[:, 2 * D_MODEL:3 * D_MODEL])
    urnn_ref[...] = ur.astype(BF16)
    uext_ref[SUBLANES:SUBLANES + TM, :] = ur
    ue = uext_ref[...]
    v = bsc_ref[...] + wsc_ref[SHORT_CONV - 1:SHORT_CONV, :] * ur
    for k in range(SHORT_CONV - 1):
        v = v + wsc_ref[k:k + 1, :] * pltpu.roll(ue, SHORT_CONV - 1 - k, 0)[SUBLANES:SUBLANES + TM, :]
    uext_ref[0:SUBLANES, :] = uext_ref[TM:TM + SUBLANES, :]
    _rglru_coeffs(v, wa_ref, ba_ref[...], wx_ref, bx_ref[...], _log_decay(lam_ref[...]), a_ref, b_ref)
    carry_ref[...] = _scan_tile(a_ref, b_ref, carry_ref[...], TM, reverse=False)
    hf_ref[...] = b_ref[...].astype(BF16)


def _resident(shape):
    nd = len(shape)
    return pl.BlockSpec(shape, lambda b, j: (0,) * nd, pipeline_mode=pl.Buffered(1))


def _mixer_fwd_call(x, mod_x, pre1_g, w_in_bf, w_dw, b_dw, ln_g, ln_b, wco_bf,
                    w_sc, b_sc, wa_bf, b_rg_a, wx_bf, b_rg_x, lam, h0):
    bsz, seq, _ = x.shape
    nt = seq // TM
    tile = pl.BlockSpec((None, TM, D_MODEL), lambda b, j: (b, j, 0))
    act = jax.ShapeDtypeStruct((bsz, seq, D_MODEL), BF16)
    head_w = pl.BlockSpec((None, RNN_HEADS, RNN_BLOCK, RNN_BLOCK), lambda b, j: (0, 0, 0, 0),
                          pipeline_mode=pl.Buffered(1))
    dir_row = pl.BlockSpec((None, 1, D_MODEL), lambda b, j: (0, 0, 0), pipeline_mode=pl.Buffered(1))
    return pl.pallas_call(
        _mixer_fwd_kernel,
        grid=(bsz, nt),
        in_specs=[
            tile,
            pl.BlockSpec((None, 6, D_MODEL), lambda b, j: (b, 0, 0)),
            _resident((1, D_MODEL)),
            _resident((D_MODEL, 6 * D_MODEL)),
            _resident((CONV_WIDTH, D_MODEL)),
            _resident((1, D_MODEL)),
            _resident((1, D_MODEL)),
            _resident((1, D_MODEL)),
            _resident((D_MODEL, D_MODEL)),
            pl.BlockSpec((None, SHORT_CONV, D_MODEL), lambda b, j: (0, 0, 0), pipeline_mode=pl.Buffered(1)),
            dir_row, head_w, dir_row, head_w, dir_row, dir_row,
            pl.BlockSpec((None, 2, D_MODEL), lambda b, j: (b, 0, 0)),
        ],
        out_specs=[tile] * 5,
        out_shape=[act] * 5,
        scratch_shapes=[
            pltpu.VMEM((UPAD_ROWS, D_MODEL), F32),
            pltpu.VMEM((SUBLANES - 1, UPAD_ROWS, CONV_LANES), F32),
            pltpu.VMEM((CONV_WIDTH, SUBLANES, D_MODEL), F32),
            pltpu.VMEM((TM, D_MODEL), F32),
            pltpu.VMEM((TM + SUBLANES, D_MODEL), F32),
            pltpu.VMEM((TM, D_MODEL), F32),
            pltpu.VMEM((TM, D_MODEL), F32),
            pltpu.VMEM((SUBLANES, D_MODEL), F32),
        ],
        compiler_params=pltpu.CompilerParams(
            dimension_semantics=("arbitrary", "arbitrary"), vmem_limit_bytes=VMEM_LIMIT),
        name="mixer_fwd",
    )(x, mod_x, pre1_g, w_in_bf, w_dw, b_dw, ln_g, ln_b, wco_bf,
      w_sc, b_sc.reshape(2, 1, D_MODEL), wa_bf, b_rg_a.reshape(2, 1, D_MODEL), wx_bf,
      b_rg_x.reshape(2, 1, D_MODEL), lam.reshape(2, 1, D_MODEL), h0)


def _route(logits_t, bias):
    t = logits_t.shape[1]
    scores = _sigmoid(logits_t)
    sel = scores + bias
    neg_inf = jnp.float32(-jnp.inf)

    sel3 = sel.reshape(N_GROUPS, GROUP_SIZE, t)
    within = lax.broadcasted_iota(jnp.int32, sel3.shape, 1)
    m1 = jnp.max(sel3, axis=1, keepdims=True)
    first = jnp.min(jnp.where(sel3 == m1, within, GROUP_SIZE), axis=1, keepdims=True)
    m2 = jnp.max(jnp.where(within == first, neg_inf, sel3), axis=1, keepdims=True)
    gscore = (m1 + m2).reshape(N_GROUPS, t)

    gidx = lax.broadcasted_iota(jnp.int32, gscore.shape, 0)
    rank = jnp.zeros(gscore.shape, jnp.int32)
    for g in range(N_GROUPS):
        other = gscore[g:g + 1, :]
        beats = jnp.where(other > gscore, 1, jnp.where((other == gscore) & (gidx > g), 1, 0))
        rank = rank + beats
    gkeep = (rank < TOPK_GROUPS).reshape(N_GROUPS, 1, t)
    masked = jnp.where(gkeep, sel3, neg_inf).reshape(N_EXPERTS, t)

    eidx = lax.broadcasted_iota(jnp.int32, masked.shape, 0)
    picks, weights = [], []
    for _ in range(TOP_K):
        m = jnp.max(masked, axis=0, keepdims=True)
        first = jnp.min(jnp.where(masked == m, eidx, N_EXPERTS), axis=0, keepdims=True)
        pick = eidx == first
        picks.append(first)
        weights.append(jnp.sum(jnp.where(pick, scores, 0.0), axis=0, keepdims=True))
        masked = jnp.where(pick, neg_inf, masked)
    idx = jnp.concatenate(picks, axis=0)
    w = jnp.concatenate(weights, axis=0)
    return idx, ROUTED_SCALE * w / jnp.sum(w, axis=0, keepdims=True)


def _mixer_bwd_kernel(urnn_ref, hf_ref, gbr_ref, gaya_ref, gb_ref, x_ref, mod_ref,
                      wsc_ref, bsc_ref, wa_ref, ba_ref, wx_ref, bx_ref, lam_ref, h0_ref,
                      wro_ref, wout_ref, post1_ref, pre2_ref, wrt_ref, rbias_ref,
                      x1_ref, hx2w_ref, idx_ref, w_ref,
                      uext_ref, a_ref, b_ref, carry_ref):
    j = pl.program_id(1)
    zeros8 = jnp.zeros((SUBLANES, D_MODEL), F32)

    @pl.when(j == 0)
    def _():
        carry_ref[...] = jnp.broadcast_to(h0_ref[1:2, :], (SUBLANES, D_MODEL))
        uext_ref[TM:TM + SUBLANES, :] = zeros8

    ur = urnn_ref[...].astype(F32)
    uext_ref[0:TM, :] = ur
    ue = uext_ref[...]
    v = bsc_ref[...] + wsc_ref[0:1, :] * ur
    for k in range(1, SHORT_CONV):
        v = v + wsc_ref[k:k + 1, :] * pltpu.roll(ue, TM + SUBLANES - k, 0)[0:TM, :]
    uext_ref[TM:TM + SUBLANES, :] = uext_ref[0:SUBLANES, :]
    _rglru_coeffs(v, wa_ref, ba_ref[...], wx_ref, bx_ref[...], _log_decay(lam_ref[...]), a_ref, b_ref)
    carry_ref[...] = _scan_tile(a_ref, b_ref, carry_ref[...], TM, reverse=True)

    h_sum = hf_ref[...].astype(F32) + b_ref[...]
    y_b = _dot((gbr_ref[...].astype(F32) * h_sum).astype(BF16), wro_ref[...])
    mix = gaya_ref[...].astype(F32) + gb_ref[...].astype(F32) * y_b
    out = _dot(mix.astype(BF16), wout_ref[...])
    x1 = x_ref[...] + mod_ref[2:3, :] * _rms_norm(out, post1_ref[...])
    x1_ref[...] = x1

    hx2 = _rms_norm(x1, pre2_ref[...]) * (1.0 + mod_ref[4:5, :]) + mod_ref[3:4, :]
    hx2w_ref[...] = _pack_bf16_pairs(hx2)
    logits_t = lax.dot_general(wrt_ref[...], hx2, (((1,), (1,)), ((), ())),
                               preferred_element_type=F32, precision=lax.Precision.HIGHEST)
    idx, w = _route(logits_t, rbias_ref[...])
    idx_ref[...] = idx
    w_ref[...] = w


def _mixer_bwd_call(urnn, hf, gbr, gaya, gb, x, mod_x, w_sc, b_sc, wa_bf, b_rg_a, wx_bf, b_rg_x, lam, h0,
                    wro_bf, wout_bf, post1_g, pre2_g, w_router_t, router_bias):
    bsz, seq, _ = x.shape
    nt = seq // TM
    rev = lambda b, j: (b, nt - 1 - j, 0)
    tile = pl.BlockSpec((None, TM, D_MODEL), rev)
    head_w = pl.BlockSpec((None, RNN_HEADS, RNN_BLOCK, RNN_BLOCK), lambda b, j: (1, 0, 0, 0),
                          pipeline_mode=pl.Buffered(1))
    dir_row = pl.BlockSpec((None, 1, D_MODEL), lambda b, j: (1, 0, 0), pipeline_mode=pl.Buffered(1))
    return pl.pallas_call(
        _mixer_bwd_kernel,
        grid=(bsz, nt),
        in_specs=[
            tile, tile, tile, tile, tile, tile,
            pl.BlockSpec((None, 6, D_MODEL), lambda b, j: (b, 0, 0)),
            pl.BlockSpec((None, SHORT_CONV, D_MODEL), lambda b, j: (1, 0, 0), pipeline_mode=pl.Buffered(1)),
            dir_row, head_w, dir_row, head_w, dir_row, dir_row,
            pl.BlockSpec((None, 2, D_MODEL), lambda b, j: (b, 0, 0)),
            _resident((D_MODEL, D_MODEL)),
            _resident((D_MODEL, D_MODEL)),
            _resident((1, D_MODEL)),
            _resident((1, D_MODEL)),
            _resident((N_EXPERTS, D_MODEL)),
            _resident((N_EXPERTS, 1)),
        ],
        out_specs=[
            tile,
            pl.BlockSpec((None, TM, ROW_WORDS), rev),
            pl.BlockSpec((TOP_K, TM), lambda b, j: (0, b * nt + nt - 1 - j)),
            pl.BlockSpec((TOP_K, TM), lambda b, j: (0, b * nt + nt - 1 - j)),
        ],
        out_shape=[
            jax.ShapeDtypeStruct((bsz, seq, D_MODEL), F32),
            jax.ShapeDtypeStruct((bsz, seq, ROW_WORDS), jnp.int32),
            jax.ShapeDtypeStruct((TOP_K, bsz * seq), jnp.int32),
            jax.ShapeDtypeStruct((TOP_K, bsz * seq), F32),
        ],
        scratch_shapes=[
            pltpu.VMEM((TM + SUBLANES, D_MODEL), F32),
            pltpu.VMEM((TM, D_MODEL), F32),
            pltpu.VMEM((TM, D_MODEL), F32),
            pltpu.VMEM((SUBLANES, D_MODEL), F32),
        ],
        compiler_params=pltpu.CompilerParams(
            dimension_semantics=("arbitrary", "arbitrary"), vmem_limit_bytes=VMEM_LIMIT),
        name="mixer_bwd",
    )(urnn, hf, gbr, gaya, gb, x, mod_x, w_sc, b_sc.reshape(2, 1, D_MODEL), wa_bf,
      b_rg_a.reshape(2, 1, D_MODEL), wx_bf, b_rg_x.reshape(2, 1, D_MODEL), lam.reshape(2, 1, D_MODEL), h0,
      wro_bf, wout_bf, post1_g, pre2_g, w_router_t, router_bias)


def _positions_kernel(idx_ref, pos_ref, te_ref, nused_ref):
    n = idx_ref.shape[1]
    n_tiles = n // POS_TILE
    eidx = lax.broadcasted_iota(jnp.int32, (N_EXPERTS, POS_TILE), 0)

    def chosen(t):
        idx = idx_ref[:, pl.ds(pl.multiple_of(t * POS_TILE, POS_TILE), POS_TILE)]
        ch = jnp.zeros((N_EXPERTS, POS_TILE), F32)
        for k in range(TOP_K):
            ch = ch + jnp.where(eidx == idx[k:k + 1, :], 1.0, 0.0)
        return idx, ch

    def count_body(t, cnt):
        return cnt + jnp.sum(chosen(t)[1], axis=1, keepdims=True)

    cnt = lax.fori_loop(0, n_tiles, count_body, jnp.zeros((N_EXPERTS, 1), F32))
    padded = jnp.ceil(cnt * (1.0 / ROW_TILE)) * ROW_TILE
    r = lax.broadcasted_iota(jnp.int32, (N_EXPERTS, N_EXPERTS), 0)
    c = lax.broadcasted_iota(jnp.int32, (N_EXPERTS, N_EXPERTS), 1)
    off = jnp.dot(jnp.where(c < r, 1.0, 0.0), jnp.broadcast_to(padded, (N_EXPERTS, LANES)),
                  preferred_element_type=F32, precision=lax.Precision.HIGHEST)[:, 0:1]
    end = off + padded

    n_map = te_ref.shape[1]
    tstart = lax.broadcasted_iota(jnp.int32, (N_EXPERTS, n_map), 1).astype(F32) * ROW_TILE
    te_ref[...] = jnp.minimum(jnp.sum(jnp.where(end <= tstart, 1, 0), axis=0, keepdims=True), N_EXPERTS - 1)
    total = jnp.sum(padded, axis=0, keepdims=True)
    nused_ref[...] = jnp.broadcast_to(total * (1.0 / ROW_TILE), nused_ref.shape).astype(jnp.int32)

    row = lax.broadcasted_iota(jnp.int32, (POS_TILE, POS_TILE), 0)
    col = lax.broadcasted_iota(jnp.int32, (POS_TILE, POS_TILE), 1)
    before = jnp.where(row < col, 1.0, 0.0).astype(BF16)

    def pos_body(t, carry):
        idx, ch = chosen(t)
        base = _dot(ch.astype(BF16), before) + (carry + off)
        rows = [jnp.sum(jnp.where(eidx == idx[k:k + 1, :], base, 0.0), axis=0, keepdims=True)
                for k in range(TOP_K)]
        pos_ref[:, pl.ds(pl.multiple_of(t * POS_TILE, POS_TILE), POS_TILE)] = (
            jnp.concatenate(rows, axis=0).astype(jnp.int32))
        return carry + jnp.sum(ch, axis=1, keepdims=True)

    lax.fori_loop(0, n_tiles, pos_body, jnp.zeros((N_EXPERTS, 1), F32))


def _positions_call(idx_t, n_row_tiles):
    n = idx_t.shape[1]
    n_map = -(-n_row_tiles // LANES) * LANES
    return pl.pallas_call(
        _positions_kernel,
        out_shape=[
            jax.ShapeDtypeStruct((TOP_K, n), jnp.int32),
            jax.ShapeDtypeStruct((1, n_map), jnp.int32),
            jax.ShapeDtypeStruct((1, LANES), jnp.int32),
        ],
        name="moe_positions",
    )(idx_t)


def _sc_mesh():
    return plsc.VectorSubcoreMesh(core_axis_name="c", subcore_axis_name="s",
                                  num_cores=V7X_SC_CORES, num_subcores=V7X_SC_SUBCORES)


def _sc_worker():
    return lax.axis_index("s") * V7X_SC_CORES + lax.axis_index("c")


def _dispatch_call(rows, pos, n_slots):
    n = rows.shape[0]
    tok_w = n // SC_WORKERS
    n_items = tok_w // SC_CHUNK

    def body(rows_hbm, pos_hbm, xs_hbm, idx_v, rows_v, lsem, ssem):
        wid = _sc_worker()
        pltpu.sync_copy(pos_hbm.at[wid], idx_v)
        base = wid * tok_w

        def load(i):
            b = i % SC_BUFFERS
            return pltpu.async_copy(rows_hbm.at[pl.ds(base + i * SC_CHUNK, SC_CHUNK)], rows_v.at[b], lsem.at[b])

        def scatter(i):
            b = i % SC_BUFFERS
            return [pltpu.async_copy(rows_v.at[b], xs_hbm.at[idx_v.at[i, k]], ssem.at[b]) for k in range(TOP_K)]

        loads = {i: load(i) for i in range(SC_BUFFERS - 1)}
        scat = {}
        for i in range(n_items):
            loads[i].wait()
            scat[i] = scatter(i)
            if i >= 1:
                for cp in scat[i - 1]:
                    cp.wait()
            if i + SC_BUFFERS - 1 < n_items:
                loads[i + SC_BUFFERS - 1] = load(i + SC_BUFFERS - 1)
        for cp in scat[n_items - 1]:
            cp.wait()

    return pl.kernel(
        body, mesh=_sc_mesh(),
        out_type=jax.ShapeDtypeStruct((n_slots, ROW_WORDS), jnp.int32),
        scratch_types=[pltpu.VMEM((n_items, TOP_K, SC_CHUNK), jnp.int32),
                       pltpu.VMEM((SC_BUFFERS, SC_CHUNK, ROW_WORDS), jnp.int32),
                       pltpu.SemaphoreType.DMA((SC_BUFFERS,)), pltpu.SemaphoreType.DMA((SC_BUFFERS,))],
        compiler_params=pltpu.CompilerParams(use_tc_tiling_on_sc=True),
        name="moe_dispatch",
    )(rows, pos)


def _collect_call(ys, pos, n):
    tok_w = n // SC_WORKERS
    n_chunks = tok_w // SC_CHUNK
    items = [(c, k) for c in range(n_chunks) for k in range(TOP_K)]

    def body(ys_hbm, pos_hbm, yt_hbm, idx_v, rows_v, gsem, wsem):
        wid = _sc_worker()
        pltpu.sync_copy(pos_hbm.at[wid], idx_v)
        base = wid * tok_w

        def gather(j):
            c, k = items[j]
            b = j % SC_BUFFERS
            return pltpu.async_copy(ys_hbm.at[idx_v.at[c, k]], rows_v.at[b], gsem.at[b])

        def write(j):
            c, k = items[j]
            b = j % SC_BUFFERS
            return pltpu.async_copy(rows_v.at[b], yt_hbm.at[k, pl.ds(base + c * SC_CHUNK, SC_CHUNK)], wsem.at[b])

        g = {j: gather(j) for j in range(SC_BUFFERS - 1)}
        w = {}
        for j in range(len(items)):
            g[j].wait()
            w[j] = write(j)
            if j >= 1:
                w[j - 1].wait()
            if j + SC_BUFFERS - 1 < len(items):
                g[j + SC_BUFFERS - 1] = gather(j + SC_BUFFERS - 1)
        w[len(items) - 1].wait()

    return pl.kernel(
        body, mesh=_sc_mesh(),
        out_type=jax.ShapeDtypeStruct((TOP_K, n, ROW_WORDS), jnp.int32),
        scratch_types=[pltpu.VMEM((n_chunks, TOP_K, SC_CHUNK), jnp.int32),
                       pltpu.VMEM((SC_BUFFERS, SC_CHUNK, ROW_WORDS), jnp.int32),
                       pltpu.SemaphoreType.DMA((SC_BUFFERS,)), pltpu.SemaphoreType.DMA((SC_BUFFERS,))],
        compiler_params=pltpu.CompilerParams(use_tc_tiling_on_sc=True),
        name="moe_collect",
    )(ys, pos)


def _expert_gemm_kernel(te_ref, nused_ref, xs_ref, wg_ref, wu_ref, wd_ref, ys_ref, wgu_scr, wd_scr):
    i = pl.program_id(0)
    e = te_ref[i]
    prev = te_ref[jnp.maximum(i - 1, 0)]

    @pl.when((i == 0) | (e != prev))
    def _():
        wgu_scr[:, 0:D_EXPERT] = wg_ref[...].astype(BF16)
        wgu_scr[:, D_EXPERT:2 * D_EXPERT] = wu_ref[...].astype(BF16)
        wd_scr[...] = wd_ref[...].astype(BF16)

    @pl.when(i < nused_ref[0])
    def _():
        lo, hi = _unpack_bf16_pairs(xs_ref[...])
        gu = _dot(lo.astype(BF16), wgu_scr[0:ROW_WORDS, :]) + _dot(hi.astype(BF16), wgu_scr[ROW_WORDS:D_MODEL, :])
        h = _silu(gu[:, 0:D_EXPERT]) * gu[:, D_EXPERT:2 * D_EXPERT]
        ys_ref[...] = _pack_bf16_pairs(_dot(h.astype(BF16), wd_scr[...]))


def _expert_gemm_call(te, nused, xs, w_e_gate, w_e_up, w_e_down):
    n_slots = xs.shape[0]
    n_row_tiles = n_slots // ROW_TILE
    rows = pl.BlockSpec((ROW_TILE, ROW_WORDS), lambda i, te, nused: (jnp.minimum(i, nused[0] - 1), 0))
    return pl.pallas_call(
        _expert_gemm_kernel,
        grid_spec=pltpu.PrefetchScalarGridSpec(
            num_scalar_prefetch=2,
            grid=(n_row_tiles,),
            in_specs=[
                rows,
                pl.BlockSpec((None, D_MODEL, D_EXPERT), lambda i, te, nused: (te[i], 0, 0)),
                pl.BlockSpec((None, D_MODEL, D_EXPERT), lambda i, te, nused: (te[i], 0, 0)),
                pl.BlockSpec((None, D_EXPERT, D_MODEL), lambda i, te, nused: (te[i], 0, 0)),
            ],
            out_specs=rows,
            scratch_shapes=[pltpu.VMEM((D_MODEL, 2 * D_EXPERT), BF16), pltpu.VMEM((D_EXPERT, D_MODEL), BF16)],
        ),
        out_shape=jax.ShapeDtypeStruct((n_slots, ROW_WORDS), jnp.int32),
        compiler_params=pltpu.CompilerParams(dimension_semantics=("arbitrary",), vmem_limit_bytes=VMEM_LIMIT),
        name="moe_experts",
    )(te, nused, xs, w_e_gate, w_e_up, w_e_down)


def _moe_out_kernel(yt_ref, w_ref, t_ref, x1_ref, mod_ref, post2_ref, wsg_ref, wsu_ref, wsd_ref, o_ref):
    lo, hi = _unpack_bf16_pairs(t_ref[...])
    lo = lo.astype(BF16)
    hi = hi.astype(BF16)
    g = _dot(lo, wsg_ref[0:ROW_WORDS, :]) + _dot(hi, wsg_ref[ROW_WORDS:D_MODEL, :])
    u = _dot(lo, wsu_ref[0:ROW_WORDS, :]) + _dot(hi, wsu_ref[ROW_WORDS:D_MODEL, :])
    shared = _dot((_silu(g) * u).astype(BF16), wsd_ref[...])
    acc_lo = shared[:, 0:ROW_WORDS]
    acc_hi = shared[:, ROW_WORDS:D_MODEL]
    for k in range(TOP_K):
        y_lo, y_hi = _unpack_bf16_pairs(yt_ref[k])
        wk = w_ref[:, k:k + 1]
        acc_lo = acc_lo + wk * y_lo
        acc_hi = acc_hi + wk * y_hi
    moe = jnp.concatenate([acc_lo, acc_hi], axis=-1)
    o_ref[...] = x1_ref[...] + mod_ref[5:6, :] * _rms_norm(moe, post2_ref[...])


def _moe_out_call(yt, w, t, x1, mod_x, post2_g, wsg_bf, wsu_bf, wsd_bf, seq):
    n = t.shape[0]
    tiles_per_seq = seq // TOUT
    const = lambda i: (0, 0)
    return pl.pallas_call(
        _moe_out_kernel,
        grid=(n // TOUT,),
        in_specs=[
            pl.BlockSpec((TOP_K, TOUT, ROW_WORDS), lambda i: (0, i, 0)),
            pl.BlockSpec((TOUT, TOP_K), lambda i: (i, 0)),
            pl.BlockSpec((TOUT, ROW_WORDS), lambda i: (i, 0)),
            pl.BlockSpec((TOUT, D_MODEL), lambda i: (i, 0)),
            pl.BlockSpec((None, 6, D_MODEL), lambda i: (i // tiles_per_seq, 0, 0)),
            pl.BlockSpec((1, D_MODEL), const),
            pl.BlockSpec((D_MODEL, D_EXPERT), const),
            pl.BlockSpec((D_MODEL, D_EXPERT), const),
            pl.BlockSpec((D_EXPERT, D_MODEL), const),
        ],
        out_specs=pl.BlockSpec((TOUT, D_MODEL), lambda i: (i, 0)),
        out_shape=jax.ShapeDtypeStruct((n, D_MODEL), F32),
        compiler_params=pltpu.CompilerParams(dimension_semantics=("arbitrary",), vmem_limit_bytes=VMEM_LIMIT),
        name="moe_out",
    )(yt, w, t, x1, mod_x, post2_g, wsg_bf, wsu_bf, wsd_bf)


def _moe(hx2w, idx_t, w_t, x1, mod_x, post2_g, w_e_gate, w_e_up, w_e_down, w_s_gate, w_s_up, w_s_down, seq):
    n = hx2w.shape[0]
    n_slots = n * TOP_K + N_EXPERTS * ROW_TILE
    n_row_tiles = n_slots // ROW_TILE
    pos_t, te, nused = _positions_call(idx_t, n_row_tiles)
    pos = pos_t.reshape(TOP_K, SC_WORKERS, n // (SC_WORKERS * SC_CHUNK), SC_CHUNK).transpose(1, 2, 0, 3)
    xs = _dispatch_call(hx2w, pos, n_slots)
    ys = _expert_gemm_call(te[0, :n_row_tiles], nused[0, :1], xs, w_e_gate, w_e_up, w_e_down)
    yt = _collect_call(ys, pos, n)
    return _moe_out_call(yt, w_t.T, hx2w, x1, mod_x, post2_g, w_s_gate.astype(BF16), w_s_up.astype(BF16),
                         w_s_down.astype(BF16), seq)


def kernel(x, c, ctx, c_ctx, w_mod, b_mod, pre1_g, post1_g, pre2_g, post2_g, w_in, w_dw, b_dw, ln_conv_g, ln_conv_b, w_conv_out, w_sc, b_sc, w_rg_a, b_rg_a, w_rg_x, b_rg_x, lru_lambda, w_rnn_out, w_out, w_router, router_bias, w_e_gate, w_e_up, w_e_down, w_s_gate, w_s_up, w_s_down):
    assert w_mod.shape[0] == 1, "single-layer block"
    bsz, seq, d = x.shape
    n = bsz * seq
    assert d == D_MODEL and seq % TM == 0 and seq % TOUT == 0 and bsz + 1 <= SUBLANES
    assert n % (SC_WORKERS * SC_CHUNK) == 0 and n % POS_TILE == 0

    cc = jnp.zeros((SUBLANES, D_MODEL), F32).at[:bsz].set(c).at[bsz].set(c_ctx)
    mod = _mod_call(cc, w_mod[0], b_mod)
    mod_x = mod[:bsz].reshape(bsz, 6, D_MODEL)
    mod_c = mod[bsz].reshape(6, D_MODEL)

    w_in_bf = w_in[0].astype(BF16)
    wa_bf = w_rg_a[0].astype(BF16)
    wx_bf = w_rg_x[0].astype(BF16)

    h0 = _ctx_call(ctx, mod_c, pre1_g, w_in_bf, w_sc[0], b_sc[0], wa_bf, b_rg_a[0], wx_bf, b_rg_x[0],
                   lru_lambda[0])

    gaya, gb, gbr, urnn, hf = _mixer_fwd_call(
        x, mod_x, pre1_g, w_in_bf, w_dw[0], b_dw, ln_conv_g, ln_conv_b, w_conv_out[0].astype(BF16),
        w_sc[0], b_sc[0], wa_bf, b_rg_a[0], wx_bf, b_rg_x[0], lru_lambda[0], h0)

    x1, hx2w, idx_t, w_t = _mixer_bwd_call(
        urnn, hf, gbr, gaya, gb, x, mod_x, w_sc[0], b_sc[0], wa_bf, b_rg_a[0], wx_bf, b_rg_x[0],
        lru_lambda[0], h0, w_rnn_out[0].astype(BF16), w_out[0].astype(BF16), post1_g, pre2_g,
        w_router[0].T, router_bias.reshape(N_EXPERTS, 1))

    out = _moe(hx2w.reshape(n, ROW_WORDS), idx_t, w_t, x1.reshape(n, D_MODEL), mod_x, post2_g,
               w_e_gate[0], w_e_up[0], w_e_down[0], w_s_gate[0], w_s_up[0], w_s_down[0], seq)
    return out.reshape(bsz, seq, D_MODEL)
```

```python
import jax
import jax.numpy as jnp
from jax import lax
from jax.experimental import pallas as pl
from jax.experimental.pallas import tpu as pltpu
from jax.experimental.pallas import tpu_sc as plsc

F32 = jnp.float32
BF16 = jnp.bfloat16

D_MODEL = 1024
GRID_W = 64
CONV_WIDTH = 31
CONV_HALF = CONV_WIDTH // 2
SHORT_CONV = 4
RNN_HEADS = 4
RNN_BLOCK = D_MODEL // RNN_HEADS
LRU_C = 8.0
N_EXPERTS = 64
N_GROUPS = 8
GROUP_SIZE = N_EXPERTS // N_GROUPS
TOPK_GROUPS = 4
TOP_K = 8
D_EXPERT = 256
ROUTED_SCALE = 2.5
EPS = 1e-6

SUBLANES = 8
TM = 512
ROWS_PER_TILE = TM // GRID_W
PAD = 16
ROW_STRIDE = GRID_W + PAD
UPAD_ROWS = ROWS_PER_TILE * ROW_STRIDE + PAD
CONV_LANES = 256
LANES = 128
ROW_WORDS = D_MODEL // 2
ROW_TILE = 512
POS_TILE = 512
TOUT = 256
TOKEN_GROUPS = 2
V7X_SC_CORES = 2
V7X_SC_SUBCORES = 16
SC_WORKERS = V7X_SC_CORES * V7X_SC_SUBCORES
SC_CHUNK = 64
SC_BUFFERS = 2
VMEM_LIMIT = 58 * 1024 * 1024


def _sigmoid(x):
    return 0.5 * (jnp.tanh(0.5 * x) + 1.0)


def _silu(x):
    return x * _sigmoid(x)


def _gelu_tanh(x):
    return 0.5 * x * (1.0 + jnp.tanh(0.7978845608028654 * (x + 0.044715 * (x * x * x))))


def _rms_norm(x, g):
    return x * lax.rsqrt(jnp.mean(x * x, axis=-1, keepdims=True) + EPS) * g


def _dot(a, b):
    return jnp.dot(a, b, preferred_element_type=F32)


def _pack_bf16_pairs(x):
    half = x.shape[-1] // 2
    lo = lax.bitcast_convert_type(x[:, :half].astype(BF16).astype(F32), jnp.uint32)
    hi = lax.bitcast_convert_type(x[:, half:].astype(BF16).astype(F32), jnp.uint32)
    return lax.bitcast_convert_type(hi | (lo >> 16), jnp.int32)


def _unpack_bf16_pairs(words):
    u = lax.bitcast_convert_type(words, jnp.uint32)
    return (lax.bitcast_convert_type(u << 16, F32),
            lax.bitcast_convert_type(u & jnp.uint32(0xFFFF0000), F32))


def _log_decay(lam):
    return LRU_C * (jnp.minimum(lam, 0.0) - jnp.log1p(jnp.exp(-jnp.abs(lam))))


def _rglru_coeffs(v, wa_ref, ba, wx_ref, bx, c_lam, a_ref, b_ref):
    vb = v.astype(BF16)
    for h in range(RNN_HEADS):
        cs = slice(h * RNN_BLOCK, (h + 1) * RNN_BLOCK)
        vh = vb[:, cs]
        r = _sigmoid(_dot(vh, wa_ref[h]) + ba[:, cs])
        i = _sigmoid(_dot(vh, wx_ref[h]) + bx[:, cs])
        log_a = c_lam[:, cs] * r
        a = jnp.exp(log_a)
        a_ref[:, cs] = a
        b_ref[:, cs] = jnp.sqrt(jnp.tanh(-log_a) * (1.0 + a * a)) * (i * v[:, cs])


def _scan_tile(a_ref, b_ref, carry, n_rows, reverse):
    row = lax.broadcasted_iota(jnp.int32, (SUBLANES, D_MODEL), 0)
    n_groups = n_rows // SUBLANES

    def body(j, carry):
        g = (n_groups - 1 - j) if reverse else j
        off = pl.multiple_of(g * SUBLANES, SUBLANES)
        a = a_ref[pl.ds(off, SUBLANES), :]
        b = b_ref[pl.ds(off, SUBLANES), :]
        for s in (1, 2, 4):
            keep = (row < SUBLANES - s) if reverse else (row >= s)
            shift = (SUBLANES - s) if reverse else s
            a_sh = jnp.where(keep, pltpu.roll(a, shift, 0), 1.0)
            b_sh = jnp.where(keep, pltpu.roll(b, shift, 0), 0.0)
            b = a * b_sh + b
            a = a * a_sh
        h = a * carry + b
        b_ref[pl.ds(off, SUBLANES), :] = h
        last = h[0:1, :] if reverse else h[SUBLANES - 1:SUBLANES, :]
        return jnp.broadcast_to(last, (SUBLANES, D_MODEL))

    return lax.fori_loop(0, n_groups, body, carry, unroll=2)


def _mod_kernel(c_ref, w_ref, b_ref, o_ref):
    o_ref[...] = jnp.dot(_silu(c_ref[...]), w_ref[...], preferred_element_type=F32,
                         precision=lax.Precision.HIGHEST) + b_ref[...]


def _mod_call(cc, w_mod, b_mod):
    n_mod = w_mod.shape[1]
    return pl.pallas_call(
        _mod_kernel,
        grid=(n_mod // D_MODEL,),
        in_specs=[
            pl.BlockSpec((SUBLANES, D_MODEL), lambda j: (0, 0)),
            pl.BlockSpec((D_MODEL, D_MODEL), lambda j: (0, j)),
            pl.BlockSpec((1, D_MODEL), lambda j: (0, j)),
        ],
        out_specs=pl.BlockSpec((SUBLANES, D_MODEL), lambda j: (0, j)),
        out_shape=jax.ShapeDtypeStruct((SUBLANES, n_mod), F32),
        name="mod",
    )(cc, w_mod, b_mod)


def _ctx_kernel(ctx_ref, mod_ref, g_ref, w_ref, wsc_ref, bsc_ref, wa_ref, ba_ref, wx_ref, bx_ref,
                lam_ref, o_ref, uext_ref, a_ref, b_ref):
    n = ctx_ref.shape[0]
    hc = _rms_norm(ctx_ref[...], g_ref[...]) * (1.0 + mod_ref[1:2, :]) + mod_ref[0:1, :]
    u = _dot(hc.astype(BF16), w_ref[...])
    zeros8 = jnp.zeros((SUBLANES, D_MODEL), F32)
    uext_ref[0:SUBLANES, :] = zeros8
    uext_ref[SUBLANES:SUBLANES + n, :] = u
    uext_ref[SUBLANES + n:2 * SUBLANES + n, :] = zeros8
    for d in range(2):
        v = jnp.broadcast_to(bsc_ref[d:d + 1, :], (n, D_MODEL))
        for k in range(SHORT_CONV):
            start = SUBLANES + k - (SHORT_CONV - 1) * (1 - d)
            v = v + wsc_ref[d, k:k + 1, :] * uext_ref[start:start + n, :]
        _rglru_coeffs(v, wa_ref.at[d], ba_ref[d:d + 1, :], wx_ref.at[d], bx_ref[d:d + 1, :],
                      _log_decay(lam_ref[d:d + 1, :]), a_ref, b_ref)
        final = _scan_tile(a_ref, b_ref, zeros8, n, reverse=(d == 1))
        o_ref[d:d + 1, :] = final[0:1, :]


def _ctx_call(ctx, mod_c, pre1_g, w_in_bf, w_sc, b_sc, wa_bf, b_rg_a, wx_bf, b_rg_x, lam):
    bsz, n, _ = ctx.shape
    const2 = lambda b: (0, 0)
    const3 = lambda b: (0, 0, 0)
    const4 = lambda b: (0, 0, 0, 0)
    return pl.pallas_call(
        _ctx_kernel,
        grid=(bsz,),
        in_specs=[
            pl.BlockSpec((None, n, D_MODEL), lambda b: (b, 0, 0)),
            pl.BlockSpec((6, D_MODEL), const2),
            pl.BlockSpec((1, D_MODEL), const2),
            pl.BlockSpec((D_MODEL, D_MODEL), lambda b: (0, 2)),
            pl.BlockSpec((2, SHORT_CONV, D_MODEL), const3),
            pl.BlockSpec((2, D_MODEL), const2),
            pl.BlockSpec((2, RNN_HEADS, RNN_BLOCK, RNN_BLOCK), const4),
            pl.BlockSpec((2, D_MODEL), const2),
            pl.BlockSpec((2, RNN_HEADS, RNN_BLOCK, RNN_BLOCK), const4),
            pl.BlockSpec((2, D_MODEL), const2),
            pl.BlockSpec((2, D_MODEL), const2),
        ],
        out_specs=pl.BlockSpec((None, 2, D_MODEL), lambda b: (b, 0, 0)),
        out_shape=jax.ShapeDtypeStruct((bsz, 2, D_MODEL), F32),
        scratch_shapes=[
            pltpu.VMEM((n + 2 * SUBLANES, D_MODEL), F32),
            pltpu.VMEM((n, D_MODEL), F32),
            pltpu.VMEM((n, D_MODEL), F32),
        ],
        name="ctx",
    )(ctx, mod_c, pre1_g, w_in_bf, w_sc, b_sc, wa_bf, b_rg_a, wx_bf, b_rg_x, lam)


def _mixer_fwd_kernel(x_ref, mod_ref, g_ref, win_ref, wdw_ref, bdw_ref, lng_ref, lnb_ref, wco_ref,
                      wsc_ref, bsc_ref, wa_ref, ba_ref, wx_ref, bx_ref, lam_ref, h0_ref,
                      gaya_ref, gb_ref, gbr_ref, urnn_ref, hf_ref,
                      upad_ref, ush_ref, wb_ref, cv_ref, uext_ref, a_ref, b_ref, carry_ref):
    j = pl.program_id(1)
    zeros8 = jnp.zeros((SUBLANES, D_MODEL), F32)

    @pl.when(j == 0)
    def _():
        carry_ref[...] = jnp.broadcast_to(h0_ref[0:1, :], (SUBLANES, D_MODEL))
        uext_ref[0:SUBLANES, :] = zeros8
        zeros_pad = jnp.zeros((PAD, D_MODEL), F32)
        for r in range(ROWS_PER_TILE + 1):
            upad_ref[r * ROW_STRIDE:r * ROW_STRIDE + PAD, :] = zeros_pad
        for k in range(CONV_WIDTH):
            wb_ref[k] = jnp.broadcast_to(wdw_ref[k:k + 1, :], (SUBLANES, D_MODEL))

    hx = (_rms_norm(x_ref[...], g_ref[...]) * (1.0 + mod_ref[1:2, :]) + mod_ref[0:1, :]).astype(BF16)

    u = _dot(hx, win_ref[:, 0:D_MODEL]) * _sigmoid(_dot(hx, win_ref[:, D_MODEL:2 * D_MODEL]))
    for r in range(ROWS_PER_TILE):
        upad_ref[PAD + r * ROW_STRIDE:PAD + r * ROW_STRIDE + GRID_W, :] = u[r * GRID_W:(r + 1) * GRID_W, :]
    vregs_per_row = GRID_W // SUBLANES
    g_a = ur = None
    for c in range(D_MODEL // CONV_LANES):
        if c == 0:
            gb_ref[...] = _sigmoid(_dot(hx, win_ref[:, 5 * D_MODEL:6 * D_MODEL])).astype(BF16)
        elif c == 1:
            gbr_ref[...] = _gelu_tanh(_dot(hx, win_ref[:, 3 * D_MODEL:4 * D_MODEL])).astype(BF16)
        elif c == 2:
            g_a = _sigmoid(_dot(hx, win_ref[:, 4 * D_MODEL:5 * D_MODEL]))
        else:
            ur = _dot(hx, win_ref[:, 2 * D_MODEL:3 * D_MODEL])
            urnn_ref[...] = ur.astype(BF16)
            uext_ref[SUBLANES:SUBLANES + TM, :] = ur
        cs = slice(c * CONV_LANES, (c + 1) * CONV_LANES)
        xpad = upad_ref[:, cs]
        for s in range(1, SUBLANES):
            ush_ref[s - 1] = pltpu.roll(xpad, UPAD_ROWS - s, 0)
        for r in range(ROWS_PER_TILE):
            acc = jnp.broadcast_to(bdw_ref[:, cs].reshape(1, 1, CONV_LANES), (vregs_per_row, SUBLANES, CONV_LANES))
            for k in range(CONV_WIDTH):
                q, s = divmod(r * ROW_STRIDE + PAD - CONV_HALF + k, SUBLANES)
                rows = slice(q * SUBLANES, q * SUBLANES + GRID_W)
                win = upad_ref[rows, cs] if s == 0 else ush_ref[s - 1, rows, :]
                acc = acc + wb_ref[k, :, cs] * win.reshape(vregs_per_row, SUBLANES, CONV_LANES)
            cv_ref[r * GRID_W:(r + 1) * GRID_W, cs] = acc.reshape(GRID_W, CONV_LANES)
    cv = cv_ref[...]
    cvc = cv - jnp.mean(cv, axis=-1, keepdims=True)
    cvn = cvc * lax.rsqrt(jnp.mean(cvc * cvc, axis=-1, keepdims=True) + EPS) * lng_ref[...] + lnb_ref[...]
    y_a = _dot(_silu(cvn).astype(BF16), wco_ref[...])

    gaya_ref[...] = (g_a * y_a).astype(BF16)

    ue = uext_ref[...]
    v = bsc_ref[...] + wsc_ref[SHORT_CONV - 1:SHORT_CONV, :] * ur
    for k in range(SHORT_CONV - 1):
        v = v + wsc_ref[k:k + 1, :] * pltpu.roll(ue, SHORT_CONV - 1 - k, 0)[SUBLANES:SUBLANES + TM, :]
    uext_ref[0:SUBLANES, :] = uext_ref[TM:TM + SUBLANES, :]
    _rglru_coeffs(v, wa_ref, ba_ref[...], wx_ref, bx_ref[...], _log_decay(lam_ref[...]), a_ref, b_ref)
    carry_ref[...] = _scan_tile(a_ref, b_ref, carry_ref[...], TM, reverse=False)
    hf_ref[...] = b_ref[...].astype(BF16)


def _resident(shape):
    nd = len(shape)
    return pl.BlockSpec(shape, lambda b, j: (0,) * nd, pipeline_mode=pl.Buffered(1))


def _mixer_fwd_call(x, mod_x, pre1_g, w_in_bf, w_dw, b_dw, ln_g, ln_b, wco_bf,
                    w_sc, b_sc, wa_bf, b_rg_a, wx_bf, b_rg_x, lam, h0):
    bsz, seq, _ = x.shape
    nt = seq // TM
    tile = pl.BlockSpec((None, TM, D_MODEL), lambda b, j: (b, j, 0))
    act = jax.ShapeDtypeStruct((bsz, seq, D_MODEL), BF16)
    head_w = pl.BlockSpec((None, RNN_HEADS, RNN_BLOCK, RNN_BLOCK), lambda b, j: (0, 0, 0, 0),
                          pipeline_mode=pl.Buffered(1))
    dir_row = pl.BlockSpec((None, 1, D_MODEL), lambda b, j: (0, 0, 0), pipeline_mode=pl.Buffered(1))
    return pl.pallas_call(
        _mixer_fwd_kernel,
        grid=(bsz, nt),
        in_specs=[
            tile,
            pl.BlockSpec((None, 6, D_MODEL), lambda b, j: (b, 0, 0)),
            _resident((1, D_MODEL)),
            _resident((D_MODEL, 6 * D_MODEL)),
            _resident((CONV_WIDTH, D_MODEL)),
            _resident((1, D_MODEL)),
            _resident((1, D_MODEL)),
            _resident((1, D_MODEL)),
            _resident((D_MODEL, D_MODEL)),
            pl.BlockSpec((None, SHORT_CONV, D_MODEL), lambda b, j: (0, 0, 0), pipeline_mode=pl.Buffered(1)),
            dir_row, head_w, dir_row, head_w, dir_row, dir_row,
            pl.BlockSpec((None, 2, D_MODEL), lambda b, j: (b, 0, 0)),
        ],
        out_specs=[tile] * 5,
        out_shape=[act] * 5,
        scratch_shapes=[
            pltpu.VMEM((UPAD_ROWS, D_MODEL), F32),
            pltpu.VMEM((SUBLANES - 1, UPAD_ROWS, CONV_LANES), F32),
            pltpu.VMEM((CONV_WIDTH, SUBLANES, D_MODEL), F32),
            pltpu.VMEM((TM, D_MODEL), F32),
            pltpu.VMEM((TM + SUBLANES, D_MODEL), F32),
            pltpu.VMEM((TM, D_MODEL), F32),
            pltpu.VMEM((TM, D_MODEL), F32),
            pltpu.VMEM((SUBLANES, D_MODEL), F32),
        ],
        compiler_params=pltpu.CompilerParams(
            dimension_semantics=("arbitrary", "arbitrary"), vmem_limit_bytes=VMEM_LIMIT),
        name="mixer_fwd",
    )(x, mod_x, pre1_g, w_in_bf, w_dw, b_dw, ln_g, ln_b, wco_bf,
      w_sc, b_sc.reshape(2, 1, D_MODEL), wa_bf, b_rg_a.reshape(2, 1, D_MODEL), wx_bf,
      b_rg_x.reshape(2, 1, D_MODEL), lam.reshape(2, 1, D_MODEL), h0)


def _route(logits_t, bias):
    t = logits_t.shape[1]
    scores = _sigmoid(logits_t)
    sel = scores + bias
    neg_inf = jnp.float32(-jnp.inf)

    sel3 = sel.reshape(N_GROUPS, GROUP_SIZE, t)
    within = lax.broadcasted_iota(jnp.int32, sel3.shape, 1)
    m1 = jnp.max(sel3, axis=1, keepdims=True)
    first = jnp.min(jnp.where(sel3 == m1, within, GROUP_SIZE), axis=1, keepdims=True)
    m2 = jnp.max(jnp.where(within == first, neg_inf, sel3), axis=1, keepdims=True)
    gscore = (m1 + m2).reshape(N_GROUPS, t)

    gidx = lax.broadcasted_iota(jnp.int32, gscore.shape, 0)
    rank = jnp.zeros(gscore.shape, jnp.int32)
    for g in range(N_GROUPS):
        other = gscore[g:g + 1, :]
        beats = jnp.where(other > gscore, 1, jnp.where((other == gscore) & (gidx > g), 1, 0))
        rank = rank + beats
    gkeep = (rank < TOPK_GROUPS).reshape(N_GROUPS, 1, t)
    masked = jnp.where(gkeep, sel3, neg_inf).reshape(N_EXPERTS, t)

    eidx = lax.broadcasted_iota(jnp.int32, masked.shape, 0)
    picks, weights = [], []
    for _ in range(TOP_K):
        m = jnp.max(masked, axis=0, keepdims=True)
        first = jnp.min(jnp.where(masked == m, eidx, N_EXPERTS), axis=0, keepdims=True)
        pick = eidx == first
        picks.append(first)
        weights.append(jnp.sum(jnp.where(pick, scores, 0.0), axis=0, keepdims=True))
        masked = jnp.where(pick, neg_inf, masked)
    idx = jnp.concatenate(picks, axis=0)
    w = jnp.concatenate(weights, axis=0)
    return idx, ROUTED_SCALE * w / jnp.sum(w, axis=0, keepdims=True)


def _mixer_bwd_kernel(urnn_ref, hf_ref, gbr_ref, gaya_ref, gb_ref, x_ref, mod_ref,
                      wsc_ref, bsc_ref, wa_ref, ba_ref, wx_ref, bx_ref, lam_ref, h0_ref,
                      wro_ref, wout_ref, post1_ref, pre2_ref, wrt_ref, rbias_ref,
                      x1_ref, hx2w_ref, idx_ref, w_ref,
                      uext_ref, a_ref, b_ref, carry_ref):
    j = pl.program_id(1)
    zeros8 = jnp.zeros((SUBLANES, D_MODEL), F32)

    @pl.when(j == 0)
    def _():
        carry_ref[...] = jnp.broadcast_to(h0_ref[1:2, :], (SUBLANES, D_MODEL))
        uext_ref[TM:TM + SUBLANES, :] = zeros8

    ur = urnn_ref[...].astype(F32)
    uext_ref[0:TM, :] = ur
    ue = uext_ref[...]
    v = bsc_ref[...] + wsc_ref[0:1, :] * ur
    for k in range(1, SHORT_CONV):
        v = v + wsc_ref[k:k + 1, :] * pltpu.roll(ue, TM + SUBLANES - k, 0)[0:TM, :]
    uext_ref[TM:TM + SUBLANES, :] = uext_ref[0:SUBLANES, :]
    _rglru_coeffs(v, wa_ref, ba_ref[...], wx_ref, bx_ref[...], _log_decay(lam_ref[...]), a_ref, b_ref)
    carry_ref[...] = _scan_tile(a_ref, b_ref, carry_ref[...], TM, reverse=True)

    h_sum = hf_ref[...].astype(F32) + b_ref[...]
    y_b = _dot((gbr_ref[...].astype(F32) * h_sum).astype(BF16), wro_ref[...])
    mix = gaya_ref[...].astype(F32) + gb_ref[...].astype(F32) * y_b
    out = _dot(mix.astype(BF16), wout_ref[...])
    x1 = x_ref[...] + mod_ref[2:3, :] * _rms_norm(out, post1_ref[...])
    x1_ref[...] = x1

    hx2 = _rms_norm(x1, pre2_ref[...]) * (1.0 + mod_ref[4:5, :]) + mod_ref[3:4, :]
    hx2w_ref[...] = _pack_bf16_pairs(hx2)
    logits_t = lax.dot_general(wrt_ref[...], hx2, (((1,), (1,)), ((), ())),
                               preferred_element_type=F32, precision=lax.Precision.HIGHEST)
    idx, w = _route(logits_t, rbias_ref[...])
    idx_ref[...] = idx
    w_ref[...] = w


def _mixer_bwd_call(b0, bsz, urnn, hf, gbr, gaya, gb, x, mod_x, w_sc, b_sc, wa_bf, b_rg_a, wx_bf, b_rg_x, lam, h0,
                    wro_bf, wout_bf, post1_g, pre2_g, w_router_t, router_bias):
    seq = x.shape[1]
    nt = seq // TM
    rev = lambda b, j: (b, nt - 1 - j, 0)
    tile = pl.BlockSpec((None, TM, D_MODEL), lambda b, j: (b0 + b, nt - 1 - j, 0))
    head_w = pl.BlockSpec((None, RNN_HEADS, RNN_BLOCK, RNN_BLOCK), lambda b, j: (1, 0, 0, 0),
                          pipeline_mode=pl.Buffered(1))
    dir_row = pl.BlockSpec((None, 1, D_MODEL), lambda b, j: (1, 0, 0), pipeline_mode=pl.Buffered(1))
    return pl.pallas_call(
        _mixer_bwd_kernel,
        grid=(bsz, nt),
        in_specs=[
            tile, tile, tile, tile, tile, tile,
            pl.BlockSpec((None, 6, D_MODEL), lambda b, j: (b0 + b, 0, 0)),
            pl.BlockSpec((None, SHORT_CONV, D_MODEL), lambda b, j: (1, 0, 0), pipeline_mode=pl.Buffered(1)),
            dir_row, head_w, dir_row, head_w, dir_row, dir_row,
            pl.BlockSpec((None, 2, D_MODEL), lambda b, j: (b0 + b, 0, 0)),
            _resident((D_MODEL, D_MODEL)),
            _resident((D_MODEL, D_MODEL)),
            _resident((1, D_MODEL)),
            _resident((1, D_MODEL)),
            _resident((N_EXPERTS, D_MODEL)),
            _resident((N_EXPERTS, 1)),
        ],
        out_specs=[
            pl.BlockSpec((None, TM, D_MODEL), rev),
            pl.BlockSpec((None, TM, ROW_WORDS), rev),
            pl.BlockSpec((TOP_K, TM), lambda b, j: (0, b * nt + nt - 1 - j)),
            pl.BlockSpec((TOP_K, TM), lambda b, j: (0, b * nt + nt - 1 - j)),
        ],
        out_shape=[
            jax.ShapeDtypeStruct((bsz, seq, D_MODEL), F32),
            jax.ShapeDtypeStruct((bsz, seq, ROW_WORDS), jnp.int32),
            jax.ShapeDtypeStruct((TOP_K, bsz * seq), jnp.int32),
            jax.ShapeDtypeStruct((TOP_K, bsz * seq), F32),
        ],
        scratch_shapes=[
            pltpu.VMEM((TM + SUBLANES, D_MODEL), F32),
            pltpu.VMEM((TM, D_MODEL), F32),
            pltpu.VMEM((TM, D_MODEL), F32),
            pltpu.VMEM((SUBLANES, D_MODEL), F32),
        ],
        compiler_params=pltpu.CompilerParams(
            dimension_semantics=("arbitrary", "arbitrary"), vmem_limit_bytes=VMEM_LIMIT),
        name="mixer_bwd",
    )(urnn, hf, gbr, gaya, gb, x, mod_x, w_sc, b_sc.reshape(2, 1, D_MODEL), wa_bf,
      b_rg_a.reshape(2, 1, D_MODEL), wx_bf, b_rg_x.reshape(2, 1, D_MODEL), lam.reshape(2, 1, D_MODEL), h0,
      wro_bf, wout_bf, post1_g, pre2_g, w_router_t, router_bias)


def _positions_kernel(idx_ref, pos_ref, te_ref, nused_ref):
    n = idx_ref.shape[1]
    n_tiles = n // POS_TILE
    eidx = lax.broadcasted_iota(jnp.int32, (N_EXPERTS, POS_TILE), 0)

    def chosen(t):
        idx = idx_ref[:, pl.ds(pl.multiple_of(t * POS_TILE, POS_TILE), POS_TILE)]
        ch = jnp.zeros((N_EXPERTS, POS_TILE), F32)
        for k in range(TOP_K):
            ch = ch + jnp.where(eidx == idx[k:k + 1, :], 1.0, 0.0)
        return idx, ch

    def count_body(t, cnt):
        return cnt + jnp.sum(chosen(t)[1], axis=1, keepdims=True)

    cnt = lax.fori_loop(0, n_tiles, count_body, jnp.zeros((N_EXPERTS, 1), F32))
    padded = jnp.ceil(cnt * (1.0 / ROW_TILE)) * ROW_TILE
    r = lax.broadcasted_iota(jnp.int32, (N_EXPERTS, N_EXPERTS), 0)
    c = lax.broadcasted_iota(jnp.int32, (N_EXPERTS, N_EXPERTS), 1)
    off = jnp.dot(jnp.where(c < r, 1.0, 0.0), jnp.broadcast_to(padded, (N_EXPERTS, LANES)),
                  preferred_element_type=F32, precision=lax.Precision.HIGHEST)[:, 0:1]
    end = off + padded

    n_map = te_ref.shape[1]
    tstart = lax.broadcasted_iota(jnp.int32, (N_EXPERTS, n_map), 1).astype(F32) * ROW_TILE
    te_ref[...] = jnp.minimum(jnp.sum(jnp.where(end <= tstart, 1, 0), axis=0, keepdims=True), N_EXPERTS - 1)
    total = jnp.sum(padded, axis=0, keepdims=True)
    nused_ref[...] = jnp.broadcast_to(total * (1.0 / ROW_TILE), nused_ref.shape).astype(jnp.int32)

    row = lax.broadcasted_iota(jnp.int32, (POS_TILE, POS_TILE), 0)
    col = lax.broadcasted_iota(jnp.int32, (POS_TILE, POS_TILE), 1)
    before = jnp.where(row < col, 1.0, 0.0).astype(BF16)

    def pos_body(t, carry):
        idx, ch = chosen(t)
        base = _dot(ch.astype(BF16), before) + (carry + off)
        rows = [jnp.sum(jnp.where(eidx == idx[k:k + 1, :], base, 0.0), axis=0, keepdims=True)
                for k in range(TOP_K)]
        pos_ref[:, pl.ds(pl.multiple_of(t * POS_TILE, POS_TILE), POS_TILE)] = (
            jnp.concatenate(rows, axis=0).astype(jnp.int32))
        return carry + jnp.sum(ch, axis=1, keepdims=True)

    lax.fori_loop(0, n_tiles, pos_body, jnp.zeros((N_EXPERTS, 1), F32))


def _positions_call(idx_t, n_row_tiles):
    n = idx_t.shape[1]
    n_map = -(-n_row_tiles // LANES) * LANES
    return pl.pallas_call(
        _positions_kernel,
        out_shape=[
            jax.ShapeDtypeStruct((TOP_K, n), jnp.int32),
            jax.ShapeDtypeStruct((1, n_map), jnp.int32),
            jax.ShapeDtypeStruct((1, LANES), jnp.int32),
        ],
        name="moe_positions",
    )(idx_t)


def _sc_mesh():
    return plsc.VectorSubcoreMesh(core_axis_name="c", subcore_axis_name="s",
                                  num_cores=V7X_SC_CORES, num_subcores=V7X_SC_SUBCORES)


def _sc_worker():
    return lax.axis_index("s") * V7X_SC_CORES + lax.axis_index("c")


def _dispatch_call(rows, pos, n_slots):
    n = rows.shape[0]
    tok_w = n // SC_WORKERS
    n_items = tok_w // SC_CHUNK

    def body(rows_hbm, pos_hbm, xs_hbm, idx_v, rows_v, lsem, ssem):
        wid = _sc_worker()
        pltpu.sync_copy(pos_hbm.at[wid], idx_v)
        base = wid * tok_w

        def load(i):
            b = i % SC_BUFFERS
            return pltpu.async_copy(rows_hbm.at[pl.ds(base + i * SC_CHUNK, SC_CHUNK)], rows_v.at[b], lsem.at[b])

        def scatter(i):
            b = i % SC_BUFFERS
            return [pltpu.async_copy(rows_v.at[b], xs_hbm.at[idx_v.at[i, k]], ssem.at[b]) for k in range(TOP_K)]

        loads = {i: load(i) for i in range(SC_BUFFERS - 1)}
        scat = {}
        for i in range(n_items):
            loads[i].wait()
            scat[i] = scatter(i)
            if i >= 1:
                for cp in scat[i - 1]:
                    cp.wait()
            if i + SC_BUFFERS - 1 < n_items:
                loads[i + SC_BUFFERS - 1] = load(i + SC_BUFFERS - 1)
        for cp in scat[n_items - 1]:
            cp.wait()

    return pl.kernel(
        body, mesh=_sc_mesh(),
        out_type=jax.ShapeDtypeStruct((n_slots, ROW_WORDS), jnp.int32),
        scratch_types=[pltpu.VMEM((n_items, TOP_K, SC_CHUNK), jnp.int32),
                       pltpu.VMEM((SC_BUFFERS, SC_CHUNK, ROW_WORDS), jnp.int32),
                       pltpu.SemaphoreType.DMA((SC_BUFFERS,)), pltpu.SemaphoreType.DMA((SC_BUFFERS,))],
        compiler_params=pltpu.CompilerParams(use_tc_tiling_on_sc=True),
        name="moe_dispatch",
    )(rows, pos)


def _collect_call(ys, pos, n):
    tok_w = n // SC_WORKERS
    n_chunks = tok_w // SC_CHUNK
    items = [(c, k) for c in range(n_chunks) for k in range(TOP_K)]

    def body(ys_hbm, pos_hbm, yt_hbm, idx_v, rows_v, gsem, wsem):
        wid = _sc_worker()
        pltpu.sync_copy(pos_hbm.at[wid], idx_v)
        base = wid * tok_w

        def gather(j):
            c, k = items[j]
            b = j % SC_BUFFERS
            return pltpu.async_copy(ys_hbm.at[idx_v.at[c, k]], rows_v.at[b], gsem.at[b])

        def write(j):
            c, k = items[j]
            b = j % SC_BUFFERS
            return pltpu.async_copy(rows_v.at[b], yt_hbm.at[k, pl.ds(base + c * SC_CHUNK, SC_CHUNK)], wsem.at[b])

        g = {j: gather(j) for j in range(SC_BUFFERS - 1)}
        w = {}
        for j in range(len(items)):
            g[j].wait()
            w[j] = write(j)
            if j >= 1:
                w[j - 1].wait()
            if j + SC_BUFFERS - 1 < len(items):
                g[j + SC_BUFFERS - 1] = gather(j + SC_BUFFERS - 1)
        w[len(items) - 1].wait()

    return pl.kernel(
        body, mesh=_sc_mesh(),
        out_type=jax.ShapeDtypeStruct((TOP_K, n, ROW_WORDS), jnp.int32),
        scratch_types=[pltpu.VMEM((n_chunks, TOP_K, SC_CHUNK), jnp.int32),
                       pltpu.VMEM((SC_BUFFERS, SC_CHUNK, ROW_WORDS), jnp.int32),
                       pltpu.SemaphoreType.DMA((SC_BUFFERS,)), pltpu.SemaphoreType.DMA((SC_BUFFERS,))],
        compiler_params=pltpu.CompilerParams(use_tc_tiling_on_sc=True),
        name="moe_collect",
    )(ys, pos)


def _expert_gemm_kernel(te_ref, nused_ref, xs_ref, wg_ref, wu_ref, wd_ref, ys_ref, wgu_scr, wd_scr):
    i = pl.program_id(0)
    e = te_ref[i]
    prev = te_ref[jnp.maximum(i - 1, 0)]

    @pl.when((i == 0) | (e != prev))
    def _():
        wgu_scr[:, 0:D_EXPERT] = wg_ref[...].astype(BF16)
        wgu_scr[:, D_EXPERT:2 * D_EXPERT] = wu_ref[...].astype(BF16)
        wd_scr[...] = wd_ref[...].astype(BF16)

    @pl.when(i < nused_ref[0])
    def _():
        lo, hi = _unpack_bf16_pairs(xs_ref[...])
        gu = _dot(lo.astype(BF16), wgu_scr[0:ROW_WORDS, :]) + _dot(hi.astype(BF16), wgu_scr[ROW_WORDS:D_MODEL, :])
        h = _silu(gu[:, 0:D_EXPERT]) * gu[:, D_EXPERT:2 * D_EXPERT]
        ys_ref[...] = _pack_bf16_pairs(_dot(h.astype(BF16), wd_scr[...]))


def _expert_gemm_call(te, nused, xs, w_e_gate, w_e_up, w_e_down):
    n_slots = xs.shape[0]
    n_row_tiles = n_slots // ROW_TILE
    rows = pl.BlockSpec((ROW_TILE, ROW_WORDS), lambda i, te, nused: (jnp.minimum(i, nused[0] - 1), 0))
    return pl.pallas_call(
        _expert_gemm_kernel,
        grid_spec=pltpu.PrefetchScalarGridSpec(
            num_scalar_prefetch=2,
            grid=(n_row_tiles,),
            in_specs=[
                rows,
                pl.BlockSpec((None, D_MODEL, D_EXPERT), lambda i, te, nused: (te[i], 0, 0)),
                pl.BlockSpec((None, D_MODEL, D_EXPERT), lambda i, te, nused: (te[i], 0, 0)),
                pl.BlockSpec((None, D_EXPERT, D_MODEL), lambda i, te, nused: (te[i], 0, 0)),
            ],
            out_specs=rows,
            scratch_shapes=[pltpu.VMEM((D_MODEL, 2 * D_EXPERT), BF16), pltpu.VMEM((D_EXPERT, D_MODEL), BF16)],
        ),
        out_shape=jax.ShapeDtypeStruct((n_slots, ROW_WORDS), jnp.int32),
        compiler_params=pltpu.CompilerParams(dimension_semantics=("arbitrary",), vmem_limit_bytes=VMEM_LIMIT),
        name="moe_experts",
    )(te, nused, xs, w_e_gate, w_e_up, w_e_down)


def _moe_out_kernel(yt_ref, w_ref, t_ref, x1_ref, mod_ref, post2_ref, wsg_ref, wsu_ref, wsd_ref, o_ref):
    lo, hi = _unpack_bf16_pairs(t_ref[...])
    lo = lo.astype(BF16)
    hi = hi.astype(BF16)
    g = _dot(lo, wsg_ref[0:ROW_WORDS, :]) + _dot(hi, wsg_ref[ROW_WORDS:D_MODEL, :])
    u = _dot(lo, wsu_ref[0:ROW_WORDS, :]) + _dot(hi, wsu_ref[ROW_WORDS:D_MODEL, :])
    shared = _dot((_silu(g) * u).astype(BF16), wsd_ref[...])
    acc_lo = shared[:, 0:ROW_WORDS]
    acc_hi = shared[:, ROW_WORDS:D_MODEL]
    for k in range(TOP_K):
        y_lo, y_hi = _unpack_bf16_pairs(yt_ref[k])
        wk = w_ref[:, k:k + 1]
        acc_lo = acc_lo + wk * y_lo
        acc_hi = acc_hi + wk * y_hi
    moe = jnp.concatenate([acc_lo, acc_hi], axis=-1)
    o_ref[...] = x1_ref[...] + mod_ref[5:6, :] * _rms_norm(moe, post2_ref[...])


def _moe_out_into_kernel(prev_ref, *refs):
    del prev_ref
    _moe_out_kernel(*refs)


def _moe_out_call(prev_out, n_total, b0, yt, w, t, x1, mod_x, post2_g, wsg_bf, wsu_bf, wsd_bf, seq):
    n = t.shape[0]
    tiles_per_seq = seq // TOUT
    first_tile = b0 * tiles_per_seq
    const = lambda i: (0, 0)
    in_specs = [
        pl.BlockSpec((TOP_K, TOUT, ROW_WORDS), lambda i: (0, i, 0)),
        pl.BlockSpec((TOUT, TOP_K), lambda i: (i, 0)),
        pl.BlockSpec((TOUT, ROW_WORDS), lambda i: (i, 0)),
        pl.BlockSpec((TOUT, D_MODEL), lambda i: (i, 0)),
        pl.BlockSpec((None, 6, D_MODEL), lambda i: (b0 + i // tiles_per_seq, 0, 0)),
        pl.BlockSpec((1, D_MODEL), const),
        pl.BlockSpec((D_MODEL, D_EXPERT), const),
        pl.BlockSpec((D_MODEL, D_EXPERT), const),
        pl.BlockSpec((D_EXPERT, D_MODEL), const),
    ]
    args = (yt, w, t, x1, mod_x, post2_g, wsg_bf, wsu_bf, wsd_bf)
    aliased = prev_out is not None
    return pl.pallas_call(
        _moe_out_into_kernel if aliased else _moe_out_kernel,
        grid=(n // TOUT,),
        in_specs=([pl.BlockSpec(memory_space=pl.ANY)] if aliased else []) + in_specs,
        out_specs=pl.BlockSpec((TOUT, D_MODEL), lambda i: (first_tile + i, 0)),
        out_shape=jax.ShapeDtypeStruct((n_total, D_MODEL), F32),
        input_output_aliases={0: 0} if aliased else {},
        compiler_params=pltpu.CompilerParams(dimension_semantics=("arbitrary",), vmem_limit_bytes=VMEM_LIMIT),
        name="moe_out",
    )(*(((prev_out,) if aliased else ()) + args))


def _moe(prev_out, n_total, b0, hx2w, idx_t, w_t, x1, mod_x, post2_g, w_e_gate, w_e_up, w_e_down,
         wsg_bf, wsu_bf, wsd_bf, seq):
    n = hx2w.shape[0]
    n_slots = n * TOP_K + N_EXPERTS * ROW_TILE
    n_row_tiles = n_slots // ROW_TILE
    pos_t, te, nused = _positions_call(idx_t, n_row_tiles)
    pos = pos_t.reshape(TOP_K, SC_WORKERS, n // (SC_WORKERS * SC_CHUNK), SC_CHUNK).transpose(1, 2, 0, 3)
    xs = _dispatch_call(hx2w, pos, n_slots)
    ys = _expert_gemm_call(te[0, :n_row_tiles], nused[0, :1], xs, w_e_gate, w_e_up, w_e_down)
    yt = _collect_call(ys, pos, n)
    return _moe_out_call(prev_out, n_total, b0, yt, w_t.T, hx2w, x1, mod_x, post2_g, wsg_bf, wsu_bf, wsd_bf, seq)


def kernel(x, c, ctx, c_ctx, w_mod, b_mod, pre1_g, post1_g, pre2_g, post2_g, w_in, w_dw, b_dw, ln_conv_g, ln_conv_b, w_conv_out, w_sc, b_sc, w_rg_a, b_rg_a, w_rg_x, b_rg_x, lru_lambda, w_rnn_out, w_out, w_router, router_bias, w_e_gate, w_e_up, w_e_down, w_s_gate, w_s_up, w_s_down):
    assert w_mod.shape[0] == 1, "single-layer block"
    bsz, seq, d = x.shape
    n = bsz * seq
    assert d == D_MODEL and seq % TM == 0 and seq % TOUT == 0 and bsz + 1 <= SUBLANES
    assert bsz % TOKEN_GROUPS == 0 and (n // TOKEN_GROUPS) % (SC_WORKERS * SC_CHUNK) == 0
    assert (n // TOKEN_GROUPS) % POS_TILE == 0

    cc = jnp.zeros((SUBLANES, D_MODEL), F32).at[:bsz].set(c).at[bsz].set(c_ctx)
    mod = _mod_call(cc, w_mod[0], b_mod)
    mod_x = mod[:bsz].reshape(bsz, 6, D_MODEL)
    mod_c = mod[bsz].reshape(6, D_MODEL)

    w_in_bf = w_in[0].astype(BF16)
    wa_bf = w_rg_a[0].astype(BF16)
    wx_bf = w_rg_x[0].astype(BF16)

    h0 = _ctx_call(ctx, mod_c, pre1_g, w_in_bf, w_sc[0], b_sc[0], wa_bf, b_rg_a[0], wx_bf, b_rg_x[0],
                   lru_lambda[0])

    gaya, gb, gbr, urnn, hf = _mixer_fwd_call(
        x, mod_x, pre1_g, w_in_bf, w_dw[0], b_dw, ln_conv_g, ln_conv_b, w_conv_out[0].astype(BF16),
        w_sc[0], b_sc[0], wa_bf, b_rg_a[0], wx_bf, b_rg_x[0], lru_lambda[0], h0)

    wro_bf, wout_bf = w_rnn_out[0].astype(BF16), w_out[0].astype(BF16)
    wsg_bf, wsu_bf, wsd_bf = w_s_gate[0].astype(BF16), w_s_up[0].astype(BF16), w_s_down[0].astype(BF16)
    w_router_t, rbias = w_router[0].T, router_bias.reshape(N_EXPERTS, 1)
    gsz = bsz // TOKEN_GROUPS
    gn = gsz * seq
    out = None
    for g in range(TOKEN_GROUPS):
        x1, hx2w, idx_t, w_t = _mixer_bwd_call(
            g * gsz, gsz, urnn, hf, gbr, gaya, gb, x, mod_x, w_sc[0], b_sc[0], wa_bf, b_rg_a[0], wx_bf,
            b_rg_x[0], lru_lambda[0], h0, wro_bf, wout_bf, post1_g, pre2_g, w_router_t, rbias)
        out = _moe(out, n, g * gsz, hx2w.reshape(gn, ROW_WORDS), idx_t, w_t, x1.reshape(gn, D_MODEL), mod_x,
                   post2_g, w_e_gate[0], w_e_up[0], w_e_down[0], wsg_bf, wsu_bf, wsd_bf, seq)
    return out.reshape(bsz, seq, D_MODEL)
```

```python
import jax
import jax.numpy as jnp
from jax import lax
from jax.experimental import pallas as pl
from jax.experimental.pallas import tpu as pltpu
from jax.experimental.pallas import tpu_sc as plsc

F32 = jnp.float32
BF16 = jnp.bfloat16

D_MODEL = 1024
GRID_W = 64
CONV_WIDTH = 31
CONV_HALF = CONV_WIDTH // 2
SHORT_CONV = 4
RNN_HEADS = 4
RNN_BLOCK = D_MODEL // RNN_HEADS
LRU_C = 8.0
N_EXPERTS = 64
N_GROUPS = 8
GROUP_SIZE = N_EXPERTS // N_GROUPS
TOPK_GROUPS = 4
TOP_K = 8
D_EXPERT = 256
ROUTED_SCALE = 2.5
EPS = 1e-6

SUBLANES = 8
TM = 512
ROWS_PER_TILE = TM // GRID_W
PAD = 16
ROW_STRIDE = GRID_W + PAD
UPAD_ROWS = ROWS_PER_TILE * ROW_STRIDE + PAD
CONV_LANES = 256
LANES = 128
ROW_WORDS = D_MODEL // 2
ROW_TILE = 512
POS_TILE = 512
TOUT = 256
TOKEN_GROUPS = 2
V7X_SC_CORES = 2
V7X_SC_SUBCORES = 16
SC_WORKERS = V7X_SC_CORES * V7X_SC_SUBCORES
SC_CHUNK = 64
SC_BUFFERS = 2
VMEM_LIMIT = 58 * 1024 * 1024


def _sigmoid(x):
    return 0.5 * (jnp.tanh(0.5 * x) + 1.0)


def _silu(x):
    return x * _sigmoid(x)


def _gelu_tanh(x):
    return 0.5 * x * (1.0 + jnp.tanh(0.7978845608028654 * (x + 0.044715 * (x * x * x))))


def _rms_norm(x, g):
    return x * lax.rsqrt(jnp.mean(x * x, axis=-1, keepdims=True) + EPS) * g


def _dot(a, b):
    return jnp.dot(a, b, preferred_element_type=F32)


def _pack_bf16_pairs(x):
    half = x.shape[-1] // 2
    lo = lax.bitcast_convert_type(x[:, :half].astype(BF16).astype(F32), jnp.uint32)
    hi = lax.bitcast_convert_type(x[:, half:].astype(BF16).astype(F32), jnp.uint32)
    return lax.bitcast_convert_type(hi | (lo >> 16), jnp.int32)


def _unpack_bf16_pairs(words):
    u = lax.bitcast_convert_type(words, jnp.uint32)
    return (lax.bitcast_convert_type(u << 16, F32),
            lax.bitcast_convert_type(u & jnp.uint32(0xFFFF0000), F32))


def _log_decay(lam):
    return LRU_C * (jnp.minimum(lam, 0.0) - jnp.log1p(jnp.exp(-jnp.abs(lam))))


def _rglru_coeffs(v, wa_ref, ba, wx_ref, bx, c_lam, a_ref, b_ref):
    vb = v.astype(BF16)
    for h in range(RNN_HEADS):
        cs = slice(h * RNN_BLOCK, (h + 1) * RNN_BLOCK)
        vh = vb[:, cs]
        r = _sigmoid(_dot(vh, wa_ref[h]) + ba[:, cs])
        i = _sigmoid(_dot(vh, wx_ref[h]) + bx[:, cs])
        log_a = c_lam[:, cs] * r
        a = jnp.exp(log_a)
        a_ref[:, cs] = a
        b_ref[:, cs] = jnp.sqrt(jnp.tanh(-log_a) * (1.0 + a * a)) * (i * v[:, cs])


def _scan_tile(a_ref, b_ref, carry, n_rows, reverse):
    row = lax.broadcasted_iota(jnp.int32, (SUBLANES, D_MODEL), 0)
    n_groups = n_rows // SUBLANES

    def body(j, carry):
        g = (n_groups - 1 - j) if reverse else j
        off = pl.multiple_of(g * SUBLANES, SUBLANES)
        a = a_ref[pl.ds(off, SUBLANES), :]
        b = b_ref[pl.ds(off, SUBLANES), :]
        for s in (1, 2, 4):
            keep = (row < SUBLANES - s) if reverse else (row >= s)
            shift = (SUBLANES - s) if reverse else s
            a_sh = jnp.where(keep, pltpu.roll(a, shift, 0), 1.0)
            b_sh = jnp.where(keep, pltpu.roll(b, shift, 0), 0.0)
            b = a * b_sh + b
            a = a * a_sh
        h = a * carry + b
        b_ref[pl.ds(off, SUBLANES), :] = h
        last = h[0:1, :] if reverse else h[SUBLANES - 1:SUBLANES, :]
        return jnp.broadcast_to(last, (SUBLANES, D_MODEL))

    return lax.fori_loop(0, n_groups, body, carry, unroll=2)


def _mod_kernel(c_ref, w_ref, b_ref, o_ref):
    o_ref[...] = jnp.dot(_silu(c_ref[...]), w_ref[...], preferred_element_type=F32,
                         precision=lax.Precision.HIGHEST) + b_ref[...]


def _mod_call(cc, w_mod, b_mod):
    n_mod = w_mod.shape[1]
    return pl.pallas_call(
        _mod_kernel,
        grid=(n_mod // D_MODEL,),
        in_specs=[
            pl.BlockSpec((SUBLANES, D_MODEL), lambda j: (0, 0)),
            pl.BlockSpec((D_MODEL, D_MODEL), lambda j: (0, j)),
            pl.BlockSpec((1, D_MODEL), lambda j: (0, j)),
        ],
        out_specs=pl.BlockSpec((SUBLANES, D_MODEL), lambda j: (0, j)),
        out_shape=jax.ShapeDtypeStruct((SUBLANES, n_mod), F32),
        name="mod",
    )(cc, w_mod, b_mod)


def _ctx_kernel(ctx_ref, mod_ref, g_ref, w_ref, wsc_ref, bsc_ref, wa_ref, ba_ref, wx_ref, bx_ref,
                lam_ref, o_ref, uext_ref, a_ref, b_ref):
    n = ctx_ref.shape[0]
    hc = _rms_norm(ctx_ref[...], g_ref[...]) * (1.0 + mod_ref[1:2, :]) + mod_ref[0:1, :]
    u = _dot(hc.astype(BF16), w_ref[...])
    zeros8 = jnp.zeros((SUBLANES, D_MODEL), F32)
    uext_ref[0:SUBLANES, :] = zeros8
    uext_ref[SUBLANES:SUBLANES + n, :] = u
    uext_ref[SUBLANES + n:2 * SUBLANES + n, :] = zeros8
    for d in range(2):
        v = jnp.broadcast_to(bsc_ref[d:d + 1, :], (n, D_MODEL))
        for k in range(SHORT_CONV):
            start = SUBLANES + k - (SHORT_CONV - 1) * (1 - d)
            v = v + wsc_ref[d, k:k + 1, :] * uext_ref[start:start + n, :]
        _rglru_coeffs(v, wa_ref.at[d], ba_ref[d:d + 1, :], wx_ref.at[d], bx_ref[d:d + 1, :],
                      _log_decay(lam_ref[d:d + 1, :]), a_ref, b_ref)
        final = _scan_tile(a_ref, b_ref, zeros8, n, reverse=(d == 1))
        o_ref[d:d + 1, :] = final[0:1, :]


def _ctx_call(ctx, mod_c, pre1_g, w_in_bf, w_sc, b_sc, wa_bf, b_rg_a, wx_bf, b_rg_x, lam):
    bsz, n, _ = ctx.shape
    const2 = lambda b: (0, 0)
    const3 = lambda b: (0, 0, 0)
    const4 = lambda b: (0, 0, 0, 0)
    return pl.pallas_call(
        _ctx_kernel,
        grid=(bsz,),
        in_specs=[
            pl.BlockSpec((None, n, D_MODEL), lambda b: (b, 0, 0)),
            pl.BlockSpec((6, D_MODEL), const2),
            pl.BlockSpec((1, D_MODEL), const2),
            pl.BlockSpec((D_MODEL, D_MODEL), lambda b: (0, 2)),
            pl.BlockSpec((2, SHORT_CONV, D_MODEL), const3),
            pl.BlockSpec((2, D_MODEL), const2),
            pl.BlockSpec((2, RNN_HEADS, RNN_BLOCK, RNN_BLOCK), const4),
            pl.BlockSpec((2, D_MODEL), const2),
            pl.BlockSpec((2, RNN_HEADS, RNN_BLOCK, RNN_BLOCK), const4),
            pl.BlockSpec((2, D_MODEL), const2),
            pl.BlockSpec((2, D_MODEL), const2),
        ],
        out_specs=pl.BlockSpec((None, 2, D_MODEL), lambda b: (b, 0, 0)),
        out_shape=jax.ShapeDtypeStruct((bsz, 2, D_MODEL), F32),
        scratch_shapes=[
            pltpu.VMEM((n + 2 * SUBLANES, D_MODEL), F32),
            pltpu.VMEM((n, D_MODEL), F32),
            pltpu.VMEM((n, D_MODEL), F32),
        ],
        name="ctx",
    )(ctx, mod_c, pre1_g, w_in_bf, w_sc, b_sc, wa_bf, b_rg_a, wx_bf, b_rg_x, lam)


def _mixer_fwd_kernel(after_ref, x_ref, mod_ref, g_ref, win_ref, wdw_ref, bdw_ref, lng_ref, lnb_ref, wco_ref,
                      wsc_ref, bsc_ref, wa_ref, ba_ref, wx_ref, bx_ref, lam_ref, h0_ref,
                      gaya_ref, gb_ref, gbr_ref, urnn_ref, hf_ref,
                      upad_ref, ush_ref, wb_ref, cv_ref, uext_ref, a_ref, b_ref, carry_ref):
    del after_ref
    j = pl.program_id(1)
    zeros8 = jnp.zeros((SUBLANES, D_MODEL), F32)

    @pl.when(j == 0)
    def _():
        carry_ref[...] = jnp.broadcast_to(h0_ref[0:1, :], (SUBLANES, D_MODEL))
        uext_ref[0:SUBLANES, :] = zeros8
        zeros_pad = jnp.zeros((PAD, D_MODEL), F32)
        for r in range(ROWS_PER_TILE + 1):
            upad_ref[r * ROW_STRIDE:r * ROW_STRIDE + PAD, :] = zeros_pad
        for k in range(CONV_WIDTH):
            wb_ref[k] = jnp.broadcast_to(wdw_ref[k:k + 1, :], (SUBLANES, D_MODEL))

    hx = (_rms_norm(x_ref[...], g_ref[...]) * (1.0 + mod_ref[1:2, :]) + mod_ref[0:1, :]).astype(BF16)

    u = _dot(hx, win_ref[:, 0:D_MODEL]) * _sigmoid(_dot(hx, win_ref[:, D_MODEL:2 * D_MODEL]))
    for r in range(ROWS_PER_TILE):
        upad_ref[PAD + r * ROW_STRIDE:PAD + r * ROW_STRIDE + GRID_W, :] = u[r * GRID_W:(r + 1) * GRID_W, :]
    vregs_per_row = GRID_W // SUBLANES
    g_a = ur = None
    for c in range(D_MODEL // CONV_LANES):
        if c == 0:
            gb_ref[...] = _sigmoid(_dot(hx, win_ref[:, 5 * D_MODEL:6 * D_MODEL])).astype(BF16)
        elif c == 1:
            gbr_ref[...] = _gelu_tanh(_dot(hx, win_ref[:, 3 * D_MODEL:4 * D_MODEL])).astype(BF16)
        elif c == 2:
            g_a = _sigmoid(_dot(hx, win_ref[:, 4 * D_MODEL:5 * D_MODEL]))
        else:
            ur = _dot(hx, win_ref[:, 2 * D_MODEL:3 * D_MODEL])
            urnn_ref[...] = ur.astype(BF16)
            uext_ref[SUBLANES:SUBLANES + TM, :] = ur
        cs = slice(c * CONV_LANES, (c + 1) * CONV_LANES)
        xpad = upad_ref[:, cs]
        for s in range(1, SUBLANES):
            ush_ref[s - 1] = pltpu.roll(xpad, UPAD_ROWS - s, 0)
        for r in range(ROWS_PER_TILE):
            acc = jnp.broadcast_to(bdw_ref[:, cs].reshape(1, 1, CONV_LANES), (vregs_per_row, SUBLANES, CONV_LANES))
            for k in range(CONV_WIDTH):
                q, s = divmod(r * ROW_STRIDE + PAD - CONV_HALF + k, SUBLANES)
                rows = slice(q * SUBLANES, q * SUBLANES + GRID_W)
                win = upad_ref[rows, cs] if s == 0 else ush_ref[s - 1, rows, :]
                acc = acc + wb_ref[k, :, cs] * win.reshape(vregs_per_row, SUBLANES, CONV_LANES)
            cv_ref[r * GRID_W:(r + 1) * GRID_W, cs] = acc.reshape(GRID_W, CONV_LANES)
    cv = cv_ref[...]
    cvc = cv - jnp.mean(cv, axis=-1, keepdims=True)
    cvn = cvc * lax.rsqrt(jnp.mean(cvc * cvc, axis=-1, keepdims=True) + EPS) * lng_ref[...] + lnb_ref[...]
    y_a = _dot(_silu(cvn).astype(BF16), wco_ref[...])

    gaya_ref[...] = (g_a * y_a).astype(BF16)

    ue = uext_ref[...]
    v = bsc_ref[...] + wsc_ref[SHORT_CONV - 1:SHORT_CONV, :] * ur
    for k in range(SHORT_CONV - 1):
        v = v + wsc_ref[k:k + 1, :] * pltpu.roll(ue, SHORT_CONV - 1 - k, 0)[SUBLANES:SUBLANES + TM, :]
    uext_ref[0:SUBLANES, :] = uext_ref[TM:TM + SUBLANES, :]
    _rglru_coeffs(v, wa_ref, ba_ref[...], wx_ref, bx_ref[...], _log_decay(lam_ref[...]), a_ref, b_ref)
    carry_ref[...] = _scan_tile(a_ref, b_ref, carry_ref[...], TM, reverse=False)
    hf_ref[...] = b_ref[...].astype(BF16)


def _resident(shape):
    nd = len(shape)
    return pl.BlockSpec(shape, lambda b, j: (0,) * nd, pipeline_mode=pl.Buffered(1))


def _mixer_fwd_call(b0, bsz, after, x, mod_x, pre1_g, w_in_bf, w_dw, b_dw, ln_g, ln_b, wco_bf,
                    w_sc, b_sc, wa_bf, b_rg_a, wx_bf, b_rg_x, lam, h0):
    seq = x.shape[1]
    nt = seq // TM
    tile = pl.BlockSpec((None, TM, D_MODEL), lambda b, j: (b, j, 0))
    act = jax.ShapeDtypeStruct((bsz, seq, D_MODEL), BF16)
    head_w = pl.BlockSpec((None, RNN_HEADS, RNN_BLOCK, RNN_BLOCK), lambda b, j: (0, 0, 0, 0),
                          pipeline_mode=pl.Buffered(1))
    dir_row = pl.BlockSpec((None, 1, D_MODEL), lambda b, j: (0, 0, 0), pipeline_mode=pl.Buffered(1))
    return pl.pallas_call(
        _mixer_fwd_kernel,
        grid=(bsz, nt),
        in_specs=[
            pl.BlockSpec(memory_space=pl.ANY),
            pl.BlockSpec((None, TM, D_MODEL), lambda b, j: (b0 + b, j, 0)),
            pl.BlockSpec((None, 6, D_MODEL), lambda b, j: (b0 + b, 0, 0)),
            _resident((1, D_MODEL)),
            _resident((D_MODEL, 6 * D_MODEL)),
            _resident((CONV_WIDTH, D_MODEL)),
            _resident((1, D_MODEL)),
            _resident((1, D_MODEL)),
            _resident((1, D_MODEL)),
            _resident((D_MODEL, D_MODEL)),
            pl.BlockSpec((None, SHORT_CONV, D_MODEL), lambda b, j: (0, 0, 0), pipeline_mode=pl.Buffered(1)),
            dir_row, head_w, dir_row, head_w, dir_row, dir_row,
            pl.BlockSpec((None, 2, D_MODEL), lambda b, j: (b0 + b, 0, 0)),
        ],
        out_specs=[tile] * 5,
        out_shape=[act] * 5,
        scratch_shapes=[
            pltpu.VMEM((UPAD_ROWS, D_MODEL), F32),
            pltpu.VMEM((SUBLANES - 1, UPAD_ROWS, CONV_LANES), F32),
            pltpu.VMEM((CONV_WIDTH, SUBLANES, D_MODEL), F32),
            pltpu.VMEM((TM, D_MODEL), F32),
            pltpu.VMEM((TM + SUBLANES, D_MODEL), F32),
            pltpu.VMEM((TM, D_MODEL), F32),
            pltpu.VMEM((TM, D_MODEL), F32),
            pltpu.VMEM((SUBLANES, D_MODEL), F32),
        ],
        compiler_params=pltpu.CompilerParams(
            dimension_semantics=("arbitrary", "arbitrary"), vmem_limit_bytes=VMEM_LIMIT),
        name="mixer_fwd",
    )(after, x, mod_x, pre1_g, w_in_bf, w_dw, b_dw, ln_g, ln_b, wco_bf,
      w_sc, b_sc.reshape(2, 1, D_MODEL), wa_bf, b_rg_a.reshape(2, 1, D_MODEL), wx_bf,
      b_rg_x.reshape(2, 1, D_MODEL), lam.reshape(2, 1, D_MODEL), h0)


def _route(logits_t, bias):
    t = logits_t.shape[1]
    scores = _sigmoid(logits_t)
    sel = scores + bias
    neg_inf = jnp.float32(-jnp.inf)

    sel3 = sel.reshape(N_GROUPS, GROUP_SIZE, t)
    within = lax.broadcasted_iota(jnp.int32, sel3.shape, 1)
    m1 = jnp.max(sel3, axis=1, keepdims=True)
    first = jnp.min(jnp.where(sel3 == m1, within, GROUP_SIZE), axis=1, keepdims=True)
    m2 = jnp.max(jnp.where(within == first, neg_inf, sel3), axis=1, keepdims=True)
    gscore = (m1 + m2).reshape(N_GROUPS, t)

    gidx = lax.broadcasted_iota(jnp.int32, gscore.shape, 0)
    rank = jnp.zeros(gscore.shape, jnp.int32)
    for g in range(N_GROUPS):
        other = gscore[g:g + 1, :]
        beats = jnp.where(other > gscore, 1, jnp.where((other == gscore) & (gidx > g), 1, 0))
        rank = rank + beats
    gkeep = (rank < TOPK_GROUPS).reshape(N_GROUPS, 1, t)
    masked = jnp.where(gkeep, sel3, neg_inf).reshape(N_EXPERTS, t)

    eidx = lax.broadcasted_iota(jnp.int32, masked.shape, 0)
    picks, weights = [], []
    for _ in range(TOP_K):
        m = jnp.max(masked, axis=0, keepdims=True)
        first = jnp.min(jnp.where(masked == m, eidx, N_EXPERTS), axis=0, keepdims=True)
        pick = eidx == first
        picks.append(first)
        weights.append(jnp.sum(jnp.where(pick, scores, 0.0), axis=0, keepdims=True))
        masked = jnp.where(pick, neg_inf, masked)
    idx = jnp.concatenate(picks, axis=0)
    w = jnp.concatenate(weights, axis=0)
    return idx, ROUTED_SCALE * w / jnp.sum(w, axis=0, keepdims=True)


def _mixer_bwd_kernel(after_ref, urnn_ref, hf_ref, gbr_ref, gaya_ref, gb_ref, x_ref, mod_ref,
                      wsc_ref, bsc_ref, wa_ref, ba_ref, wx_ref, bx_ref, lam_ref, h0_ref,
                      wro_ref, wout_ref, post1_ref, pre2_ref, wrt_ref, rbias_ref,
                      x1_ref, hx2w_ref, idx_ref, w_ref,
                      uext_ref, a_ref, b_ref, carry_ref):
    del after_ref
    j = pl.program_id(1)
    zeros8 = jnp.zeros((SUBLANES, D_MODEL), F32)

    @pl.when(j == 0)
    def _():
        carry_ref[...] = jnp.broadcast_to(h0_ref[1:2, :], (SUBLANES, D_MODEL))
        uext_ref[TM:TM + SUBLANES, :] = zeros8

    ur = urnn_ref[...].astype(F32)
    uext_ref[0:TM, :] = ur
    ue = uext_ref[...]
    v = bsc_ref[...] + wsc_ref[0:1, :] * ur
    for k in range(1, SHORT_CONV):
        v = v + wsc_ref[k:k + 1, :] * pltpu.roll(ue, TM + SUBLANES - k, 0)[0:TM, :]
    uext_ref[TM:TM + SUBLANES, :] = uext_ref[0:SUBLANES, :]
    _rglru_coeffs(v, wa_ref, ba_ref[...], wx_ref, bx_ref[...], _log_decay(lam_ref[...]), a_ref, b_ref)
    carry_ref[...] = _scan_tile(a_ref, b_ref, carry_ref[...], TM, reverse=True)

    h_sum = hf_ref[...].astype(F32) + b_ref[...]
    y_b = _dot((gbr_ref[...].astype(F32) * h_sum).astype(BF16), wro_ref[...])
    mix = gaya_ref[...].astype(F32) + gb_ref[...].astype(F32) * y_b
    out = _dot(mix.astype(BF16), wout_ref[...])
    x1 = x_ref[...] + mod_ref[2:3, :] * _rms_norm(out, post1_ref[...])
    x1_ref[...] = x1

    hx2 = _rms_norm(x1, pre2_ref[...]) * (1.0 + mod_ref[4:5, :]) + mod_ref[3:4, :]
    hx2w_ref[...] = _pack_bf16_pairs(hx2)
    logits_t = lax.dot_general(wrt_ref[...], hx2, (((1,), (1,)), ((), ())),
                               preferred_element_type=F32, precision=lax.Precision.HIGHEST)
    idx, w = _route(logits_t, rbias_ref[...])
    idx_ref[...] = idx
    w_ref[...] = w


def _mixer_bwd_call(b0, bsz, after, urnn, hf, gbr, gaya, gb, x, mod_x, w_sc, b_sc, wa_bf, b_rg_a, wx_bf, b_rg_x, lam, h0,
                    wro_bf, wout_bf, post1_g, pre2_g, w_router_t, router_bias):
    seq = x.shape[1]
    nt = seq // TM
    rev = lambda b, j: (b, nt - 1 - j, 0)
    tile = pl.BlockSpec((None, TM, D_MODEL), rev)
    head_w = pl.BlockSpec((None, RNN_HEADS, RNN_BLOCK, RNN_BLOCK), lambda b, j: (1, 0, 0, 0),
                          pipeline_mode=pl.Buffered(1))
    dir_row = pl.BlockSpec((None, 1, D_MODEL), lambda b, j: (1, 0, 0), pipeline_mode=pl.Buffered(1))
    return pl.pallas_call(
        _mixer_bwd_kernel,
        grid=(bsz, nt),
        in_specs=[
            pl.BlockSpec(memory_space=pl.ANY),
            tile, tile, tile, tile, tile,
            pl.BlockSpec((None, TM, D_MODEL), lambda b, j: (b0 + b, nt - 1 - j, 0)),
            pl.BlockSpec((None, 6, D_MODEL), lambda b, j: (b0 + b, 0, 0)),
            pl.BlockSpec((None, SHORT_CONV, D_MODEL), lambda b, j: (1, 0, 0), pipeline_mode=pl.Buffered(1)),
            dir_row, head_w, dir_row, head_w, dir_row, dir_row,
            pl.BlockSpec((None, 2, D_MODEL), lambda b, j: (b0 + b, 0, 0)),
            _resident((D_MODEL, D_MODEL)),
            _resident((D_MODEL, D_MODEL)),
            _resident((1, D_MODEL)),
            _resident((1, D_MODEL)),
            _resident((N_EXPERTS, D_MODEL)),
            _resident((N_EXPERTS, 1)),
        ],
        out_specs=[
            pl.BlockSpec((None, TM, D_MODEL), rev),
            pl.BlockSpec((None, TM, ROW_WORDS), rev),
            pl.BlockSpec((TOP_K, TM), lambda b, j: (0, b * nt + nt - 1 - j)),
            pl.BlockSpec((TOP_K, TM), lambda b, j: (0, b * nt + nt - 1 - j)),
        ],
        out_shape=[
            jax.ShapeDtypeStruct((bsz, seq, D_MODEL), F32),
            jax.ShapeDtypeStruct((bsz, seq, ROW_WORDS), jnp.int32),
            jax.ShapeDtypeStruct((TOP_K, bsz * seq), jnp.int32),
            jax.ShapeDtypeStruct((TOP_K, bsz * seq), F32),
        ],
        scratch_shapes=[
            pltpu.VMEM((TM + SUBLANES, D_MODEL), F32),
            pltpu.VMEM((TM, D_MODEL), F32),
            pltpu.VMEM((TM, D_MODEL), F32),
            pltpu.VMEM((SUBLANES, D_MODEL), F32),
        ],
        compiler_params=pltpu.CompilerParams(
            dimension_semantics=("arbitrary", "arbitrary"), vmem_limit_bytes=VMEM_LIMIT),
        name="mixer_bwd",
    )(after, urnn, hf, gbr, gaya, gb, x, mod_x, w_sc, b_sc.reshape(2, 1, D_MODEL), wa_bf,
      b_rg_a.reshape(2, 1, D_MODEL), wx_bf, b_rg_x.reshape(2, 1, D_MODEL), lam.reshape(2, 1, D_MODEL), h0,
      wro_bf, wout_bf, post1_g, pre2_g, w_router_t, router_bias)


def _positions_kernel(after_ref, idx_ref, pos_ref, te_ref, nused_ref):
    del after_ref
    n = idx_ref.shape[1]
    n_tiles = n // POS_TILE
    eidx = lax.broadcasted_iota(jnp.int32, (N_EXPERTS, POS_TILE), 0)

    def chosen(t):
        idx = idx_ref[:, pl.ds(pl.multiple_of(t * POS_TILE, POS_TILE), POS_TILE)]
        ch = jnp.zeros((N_EXPERTS, POS_TILE), F32)
        for k in range(TOP_K):
            ch = ch + jnp.where(eidx == idx[k:k + 1, :], 1.0, 0.0)
        return idx, ch

    def count_body(t, cnt):
        return cnt + jnp.sum(chosen(t)[1], axis=1, keepdims=True)

    cnt = lax.fori_loop(0, n_tiles, count_body, jnp.zeros((N_EXPERTS, 1), F32))
    padded = jnp.ceil(cnt * (1.0 / ROW_TILE)) * ROW_TILE
    r = lax.broadcasted_iota(jnp.int32, (N_EXPERTS, N_EXPERTS), 0)
    c = lax.broadcasted_iota(jnp.int32, (N_EXPERTS, N_EXPERTS), 1)
    off = jnp.dot(jnp.where(c < r, 1.0, 0.0), jnp.broadcast_to(padded, (N_EXPERTS, LANES)),
                  preferred_element_type=F32, precision=lax.Precision.HIGHEST)[:, 0:1]
    end = off + padded

    n_map = te_ref.shape[1]
    tstart = lax.broadcasted_iota(jnp.int32, (N_EXPERTS, n_map), 1).astype(F32) * ROW_TILE
    te_ref[...] = jnp.minimum(jnp.sum(jnp.where(end <= tstart, 1, 0), axis=0, keepdims=True), N_EXPERTS - 1)
    total = jnp.sum(padded, axis=0, keepdims=True)
    nused_ref[...] = jnp.broadcast_to(total * (1.0 / ROW_TILE), nused_ref.shape).astype(jnp.int32)

    row = lax.broadcasted_iota(jnp.int32, (POS_TILE, POS_TILE), 0)
    col = lax.broadcasted_iota(jnp.int32, (POS_TILE, POS_TILE), 1)
    before = jnp.where(row < col, 1.0, 0.0).astype(BF16)

    def pos_body(t, carry):
        idx, ch = chosen(t)
        base = _dot(ch.astype(BF16), before) + (carry + off)
        rows = [jnp.sum(jnp.where(eidx == idx[k:k + 1, :], base, 0.0), axis=0, keepdims=True)
                for k in range(TOP_K)]
        pos_ref[:, pl.ds(pl.multiple_of(t * POS_TILE, POS_TILE), POS_TILE)] = (
            jnp.concatenate(rows, axis=0).astype(jnp.int32))
        return carry + jnp.sum(ch, axis=1, keepdims=True)

    lax.fori_loop(0, n_tiles, pos_body, jnp.zeros((N_EXPERTS, 1), F32))


def _positions_call(after, idx_t, n_row_tiles):
    n = idx_t.shape[1]
    n_map = -(-n_row_tiles // LANES) * LANES
    return pl.pallas_call(
        _positions_kernel,
        in_specs=[pl.BlockSpec(memory_space=pl.ANY), pl.BlockSpec(memory_space=pltpu.VMEM)],
        out_shape=[
            jax.ShapeDtypeStruct((TOP_K, n), jnp.int32),
            jax.ShapeDtypeStruct((1, n_map), jnp.int32),
            jax.ShapeDtypeStruct((1, LANES), jnp.int32),
        ],
        name="moe_positions",
    )(after, idx_t)


def _sc_mesh():
    return plsc.VectorSubcoreMesh(core_axis_name="c", subcore_axis_name="s",
                                  num_cores=V7X_SC_CORES, num_subcores=V7X_SC_SUBCORES)


def _sc_worker():
    return lax.axis_index("s") * V7X_SC_CORES + lax.axis_index("c")


def _dispatch_call(rows, pos, n_slots):
    n = rows.shape[0]
    tok_w = n // SC_WORKERS
    n_items = tok_w // SC_CHUNK

    def body(rows_hbm, pos_hbm, xs_hbm, idx_v, rows_v, lsem, ssem):
        wid = _sc_worker()
        pltpu.sync_copy(pos_hbm.at[wid], idx_v)
        base = wid * tok_w

        def load(i):
            b = i % SC_BUFFERS
            return pltpu.async_copy(rows_hbm.at[pl.ds(base + i * SC_CHUNK, SC_CHUNK)], rows_v.at[b], lsem.at[b])

        def scatter(i):
            b = i % SC_BUFFERS
            return [pltpu.async_copy(rows_v.at[b], xs_hbm.at[idx_v.at[i, k]], ssem.at[b]) for k in range(TOP_K)]

        loads = {i: load(i) for i in range(SC_BUFFERS - 1)}
        scat = {}
        for i in range(n_items):
            loads[i].wait()
            scat[i] = scatter(i)
            if i >= 1:
                for cp in scat[i - 1]:
                    cp.wait()
            if i + SC_BUFFERS - 1 < n_items:
                loads[i + SC_BUFFERS - 1] = load(i + SC_BUFFERS - 1)
        for cp in scat[n_items - 1]:
            cp.wait()

    return pl.kernel(
        body, mesh=_sc_mesh(),
        out_type=jax.ShapeDtypeStruct((n_slots, ROW_WORDS), jnp.int32),
        scratch_types=[pltpu.VMEM((n_items, TOP_K, SC_CHUNK), jnp.int32),
                       pltpu.VMEM((SC_BUFFERS, SC_CHUNK, ROW_WORDS), jnp.int32),
                       pltpu.SemaphoreType.DMA((SC_BUFFERS,)), pltpu.SemaphoreType.DMA((SC_BUFFERS,))],
        compiler_params=pltpu.CompilerParams(use_tc_tiling_on_sc=True),
        name="moe_dispatch",
    )(rows, pos)


def _collect_call(ys, pos, n):
    tok_w = n // SC_WORKERS
    n_chunks = tok_w // SC_CHUNK
    items = [(c, k) for c in range(n_chunks) for k in range(TOP_K)]

    def body(ys_hbm, pos_hbm, yt_hbm, idx_v, rows_v, gsem, wsem):
        wid = _sc_worker()
        pltpu.sync_copy(pos_hbm.at[wid], idx_v)
        base = wid * tok_w

        def gather(j):
            c, k = items[j]
            b = j % SC_BUFFERS
            return pltpu.async_copy(ys_hbm.at[idx_v.at[c, k]], rows_v.at[b], gsem.at[b])

        def write(j):
            c, k = items[j]
            b = j % SC_BUFFERS
            return pltpu.async_copy(rows_v.at[b], yt_hbm.at[k, pl.ds(base + c * SC_CHUNK, SC_CHUNK)], wsem.at[b])

        g = {j: gather(j) for j in range(SC_BUFFERS - 1)}
        w = {}
        for j in range(len(items)):
            g[j].wait()
            w[j] = write(j)
            if j >= 1:
                w[j - 1].wait()
            if j + SC_BUFFERS - 1 < len(items):
                g[j + SC_BUFFERS - 1] = gather(j + SC_BUFFERS - 1)
        w[len(items) - 1].wait()

    return pl.kernel(
        body, mesh=_sc_mesh(),
        out_type=jax.ShapeDtypeStruct((TOP_K, n, ROW_WORDS), jnp.int32),
        scratch_types=[pltpu.VMEM((n_chunks, TOP_K, SC_CHUNK), jnp.int32),
                       pltpu.VMEM((SC_BUFFERS, SC_CHUNK, ROW_WORDS), jnp.int32),
                       pltpu.SemaphoreType.DMA((SC_BUFFERS,)), pltpu.SemaphoreType.DMA((SC_BUFFERS,))],
        compiler_params=pltpu.CompilerParams(use_tc_tiling_on_sc=True),
        name="moe_collect",
    )(ys, pos)


def _expert_gemm_kernel(te_ref, nused_ref, after_ref, xs_ref, wg_ref, wu_ref, wd_ref, ys_ref, wgu_scr, wd_scr):
    del after_ref
    i = pl.program_id(0)
    e = te_ref[i]
    prev = te_ref[jnp.maximum(i - 1, 0)]

    @pl.when((i == 0) | (e != prev))
    def _():
        wgu_scr[:, 0:D_EXPERT] = wg_ref[...].astype(BF16)
        wgu_scr[:, D_EXPERT:2 * D_EXPERT] = wu_ref[...].astype(BF16)
        wd_scr[...] = wd_ref[...].astype(BF16)

    @pl.when(i < nused_ref[0])
    def _():
        lo, hi = _unpack_bf16_pairs(xs_ref[...])
        gu = _dot(lo.astype(BF16), wgu_scr[0:ROW_WORDS, :]) + _dot(hi.astype(BF16), wgu_scr[ROW_WORDS:D_MODEL, :])
        h = _silu(gu[:, 0:D_EXPERT]) * gu[:, D_EXPERT:2 * D_EXPERT]
        ys_ref[...] = _pack_bf16_pairs(_dot(h.astype(BF16), wd_scr[...]))


def _expert_gemm_call(after, te, nused, xs, w_e_gate, w_e_up, w_e_down):
    n_slots = xs.shape[0]
    n_row_tiles = n_slots // ROW_TILE
    rows = pl.BlockSpec((ROW_TILE, ROW_WORDS), lambda i, te, nused: (jnp.minimum(i, nused[0] - 1), 0))
    return pl.pallas_call(
        _expert_gemm_kernel,
        grid_spec=pltpu.PrefetchScalarGridSpec(
            num_scalar_prefetch=2,
            grid=(n_row_tiles,),
            in_specs=[
                pl.BlockSpec(memory_space=pl.ANY),
                rows,
                pl.BlockSpec((None, D_MODEL, D_EXPERT), lambda i, te, nused: (te[i], 0, 0)),
                pl.BlockSpec((None, D_MODEL, D_EXPERT), lambda i, te, nused: (te[i], 0, 0)),
                pl.BlockSpec((None, D_EXPERT, D_MODEL), lambda i, te, nused: (te[i], 0, 0)),
            ],
            out_specs=rows,
            scratch_shapes=[pltpu.VMEM((D_MODEL, 2 * D_EXPERT), BF16), pltpu.VMEM((D_EXPERT, D_MODEL), BF16)],
        ),
        out_shape=jax.ShapeDtypeStruct((n_slots, ROW_WORDS), jnp.int32),
        compiler_params=pltpu.CompilerParams(dimension_semantics=("arbitrary",), vmem_limit_bytes=VMEM_LIMIT),
        name="moe_experts",
    )(te, nused, after, xs, w_e_gate, w_e_up, w_e_down)


def _moe_out_kernel(yt_ref, w_ref, t_ref, x1_ref, mod_ref, post2_ref, wsg_ref, wsu_ref, wsd_ref, o_ref):
    lo, hi = _unpack_bf16_pairs(t_ref[...])
    lo = lo.astype(BF16)
    hi = hi.astype(BF16)
    g = _dot(lo, wsg_ref[0:ROW_WORDS, :]) + _dot(hi, wsg_ref[ROW_WORDS:D_MODEL, :])
    u = _dot(lo, wsu_ref[0:ROW_WORDS, :]) + _dot(hi, wsu_ref[ROW_WORDS:D_MODEL, :])
    shared = _dot((_silu(g) * u).astype(BF16), wsd_ref[...])
    acc_lo = shared[:, 0:ROW_WORDS]
    acc_hi = shared[:, ROW_WORDS:D_MODEL]
    for k in range(TOP_K):
        y_lo, y_hi = _unpack_bf16_pairs(yt_ref[k])
        wk = w_ref[:, k:k + 1]
        acc_lo = acc_lo + wk * y_lo
        acc_hi = acc_hi + wk * y_hi
    moe = jnp.concatenate([acc_lo, acc_hi], axis=-1)
    o_ref[...] = x1_ref[...] + mod_ref[5:6, :] * _rms_norm(moe, post2_ref[...])


def _moe_out_into_kernel(prev_ref, *refs):
    del prev_ref
    _moe_out_kernel(*refs)


def _moe_out_call(prev_out, n_total, b0, yt, w, t, x1, mod_x, post2_g, wsg_bf, wsu_bf, wsd_bf, seq):
    n = t.shape[0]
    tiles_per_seq = seq // TOUT
    first_tile = b0 * tiles_per_seq
    const = lambda i: (0, 0)
    in_specs = [
        pl.BlockSpec((TOP_K, TOUT, ROW_WORDS), lambda i: (0, i, 0)),
        pl.BlockSpec((TOUT, TOP_K), lambda i: (i, 0)),
        pl.BlockSpec((TOUT, ROW_WORDS), lambda i: (i, 0)),
        pl.BlockSpec((TOUT, D_MODEL), lambda i: (i, 0)),
        pl.BlockSpec((None, 6, D_MODEL), lambda i: (b0 + i // tiles_per_seq, 0, 0)),
        pl.BlockSpec((1, D_MODEL), const),
        pl.BlockSpec((D_MODEL, D_EXPERT), const),
        pl.BlockSpec((D_MODEL, D_EXPERT), const),
        pl.BlockSpec((D_EXPERT, D_MODEL), const),
    ]
    args = (yt, w, t, x1, mod_x, post2_g, wsg_bf, wsu_bf, wsd_bf)
    aliased = prev_out is not None
    return pl.pallas_call(
        _moe_out_into_kernel if aliased else _moe_out_kernel,
        grid=(n // TOUT,),
        in_specs=([pl.BlockSpec(memory_space=pl.ANY)] if aliased else []) + in_specs,
        out_specs=pl.BlockSpec((TOUT, D_MODEL), lambda i: (first_tile + i, 0)),
        out_shape=jax.ShapeDtypeStruct((n_total, D_MODEL), F32),
        input_output_aliases={0: 0} if aliased else {},
        compiler_params=pltpu.CompilerParams(dimension_semantics=("arbitrary",), vmem_limit_bytes=VMEM_LIMIT),
        name="moe_out",
    )(*(((prev_out,) if aliased else ()) + args))


def _moe_dispatch(after, hx2w, idx_t):
    n = hx2w.shape[0]
    n_slots = n * TOP_K + N_EXPERTS * ROW_TILE
    n_row_tiles = n_slots // ROW_TILE
    pos_t, te, nused = _positions_call(after, idx_t, n_row_tiles)
    pos = pos_t.reshape(TOP_K, SC_WORKERS, n // (SC_WORKERS * SC_CHUNK), SC_CHUNK).transpose(1, 2, 0, 3)
    return _dispatch_call(hx2w, pos, n_slots), pos, te[0, :n_row_tiles], nused[0, :1]


def kernel(x, c, ctx, c_ctx, w_mod, b_mod, pre1_g, post1_g, pre2_g, post2_g, w_in, w_dw, b_dw, ln_conv_g, ln_conv_b, w_conv_out, w_sc, b_sc, w_rg_a, b_rg_a, w_rg_x, b_rg_x, lru_lambda, w_rnn_out, w_out, w_router, router_bias, w_e_gate, w_e_up, w_e_down, w_s_gate, w_s_up, w_s_down):
    assert w_mod.shape[0] == 1, "single-layer block"
    bsz, seq, d = x.shape
    n = bsz * seq
    assert d == D_MODEL and seq % TM == 0 and seq % TOUT == 0 and bsz + 1 <= SUBLANES
    assert TOKEN_GROUPS == 2 and bsz % TOKEN_GROUPS == 0 and (n // TOKEN_GROUPS) % (SC_WORKERS * SC_CHUNK) == 0
    assert (n // TOKEN_GROUPS) % POS_TILE == 0

    cc = jnp.zeros((SUBLANES, D_MODEL), F32).at[:bsz].set(c).at[bsz].set(c_ctx)
    mod = _mod_call(cc, w_mod[0], b_mod)
    mod_x = mod[:bsz].reshape(bsz, 6, D_MODEL)
    mod_c = mod[bsz].reshape(6, D_MODEL)

    w_in_bf = w_in[0].astype(BF16)
    wa_bf = w_rg_a[0].astype(BF16)
    wx_bf = w_rg_x[0].astype(BF16)

    h0 = _ctx_call(ctx, mod_c, pre1_g, w_in_bf, w_sc[0], b_sc[0], wa_bf, b_rg_a[0], wx_bf, b_rg_x[0],
                   lru_lambda[0])

    wco_bf, wro_bf, wout_bf = w_conv_out[0].astype(BF16), w_rnn_out[0].astype(BF16), w_out[0].astype(BF16)
    wsg_bf, wsu_bf, wsd_bf = w_s_gate[0].astype(BF16), w_s_up[0].astype(BF16), w_s_down[0].astype(BF16)
    w_router_t, rbias = w_router[0].T, router_bias.reshape(N_EXPERTS, 1)
    gsz = bsz // TOKEN_GROUPS
    gn = gsz * seq
    none = jnp.zeros((TOP_K, LANES), jnp.int32)

    def fwd(g, after):
        return _mixer_fwd_call(g * gsz, gsz, after, x, mod_x, pre1_g, w_in_bf, w_dw[0], b_dw, ln_conv_g,
                               ln_conv_b, wco_bf, w_sc[0], b_sc[0], wa_bf, b_rg_a[0], wx_bf, b_rg_x[0],
                               lru_lambda[0], h0)

    def bwd(g, after, acts):
        gaya, gb, gbr, urnn, hf = acts
        x1, hx2w, idx_t, w_t = _mixer_bwd_call(
            g * gsz, gsz, after, urnn, hf, gbr, gaya, gb, x, mod_x, w_sc[0], b_sc[0], wa_bf, b_rg_a[0], wx_bf,
            b_rg_x[0], lru_lambda[0], h0, wro_bf, wout_bf, post1_g, pre2_g, w_router_t, rbias)
        return x1.reshape(gn, D_MODEL), hx2w.reshape(gn, ROW_WORDS), idx_t, w_t

    def experts(after, te, nused, xs):
        return _expert_gemm_call(after, te, nused, xs, w_e_gate[0], w_e_up[0], w_e_down[0])

    def finish(g, prev_out, yt, w_t, hx2w, x1):
        return _moe_out_call(prev_out, n, g * gsz, yt, w_t.T, hx2w, x1, mod_x, post2_g, wsg_bf, wsu_bf, wsd_bf, seq)

    x1_a, hx2w_a, idx_a, w_a = bwd(0, none, fwd(0, none))
    xs_a, pos_a, te_a, nused_a = _moe_dispatch(none, hx2w_a, idx_a)
    acts_b = fwd(1, idx_a)
    ys_a = experts(acts_b[4], te_a, nused_a, xs_a)
    yt_a = _collect_call(ys_a, pos_a, gn)
    x1_b, hx2w_b, idx_b, w_b = bwd(1, ys_a, acts_b)
    xs_b, pos_b, te_b, nused_b = _moe_dispatch(yt_a, hx2w_b, idx_b)
    out = finish(0, None, yt_a, w_a, hx2w_a, x1_a)
    ys_b = experts(none, te_b, nused_b, xs_b)
    yt_b = _collect_call(ys_b, pos_b, gn)
    out = finish(1, out, yt_b, w_b, hx2w_b, x1_b)
    return out.reshape(bsz, seq, D_MODEL)
```

```python
import jax
import jax.numpy as jnp
from jax import lax
from jax.experimental import pallas as pl
from jax.experimental.pallas import tpu as pltpu
from jax.experimental.pallas import tpu_sc as plsc

F32 = jnp.float32
BF16 = jnp.bfloat16

D_MODEL = 1024
GRID_W = 64
CONV_WIDTH = 31
CONV_HALF = CONV_WIDTH // 2
SHORT_CONV = 4
RNN_HEADS = 4
RNN_BLOCK = D_MODEL // RNN_HEADS
LRU_C = 8.0
N_EXPERTS = 64
N_GROUPS = 8
GROUP_SIZE = N_EXPERTS // N_GROUPS
TOPK_GROUPS = 4
TOP_K = 8
D_EXPERT = 256
ROUTED_SCALE = 2.5
EPS = 1e-6

SUBLANES = 8
TM = 512
ROWS_PER_TILE = TM // GRID_W
PAD = 16
ROW_STRIDE = GRID_W + PAD
UPAD_ROWS = ROWS_PER_TILE * ROW_STRIDE + PAD
CONV_LANES = 256
LANES = 128
ROW_WORDS = D_MODEL // 2
ROW_TILE = 1024
POS_TILE = 512
TOUT = 256
TOKEN_GROUPS = 2
V7X_SC_CORES = 2
V7X_SC_SUBCORES = 16
SC_WORKERS = V7X_SC_CORES * V7X_SC_SUBCORES
SC_CHUNK = 64
SC_BUFFERS = 2
VMEM_LIMIT = 58 * 1024 * 1024


def _sigmoid(x):
    return 0.5 * (jnp.tanh(0.5 * x) + 1.0)


def _silu(x):
    return x * _sigmoid(x)


def _gelu_tanh(x):
    return 0.5 * x * (1.0 + jnp.tanh(0.7978845608028654 * (x + 0.044715 * (x * x * x))))


def _rms_norm(x, g):
    return x * lax.rsqrt(jnp.mean(x * x, axis=-1, keepdims=True) + EPS) * g


def _dot(a, b):
    return jnp.dot(a, b, preferred_element_type=F32)


def _pack_bf16_pairs(x):
    half = x.shape[-1] // 2
    lo = lax.bitcast_convert_type(x[:, :half].astype(BF16).astype(F32), jnp.uint32)
    hi = lax.bitcast_convert_type(x[:, half:].astype(BF16).astype(F32), jnp.uint32)
    return lax.bitcast_convert_type(hi | (lo >> 16), jnp.int32)


def _pack_bf16_pairs_native(x):
    half = x.shape[-1] // 2
    packed = pltpu.pack_elementwise([x[:, :half], x[:, half:]], packed_dtype=BF16)
    return lax.bitcast_convert_type(packed, jnp.int32)


def _unpack_bf16_pairs(words):
    u = lax.bitcast_convert_type(words, jnp.uint32)
    return (lax.bitcast_convert_type(u << 16, F32),
            lax.bitcast_convert_type(u & jnp.uint32(0xFFFF0000), F32))


def _log_decay(lam):
    return LRU_C * (jnp.minimum(lam, 0.0) - jnp.log1p(jnp.exp(-jnp.abs(lam))))


def _rglru_coeffs(v, wa_ref, ba, wx_ref, bx, c_lam, a_ref, b_ref):
    vb = v.astype(BF16)
    for h in range(RNN_HEADS):
        cs = slice(h * RNN_BLOCK, (h + 1) * RNN_BLOCK)
        vh = vb[:, cs]
        r = _sigmoid(_dot(vh, wa_ref[h]) + ba[:, cs])
        i = _sigmoid(_dot(vh, wx_ref[h]) + bx[:, cs])
        log_a = c_lam[:, cs] * r
        a = jnp.exp(log_a)
        a_ref[:, cs] = a
        b_ref[:, cs] = jnp.sqrt(jnp.tanh(-log_a) * (1.0 + a * a)) * (i * v[:, cs])


def _scan_tile(a_ref, b_ref, carry, n_rows, reverse):
    row = lax.broadcasted_iota(jnp.int32, (SUBLANES, D_MODEL), 0)
    n_groups = n_rows // SUBLANES

    def body(j, carry):
        g = (n_groups - 1 - j) if reverse else j
        off = pl.multiple_of(g * SUBLANES, SUBLANES)
        a = a_ref[pl.ds(off, SUBLANES), :]
        b = b_ref[pl.ds(off, SUBLANES), :]
        for s in (1, 2, 4):
            keep = (row < SUBLANES - s) if reverse else (row >= s)
            shift = (SUBLANES - s) if reverse else s
            a_sh = jnp.where(keep, pltpu.roll(a, shift, 0), 1.0)
            b_sh = jnp.where(keep, pltpu.roll(b, shift, 0), 0.0)
            b = a * b_sh + b
            a = a * a_sh
        h = a * carry + b
        b_ref[pl.ds(off, SUBLANES), :] = h
        last = h[0:1, :] if reverse else h[SUBLANES - 1:SUBLANES, :]
        return jnp.broadcast_to(last, (SUBLANES, D_MODEL))

    return lax.fori_loop(0, n_groups, body, carry, unroll=2)


def _mod_kernel(c_ref, w_ref, b_ref, o_ref):
    o_ref[...] = jnp.dot(_silu(c_ref[...]), w_ref[...], preferred_element_type=F32,
                         precision=lax.Precision.HIGHEST) + b_ref[...]


def _mod_call(cc, w_mod, b_mod):
    n_mod = w_mod.shape[1]
    return pl.pallas_call(
        _mod_kernel,
        grid=(n_mod // D_MODEL,),
        in_specs=[
            pl.BlockSpec((SUBLANES, D_MODEL), lambda j: (0, 0)),
            pl.BlockSpec((D_MODEL, D_MODEL), lambda j: (0, j)),
            pl.BlockSpec((1, D_MODEL), lambda j: (0, j)),
        ],
        out_specs=pl.BlockSpec((SUBLANES, D_MODEL), lambda j: (0, j)),
        out_shape=jax.ShapeDtypeStruct((SUBLANES, n_mod), F32),
        name="mod",
    )(cc, w_mod, b_mod)


def _ctx_kernel(ctx_ref, mod_ref, g_ref, w_ref, wsc_ref, bsc_ref, wa_ref, ba_ref, wx_ref, bx_ref,
                lam_ref, o_ref, uext_ref, a_ref, b_ref):
    n = ctx_ref.shape[0]
    hc = _rms_norm(ctx_ref[...], g_ref[...]) * (1.0 + mod_ref[1:2, :]) + mod_ref[0:1, :]
    u = _dot(hc.astype(BF16), w_ref[...])
    zeros8 = jnp.zeros((SUBLANES, D_MODEL), F32)
    uext_ref[0:SUBLANES, :] = zeros8
    uext_ref[SUBLANES:SUBLANES + n, :] = u
    uext_ref[SUBLANES + n:2 * SUBLANES + n, :] = zeros8
    for d in range(2):
        v = jnp.broadcast_to(bsc_ref[d:d + 1, :], (n, D_MODEL))
        for k in range(SHORT_CONV):
            start = SUBLANES + k - (SHORT_CONV - 1) * (1 - d)
            v = v + wsc_ref[d, k:k + 1, :] * uext_ref[start:start + n, :]
        _rglru_coeffs(v, wa_ref.at[d], ba_ref[d:d + 1, :], wx_ref.at[d], bx_ref[d:d + 1, :],
                      _log_decay(lam_ref[d:d + 1, :]), a_ref, b_ref)
        final = _scan_tile(a_ref, b_ref, zeros8, n, reverse=(d == 1))
        o_ref[d:d + 1, :] = final[0:1, :]


def _ctx_call(ctx, mod_c, pre1_g, w_in_bf, w_sc, b_sc, wa_bf, b_rg_a, wx_bf, b_rg_x, lam):
    bsz, n, _ = ctx.shape
    const2 = lambda b: (0, 0)
    const3 = lambda b: (0, 0, 0)
    const4 = lambda b: (0, 0, 0, 0)
    return pl.pallas_call(
        _ctx_kernel,
        grid=(bsz,),
        in_specs=[
            pl.BlockSpec((None, n, D_MODEL), lambda b: (b, 0, 0)),
            pl.BlockSpec((6, D_MODEL), const2),
            pl.BlockSpec((1, D_MODEL), const2),
            pl.BlockSpec((D_MODEL, D_MODEL), lambda b: (0, 2)),
            pl.BlockSpec((2, SHORT_CONV, D_MODEL), const3),
            pl.BlockSpec((2, D_MODEL), const2),
            pl.BlockSpec((2, RNN_HEADS, RNN_BLOCK, RNN_BLOCK), const4),
            pl.BlockSpec((2, D_MODEL), const2),
            pl.BlockSpec((2, RNN_HEADS, RNN_BLOCK, RNN_BLOCK), const4),
            pl.BlockSpec((2, D_MODEL), const2),
            pl.BlockSpec((2, D_MODEL), const2),
        ],
        out_specs=pl.BlockSpec((None, 2, D_MODEL), lambda b: (b, 0, 0)),
        out_shape=jax.ShapeDtypeStruct((bsz, 2, D_MODEL), F32),
        scratch_shapes=[
            pltpu.VMEM((n + 2 * SUBLANES, D_MODEL), F32),
            pltpu.VMEM((n, D_MODEL), F32),
            pltpu.VMEM((n, D_MODEL), F32),
        ],
        name="ctx",
    )(ctx, mod_c, pre1_g, w_in_bf, w_sc, b_sc, wa_bf, b_rg_a, wx_bf, b_rg_x, lam)


def _mixer_fwd_kernel(after_ref, x_ref, mod_ref, g_ref, win_ref, wdw_ref, bdw_ref, lng_ref, lnb_ref, wco_ref,
                      wsc_ref, bsc_ref, wa_ref, ba_ref, wx_ref, bx_ref, lam_ref, h0_ref,
                      gaya_ref, gb_ref, gbr_ref, urnn_ref, hf_ref,
                      upad_ref, ush_ref, wb_ref, cv_ref, uext_ref, a_ref, b_ref, carry_ref):
    del after_ref
    j = pl.program_id(1)
    zeros8 = jnp.zeros((SUBLANES, D_MODEL), F32)

    @pl.when(j == 0)
    def _():
        carry_ref[...] = jnp.broadcast_to(h0_ref[0:1, :], (SUBLANES, D_MODEL))
        uext_ref[0:SUBLANES, :] = zeros8
        zeros_pad = jnp.zeros((PAD, D_MODEL), F32)
        for r in range(ROWS_PER_TILE + 1):
            upad_ref[r * ROW_STRIDE:r * ROW_STRIDE + PAD, :] = zeros_pad
        for k in range(CONV_WIDTH):
            wb_ref[k] = jnp.broadcast_to(wdw_ref[k:k + 1, :], (SUBLANES, D_MODEL))

    hx = (_rms_norm(x_ref[...], g_ref[...]) * (1.0 + mod_ref[1:2, :]) + mod_ref[0:1, :]).astype(BF16)

    u = _dot(hx, win_ref[:, 0:D_MODEL]) * _sigmoid(_dot(hx, win_ref[:, D_MODEL:2 * D_MODEL]))
    for r in range(ROWS_PER_TILE):
        upad_ref[PAD + r * ROW_STRIDE:PAD + r * ROW_STRIDE + GRID_W, :] = u[r * GRID_W:(r + 1) * GRID_W, :]
    vregs_per_row = GRID_W // SUBLANES
    g_a = ur = None
    for c in range(D_MODEL // CONV_LANES):
        if c == 0:
            gb_ref[...] = _sigmoid(_dot(hx, win_ref[:, 5 * D_MODEL:6 * D_MODEL])).astype(BF16)
        elif c == 1:
            gbr_ref[...] = _gelu_tanh(_dot(hx, win_ref[:, 3 * D_MODEL:4 * D_MODEL])).astype(BF16)
        elif c == 2:
            g_a = _sigmoid(_dot(hx, win_ref[:, 4 * D_MODEL:5 * D_MODEL]))
        else:
            ur = _dot(hx, win_ref[:, 2 * D_MODEL:3 * D_MODEL])
            urnn_ref[...] = ur.astype(BF16)
            uext_ref[SUBLANES:SUBLANES + TM, :] = ur
        cs = slice(c * CONV_LANES, (c + 1) * CONV_LANES)
        xpad = upad_ref[:, cs]
        for s in range(1, SUBLANES):
            ush_ref[s - 1] = pltpu.roll(xpad, UPAD_ROWS - s, 0)
        for r in range(ROWS_PER_TILE):
            acc = jnp.broadcast_to(bdw_ref[:, cs].reshape(1, 1, CONV_LANES), (vregs_per_row, SUBLANES, CONV_LANES))
            for k in range(CONV_WIDTH):
                q, s = divmod(r * ROW_STRIDE + PAD - CONV_HALF + k, SUBLANES)
                rows = slice(q * SUBLANES, q * SUBLANES + GRID_W)
                win = upad_ref[rows, cs] if s == 0 else ush_ref[s - 1, rows, :]
                acc = acc + wb_ref[k, :, cs] * win.reshape(vregs_per_row, SUBLANES, CONV_LANES)
            cv_ref[r * GRID_W:(r + 1) * GRID_W, cs] = acc.reshape(GRID_W, CONV_LANES)
    cv = cv_ref[...]
    cvc = cv - jnp.mean(cv, axis=-1, keepdims=True)
    cvn = cvc * lax.rsqrt(jnp.mean(cvc * cvc, axis=-1, keepdims=True) + EPS) * lng_ref[...] + lnb_ref[...]
    y_a = _dot(_silu(cvn).astype(BF16), wco_ref[...])

    gaya_ref[...] = (g_a * y_a).astype(BF16)

    ue = uext_ref[...]
    v = bsc_ref[...] + wsc_ref[SHORT_CONV - 1:SHORT_CONV, :] * ur
    for k in range(SHORT_CONV - 1):
        v = v + wsc_ref[k:k + 1, :] * pltpu.roll(ue, SHORT_CONV - 1 - k, 0)[SUBLANES:SUBLANES + TM, :]
    uext_ref[0:SUBLANES, :] = uext_ref[TM:TM + SUBLANES, :]
    _rglru_coeffs(v, wa_ref, ba_ref[...], wx_ref, bx_ref[...], _log_decay(lam_ref[...]), a_ref, b_ref)
    carry_ref[...] = _scan_tile(a_ref, b_ref, carry_ref[...], TM, reverse=False)
    hf_ref[...] = b_ref[...].astype(BF16)


def _resident(shape):
    nd = len(shape)
    return pl.BlockSpec(shape, lambda b, j: (0,) * nd, pipeline_mode=pl.Buffered(1))


def _mixer_fwd_call(b0, bsz, after, x, mod_x, pre1_g, w_in_bf, w_dw, b_dw, ln_g, ln_b, wco_bf,
                    w_sc, b_sc, wa_bf, b_rg_a, wx_bf, b_rg_x, lam, h0):
    seq = x.shape[1]
    nt = seq // TM
    tile = pl.BlockSpec((None, TM, D_MODEL), lambda b, j: (b, j, 0))
    act = jax.ShapeDtypeStruct((bsz, seq, D_MODEL), BF16)
    head_w = pl.BlockSpec((None, RNN_HEADS, RNN_BLOCK, RNN_BLOCK), lambda b, j: (0, 0, 0, 0),
                          pipeline_mode=pl.Buffered(1))
    dir_row = pl.BlockSpec((None, 1, D_MODEL), lambda b, j: (0, 0, 0), pipeline_mode=pl.Buffered(1))
    return pl.pallas_call(
        _mixer_fwd_kernel,
        grid=(bsz, nt),
        in_specs=[
            pl.BlockSpec(memory_space=pl.ANY),
            pl.BlockSpec((None, TM, D_MODEL), lambda b, j: (b0 + b, j, 0)),
            pl.BlockSpec((None, 6, D_MODEL), lambda b, j: (b0 + b, 0, 0)),
            _resident((1, D_MODEL)),
            _resident((D_MODEL, 6 * D_MODEL)),
            _resident((CONV_WIDTH, D_MODEL)),
            _resident((1, D_MODEL)),
            _resident((1, D_MODEL)),
            _resident((1, D_MODEL)),
            _resident((D_MODEL, D_MODEL)),
            pl.BlockSpec((None, SHORT_CONV, D_MODEL), lambda b, j: (0, 0, 0), pipeline_mode=pl.Buffered(1)),
            dir_row, head_w, dir_row, head_w, dir_row, dir_row,
            pl.BlockSpec((None, 2, D_MODEL), lambda b, j: (b0 + b, 0, 0)),
        ],
        out_specs=[tile] * 5,
        out_shape=[act] * 5,
        scratch_shapes=[
            pltpu.VMEM((UPAD_ROWS, D_MODEL), F32),
            pltpu.VMEM((SUBLANES - 1, UPAD_ROWS, CONV_LANES), F32),
            pltpu.VMEM((CONV_WIDTH, SUBLANES, D_MODEL), F32),
            pltpu.VMEM((TM, D_MODEL), F32),
            pltpu.VMEM((TM + SUBLANES, D_MODEL), F32),
            pltpu.VMEM((TM, D_MODEL), F32),
            pltpu.VMEM((TM, D_MODEL), F32),
            pltpu.VMEM((SUBLANES, D_MODEL), F32),
        ],
        compiler_params=pltpu.CompilerParams(
            dimension_semantics=("arbitrary", "arbitrary"), vmem_limit_bytes=VMEM_LIMIT),
        name="mixer_fwd",
    )(after, x, mod_x, pre1_g, w_in_bf, w_dw, b_dw, ln_g, ln_b, wco_bf,
      w_sc, b_sc.reshape(2, 1, D_MODEL), wa_bf, b_rg_a.reshape(2, 1, D_MODEL), wx_bf,
      b_rg_x.reshape(2, 1, D_MODEL), lam.reshape(2, 1, D_MODEL), h0)


def _route(logits_t, bias):
    t = logits_t.shape[1]
    scores = _sigmoid(logits_t)
    sel = scores + bias
    neg_inf = jnp.float32(-jnp.inf)

    sel3 = sel.reshape(N_GROUPS, GROUP_SIZE, t)
    within = lax.broadcasted_iota(jnp.int32, sel3.shape, 1)
    m1 = jnp.max(sel3, axis=1, keepdims=True)
    first = jnp.min(jnp.where(sel3 == m1, within, GROUP_SIZE), axis=1, keepdims=True)
    m2 = jnp.max(jnp.where(within == first, neg_inf, sel3), axis=1, keepdims=True)
    gscore = (m1 + m2).reshape(N_GROUPS, t)

    gidx = lax.broadcasted_iota(jnp.int32, gscore.shape, 0)
    rank = jnp.zeros(gscore.shape, jnp.int32)
    for g in range(N_GROUPS):
        other = gscore[g:g + 1, :]
        beats = jnp.where(other > gscore, 1, jnp.where((other == gscore) & (gidx > g), 1, 0))
        rank = rank + beats
    gkeep = (rank < TOPK_GROUPS).reshape(N_GROUPS, 1, t)
    masked = jnp.where(gkeep, sel3, neg_inf).reshape(N_EXPERTS, t)

    eidx = lax.broadcasted_iota(jnp.int32, masked.shape, 0)
    picks, weights = [], []
    for _ in range(TOP_K):
        m = jnp.max(masked, axis=0, keepdims=True)
        first = jnp.min(jnp.where(masked == m, eidx, N_EXPERTS), axis=0, keepdims=True)
        pick = eidx == first
        picks.append(first)
        weights.append(jnp.sum(jnp.where(pick, scores, 0.0), axis=0, keepdims=True))
        masked = jnp.where(pick, neg_inf, masked)
    idx = jnp.concatenate(picks, axis=0)
    w = jnp.concatenate(weights, axis=0)
    return idx, ROUTED_SCALE * w / jnp.sum(w, axis=0, keepdims=True)


def _mixer_bwd_kernel(after_ref, urnn_ref, hf_ref, gbr_ref, gaya_ref, gb_ref, x_ref, mod_ref,
                      wsc_ref, bsc_ref, wa_ref, ba_ref, wx_ref, bx_ref, lam_ref, h0_ref,
                      wro_ref, wout_ref, post1_ref, pre2_ref, wrt_ref, rbias_ref,
                      x1_ref, hx2w_ref, idx_ref, w_ref,
                      uext_ref, a_ref, b_ref, carry_ref):
    del after_ref
    j = pl.program_id(1)
    zeros8 = jnp.zeros((SUBLANES, D_MODEL), F32)

    @pl.when(j == 0)
    def _():
        carry_ref[...] = jnp.broadcast_to(h0_ref[1:2, :], (SUBLANES, D_MODEL))
        uext_ref[TM:TM + SUBLANES, :] = zeros8

    ur = urnn_ref[...].astype(F32)
    uext_ref[0:TM, :] = ur
    ue = uext_ref[...]
    v = bsc_ref[...] + wsc_ref[0:1, :] * ur
    for k in range(1, SHORT_CONV):
        v = v + wsc_ref[k:k + 1, :] * pltpu.roll(ue, TM + SUBLANES - k, 0)[0:TM, :]
    uext_ref[TM:TM + SUBLANES, :] = uext_ref[0:SUBLANES, :]
    _rglru_coeffs(v, wa_ref, ba_ref[...], wx_ref, bx_ref[...], _log_decay(lam_ref[...]), a_ref, b_ref)
    carry_ref[...] = _scan_tile(a_ref, b_ref, carry_ref[...], TM, reverse=True)

    h_sum = hf_ref[...].astype(F32) + b_ref[...]
    y_b = _dot((gbr_ref[...].astype(F32) * h_sum).astype(BF16), wro_ref[...])
    mix = gaya_ref[...].astype(F32) + gb_ref[...].astype(F32) * y_b
    out = _dot(mix.astype(BF16), wout_ref[...])
    x1 = x_ref[...] + mod_ref[2:3, :] * _rms_norm(out, post1_ref[...])
    x1_ref[...] = x1

    hx2 = _rms_norm(x1, pre2_ref[...]) * (1.0 + mod_ref[4:5, :]) + mod_ref[3:4, :]
    hx2w_ref[...] = _pack_bf16_pairs(hx2)
    logits_t = lax.dot_general(wrt_ref[...], hx2, (((1,), (1,)), ((), ())),
                               preferred_element_type=F32, precision=lax.Precision.HIGHEST)
    idx, w = _route(logits_t, rbias_ref[...])
    idx_ref[...] = idx
    w_ref[...] = w


def _mixer_bwd_call(b0, bsz, after, urnn, hf, gbr, gaya, gb, x, mod_x, w_sc, b_sc, wa_bf, b_rg_a, wx_bf, b_rg_x, lam, h0,
                    wro_bf, wout_bf, post1_g, pre2_g, w_router_t, router_bias):
    seq = x.shape[1]
    nt = seq // TM
    rev = lambda b, j: (b, nt - 1 - j, 0)
    tile = pl.BlockSpec((None, TM, D_MODEL), rev)
    head_w = pl.BlockSpec((None, RNN_HEADS, RNN_BLOCK, RNN_BLOCK), lambda b, j: (1, 0, 0, 0),
                          pipeline_mode=pl.Buffered(1))
    dir_row = pl.BlockSpec((None, 1, D_MODEL), lambda b, j: (1, 0, 0), pipeline_mode=pl.Buffered(1))
    return pl.pallas_call(
        _mixer_bwd_kernel,
        grid=(bsz, nt),
        in_specs=[
            pl.BlockSpec(memory_space=pl.ANY),
            tile, tile, tile, tile, tile,
            pl.BlockSpec((None, TM, D_MODEL), lambda b, j: (b0 + b, nt - 1 - j, 0)),
            pl.BlockSpec((None, 6, D_MODEL), lambda b, j: (b0 + b, 0, 0)),
            pl.BlockSpec((None, SHORT_CONV, D_MODEL), lambda b, j: (1, 0, 0), pipeline_mode=pl.Buffered(1)),
            dir_row, head_w, dir_row, head_w, dir_row, dir_row,
            pl.BlockSpec((None, 2, D_MODEL), lambda b, j: (b0 + b, 0, 0)),
            _resident((D_MODEL, D_MODEL)),
            _resident((D_MODEL, D_MODEL)),
            _resident((1, D_MODEL)),
            _resident((1, D_MODEL)),
            _resident((N_EXPERTS, D_MODEL)),
            _resident((N_EXPERTS, 1)),
        ],
        out_specs=[
            pl.BlockSpec((None, TM, D_MODEL), rev),
            pl.BlockSpec((None, TM, ROW_WORDS), rev),
            pl.BlockSpec((TOP_K, TM), lambda b, j: (0, b * nt + nt - 1 - j)),
            pl.BlockSpec((TOP_K, TM), lambda b, j: (0, b * nt + nt - 1 - j)),
        ],
        out_shape=[
            jax.ShapeDtypeStruct((bsz, seq, D_MODEL), F32),
            jax.ShapeDtypeStruct((bsz, seq, ROW_WORDS), jnp.int32),
            jax.ShapeDtypeStruct((TOP_K, bsz * seq), jnp.int32),
            jax.ShapeDtypeStruct((TOP_K, bsz * seq), F32),
        ],
        scratch_shapes=[
            pltpu.VMEM((TM + SUBLANES, D_MODEL), F32),
            pltpu.VMEM((TM, D_MODEL), F32),
            pltpu.VMEM((TM, D_MODEL), F32),
            pltpu.VMEM((SUBLANES, D_MODEL), F32),
        ],
        compiler_params=pltpu.CompilerParams(
            dimension_semantics=("arbitrary", "arbitrary"), vmem_limit_bytes=VMEM_LIMIT),
        name="mixer_bwd",
    )(after, urnn, hf, gbr, gaya, gb, x, mod_x, w_sc, b_sc.reshape(2, 1, D_MODEL), wa_bf,
      b_rg_a.reshape(2, 1, D_MODEL), wx_bf, b_rg_x.reshape(2, 1, D_MODEL), lam.reshape(2, 1, D_MODEL), h0,
      wro_bf, wout_bf, post1_g, pre2_g, w_router_t, router_bias)


def _positions_kernel(after_ref, idx_ref, pos_ref, plan_ref):
    del after_ref
    n = idx_ref.shape[1]
    n_tiles = n // POS_TILE
    eidx = lax.broadcasted_iota(jnp.int32, (N_EXPERTS, POS_TILE), 0)

    def chosen(t):
        idx = idx_ref[:, pl.ds(pl.multiple_of(t * POS_TILE, POS_TILE), POS_TILE)]
        ch = jnp.zeros((N_EXPERTS, POS_TILE), F32)
        for k in range(TOP_K):
            ch = ch + jnp.where(eidx == idx[k:k + 1, :], 1.0, 0.0)
        return idx, ch

    def count_body(t, cnt):
        return cnt + jnp.sum(chosen(t)[1], axis=1, keepdims=True)

    cnt = lax.fori_loop(0, n_tiles, count_body, jnp.zeros((N_EXPERTS, 1), F32))
    padded = jnp.ceil(cnt * (1.0 / ROW_TILE)) * ROW_TILE
    r = lax.broadcasted_iota(jnp.int32, (N_EXPERTS, N_EXPERTS), 0)
    c = lax.broadcasted_iota(jnp.int32, (N_EXPERTS, N_EXPERTS), 1)
    off = jnp.dot(jnp.where(c < r, 1.0, 0.0), jnp.broadcast_to(padded, (N_EXPERTS, LANES)),
                  preferred_element_type=F32, precision=lax.Precision.HIGHEST)[:, 0:1]
    end = off + padded

    n_map = plan_ref.shape[1]
    tstart = lax.broadcasted_iota(jnp.int32, (N_EXPERTS, n_map), 1).astype(F32) * ROW_TILE
    te = jnp.minimum(jnp.sum(jnp.where(end <= tstart, 1, 0), axis=0, keepdims=True), N_EXPERTS - 1)
    emap = lax.broadcasted_iota(jnp.int32, (N_EXPERTS, n_map), 0)
    live_end = jnp.sum(jnp.where(emap == te, off + cnt, 0.0), axis=0, keepdims=True)
    total = jnp.sum(padded, axis=0, keepdims=True)
    plan_ref[0:1, :] = te
    plan_ref[1:2, :] = jnp.clip(live_end - tstart[0:1, :], 0.0, ROW_TILE).astype(jnp.int32)
    plan_ref[2:3, :] = jnp.broadcast_to(total * (1.0 / ROW_TILE), (1, n_map)).astype(jnp.int32)

    row = lax.broadcasted_iota(jnp.int32, (POS_TILE, POS_TILE), 0)
    col = lax.broadcasted_iota(jnp.int32, (POS_TILE, POS_TILE), 1)
    before = jnp.where(row < col, 1.0, 0.0).astype(BF16)

    def pos_body(t, carry):
        idx, ch = chosen(t)
        base = _dot(ch.astype(BF16), before) + (carry + off)
        rows = [jnp.sum(jnp.where(eidx == idx[k:k + 1, :], base, 0.0), axis=0, keepdims=True)
                for k in range(TOP_K)]
        pos_ref[:, pl.ds(pl.multiple_of(t * POS_TILE, POS_TILE), POS_TILE)] = (
            jnp.concatenate(rows, axis=0).astype(jnp.int32))
        return carry + jnp.sum(ch, axis=1, keepdims=True)

    lax.fori_loop(0, n_tiles, pos_body, jnp.zeros((N_EXPERTS, 1), F32))


def _positions_call(after, idx_t, n_row_tiles):
    n = idx_t.shape[1]
    n_map = -(-n_row_tiles // LANES) * LANES
    return pl.pallas_call(
        _positions_kernel,
        in_specs=[pl.BlockSpec(memory_space=pl.ANY), pl.BlockSpec(memory_space=pltpu.VMEM)],
        out_shape=[
            jax.ShapeDtypeStruct((TOP_K, n), jnp.int32),
            jax.ShapeDtypeStruct((3, n_map), jnp.int32),
        ],
        name="moe_positions",
    )(after, idx_t)


def _sc_mesh():
    return plsc.VectorSubcoreMesh(core_axis_name="c", subcore_axis_name="s",
                                  num_cores=V7X_SC_CORES, num_subcores=V7X_SC_SUBCORES)


def _sc_worker():
    return lax.axis_index("s") * V7X_SC_CORES + lax.axis_index("c")


def _dispatch_call(rows, pos, n_slots):
    n = rows.shape[0]
    tok_w = n // SC_WORKERS
    n_items = tok_w // SC_CHUNK

    def body(rows_hbm, pos_hbm, xs_hbm, idx_v, rows_v, lsem, ssem):
        wid = _sc_worker()
        pltpu.sync_copy(pos_hbm.at[wid], idx_v)
        base = wid * tok_w

        def load(i):
            b = i % SC_BUFFERS
            return pltpu.async_copy(rows_hbm.at[pl.ds(base + i * SC_CHUNK, SC_CHUNK)], rows_v.at[b], lsem.at[b])

        def scatter(i):
            b = i % SC_BUFFERS
            return [pltpu.async_copy(rows_v.at[b], xs_hbm.at[idx_v.at[i, k]], ssem.at[b]) for k in range(TOP_K)]

        loads = {i: load(i) for i in range(SC_BUFFERS - 1)}
        scat = {}
        for i in range(n_items):
            loads[i].wait()
            scat[i] = scatter(i)
            if i >= 1:
                for cp in scat[i - 1]:
                    cp.wait()
            if i + SC_BUFFERS - 1 < n_items:
                loads[i + SC_BUFFERS - 1] = load(i + SC_BUFFERS - 1)
        for cp in scat[n_items - 1]:
            cp.wait()

    return pl.kernel(
        body, mesh=_sc_mesh(),
        out_type=jax.ShapeDtypeStruct((n_slots, ROW_WORDS), jnp.int32),
        scratch_types=[pltpu.VMEM((n_items, TOP_K, SC_CHUNK), jnp.int32),
                       pltpu.VMEM((SC_BUFFERS, SC_CHUNK, ROW_WORDS), jnp.int32),
                       pltpu.SemaphoreType.DMA((SC_BUFFERS,)), pltpu.SemaphoreType.DMA((SC_BUFFERS,))],
        compiler_params=pltpu.CompilerParams(use_tc_tiling_on_sc=True),
        name="moe_dispatch",
    )(rows, pos)


def _collect_call(ys, pos, n):
    tok_w = n // SC_WORKERS
    n_chunks = tok_w // SC_CHUNK
    items = [(c, k) for c in range(n_chunks) for k in range(TOP_K)]

    def body(ys_hbm, pos_hbm, yt_hbm, idx_v, rows_v, gsem, wsem):
        wid = _sc_worker()
        pltpu.sync_copy(pos_hbm.at[wid], idx_v)
        base = wid * tok_w

        def gather(j):
            c, k = items[j]
            b = j % SC_BUFFERS
            return pltpu.async_copy(ys_hbm.at[idx_v.at[c, k]], rows_v.at[b], gsem.at[b])

        def write(j):
            c, k = items[j]
            b = j % SC_BUFFERS
            return pltpu.async_copy(rows_v.at[b], yt_hbm.at[k, pl.ds(base + c * SC_CHUNK, SC_CHUNK)], wsem.at[b])

        g = {j: gather(j) for j in range(SC_BUFFERS - 1)}
        w = {}
        for j in range(len(items)):
            g[j].wait()
            w[j] = write(j)
            if j >= 1:
                w[j - 1].wait()
            if j + SC_BUFFERS - 1 < len(items):
                g[j + SC_BUFFERS - 1] = gather(j + SC_BUFFERS - 1)
        w[len(items) - 1].wait()

    return pl.kernel(
        body, mesh=_sc_mesh(),
        out_type=jax.ShapeDtypeStruct((TOP_K, n, ROW_WORDS), jnp.int32),
        scratch_types=[pltpu.VMEM((n_chunks, TOP_K, SC_CHUNK), jnp.int32),
                       pltpu.VMEM((SC_BUFFERS, SC_CHUNK, ROW_WORDS), jnp.int32),
                       pltpu.SemaphoreType.DMA((SC_BUFFERS,)), pltpu.SemaphoreType.DMA((SC_BUFFERS,))],
        compiler_params=pltpu.CompilerParams(use_tc_tiling_on_sc=True),
        name="moe_collect",
    )(ys, pos)


def _expert_gemm_kernel(plan_ref, after_ref, xs_ref, wg_ref, wu_ref, wd_ref, ys_ref, wgu_scr, wd_scr):
    del after_ref
    i = pl.program_id(0)
    e = plan_ref[0, i]
    prev = plan_ref[0, jnp.maximum(i - 1, 0)]
    live = plan_ref[1, i]

    @pl.when((i == 0) | (e != prev))
    def _():
        wgu_scr[:, 0:D_EXPERT] = wg_ref[...].astype(BF16)
        wgu_scr[:, D_EXPERT:2 * D_EXPERT] = wu_ref[...].astype(BF16)
        wd_scr[...] = wd_ref[...].astype(BF16)

    def swiglu_rows(n_rows):
        lo, hi = _unpack_bf16_pairs(xs_ref[0:n_rows, :])
        gu = _dot(lo.astype(BF16), wgu_scr[0:ROW_WORDS, :]) + _dot(hi.astype(BF16), wgu_scr[ROW_WORDS:D_MODEL, :])
        h = _silu(gu[:, 0:D_EXPERT]) * gu[:, D_EXPERT:2 * D_EXPERT]
        y = _dot(h.astype(BF16), wd_scr[...])
        ys_ref[0:n_rows, :] = _pack_bf16_pairs_native(y)

    @pl.when((i < plan_ref[2, 0]) & (live > ROW_TILE // 2))
    def _():
        swiglu_rows(ROW_TILE)

    @pl.when((i < plan_ref[2, 0]) & (live <= ROW_TILE // 2))
    def _():
        swiglu_rows(ROW_TILE // 2)


def _expert_gemm_call(after, plan, xs, w_e_gate, w_e_up, w_e_down):
    n_slots = xs.shape[0]
    n_row_tiles = n_slots // ROW_TILE
    rows = pl.BlockSpec((ROW_TILE, ROW_WORDS), lambda i, plan: (jnp.minimum(i, plan[2, 0] - 1), 0))
    expert = lambda i, plan: (plan[0, i], 0, 0)
    return pl.pallas_call(
        _expert_gemm_kernel,
        grid_spec=pltpu.PrefetchScalarGridSpec(
            num_scalar_prefetch=1,
            grid=(n_row_tiles,),
            in_specs=[
                pl.BlockSpec(memory_space=pl.ANY),
                rows,
                pl.BlockSpec((None, D_MODEL, D_EXPERT), expert),
                pl.BlockSpec((None, D_MODEL, D_EXPERT), expert),
                pl.BlockSpec((None, D_EXPERT, D_MODEL), expert),
            ],
            out_specs=rows,
            scratch_shapes=[pltpu.VMEM((D_MODEL, 2 * D_EXPERT), BF16), pltpu.VMEM((D_EXPERT, D_MODEL), BF16)],
        ),
        out_shape=jax.ShapeDtypeStruct((n_slots, ROW_WORDS), jnp.int32),
        compiler_params=pltpu.CompilerParams(dimension_semantics=("arbitrary",), vmem_limit_bytes=VMEM_LIMIT),
        name="moe_experts",
    )(plan, after, xs, w_e_gate, w_e_up, w_e_down)


def _moe_out_kernel(yt_ref, w_ref, t_ref, x1_ref, mod_ref, post2_ref, wsg_ref, wsu_ref, wsd_ref, o_ref):
    lo, hi = _unpack_bf16_pairs(t_ref[...])
    lo = lo.astype(BF16)
    hi = hi.astype(BF16)
    g = _dot(lo, wsg_ref[0:ROW_WORDS, :]) + _dot(hi, wsg_ref[ROW_WORDS:D_MODEL, :])
    u = _dot(lo, wsu_ref[0:ROW_WORDS, :]) + _dot(hi, wsu_ref[ROW_WORDS:D_MODEL, :])
    shared = _dot((_silu(g) * u).astype(BF16), wsd_ref[...])
    acc_lo = shared[:, 0:ROW_WORDS]
    acc_hi = shared[:, ROW_WORDS:D_MODEL]
    for k in range(TOP_K):
        y_lo, y_hi = _unpack_bf16_pairs(yt_ref[k])
        wk = w_ref[:, k:k + 1]
        acc_lo = acc_lo + wk * y_lo
        acc_hi = acc_hi + wk * y_hi
    moe = jnp.concatenate([acc_lo, acc_hi], axis=-1)
    o_ref[...] = x1_ref[...] + mod_ref[5:6, :] * _rms_norm(moe, post2_ref[...])


def _moe_out_into_kernel(prev_ref, *refs):
    del prev_ref
    _moe_out_kernel(*refs)


def _moe_out_call(prev_out, n_total, b0, yt, w, t, x1, mod_x, post2_g, wsg_bf, wsu_bf, wsd_bf, seq):
    n = t.shape[0]
    tiles_per_seq = seq // TOUT
    first_tile = b0 * tiles_per_seq
    const = lambda i: (0, 0)
    in_specs = [
        pl.BlockSpec((TOP_K, TOUT, ROW_WORDS), lambda i: (0, i, 0)),
        pl.BlockSpec((TOUT, TOP_K), lambda i: (i, 0)),
        pl.BlockSpec((TOUT, ROW_WORDS), lambda i: (i, 0)),
        pl.BlockSpec((TOUT, D_MODEL), lambda i: (i, 0)),
        pl.BlockSpec((None, 6, D_MODEL), lambda i: (b0 + i // tiles_per_seq, 0, 0)),
        pl.BlockSpec((1, D_MODEL), const),
        pl.BlockSpec((D_MODEL, D_EXPERT), const),
        pl.BlockSpec((D_MODEL, D_EXPERT), const),
        pl.BlockSpec((D_EXPERT, D_MODEL), const),
    ]
    args = (yt, w, t, x1, mod_x, post2_g, wsg_bf, wsu_bf, wsd_bf)
    aliased = prev_out is not None
    return pl.pallas_call(
        _moe_out_into_kernel if aliased else _moe_out_kernel,
        grid=(n // TOUT,),
        in_specs=([pl.BlockSpec(memory_space=pl.ANY)] if aliased else []) + in_specs,
        out_specs=pl.BlockSpec((TOUT, D_MODEL), lambda i: (first_tile + i, 0)),
        out_shape=jax.ShapeDtypeStruct((n_total, D_MODEL), F32),
        input_output_aliases={0: 0} if aliased else {},
        compiler_params=pltpu.CompilerParams(dimension_semantics=("arbitrary",), vmem_limit_bytes=VMEM_LIMIT),
        name="moe_out",
    )(*(((prev_out,) if aliased else ()) + args))


def _moe_dispatch(after, hx2w, idx_t):
    n = hx2w.shape[0]
    n_slots = n * TOP_K + N_EXPERTS * ROW_TILE
    pos_t, plan = _positions_call(after, idx_t, n_slots // ROW_TILE)
    pos = pos_t.reshape(TOP_K, SC_WORKERS, n // (SC_WORKERS * SC_CHUNK), SC_CHUNK).transpose(1, 2, 0, 3)
    return _dispatch_call(hx2w, pos, n_slots), pos, plan


def kernel(x, c, ctx, c_ctx, w_mod, b_mod, pre1_g, post1_g, pre2_g, post2_g, w_in, w_dw, b_dw, ln_conv_g, ln_conv_b, w_conv_out, w_sc, b_sc, w_rg_a, b_rg_a, w_rg_x, b_rg_x, lru_lambda, w_rnn_out, w_out, w_router, router_bias, w_e_gate, w_e_up, w_e_down, w_s_gate, w_s_up, w_s_down):
    assert w_mod.shape[0] == 1, "single-layer block"
    bsz, seq, d = x.shape
    n = bsz * seq
    assert d == D_MODEL and seq % TM == 0 and seq % TOUT == 0 and bsz + 1 <= SUBLANES
    assert TOKEN_GROUPS == 2 and bsz % TOKEN_GROUPS == 0 and (n // TOKEN_GROUPS) % (SC_WORKERS * SC_CHUNK) == 0
    assert (n // TOKEN_GROUPS) % POS_TILE == 0

    cc = jnp.zeros((SUBLANES, D_MODEL), F32).at[:bsz].set(c).at[bsz].set(c_ctx)
    mod = _mod_call(cc, w_mod[0], b_mod)
    mod_x = mod[:bsz].reshape(bsz, 6, D_MODEL)
    mod_c = mod[bsz].reshape(6, D_MODEL)

    w_in_bf = w_in[0].astype(BF16)
    wa_bf = w_rg_a[0].astype(BF16)
    wx_bf = w_rg_x[0].astype(BF16)

    h0 = _ctx_call(ctx, mod_c, pre1_g, w_in_bf, w_sc[0], b_sc[0], wa_bf, b_rg_a[0], wx_bf, b_rg_x[0],
                   lru_lambda[0])

    wco_bf, wro_bf, wout_bf = w_conv_out[0].astype(BF16), w_rnn_out[0].astype(BF16), w_out[0].astype(BF16)
    wsg_bf, wsu_bf, wsd_bf = w_s_gate[0].astype(BF16), w_s_up[0].astype(BF16), w_s_down[0].astype(BF16)
    w_router_t, rbias = w_router[0].T, router_bias.reshape(N_EXPERTS, 1)
    gsz = bsz // TOKEN_GROUPS
    gn = gsz * seq
    none = jnp.zeros((TOP_K, LANES), jnp.int32)

    def fwd(g, after):
        return _mixer_fwd_call(g * gsz, gsz, after, x, mod_x, pre1_g, w_in_bf, w_dw[0], b_dw, ln_conv_g,
                               ln_conv_b, wco_bf, w_sc[0], b_sc[0], wa_bf, b_rg_a[0], wx_bf, b_rg_x[0],
                               lru_lambda[0], h0)

    def bwd(g, after, acts):
        gaya, gb, gbr, urnn, hf = acts
        x1, hx2w, idx_t, w_t = _mixer_bwd_call(
            g * gsz, gsz, after, urnn, hf, gbr, gaya, gb, x, mod_x, w_sc[0], b_sc[0], wa_bf, b_rg_a[0], wx_bf,
            b_rg_x[0], lru_lambda[0], h0, wro_bf, wout_bf, post1_g, pre2_g, w_router_t, rbias)
        return x1.reshape(gn, D_MODEL), hx2w.reshape(gn, ROW_WORDS), idx_t, w_t

    def experts(after, plan, xs):
        return _expert_gemm_call(after, plan, xs, w_e_gate[0], w_e_up[0], w_e_down[0])

    def finish(g, prev_out, yt, w_t, hx2w, x1):
        return _moe_out_call(prev_out, n, g * gsz, yt, w_t.T, hx2w, x1, mod_x, post2_g, wsg_bf, wsu_bf, wsd_bf, seq)

    x1_a, hx2w_a, idx_a, w_a = bwd(0, none, fwd(0, none))
    xs_a, pos_a, plan_a = _moe_dispatch(none, hx2w_a, idx_a)
    acts_b = fwd(1, idx_a)
    ys_a = experts(acts_b[4], plan_a, xs_a)
    yt_a = _collect_call(ys_a, pos_a, gn)
    x1_b, hx2w_b, idx_b, w_b = bwd(1, ys_a, acts_b)
    xs_b, pos_b, plan_b = _moe_dispatch(yt_a, hx2w_b, idx_b)
    out = finish(0, None, yt_a, w_a, hx2w_a, x1_a)
    ys_b = experts(none, plan_b, xs_b)
    yt_b = _collect_call(ys_b, pos_b, gn)
    out = finish(1, out, yt_b, w_b, hx2w_b, x1_b)
    return out.reshape(bsz, seq, D_MODEL)
```

```python
import jax
import jax.numpy as jnp
from jax import lax
from jax.experimental import pallas as pl
from jax.experimental.pallas import tpu as pltpu
from jax.experimental.pallas import tpu_sc as plsc

F32 = jnp.float32
BF16 = jnp.bfloat16

D_MODEL = 1024
GRID_W = 64
CONV_WIDTH = 31
CONV_HALF = CONV_WIDTH // 2
SHORT_CONV = 4
RNN_HEADS = 4
RNN_BLOCK = D_MODEL // RNN_HEADS
LRU_C = 8.0
N_EXPERTS = 64
N_GROUPS = 8
GROUP_SIZE = N_EXPERTS // N_GROUPS
TOPK_GROUPS = 4
TOP_K = 8
D_EXPERT = 256
ROUTED_SCALE = 2.5
EPS = 1e-6

SUBLANES = 8
TM = 512
ROWS_PER_TILE = TM // GRID_W
PAD = 16
ROW_STRIDE = GRID_W + PAD
UPAD_ROWS = ROWS_PER_TILE * ROW_STRIDE + PAD
CONV_LANES = 256
LANES = 128
ROW_WORDS = D_MODEL // 2
ROW_TILE = 1024
POS_TILE = 512
TOUT = 256
TOKEN_GROUPS = 2
V7X_SC_CORES = 2
V7X_SC_SUBCORES = 16
SC_WORKERS = V7X_SC_CORES * V7X_SC_SUBCORES
SC_CHUNK = 64
SC_BUFFERS = 2
VMEM_LIMIT = 58 * 1024 * 1024


def _sigmoid(x):
    return 0.5 * (jnp.tanh(0.5 * x) + 1.0)


def _silu(x):
    return x * _sigmoid(x)


def _gelu_tanh_bf16(x):
    inner = x * (0.7978845608028654 + (0.7978845608028654 * 0.044715) * (x * x))
    return (0.5 * x.astype(BF16)) * (1.0 + jnp.tanh(inner.astype(BF16)))


def _rms_norm(x, g):
    return x * lax.rsqrt(jnp.mean(x * x, axis=-1, keepdims=True) + EPS) * g


def _dot(a, b):
    return jnp.dot(a, b, preferred_element_type=F32)


def _pack_bf16_pairs(x):
    half = x.shape[-1] // 2
    lo = lax.bitcast_convert_type(x[:, :half].astype(BF16).astype(F32), jnp.uint32)
    hi = lax.bitcast_convert_type(x[:, half:].astype(BF16).astype(F32), jnp.uint32)
    return lax.bitcast_convert_type(hi | (lo >> 16), jnp.int32)


def _pack_bf16_pairs_native(x):
    half = x.shape[-1] // 2
    packed = pltpu.pack_elementwise([x[:, :half], x[:, half:]], packed_dtype=BF16)
    return lax.bitcast_convert_type(packed, jnp.int32)


def _unpack_bf16_pairs(words):
    u = lax.bitcast_convert_type(words, jnp.uint32)
    return (lax.bitcast_convert_type(u << 16, F32),
            lax.bitcast_convert_type(u & jnp.uint32(0xFFFF0000), F32))


def _log_decay(lam):
    return LRU_C * (jnp.minimum(lam, 0.0) - jnp.log1p(jnp.exp(-jnp.abs(lam))))


def _rglru_coeffs(v, wa_ref, ba, wx_ref, bx, c_lam, a_ref, b_ref):
    vb = v.astype(BF16)
    for h in range(RNN_HEADS):
        cs = slice(h * RNN_BLOCK, (h + 1) * RNN_BLOCK)
        vh = vb[:, cs]
        r = _sigmoid(_dot(vh, wa_ref[h]) + ba[:, cs])
        i = _sigmoid(_dot(vh, wx_ref[h]) + bx[:, cs])
        log_a = c_lam[:, cs] * r
        a = jnp.exp(log_a)
        a_ref[:, cs] = a
        b_ref[:, cs] = jnp.sqrt(jnp.tanh(-log_a) * (1.0 + a * a)) * (i * v[:, cs])


def _scan_tile(a_ref, b_ref, carry, n_rows, reverse):
    row = lax.broadcasted_iota(jnp.int32, (SUBLANES, D_MODEL), 0)
    n_groups = n_rows // SUBLANES

    def body(j, carry):
        g = (n_groups - 1 - j) if reverse else j
        off = pl.multiple_of(g * SUBLANES, SUBLANES)
        a = a_ref[pl.ds(off, SUBLANES), :]
        b = b_ref[pl.ds(off, SUBLANES), :]
        for s in (1, 2, 4):
            keep = (row < SUBLANES - s) if reverse else (row >= s)
            shift = (SUBLANES - s) if reverse else s
            a_sh = jnp.where(keep, pltpu.roll(a, shift, 0), 1.0)
            b_sh = jnp.where(keep, pltpu.roll(b, shift, 0), 0.0)
            b = a * b_sh + b
            a = a * a_sh
        h = a * carry + b
        b_ref[pl.ds(off, SUBLANES), :] = h
        last = h[0:1, :] if reverse else h[SUBLANES - 1:SUBLANES, :]
        return jnp.broadcast_to(last, (SUBLANES, D_MODEL))

    return lax.fori_loop(0, n_groups, body, carry, unroll=2)


def _mod_kernel(c_ref, w_ref, b_ref, o_ref):
    o_ref[...] = jnp.dot(_silu(c_ref[...]), w_ref[...], preferred_element_type=F32,
                         precision=lax.Precision.HIGHEST) + b_ref[...]


def _mod_call(cc, w_mod, b_mod):
    n_mod = w_mod.shape[1]
    return pl.pallas_call(
        _mod_kernel,
        grid=(n_mod // D_MODEL,),
        in_specs=[
            pl.BlockSpec((SUBLANES, D_MODEL), lambda j: (0, 0)),
            pl.BlockSpec((D_MODEL, D_MODEL), lambda j: (0, j)),
            pl.BlockSpec((1, D_MODEL), lambda j: (0, j)),
        ],
        out_specs=pl.BlockSpec((SUBLANES, D_MODEL), lambda j: (0, j)),
        out_shape=jax.ShapeDtypeStruct((SUBLANES, n_mod), F32),
        name="mod",
    )(cc, w_mod, b_mod)


def _ctx_kernel(ctx_ref, mod_ref, g_ref, w_ref, wsc_ref, bsc_ref, wa_ref, ba_ref, wx_ref, bx_ref,
                lam_ref, o_ref, uext_ref, a_ref, b_ref):
    n = ctx_ref.shape[0]
    hc = _rms_norm(ctx_ref[...], g_ref[...]) * (1.0 + mod_ref[1:2, :]) + mod_ref[0:1, :]
    u = _dot(hc.astype(BF16), w_ref[...])
    zeros8 = jnp.zeros((SUBLANES, D_MODEL), F32)
    uext_ref[0:SUBLANES, :] = zeros8
    uext_ref[SUBLANES:SUBLANES + n, :] = u
    uext_ref[SUBLANES + n:2 * SUBLANES + n, :] = zeros8
    for d in range(2):
        v = jnp.broadcast_to(bsc_ref[d:d + 1, :], (n, D_MODEL))
        for k in range(SHORT_CONV):
            start = SUBLANES + k - (SHORT_CONV - 1) * (1 - d)
            v = v + wsc_ref[d, k:k + 1, :] * uext_ref[start:start + n, :]
        _rglru_coeffs(v, wa_ref.at[d], ba_ref[d:d + 1, :], wx_ref.at[d], bx_ref[d:d + 1, :],
                      _log_decay(lam_ref[d:d + 1, :]), a_ref, b_ref)
        final = _scan_tile(a_ref, b_ref, zeros8, n, reverse=(d == 1))
        o_ref[d:d + 1, :] = final[0:1, :]


def _ctx_call(ctx, mod_c, pre1_g, w_in_bf, w_sc, b_sc, wa_bf, b_rg_a, wx_bf, b_rg_x, lam):
    bsz, n, _ = ctx.shape
    const2 = lambda b: (0, 0)
    const3 = lambda b: (0, 0, 0)
    const4 = lambda b: (0, 0, 0, 0)
    return pl.pallas_call(
        _ctx_kernel,
        grid=(bsz,),
        in_specs=[
            pl.BlockSpec((None, n, D_MODEL), lambda b: (b, 0, 0)),
            pl.BlockSpec((6, D_MODEL), const2),
            pl.BlockSpec((1, D_MODEL), const2),
            pl.BlockSpec((D_MODEL, D_MODEL), lambda b: (0, 2)),
            pl.BlockSpec((2, SHORT_CONV, D_MODEL), const3),
            pl.BlockSpec((2, D_MODEL), const2),
            pl.BlockSpec((2, RNN_HEADS, RNN_BLOCK, RNN_BLOCK), const4),
            pl.BlockSpec((2, D_MODEL), const2),
            pl.BlockSpec((2, RNN_HEADS, RNN_BLOCK, RNN_BLOCK), const4),
            pl.BlockSpec((2, D_MODEL), const2),
            pl.BlockSpec((2, D_MODEL), const2),
        ],
        out_specs=pl.BlockSpec((None, 2, D_MODEL), lambda b: (b, 0, 0)),
        out_shape=jax.ShapeDtypeStruct((bsz, 2, D_MODEL), F32),
        scratch_shapes=[
            pltpu.VMEM((n + 2 * SUBLANES, D_MODEL), F32),
            pltpu.VMEM((n, D_MODEL), F32),
            pltpu.VMEM((n, D_MODEL), F32),
        ],
        name="ctx",
    )(ctx, mod_c, pre1_g, w_in_bf, w_sc, b_sc, wa_bf, b_rg_a, wx_bf, b_rg_x, lam)


def _mixer_fwd_kernel(after_ref, x_ref, mod_ref, g_ref, win_ref, wdw_ref, bdw_ref, lng_ref, lnb_ref, wco_ref,
                      wsc_ref, bsc_ref, wa_ref, ba_ref, wx_ref, bx_ref, lam_ref, h0_ref,
                      gaya_ref, gb_ref, gbr_ref, urnn_ref, hf_ref,
                      upad_ref, ush_ref, wb_ref, cv_ref, uext_ref, a_ref, b_ref, carry_ref):
    del after_ref
    j = pl.program_id(1)
    zeros8 = jnp.zeros((SUBLANES, D_MODEL), F32)

    @pl.when(j == 0)
    def _():
        carry_ref[...] = jnp.broadcast_to(h0_ref[0:1, :], (SUBLANES, D_MODEL))
        uext_ref[0:SUBLANES, :] = zeros8
        zeros_pad = jnp.zeros((PAD, D_MODEL), F32)
        for r in range(ROWS_PER_TILE + 1):
            upad_ref[r * ROW_STRIDE:r * ROW_STRIDE + PAD, :] = zeros_pad
        for k in range(CONV_WIDTH):
            wb_ref[k] = jnp.broadcast_to(wdw_ref[k:k + 1, :], (SUBLANES, D_MODEL))

    hx = (_rms_norm(x_ref[...], g_ref[...]) * (1.0 + mod_ref[1:2, :]) + mod_ref[0:1, :]).astype(BF16)

    u = _dot(hx, win_ref[:, 0:D_MODEL]) * _sigmoid(_dot(hx, win_ref[:, D_MODEL:2 * D_MODEL]))
    for r in range(ROWS_PER_TILE):
        upad_ref[PAD + r * ROW_STRIDE:PAD + r * ROW_STRIDE + GRID_W, :] = u[r * GRID_W:(r + 1) * GRID_W, :]
    vregs_per_row = GRID_W // SUBLANES
    g_a = ur = None
    for c in range(D_MODEL // CONV_LANES):
        if c == 0:
            gb_ref[...] = _sigmoid(_dot(hx, win_ref[:, 5 * D_MODEL:6 * D_MODEL]).astype(BF16))
        elif c == 1:
            gbr_ref[...] = _gelu_tanh_bf16(_dot(hx, win_ref[:, 3 * D_MODEL:4 * D_MODEL]))
        elif c == 2:
            g_a = _sigmoid(_dot(hx, win_ref[:, 4 * D_MODEL:5 * D_MODEL]).astype(BF16))
        else:
            ur = _dot(hx, win_ref[:, 2 * D_MODEL:3 * D_MODEL])
            urnn_ref[...] = ur.astype(BF16)
            uext_ref[SUBLANES:SUBLANES + TM, :] = ur
        cs = slice(c * CONV_LANES, (c + 1) * CONV_LANES)
        xpad = upad_ref[:, cs]
        for s in range(1, SUBLANES):
            ush_ref[s - 1] = pltpu.roll(xpad, UPAD_ROWS - s, 0)
        for r in range(ROWS_PER_TILE):
            acc = jnp.broadcast_to(bdw_ref[:, cs].reshape(1, 1, CONV_LANES), (vregs_per_row, SUBLANES, CONV_LANES))
            for k in range(CONV_WIDTH):
                q, s = divmod(r * ROW_STRIDE + PAD - CONV_HALF + k, SUBLANES)
                rows = slice(q * SUBLANES, q * SUBLANES + GRID_W)
                win = upad_ref[rows, cs] if s == 0 else ush_ref[s - 1, rows, :]
                acc = acc + wb_ref[k, :, cs] * win.reshape(vregs_per_row, SUBLANES, CONV_LANES)
            cv_ref[r * GRID_W:(r + 1) * GRID_W, cs] = acc.reshape(GRID_W, CONV_LANES)
    cv = cv_ref[...]
    cvc = cv - jnp.mean(cv, axis=-1, keepdims=True)
    cvn = cvc * lax.rsqrt(jnp.mean(cvc * cvc, axis=-1, keepdims=True) + EPS) * lng_ref[...] + lnb_ref[...]
    y_a = _dot(_silu(cvn.astype(BF16)), wco_ref[...])

    gaya_ref[...] = g_a * y_a.astype(BF16)

    ue = uext_ref[...]
    v = bsc_ref[...] + wsc_ref[SHORT_CONV - 1:SHORT_CONV, :] * ur
    for k in range(SHORT_CONV - 1):
        v = v + wsc_ref[k:k + 1, :] * pltpu.roll(ue, SHORT_CONV - 1 - k, 0)[SUBLANES:SUBLANES + TM, :]
    uext_ref[0:SUBLANES, :] = uext_ref[TM:TM + SUBLANES, :]
    _rglru_coeffs(v, wa_ref, ba_ref[...], wx_ref, bx_ref[...], _log_decay(lam_ref[...]), a_ref, b_ref)
    carry_ref[...] = _scan_tile(a_ref, b_ref, carry_ref[...], TM, reverse=False)
    hf_ref[...] = b_ref[...].astype(BF16)


def _resident(shape):
    nd = len(shape)
    return pl.BlockSpec(shape, lambda b, j: (0,) * nd, pipeline_mode=pl.Buffered(1))


def _mixer_fwd_call(b0, bsz, after, x, mod_x, pre1_g, w_in_bf, w_dw, b_dw, ln_g, ln_b, wco_bf,
                    w_sc, b_sc, wa_bf, b_rg_a, wx_bf, b_rg_x, lam, h0):
    seq = x.shape[1]
    nt = seq // TM
    tile = pl.BlockSpec((None, TM, D_MODEL), lambda b, j: (b, j, 0))
    act = jax.ShapeDtypeStruct((bsz, seq, D_MODEL), BF16)
    head_w = pl.BlockSpec((None, RNN_HEADS, RNN_BLOCK, RNN_BLOCK), lambda b, j: (0, 0, 0, 0),
                          pipeline_mode=pl.Buffered(1))
    dir_row = pl.BlockSpec((None, 1, D_MODEL), lambda b, j: (0, 0, 0), pipeline_mode=pl.Buffered(1))
    return pl.pallas_call(
        _mixer_fwd_kernel,
        grid=(bsz, nt),
        in_specs=[
            pl.BlockSpec(memory_space=pl.ANY),
            pl.BlockSpec((None, TM, D_MODEL), lambda b, j: (b0 + b, j, 0)),
            pl.BlockSpec((None, 6, D_MODEL), lambda b, j: (b0 + b, 0, 0)),
            _resident((1, D_MODEL)),
            _resident((D_MODEL, 6 * D_MODEL)),
            _resident((CONV_WIDTH, D_MODEL)),
            _resident((1, D_MODEL)),
            _resident((1, D_MODEL)),
            _resident((1, D_MODEL)),
            _resident((D_MODEL, D_MODEL)),
            pl.BlockSpec((None, SHORT_CONV, D_MODEL), lambda b, j: (0, 0, 0), pipeline_mode=pl.Buffered(1)),
            dir_row, head_w, dir_row, head_w, dir_row, dir_row,
            pl.BlockSpec((None, 2, D_MODEL), lambda b, j: (b0 + b, 0, 0)),
        ],
        out_specs=[tile] * 5,
        out_shape=[act] * 5,
        scratch_shapes=[
            pltpu.VMEM((UPAD_ROWS, D_MODEL), F32),
            pltpu.VMEM((SUBLANES - 1, UPAD_ROWS, CONV_LANES), F32),
            pltpu.VMEM((CONV_WIDTH, SUBLANES, D_MODEL), F32),
            pltpu.VMEM((TM, D_MODEL), F32),
            pltpu.VMEM((TM + SUBLANES, D_MODEL), F32),
            pltpu.VMEM((TM, D_MODEL), F32),
            pltpu.VMEM((TM, D_MODEL), F32),
            pltpu.VMEM((SUBLANES, D_MODEL), F32),
        ],
        compiler_params=pltpu.CompilerParams(
            dimension_semantics=("arbitrary", "arbitrary"), vmem_limit_bytes=VMEM_LIMIT),
        name="mixer_fwd",
    )(after, x, mod_x, pre1_g, w_in_bf, w_dw, b_dw, ln_g, ln_b, wco_bf,
      w_sc, b_sc.reshape(2, 1, D_MODEL), wa_bf, b_rg_a.reshape(2, 1, D_MODEL), wx_bf,
      b_rg_x.reshape(2, 1, D_MODEL), lam.reshape(2, 1, D_MODEL), h0)


def _route(logits_t, bias):
    t = logits_t.shape[1]
    scores = _sigmoid(logits_t)
    sel = scores + bias
    neg_inf = jnp.float32(-jnp.inf)

    sel3 = sel.reshape(N_GROUPS, GROUP_SIZE, t)
    within = lax.broadcasted_iota(jnp.int32, sel3.shape, 1)
    m1 = jnp.max(sel3, axis=1, keepdims=True)
    first = jnp.min(jnp.where(sel3 == m1, within, GROUP_SIZE), axis=1, keepdims=True)
    m2 = jnp.max(jnp.where(within == first, neg_inf, sel3), axis=1, keepdims=True)
    gscore = (m1 + m2).reshape(N_GROUPS, t)

    gidx = lax.broadcasted_iota(jnp.int32, gscore.shape, 0)
    rank = jnp.zeros(gscore.shape, jnp.int32)
    for g in range(N_GROUPS):
        other = gscore[g:g + 1, :]
        beats = jnp.where(other > gscore, 1, jnp.where((other == gscore) & (gidx > g), 1, 0))
        rank = rank + beats
    gkeep = (rank < TOPK_GROUPS).reshape(N_GROUPS, 1, t)
    masked = jnp.where(gkeep, sel3, neg_inf).reshape(N_EXPERTS, t)

    eidx = lax.broadcasted_iota(jnp.int32, masked.shape, 0)
    picks, weights = [], []
    for _ in range(TOP_K):
        m = jnp.max(masked, axis=0, keepdims=True)
        first = jnp.min(jnp.where(masked == m, eidx, N_EXPERTS), axis=0, keepdims=True)
        pick = eidx == first
        picks.append(first)
        weights.append(jnp.sum(jnp.where(pick, scores, 0.0), axis=0, keepdims=True))
        masked = jnp.where(pick, neg_inf, masked)
    idx = jnp.concatenate(picks, axis=0)
    w = jnp.concatenate(weights, axis=0)
    return idx, ROUTED_SCALE * w / jnp.sum(w, axis=0, keepdims=True)


def _mixer_bwd_kernel(after_ref, urnn_ref, hf_ref, gbr_ref, gaya_ref, gb_ref, x_ref, mod_ref,
                      wsc_ref, bsc_ref, wa_ref, ba_ref, wx_ref, bx_ref, lam_ref, h0_ref,
                      wro_ref, wout_ref, post1_ref, pre2_ref, wrt_ref, rbias_ref,
                      x1_ref, hx2w_ref, idx_ref, w_ref,
                      uext_ref, a_ref, b_ref, carry_ref):
    del after_ref
    j = pl.program_id(1)
    zeros8 = jnp.zeros((SUBLANES, D_MODEL), F32)

    @pl.when(j == 0)
    def _():
        carry_ref[...] = jnp.broadcast_to(h0_ref[1:2, :], (SUBLANES, D_MODEL))
        uext_ref[TM:TM + SUBLANES, :] = zeros8

    ur = urnn_ref[...].astype(F32)
    uext_ref[0:TM, :] = ur
    ue = uext_ref[...]
    v = bsc_ref[...] + wsc_ref[0:1, :] * ur
    for k in range(1, SHORT_CONV):
        v = v + wsc_ref[k:k + 1, :] * pltpu.roll(ue, TM + SUBLANES - k, 0)[0:TM, :]
    uext_ref[TM:TM + SUBLANES, :] = uext_ref[0:SUBLANES, :]
    _rglru_coeffs(v, wa_ref, ba_ref[...], wx_ref, bx_ref[...], _log_decay(lam_ref[...]), a_ref, b_ref)
    carry_ref[...] = _scan_tile(a_ref, b_ref, carry_ref[...], TM, reverse=True)

    h_sum = hf_ref[...] + b_ref[...].astype(BF16)
    y_b = _dot(gbr_ref[...] * h_sum, wro_ref[...])
    mix = gaya_ref[...] + gb_ref[...] * y_b.astype(BF16)
    out = _dot(mix, wout_ref[...])
    x1 = x_ref[...] + mod_ref[2:3, :] * _rms_norm(out, post1_ref[...])
    x1_ref[...] = x1

    hx2 = _rms_norm(x1, pre2_ref[...]) * (1.0 + mod_ref[4:5, :]) + mod_ref[3:4, :]
    hx2w_ref[...] = _pack_bf16_pairs(hx2)
    logits_t = lax.dot_general(wrt_ref[...], hx2, (((1,), (1,)), ((), ())),
                               preferred_element_type=F32, precision=lax.Precision.HIGHEST)
    idx, w = _route(logits_t, rbias_ref[...])
    idx_ref[...] = idx
    w_ref[...] = w


def _mixer_bwd_call(b0, bsz, after, urnn, hf, gbr, gaya, gb, x, mod_x, w_sc, b_sc, wa_bf, b_rg_a, wx_bf, b_rg_x, lam, h0,
                    wro_bf, wout_bf, post1_g, pre2_g, w_router_t, router_bias):
    seq = x.shape[1]
    nt = seq // TM
    rev = lambda b, j: (b, nt - 1 - j, 0)
    tile = pl.BlockSpec((None, TM, D_MODEL), rev)
    head_w = pl.BlockSpec((None, RNN_HEADS, RNN_BLOCK, RNN_BLOCK), lambda b, j: (1, 0, 0, 0),
                          pipeline_mode=pl.Buffered(1))
    dir_row = pl.BlockSpec((None, 1, D_MODEL), lambda b, j: (1, 0, 0), pipeline_mode=pl.Buffered(1))
    return pl.pallas_call(
        _mixer_bwd_kernel,
        grid=(bsz, nt),
        in_specs=[
            pl.BlockSpec(memory_space=pl.ANY),
            tile, tile, tile, tile, tile,
            pl.BlockSpec((None, TM, D_MODEL), lambda b, j: (b0 + b, nt - 1 - j, 0)),
            pl.BlockSpec((None, 6, D_MODEL), lambda b, j: (b0 + b, 0, 0)),
            pl.BlockSpec((None, SHORT_CONV, D_MODEL), lambda b, j: (1, 0, 0), pipeline_mode=pl.Buffered(1)),
            dir_row, head_w, dir_row, head_w, dir_row, dir_row,
            pl.BlockSpec((None, 2, D_MODEL), lambda b, j: (b0 + b, 0, 0)),
            _resident((D_MODEL, D_MODEL)),
            _resident((D_MODEL, D_MODEL)),
            _resident((1, D_MODEL)),
            _resident((1, D_MODEL)),
            _resident((N_EXPERTS, D_MODEL)),
            _resident((N_EXPERTS, 1)),
        ],
        out_specs=[
            pl.BlockSpec((None, TM, D_MODEL), rev),
            pl.BlockSpec((None, TM, ROW_WORDS), rev),
            pl.BlockSpec((TOP_K, TM), lambda b, j: (0, b * nt + nt - 1 - j)),
            pl.BlockSpec((TOP_K, TM), lambda b, j: (0, b * nt + nt - 1 - j)),
        ],
        out_shape=[
            jax.ShapeDtypeStruct((bsz, seq, D_MODEL), F32),
            jax.ShapeDtypeStruct((bsz, seq, ROW_WORDS), jnp.int32),
            jax.ShapeDtypeStruct((TOP_K, bsz * seq), jnp.int32),
            jax.ShapeDtypeStruct((TOP_K, bsz * seq), F32),
        ],
        scratch_shapes=[
            pltpu.VMEM((TM + SUBLANES, D_MODEL), F32),
            pltpu.VMEM((TM, D_MODEL), F32),
            pltpu.VMEM((TM, D_MODEL), F32),
            pltpu.VMEM((SUBLANES, D_MODEL), F32),
        ],
        compiler_params=pltpu.CompilerParams(
            dimension_semantics=("arbitrary", "arbitrary"), vmem_limit_bytes=VMEM_LIMIT),
        name="mixer_bwd",
    )(after, urnn, hf, gbr, gaya, gb, x, mod_x, w_sc, b_sc.reshape(2, 1, D_MODEL), wa_bf,
      b_rg_a.reshape(2, 1, D_MODEL), wx_bf, b_rg_x.reshape(2, 1, D_MODEL), lam.reshape(2, 1, D_MODEL), h0,
      wro_bf, wout_bf, post1_g, pre2_g, w_router_t, router_bias)


def _positions_kernel(after_ref, idx_ref, pos_ref, plan_ref):
    del after_ref
    n = idx_ref.shape[1]
    n_tiles = n // POS_TILE
    eidx = lax.broadcasted_iota(jnp.int32, (N_EXPERTS, POS_TILE), 0)

    def chosen(t):
        idx = idx_ref[:, pl.ds(pl.multiple_of(t * POS_TILE, POS_TILE), POS_TILE)]
        ch = jnp.zeros((N_EXPERTS, POS_TILE), F32)
        for k in range(TOP_K):
            ch = ch + jnp.where(eidx == idx[k:k + 1, :], 1.0, 0.0)
        return idx, ch

    def count_body(t, cnt):
        return cnt + jnp.sum(chosen(t)[1], axis=1, keepdims=True)

    cnt = lax.fori_loop(0, n_tiles, count_body, jnp.zeros((N_EXPERTS, 1), F32))
    padded = jnp.ceil(cnt * (1.0 / ROW_TILE)) * ROW_TILE
    r = lax.broadcasted_iota(jnp.int32, (N_EXPERTS, N_EXPERTS), 0)
    c = lax.broadcasted_iota(jnp.int32, (N_EXPERTS, N_EXPERTS), 1)
    off = jnp.dot(jnp.where(c < r, 1.0, 0.0), jnp.broadcast_to(padded, (N_EXPERTS, LANES)),
                  preferred_element_type=F32, precision=lax.Precision.HIGHEST)[:, 0:1]
    end = off + padded

    n_map = plan_ref.shape[1]
    tstart = lax.broadcasted_iota(jnp.int32, (N_EXPERTS, n_map), 1).astype(F32) * ROW_TILE
    te = jnp.minimum(jnp.sum(jnp.where(end <= tstart, 1, 0), axis=0, keepdims=True), N_EXPERTS - 1)
    emap = lax.broadcasted_iota(jnp.int32, (N_EXPERTS, n_map), 0)
    live_end = jnp.sum(jnp.where(emap == te, off + cnt, 0.0), axis=0, keepdims=True)
    total = jnp.sum(padded, axis=0, keepdims=True)
    plan_ref[0:1, :] = te
    plan_ref[1:2, :] = jnp.clip(live_end - tstart[0:1, :], 0.0, ROW_TILE).astype(jnp.int32)
    plan_ref[2:3, :] = jnp.broadcast_to(total * (1.0 / ROW_TILE), (1, n_map)).astype(jnp.int32)

    row = lax.broadcasted_iota(jnp.int32, (POS_TILE, POS_TILE), 0)
    col = lax.broadcasted_iota(jnp.int32, (POS_TILE, POS_TILE), 1)
    before = jnp.where(row < col, 1.0, 0.0).astype(BF16)

    def pos_body(t, carry):
        idx, ch = chosen(t)
        base = _dot(ch.astype(BF16), before) + (carry + off)
        rows = [jnp.sum(jnp.where(eidx == idx[k:k + 1, :], base, 0.0), axis=0, keepdims=True)
                for k in range(TOP_K)]
        pos_ref[:, pl.ds(pl.multiple_of(t * POS_TILE, POS_TILE), POS_TILE)] = (
            jnp.concatenate(rows, axis=0).astype(jnp.int32))
        return carry + jnp.sum(ch, axis=1, keepdims=True)

    lax.fori_loop(0, n_tiles, pos_body, jnp.zeros((N_EXPERTS, 1), F32))


def _positions_call(after, idx_t, n_row_tiles):
    n = idx_t.shape[1]
    n_map = -(-n_row_tiles // LANES) * LANES
    return pl.pallas_call(
        _positions_kernel,
        in_specs=[pl.BlockSpec(memory_space=pl.ANY), pl.BlockSpec(memory_space=pltpu.VMEM)],
        out_shape=[
            jax.ShapeDtypeStruct((TOP_K, n), jnp.int32),
            jax.ShapeDtypeStruct((3, n_map), jnp.int32),
        ],
        name="moe_positions",
    )(after, idx_t)


def _sc_mesh():
    return plsc.VectorSubcoreMesh(core_axis_name="c", subcore_axis_name="s",
                                  num_cores=V7X_SC_CORES, num_subcores=V7X_SC_SUBCORES)


def _sc_worker():
    return lax.axis_index("s") * V7X_SC_CORES + lax.axis_index("c")


def _dispatch_call(rows, pos, n_slots):
    n = rows.shape[0]
    tok_w = n // SC_WORKERS
    n_items = tok_w // SC_CHUNK

    def body(rows_hbm, pos_hbm, xs_hbm, idx_v, rows_v, lsem, ssem):
        wid = _sc_worker()
        pltpu.sync_copy(pos_hbm.at[wid], idx_v)
        base = wid * tok_w

        def load(i):
            b = i % SC_BUFFERS
            return pltpu.async_copy(rows_hbm.at[pl.ds(base + i * SC_CHUNK, SC_CHUNK)], rows_v.at[b], lsem.at[b])

        def scatter(i):
            b = i % SC_BUFFERS
            return [pltpu.async_copy(rows_v.at[b], xs_hbm.at[idx_v.at[i, k]], ssem.at[b]) for k in range(TOP_K)]

        loads = {i: load(i) for i in range(SC_BUFFERS - 1)}
        scat = {}
        for i in range(n_items):
            loads[i].wait()
            scat[i] = scatter(i)
            if i >= 1:
                for cp in scat[i - 1]:
                    cp.wait()
            if i + SC_BUFFERS - 1 < n_items:
                loads[i + SC_BUFFERS - 1] = load(i + SC_BUFFERS - 1)
        for cp in scat[n_items - 1]:
            cp.wait()

    return pl.kernel(
        body, mesh=_sc_mesh(),
        out_type=jax.ShapeDtypeStruct((n_slots, ROW_WORDS), jnp.int32),
        scratch_types=[pltpu.VMEM((n_items, TOP_K, SC_CHUNK), jnp.int32),
                       pltpu.VMEM((SC_BUFFERS, SC_CHUNK, ROW_WORDS), jnp.int32),
                       pltpu.SemaphoreType.DMA((SC_BUFFERS,)), pltpu.SemaphoreType.DMA((SC_BUFFERS,))],
        compiler_params=pltpu.CompilerParams(use_tc_tiling_on_sc=True),
        name="moe_dispatch",
    )(rows, pos)


def _collect_call(ys, pos, n):
    tok_w = n // SC_WORKERS
    n_chunks = tok_w // SC_CHUNK
    items = [(c, k) for c in range(n_chunks) for k in range(TOP_K)]

    def body(ys_hbm, pos_hbm, yt_hbm, idx_v, rows_v, gsem, wsem):
        wid = _sc_worker()
        pltpu.sync_copy(pos_hbm.at[wid], idx_v)
        base = wid * tok_w

        def gather(j):
            c, k = items[j]
            b = j % SC_BUFFERS
            return pltpu.async_copy(ys_hbm.at[idx_v.at[c, k]], rows_v.at[b], gsem.at[b])

        def write(j):
            c, k = items[j]
            b = j % SC_BUFFERS
            return pltpu.async_copy(rows_v.at[b], yt_hbm.at[k, pl.ds(base + c * SC_CHUNK, SC_CHUNK)], wsem.at[b])

        g = {j: gather(j) for j in range(SC_BUFFERS - 1)}
        w = {}
        for j in range(len(items)):
            g[j].wait()
            w[j] = write(j)
            if j >= 1:
                w[j - 1].wait()
            if j + SC_BUFFERS - 1 < len(items):
                g[j + SC_BUFFERS - 1] = gather(j + SC_BUFFERS - 1)
        w[len(items) - 1].wait()

    return pl.kernel(
        body, mesh=_sc_mesh(),
        out_type=jax.ShapeDtypeStruct((TOP_K, n, ROW_WORDS), jnp.int32),
        scratch_types=[pltpu.VMEM((n_chunks, TOP_K, SC_CHUNK), jnp.int32),
                       pltpu.VMEM((SC_BUFFERS, SC_CHUNK, ROW_WORDS), jnp.int32),
                       pltpu.SemaphoreType.DMA((SC_BUFFERS,)), pltpu.SemaphoreType.DMA((SC_BUFFERS,))],
        compiler_params=pltpu.CompilerParams(use_tc_tiling_on_sc=True),
        name="moe_collect",
    )(ys, pos)


def _expert_gemm_kernel(plan_ref, after_ref, xs_ref, wg_ref, wu_ref, wd_ref, ys_ref, wgu_scr, wd_scr):
    del after_ref
    i = pl.program_id(0)
    e = plan_ref[0, i]
    prev = plan_ref[0, jnp.maximum(i - 1, 0)]
    live = plan_ref[1, i]

    @pl.when((i == 0) | (e != prev))
    def _():
        wgu_scr[:, 0:D_EXPERT] = wg_ref[...].astype(BF16)
        wgu_scr[:, D_EXPERT:2 * D_EXPERT] = wu_ref[...].astype(BF16)
        wd_scr[...] = wd_ref[...].astype(BF16)

    def swiglu_rows(n_rows):
        lo, hi = _unpack_bf16_pairs(xs_ref[0:n_rows, :])
        gu = _dot(lo.astype(BF16), wgu_scr[0:ROW_WORDS, :]) + _dot(hi.astype(BF16), wgu_scr[ROW_WORDS:D_MODEL, :])
        h = _silu(gu[:, 0:D_EXPERT]) * gu[:, D_EXPERT:2 * D_EXPERT]
        y = _dot(h.astype(BF16), wd_scr[...])
        ys_ref[0:n_rows, :] = _pack_bf16_pairs_native(y)

    @pl.when((i < plan_ref[2, 0]) & (live > ROW_TILE // 2))
    def _():
        swiglu_rows(ROW_TILE)

    @pl.when((i < plan_ref[2, 0]) & (live <= ROW_TILE // 2))
    def _():
        swiglu_rows(ROW_TILE // 2)


def _expert_gemm_call(after, plan, xs, w_e_gate, w_e_up, w_e_down):
    n_slots = xs.shape[0]
    n_row_tiles = n_slots // ROW_TILE
    rows = pl.BlockSpec((ROW_TILE, ROW_WORDS), lambda i, plan: (jnp.minimum(i, plan[2, 0] - 1), 0))
    expert = lambda i, plan: (plan[0, i], 0, 0)
    return pl.pallas_call(
        _expert_gemm_kernel,
        grid_spec=pltpu.PrefetchScalarGridSpec(
            num_scalar_prefetch=1,
            grid=(n_row_tiles,),
            in_specs=[
                pl.BlockSpec(memory_space=pl.ANY),
                rows,
                pl.BlockSpec((None, D_MODEL, D_EXPERT), expert),
                pl.BlockSpec((None, D_MODEL, D_EXPERT), expert),
                pl.BlockSpec((None, D_EXPERT, D_MODEL), expert),
            ],
            out_specs=rows,
            scratch_shapes=[pltpu.VMEM((D_MODEL, 2 * D_EXPERT), BF16), pltpu.VMEM((D_EXPERT, D_MODEL), BF16)],
        ),
        out_shape=jax.ShapeDtypeStruct((n_slots, ROW_WORDS), jnp.int32),
        compiler_params=pltpu.CompilerParams(dimension_semantics=("arbitrary",), vmem_limit_bytes=VMEM_LIMIT),
        name="moe_experts",
    )(plan, after, xs, w_e_gate, w_e_up, w_e_down)


def _moe_out_kernel(yt_ref, w_ref, t_ref, x1_ref, mod_ref, post2_ref, wsg_ref, wsu_ref, wsd_ref, o_ref):
    lo, hi = _unpack_bf16_pairs(t_ref[...])
    lo = lo.astype(BF16)
    hi = hi.astype(BF16)
    g = _dot(lo, wsg_ref[0:ROW_WORDS, :]) + _dot(hi, wsg_ref[ROW_WORDS:D_MODEL, :])
    u = _dot(lo, wsu_ref[0:ROW_WORDS, :]) + _dot(hi, wsu_ref[ROW_WORDS:D_MODEL, :])
    shared = _dot((_silu(g) * u).astype(BF16), wsd_ref[...])
    acc_lo = shared[:, 0:ROW_WORDS]
    acc_hi = shared[:, ROW_WORDS:D_MODEL]
    for k in range(TOP_K):
        y_lo, y_hi = _unpack_bf16_pairs(yt_ref[k])
        wk = w_ref[:, k:k + 1]
        acc_lo = acc_lo + wk * y_lo
        acc_hi = acc_hi + wk * y_hi
    moe = jnp.concatenate([acc_lo, acc_hi], axis=-1)
    o_ref[...] = x1_ref[...] + mod_ref[5:6, :] * _rms_norm(moe, post2_ref[...])


def _moe_out_into_kernel(prev_ref, *refs):
    del prev_ref
    _moe_out_kernel(*refs)


def _moe_out_call(prev_out, n_total, b0, yt, w, t, x1, mod_x, post2_g, wsg_bf, wsu_bf, wsd_bf, seq):
    n = t.shape[0]
    tiles_per_seq = seq // TOUT
    first_tile = b0 * tiles_per_seq
    const = lambda i: (0, 0)
    in_specs = [
        pl.BlockSpec((TOP_K, TOUT, ROW_WORDS), lambda i: (0, i, 0)),
        pl.BlockSpec((TOUT, TOP_K), lambda i: (i, 0)),
        pl.BlockSpec((TOUT, ROW_WORDS), lambda i: (i, 0)),
        pl.BlockSpec((TOUT, D_MODEL), lambda i: (i, 0)),
        pl.BlockSpec((None, 6, D_MODEL), lambda i: (b0 + i // tiles_per_seq, 0, 0)),
        pl.BlockSpec((1, D_MODEL), const),
        pl.BlockSpec((D_MODEL, D_EXPERT), const),
        pl.BlockSpec((D_MODEL, D_EXPERT), const),
        pl.BlockSpec((D_EXPERT, D_MODEL), const),
    ]
    args = (yt, w, t, x1, mod_x, post2_g, wsg_bf, wsu_bf, wsd_bf)
    aliased = prev_out is not None
    return pl.pallas_call(
        _moe_out_into_kernel if aliased else _moe_out_kernel,
        grid=(n // TOUT,),
        in_specs=([pl.BlockSpec(memory_space=pl.ANY)] if aliased else []) + in_specs,
        out_specs=pl.BlockSpec((TOUT, D_MODEL), lambda i: (first_tile + i, 0)),
        out_shape=jax.ShapeDtypeStruct((n_total, D_MODEL), F32),
        input_output_aliases={0: 0} if aliased else {},
        compiler_params=pltpu.CompilerParams(dimension_semantics=("arbitrary",), vmem_limit_bytes=VMEM_LIMIT),
        name="moe_out",
    )(*(((prev_out,) if aliased else ()) + args))


def _moe_dispatch(after, hx2w, idx_t):
    n = hx2w.shape[0]
    n_slots = n * TOP_K + N_EXPERTS * ROW_TILE
    pos_t, plan = _positions_call(after, idx_t, n_slots // ROW_TILE)
    pos = pos_t.reshape(TOP_K, SC_WORKERS, n // (SC_WORKERS * SC_CHUNK), SC_CHUNK).transpose(1, 2, 0, 3)
    return _dispatch_call(hx2w, pos, n_slots), pos, plan


def kernel(x, c, ctx, c_ctx, w_mod, b_mod, pre1_g, post1_g, pre2_g, post2_g, w_in, w_dw, b_dw, ln_conv_g, ln_conv_b, w_conv_out, w_sc, b_sc, w_rg_a, b_rg_a, w_rg_x, b_rg_x, lru_lambda, w_rnn_out, w_out, w_router, router_bias, w_e_gate, w_e_up, w_e_down, w_s_gate, w_s_up, w_s_down):
    assert w_mod.shape[0] == 1, "single-layer block"
    bsz, seq, d = x.shape
    n = bsz * seq
    assert d == D_MODEL and seq % TM == 0 and seq % TOUT == 0 and bsz + 1 <= SUBLANES
    assert TOKEN_GROUPS == 2 and bsz % TOKEN_GROUPS == 0 and (n // TOKEN_GROUPS) % (SC_WORKERS * SC_CHUNK) == 0
    assert (n // TOKEN_GROUPS) % POS_TILE == 0

    cc = jnp.zeros((SUBLANES, D_MODEL), F32).at[:bsz].set(c).at[bsz].set(c_ctx)
    mod = _mod_call(cc, w_mod[0], b_mod)
    mod_x = mod[:bsz].reshape(bsz, 6, D_MODEL)
    mod_c = mod[bsz].reshape(6, D_MODEL)

    w_in_bf = w_in[0].astype(BF16)
    wa_bf = w_rg_a[0].astype(BF16)
    wx_bf = w_rg_x[0].astype(BF16)

    h0 = _ctx_call(ctx, mod_c, pre1_g, w_in_bf, w_sc[0], b_sc[0], wa_bf, b_rg_a[0], wx_bf, b_rg_x[0],
                   lru_lambda[0])

    wco_bf, wro_bf, wout_bf = w_conv_out[0].astype(BF16), w_rnn_out[0].astype(BF16), w_out[0].astype(BF16)
    wsg_bf, wsu_bf, wsd_bf = w_s_gate[0].astype(BF16), w_s_up[0].astype(BF16), w_s_down[0].astype(BF16)
    w_router_t, rbias = w_router[0].T, router_bias.reshape(N_EXPERTS, 1)
    gsz = bsz // TOKEN_GROUPS
    gn = gsz * seq
    none = jnp.zeros((TOP_K, LANES), jnp.int32)

    def fwd(g, after):
        return _mixer_fwd_call(g * gsz, gsz, after, x, mod_x, pre1_g, w_in_bf, w_dw[0], b_dw, ln_conv_g,
                               ln_conv_b, wco_bf, w_sc[0], b_sc[0], wa_bf, b_rg_a[0], wx_bf, b_rg_x[0],
                               lru_lambda[0], h0)

    def bwd(g, after, acts):
        gaya, gb, gbr, urnn, hf = acts
        x1, hx2w, idx_t, w_t = _mixer_bwd_call(
            g * gsz, gsz, after, urnn, hf, gbr, gaya, gb, x, mod_x, w_sc[0], b_sc[0], wa_bf, b_rg_a[0], wx_bf,
            b_rg_x[0], lru_lambda[0], h0, wro_bf, wout_bf, post1_g, pre2_g, w_router_t, rbias)
        return x1.reshape(gn, D_MODEL), hx2w.reshape(gn, ROW_WORDS), idx_t, w_t

    def experts(after, plan, xs):
        return _expert_gemm_call(after, plan, xs, w_e_gate[0], w_e_up[0], w_e_down[0])

    def finish(g, prev_out, yt, w_t, hx2w, x1):
        return _moe_out_call(prev_out, n, g * gsz, yt, w_t.T, hx2w, x1, mod_x, post2_g, wsg_bf, wsu_bf, wsd_bf, seq)

    x1_a, hx2w_a, idx_a, w_a = bwd(0, none, fwd(0, none))
    xs_a, pos_a, plan_a = _moe_dispatch(none, hx2w_a, idx_a)
    acts_b = fwd(1, idx_a)
    ys_a = experts(acts_b[4], plan_a, xs_a)
    yt_a = _collect_call(ys_a, pos_a, gn)
    x1_b, hx2w_b, idx_b, w_b = bwd(1, ys_a, acts_b)
    xs_b, pos_b, plan_b = _moe_dispatch(yt_a, hx2w_b, idx_b)
    out = finish(0, None, yt_a, w_a, hx2w_a, x1_a)
    ys_b = experts(none, plan_b, xs_b)
    yt_b = _collect_call(ys_b, pos_b, gn)
    out = finish(1, out, yt_b, w_b, hx2w_b, x1_b)
    return out.reshape(bsz, seq, D_MODEL)
```

```python
import jax
import jax.numpy as jnp
from jax import lax
from jax.experimental import pallas as pl
from jax.experimental.pallas import tpu as pltpu
from jax.experimental.pallas import tpu_sc as plsc

F32 = jnp.float32
BF16 = jnp.bfloat16

D_MODEL = 1024
GRID_W = 64
CONV_WIDTH = 31
CONV_HALF = CONV_WIDTH // 2
SHORT_CONV = 4
RNN_HEADS = 4
RNN_BLOCK = D_MODEL // RNN_HEADS
LRU_C = 8.0
N_EXPERTS = 64
N_GROUPS = 8
GROUP_SIZE = N_EXPERTS // N_GROUPS
TOPK_GROUPS = 4
TOP_K = 8
D_EXPERT = 256
ROUTED_SCALE = 2.5
EPS = 1e-6

SUBLANES = 8
TM = 512
ROWS_PER_TILE = TM // GRID_W
PAD = 16
ROW_STRIDE = GRID_W + PAD
UPAD_ROWS = ROWS_PER_TILE * ROW_STRIDE + PAD
CONV_LANES = 256
LANES = 128
ROW_WORDS = D_MODEL // 2
ROW_TILE = 1024
POS_TILE = 512
TOUT = 256
TAIL_BATCHES = 1
V7X_SC_CORES = 2
V7X_SC_SUBCORES = 16
SC_WORKERS = V7X_SC_CORES * V7X_SC_SUBCORES
SC_CHUNK = 64
SC_BUFFERS = 2
VMEM_LIMIT = 58 * 1024 * 1024


def _sigmoid(x):
    return 0.5 * (jnp.tanh(0.5 * x) + 1.0)


def _silu(x):
    return x * _sigmoid(x)


def _gelu_tanh(x):
    return 0.5 * x * (1.0 + jnp.tanh(0.7978845608028654 * (x + 0.044715 * (x * x * x))))


def _rms_norm(x, g):
    return x * lax.rsqrt(jnp.mean(x * x, axis=-1, keepdims=True) + EPS) * g


def _dot(a, b):
    return jnp.dot(a, b, preferred_element_type=F32)


def _pack_bf16_pairs(x):
    half = x.shape[-1] // 2
    lo = lax.bitcast_convert_type(x[:, :half].astype(BF16).astype(F32), jnp.uint32)
    hi = lax.bitcast_convert_type(x[:, half:].astype(BF16).astype(F32), jnp.uint32)
    return lax.bitcast_convert_type(hi | (lo >> 16), jnp.int32)


def _pack_bf16_pairs_native(x):
    half = x.shape[-1] // 2
    packed = pltpu.pack_elementwise([x[:, :half], x[:, half:]], packed_dtype=BF16)
    return lax.bitcast_convert_type(packed, jnp.int32)


def _unpack_bf16_pairs(words):
    u = lax.bitcast_convert_type(words, jnp.uint32)
    return (lax.bitcast_convert_type(u << 16, F32),
            lax.bitcast_convert_type(u & jnp.uint32(0xFFFF0000), F32))


def _log_decay(lam):
    return LRU_C * (jnp.minimum(lam, 0.0) - jnp.log1p(jnp.exp(-jnp.abs(lam))))


def _rglru_coeffs(v, wa_ref, ba, wx_ref, bx, c_lam, a_ref, b_ref):
    vb = v.astype(BF16)
    for h in range(RNN_HEADS):
        cs = slice(h * RNN_BLOCK, (h + 1) * RNN_BLOCK)
        vh = vb[:, cs]
        r = _sigmoid(_dot(vh, wa_ref[h]) + ba[:, cs])
        i = _sigmoid(_dot(vh, wx_ref[h]) + bx[:, cs])
        log_a = c_lam[:, cs] * r
        a = jnp.exp(log_a)
        a_ref[:, cs] = a
        b_ref[:, cs] = jnp.sqrt(jnp.tanh(-log_a) * (1.0 + a * a)) * (i * v[:, cs])


def _scan_tile(a_ref, b_ref, carry, n_rows, reverse):
    row = lax.broadcasted_iota(jnp.int32, (SUBLANES, D_MODEL), 0)
    n_groups = n_rows // SUBLANES

    def body(j, carry):
        g = (n_groups - 1 - j) if reverse else j
        off = pl.multiple_of(g * SUBLANES, SUBLANES)
        a = a_ref[pl.ds(off, SUBLANES), :]
        b = b_ref[pl.ds(off, SUBLANES), :]
        for s in (1, 2, 4):
            keep = (row < SUBLANES - s) if reverse else (row >= s)
            shift = (SUBLANES - s) if reverse else s
            a_sh = jnp.where(keep, pltpu.roll(a, shift, 0), 1.0)
            b_sh = jnp.where(keep, pltpu.roll(b, shift, 0), 0.0)
            b = a * b_sh + b
            a = a * a_sh
        h = a * carry + b
        b_ref[pl.ds(off, SUBLANES), :] = h
        last = h[0:1, :] if reverse else h[SUBLANES - 1:SUBLANES, :]
        return jnp.broadcast_to(last, (SUBLANES, D_MODEL))

    return lax.fori_loop(0, n_groups, body, carry, unroll=2)


def _mod_kernel(c_ref, w_ref, b_ref, o_ref):
    o_ref[...] = jnp.dot(_silu(c_ref[...]), w_ref[...], preferred_element_type=F32,
                         precision=lax.Precision.HIGHEST) + b_ref[...]


def _mod_call(cc, w_mod, b_mod):
    n_mod = w_mod.shape[1]
    return pl.pallas_call(
        _mod_kernel,
        grid=(n_mod // D_MODEL,),
        in_specs=[
            pl.BlockSpec((SUBLANES, D_MODEL), lambda j: (0, 0)),
            pl.BlockSpec((D_MODEL, D_MODEL), lambda j: (0, j)),
            pl.BlockSpec((1, D_MODEL), lambda j: (0, j)),
        ],
        out_specs=pl.BlockSpec((SUBLANES, D_MODEL), lambda j: (0, j)),
        out_shape=jax.ShapeDtypeStruct((SUBLANES, n_mod), F32),
        name="mod",
    )(cc, w_mod, b_mod)


def _ctx_kernel(ctx_ref, mod_ref, g_ref, w_ref, wsc_ref, bsc_ref, wa_ref, ba_ref, wx_ref, bx_ref,
                lam_ref, o_ref, uext_ref, a_ref, b_ref):
    n = ctx_ref.shape[0]
    hc = _rms_norm(ctx_ref[...], g_ref[...]) * (1.0 + mod_ref[1:2, :]) + mod_ref[0:1, :]
    u = _dot(hc.astype(BF16), w_ref[...])
    zeros8 = jnp.zeros((SUBLANES, D_MODEL), F32)
    uext_ref[0:SUBLANES, :] = zeros8
    uext_ref[SUBLANES:SUBLANES + n, :] = u
    uext_ref[SUBLANES + n:2 * SUBLANES + n, :] = zeros8
    for d in range(2):
        v = jnp.broadcast_to(bsc_ref[d:d + 1, :], (n, D_MODEL))
        for k in range(SHORT_CONV):
            start = SUBLANES + k - (SHORT_CONV - 1) * (1 - d)
            v = v + wsc_ref[d, k:k + 1, :] * uext_ref[start:start + n, :]
        _rglru_coeffs(v, wa_ref.at[d], ba_ref[d:d + 1, :], wx_ref.at[d], bx_ref[d:d + 1, :],
                      _log_decay(lam_ref[d:d + 1, :]), a_ref, b_ref)
        final = _scan_tile(a_ref, b_ref, zeros8, n, reverse=(d == 1))
        o_ref[d:d + 1, :] = final[0:1, :]


def _ctx_call(ctx, mod_c, pre1_g, w_in_bf, w_sc, b_sc, wa_bf, b_rg_a, wx_bf, b_rg_x, lam):
    bsz, n, _ = ctx.shape
    const2 = lambda b: (0, 0)
    const3 = lambda b: (0, 0, 0)
    const4 = lambda b: (0, 0, 0, 0)
    return pl.pallas_call(
        _ctx_kernel,
        grid=(bsz,),
        in_specs=[
            pl.BlockSpec((None, n, D_MODEL), lambda b: (b, 0, 0)),
            pl.BlockSpec((6, D_MODEL), const2),
            pl.BlockSpec((1, D_MODEL), const2),
            pl.BlockSpec((D_MODEL, D_MODEL), lambda b: (0, 2)),
            pl.BlockSpec((2, SHORT_CONV, D_MODEL), const3),
            pl.BlockSpec((2, D_MODEL), const2),
            pl.BlockSpec((2, RNN_HEADS, RNN_BLOCK, RNN_BLOCK), const4),
            pl.BlockSpec((2, D_MODEL), const2),
            pl.BlockSpec((2, RNN_HEADS, RNN_BLOCK, RNN_BLOCK), const4),
            pl.BlockSpec((2, D_MODEL), const2),
            pl.BlockSpec((2, D_MODEL), const2),
        ],
        out_specs=pl.BlockSpec((None, 2, D_MODEL), lambda b: (b, 0, 0)),
        out_shape=jax.ShapeDtypeStruct((bsz, 2, D_MODEL), F32),
        scratch_shapes=[
            pltpu.VMEM((n + 2 * SUBLANES, D_MODEL), F32),
            pltpu.VMEM((n, D_MODEL), F32),
            pltpu.VMEM((n, D_MODEL), F32),
        ],
        name="ctx",
    )(ctx, mod_c, pre1_g, w_in_bf, w_sc, b_sc, wa_bf, b_rg_a, wx_bf, b_rg_x, lam)


def _mixer_fwd_kernel(after_ref, x_ref, mod_ref, g_ref, win_ref, wdw_ref, bdw_ref, lng_ref, lnb_ref, wco_ref,
                      wsc_ref, bsc_ref, wa_ref, ba_ref, wx_ref, bx_ref, lam_ref, h0_ref,
                      gaya_ref, gb_ref, gbr_ref, urnn_ref, hf_ref,
                      upad_ref, ush_ref, wb_ref, cv_ref, uext_ref, a_ref, b_ref, carry_ref):
    del after_ref
    j = pl.program_id(1)
    zeros8 = jnp.zeros((SUBLANES, D_MODEL), F32)

    @pl.when(j == 0)
    def _():
        carry_ref[...] = jnp.broadcast_to(h0_ref[0:1, :], (SUBLANES, D_MODEL))
        uext_ref[0:SUBLANES, :] = zeros8
        zeros_pad = jnp.zeros((PAD, D_MODEL), F32)
        for r in range(ROWS_PER_TILE + 1):
            upad_ref[r * ROW_STRIDE:r * ROW_STRIDE + PAD, :] = zeros_pad
        for k in range(CONV_WIDTH):
            wb_ref[k] = jnp.broadcast_to(wdw_ref[k:k + 1, :], (SUBLANES, D_MODEL))

    hx = (_rms_norm(x_ref[...], g_ref[...]) * (1.0 + mod_ref[1:2, :]) + mod_ref[0:1, :]).astype(BF16)

    u = _dot(hx, win_ref[:, 0:D_MODEL]) * _sigmoid(_dot(hx, win_ref[:, D_MODEL:2 * D_MODEL]))
    for r in range(ROWS_PER_TILE):
        upad_ref[PAD + r * ROW_STRIDE:PAD + r * ROW_STRIDE + GRID_W, :] = u[r * GRID_W:(r + 1) * GRID_W, :]
    vregs_per_row = GRID_W // SUBLANES
    g_a = ur = None
    for c in range(D_MODEL // CONV_LANES):
        if c == 0:
            gb_ref[...] = _sigmoid(_dot(hx, win_ref[:, 5 * D_MODEL:6 * D_MODEL])).astype(BF16)
        elif c == 1:
            gbr_ref[...] = _gelu_tanh(_dot(hx, win_ref[:, 3 * D_MODEL:4 * D_MODEL])).astype(BF16)
        elif c == 2:
            g_a = _sigmoid(_dot(hx, win_ref[:, 4 * D_MODEL:5 * D_MODEL]))
        else:
            ur = _dot(hx, win_ref[:, 2 * D_MODEL:3 * D_MODEL])
            urnn_ref[...] = ur.astype(BF16)
            uext_ref[SUBLANES:SUBLANES + TM, :] = ur
        cs = slice(c * CONV_LANES, (c + 1) * CONV_LANES)
        xpad = upad_ref[:, cs]
        for s in range(1, SUBLANES):
            ush_ref[s - 1] = pltpu.roll(xpad, UPAD_ROWS - s, 0)
        for r in range(ROWS_PER_TILE):
            acc = jnp.broadcast_to(bdw_ref[:, cs].reshape(1, 1, CONV_LANES), (vregs_per_row, SUBLANES, CONV_LANES))
            for k in range(CONV_WIDTH):
                q, s = divmod(r * ROW_STRIDE + PAD - CONV_HALF + k, SUBLANES)
                rows = slice(q * SUBLANES, q * SUBLANES + GRID_W)
                win = upad_ref[rows, cs] if s == 0 else ush_ref[s - 1, rows, :]
                acc = acc + wb_ref[k, :, cs] * win.reshape(vregs_per_row, SUBLANES, CONV_LANES)
            cv_ref[r * GRID_W:(r + 1) * GRID_W, cs] = acc.reshape(GRID_W, CONV_LANES)
    cv = cv_ref[...]
    cvc = cv - jnp.mean(cv, axis=-1, keepdims=True)
    cvn = cvc * lax.rsqrt(jnp.mean(cvc * cvc, axis=-1, keepdims=True) + EPS) * lng_ref[...] + lnb_ref[...]
    y_a = _dot(_silu(cvn).astype(BF16), wco_ref[...])

    gaya_ref[...] = (g_a * y_a).astype(BF16)

    ue = uext_ref[...]
    v = bsc_ref[...] + wsc_ref[SHORT_CONV - 1:SHORT_CONV, :] * ur
    for k in range(SHORT_CONV - 1):
        v = v + wsc_ref[k:k + 1, :] * pltpu.roll(ue, SHORT_CONV - 1 - k, 0)[SUBLANES:SUBLANES + TM, :]
    uext_ref[0:SUBLANES, :] = uext_ref[TM:TM + SUBLANES, :]
    _rglru_coeffs(v, wa_ref, ba_ref[...], wx_ref, bx_ref[...], _log_decay(lam_ref[...]), a_ref, b_ref)
    carry_ref[...] = _scan_tile(a_ref, b_ref, carry_ref[...], TM, reverse=False)
    hf_ref[...] = b_ref[...].astype(BF16)


def _resident(shape):
    nd = len(shape)
    return pl.BlockSpec(shape, lambda b, j: (0,) * nd, pipeline_mode=pl.Buffered(1))


def _mixer_fwd_call(b0, bsz, after, x, mod_x, pre1_g, w_in_bf, w_dw, b_dw, ln_g, ln_b, wco_bf,
                    w_sc, b_sc, wa_bf, b_rg_a, wx_bf, b_rg_x, lam, h0):
    seq = x.shape[1]
    nt = seq // TM
    tile = pl.BlockSpec((None, TM, D_MODEL), lambda b, j: (b, j, 0))
    act = jax.ShapeDtypeStruct((bsz, seq, D_MODEL), BF16)
    head_w = pl.BlockSpec((None, RNN_HEADS, RNN_BLOCK, RNN_BLOCK), lambda b, j: (0, 0, 0, 0),
                          pipeline_mode=pl.Buffered(1))
    dir_row = pl.BlockSpec((None, 1, D_MODEL), lambda b, j: (0, 0, 0), pipeline_mode=pl.Buffered(1))
    return pl.pallas_call(
        _mixer_fwd_kernel,
        grid=(bsz, nt),
        in_specs=[
            pl.BlockSpec(memory_space=pl.ANY),
            pl.BlockSpec((None, TM, D_MODEL), lambda b, j: (b0 + b, j, 0)),
            pl.BlockSpec((None, 6, D_MODEL), lambda b, j: (b0 + b, 0, 0)),
            _resident((1, D_MODEL)),
            _resident((D_MODEL, 6 * D_MODEL)),
            _resident((CONV_WIDTH, D_MODEL)),
            _resident((1, D_MODEL)),
            _resident((1, D_MODEL)),
            _resident((1, D_MODEL)),
            _resident((D_MODEL, D_MODEL)),
            pl.BlockSpec((None, SHORT_CONV, D_MODEL), lambda b, j: (0, 0, 0), pipeline_mode=pl.Buffered(1)),
            dir_row, head_w, dir_row, head_w, dir_row, dir_row,
            pl.BlockSpec((None, 2, D_MODEL), lambda b, j: (b0 + b, 0, 0)),
        ],
        out_specs=[tile] * 5,
        out_shape=[act] * 5,
        scratch_shapes=[
            pltpu.VMEM((UPAD_ROWS, D_MODEL), F32),
            pltpu.VMEM((SUBLANES - 1, UPAD_ROWS, CONV_LANES), F32),
            pltpu.VMEM((CONV_WIDTH, SUBLANES, D_MODEL), F32),
            pltpu.VMEM((TM, D_MODEL), F32),
            pltpu.VMEM((TM + SUBLANES, D_MODEL), F32),
            pltpu.VMEM((TM, D_MODEL), F32),
            pltpu.VMEM((TM, D_MODEL), F32),
            pltpu.VMEM((SUBLANES, D_MODEL), F32),
        ],
        compiler_params=pltpu.CompilerParams(
            dimension_semantics=("arbitrary", "arbitrary"), vmem_limit_bytes=VMEM_LIMIT),
        name="mixer_fwd",
    )(after, x, mod_x, pre1_g, w_in_bf, w_dw, b_dw, ln_g, ln_b, wco_bf,
      w_sc, b_sc.reshape(2, 1, D_MODEL), wa_bf, b_rg_a.reshape(2, 1, D_MODEL), wx_bf,
      b_rg_x.reshape(2, 1, D_MODEL), lam.reshape(2, 1, D_MODEL), h0)


def _route(logits_t, bias):
    t = logits_t.shape[1]
    scores = _sigmoid(logits_t)
    sel = scores + bias
    neg_inf = jnp.float32(-jnp.inf)

    sel3 = sel.reshape(N_GROUPS, GROUP_SIZE, t)
    within = lax.broadcasted_iota(jnp.int32, sel3.shape, 1)
    m1 = jnp.max(sel3, axis=1, keepdims=True)
    first = jnp.min(jnp.where(sel3 == m1, within, GROUP_SIZE), axis=1, keepdims=True)
    m2 = jnp.max(jnp.where(within == first, neg_inf, sel3), axis=1, keepdims=True)
    gscore = (m1 + m2).reshape(N_GROUPS, t)

    gidx = lax.broadcasted_iota(jnp.int32, gscore.shape, 0)
    rank = jnp.zeros(gscore.shape, jnp.int32)
    for g in range(N_GROUPS):
        other = gscore[g:g + 1, :]
        beats = jnp.where(other > gscore, 1, jnp.where((other == gscore) & (gidx > g), 1, 0))
        rank = rank + beats
    gkeep = (rank < TOPK_GROUPS).reshape(N_GROUPS, 1, t)
    masked = jnp.where(gkeep, sel3, neg_inf).reshape(N_EXPERTS, t)

    eidx = lax.broadcasted_iota(jnp.int32, masked.shape, 0)
    picks, weights = [], []
    for _ in range(TOP_K):
        m = jnp.max(masked, axis=0, keepdims=True)
        first = jnp.min(jnp.where(masked == m, eidx, N_EXPERTS), axis=0, keepdims=True)
        pick = eidx == first
        picks.append(first)
        weights.append(jnp.sum(jnp.where(pick, scores, 0.0), axis=0, keepdims=True))
        masked = jnp.where(pick, neg_inf, masked)
    idx = jnp.concatenate(picks, axis=0)
    w = jnp.concatenate(weights, axis=0)
    return idx, ROUTED_SCALE * w / jnp.sum(w, axis=0, keepdims=True)


def _mixer_bwd_kernel(after_ref, urnn_ref, hf_ref, gbr_ref, gaya_ref, gb_ref, x_ref, mod_ref,
                      wsc_ref, bsc_ref, wa_ref, ba_ref, wx_ref, bx_ref, lam_ref, h0_ref,
                      wro_ref, wout_ref, post1_ref, pre2_ref, wrt_ref, rbias_ref,
                      x1_ref, hx2w_ref, idx_ref, w_ref,
                      uext_ref, a_ref, b_ref, carry_ref):
    del after_ref
    j = pl.program_id(1)
    zeros8 = jnp.zeros((SUBLANES, D_MODEL), F32)

    @pl.when(j == 0)
    def _():
        carry_ref[...] = jnp.broadcast_to(h0_ref[1:2, :], (SUBLANES, D_MODEL))
        uext_ref[TM:TM + SUBLANES, :] = zeros8

    ur = urnn_ref[...].astype(F32)
    uext_ref[0:TM, :] = ur
    ue = uext_ref[...]
    v = bsc_ref[...] + wsc_ref[0:1, :] * ur
    for k in range(1, SHORT_CONV):
        v = v + wsc_ref[k:k + 1, :] * pltpu.roll(ue, TM + SUBLANES - k, 0)[0:TM, :]
    uext_ref[TM:TM + SUBLANES, :] = uext_ref[0:SUBLANES, :]
    _rglru_coeffs(v, wa_ref, ba_ref[...], wx_ref, bx_ref[...], _log_decay(lam_ref[...]), a_ref, b_ref)
    carry_ref[...] = _scan_tile(a_ref, b_ref, carry_ref[...], TM, reverse=True)

    h_sum = hf_ref[...].astype(F32) + b_ref[...]
    y_b = _dot((gbr_ref[...].astype(F32) * h_sum).astype(BF16), wro_ref[...])
    mix = gaya_ref[...].astype(F32) + gb_ref[...].astype(F32) * y_b
    out = _dot(mix.astype(BF16), wout_ref[...])
    x1 = x_ref[...] + mod_ref[2:3, :] * _rms_norm(out, post1_ref[...])
    x1_ref[...] = x1

    hx2 = _rms_norm(x1, pre2_ref[...]) * (1.0 + mod_ref[4:5, :]) + mod_ref[3:4, :]
    hx2w_ref[...] = _pack_bf16_pairs(hx2)
    logits_t = lax.dot_general(wrt_ref[...], hx2, (((1,), (1,)), ((), ())),
                               preferred_element_type=F32, precision=lax.Precision.HIGHEST)
    idx, w = _route(logits_t, rbias_ref[...])
    idx_ref[...] = idx
    w_ref[...] = w


def _mixer_bwd_call(b0, bsz, after, urnn, hf, gbr, gaya, gb, x, mod_x, w_sc, b_sc, wa_bf, b_rg_a, wx_bf, b_rg_x, lam, h0,
                    wro_bf, wout_bf, post1_g, pre2_g, w_router_t, router_bias):
    seq = x.shape[1]
    nt = seq // TM
    rev = lambda b, j: (b, nt - 1 - j, 0)
    tile = pl.BlockSpec((None, TM, D_MODEL), rev)
    head_w = pl.BlockSpec((None, RNN_HEADS, RNN_BLOCK, RNN_BLOCK), lambda b, j: (1, 0, 0, 0),
                          pipeline_mode=pl.Buffered(1))
    dir_row = pl.BlockSpec((None, 1, D_MODEL), lambda b, j: (1, 0, 0), pipeline_mode=pl.Buffered(1))
    return pl.pallas_call(
        _mixer_bwd_kernel,
        grid=(bsz, nt),
        in_specs=[
            pl.BlockSpec(memory_space=pl.ANY),
            tile, tile, tile, tile, tile,
            pl.BlockSpec((None, TM, D_MODEL), lambda b, j: (b0 + b, nt - 1 - j, 0)),
            pl.BlockSpec((None, 6, D_MODEL), lambda b, j: (b0 + b, 0, 0)),
            pl.BlockSpec((None, SHORT_CONV, D_MODEL), lambda b, j: (1, 0, 0), pipeline_mode=pl.Buffered(1)),
            dir_row, head_w, dir_row, head_w, dir_row, dir_row,
            pl.BlockSpec((None, 2, D_MODEL), lambda b, j: (b0 + b, 0, 0)),
            _resident((D_MODEL, D_MODEL)),
            _resident((D_MODEL, D_MODEL)),
            _resident((1, D_MODEL)),
            _resident((1, D_MODEL)),
            _resident((N_EXPERTS, D_MODEL)),
            _resident((N_EXPERTS, 1)),
        ],
        out_specs=[
            pl.BlockSpec((None, TM, D_MODEL), rev),
            pl.BlockSpec((None, TM, ROW_WORDS), rev),
            pl.BlockSpec((TOP_K, TM), lambda b, j: (0, b * nt + nt - 1 - j)),
            pl.BlockSpec((TOP_K, TM), lambda b, j: (0, b * nt + nt - 1 - j)),
        ],
        out_shape=[
            jax.ShapeDtypeStruct((bsz, seq, D_MODEL), F32),
            jax.ShapeDtypeStruct((bsz, seq, ROW_WORDS), jnp.int32),
            jax.ShapeDtypeStruct((TOP_K, bsz * seq), jnp.int32),
            jax.ShapeDtypeStruct((TOP_K, bsz * seq), F32),
        ],
        scratch_shapes=[
            pltpu.VMEM((TM + SUBLANES, D_MODEL), F32),
            pltpu.VMEM((TM, D_MODEL), F32),
            pltpu.VMEM((TM, D_MODEL), F32),
            pltpu.VMEM((SUBLANES, D_MODEL), F32),
        ],
        compiler_params=pltpu.CompilerParams(
            dimension_semantics=("arbitrary", "arbitrary"), vmem_limit_bytes=VMEM_LIMIT),
        name="mixer_bwd",
    )(after, urnn, hf, gbr, gaya, gb, x, mod_x, w_sc, b_sc.reshape(2, 1, D_MODEL), wa_bf,
      b_rg_a.reshape(2, 1, D_MODEL), wx_bf, b_rg_x.reshape(2, 1, D_MODEL), lam.reshape(2, 1, D_MODEL), h0,
      wro_bf, wout_bf, post1_g, pre2_g, w_router_t, router_bias)


def _positions_kernel(after_ref, idx_ref, pos_ref, plan_ref):
    del after_ref
    n = idx_ref.shape[1]
    n_tiles = n // POS_TILE
    eidx = lax.broadcasted_iota(jnp.int32, (N_EXPERTS, POS_TILE), 0)

    def chosen(t):
        idx = idx_ref[:, pl.ds(pl.multiple_of(t * POS_TILE, POS_TILE), POS_TILE)]
        ch = jnp.zeros((N_EXPERTS, POS_TILE), F32)
        for k in range(TOP_K):
            ch = ch + jnp.where(eidx == idx[k:k + 1, :], 1.0, 0.0)
        return idx, ch

    def count_body(t, cnt):
        return cnt + jnp.sum(chosen(t)[1], axis=1, keepdims=True)

    cnt = lax.fori_loop(0, n_tiles, count_body, jnp.zeros((N_EXPERTS, 1), F32))
    padded = jnp.ceil(cnt * (1.0 / ROW_TILE)) * ROW_TILE
    r = lax.broadcasted_iota(jnp.int32, (N_EXPERTS, N_EXPERTS), 0)
    c = lax.broadcasted_iota(jnp.int32, (N_EXPERTS, N_EXPERTS), 1)
    off = jnp.dot(jnp.where(c < r, 1.0, 0.0), jnp.broadcast_to(padded, (N_EXPERTS, LANES)),
                  preferred_element_type=F32, precision=lax.Precision.HIGHEST)[:, 0:1]
    end = off + padded

    n_map = plan_ref.shape[1]
    tstart = lax.broadcasted_iota(jnp.int32, (N_EXPERTS, n_map), 1).astype(F32) * ROW_TILE
    te = jnp.minimum(jnp.sum(jnp.where(end <= tstart, 1, 0), axis=0, keepdims=True), N_EXPERTS - 1)
    emap = lax.broadcasted_iota(jnp.int32, (N_EXPERTS, n_map), 0)
    live_end = jnp.sum(jnp.where(emap == te, off + cnt, 0.0), axis=0, keepdims=True)
    total = jnp.sum(padded, axis=0, keepdims=True)
    plan_ref[0:1, :] = te
    plan_ref[1:2, :] = jnp.clip(live_end - tstart[0:1, :], 0.0, ROW_TILE).astype(jnp.int32)
    plan_ref[2:3, :] = jnp.broadcast_to(total * (1.0 / ROW_TILE), (1, n_map)).astype(jnp.int32)

    row = lax.broadcasted_iota(jnp.int32, (POS_TILE, POS_TILE), 0)
    col = lax.broadcasted_iota(jnp.int32, (POS_TILE, POS_TILE), 1)
    before = jnp.where(row < col, 1.0, 0.0).astype(BF16)

    def pos_body(t, carry):
        idx, ch = chosen(t)
        base = _dot(ch.astype(BF16), before) + (carry + off)
        rows = [jnp.sum(jnp.where(eidx == idx[k:k + 1, :], base, 0.0), axis=0, keepdims=True)
                for k in range(TOP_K)]
        pos_ref[:, pl.ds(pl.multiple_of(t * POS_TILE, POS_TILE), POS_TILE)] = (
            jnp.concatenate(rows, axis=0).astype(jnp.int32))
        return carry + jnp.sum(ch, axis=1, keepdims=True)

    lax.fori_loop(0, n_tiles, pos_body, jnp.zeros((N_EXPERTS, 1), F32))


def _positions_call(after, idx_t, n_row_tiles):
    n = idx_t.shape[1]
    n_map = -(-n_row_tiles // LANES) * LANES
    return pl.pallas_call(
        _positions_kernel,
        in_specs=[pl.BlockSpec(memory_space=pl.ANY), pl.BlockSpec(memory_space=pltpu.VMEM)],
        out_shape=[
            jax.ShapeDtypeStruct((TOP_K, n), jnp.int32),
            jax.ShapeDtypeStruct((3, n_map), jnp.int32),
        ],
        name="moe_positions",
    )(after, idx_t)


def _sc_mesh():
    return plsc.VectorSubcoreMesh(core_axis_name="c", subcore_axis_name="s",
                                  num_cores=V7X_SC_CORES, num_subcores=V7X_SC_SUBCORES)


def _sc_worker():
    return lax.axis_index("s") * V7X_SC_CORES + lax.axis_index("c")


def _dispatch_call(rows, pos, n_slots):
    n = rows.shape[0]
    tok_w = n // SC_WORKERS
    n_items = tok_w // SC_CHUNK

    def body(rows_hbm, pos_hbm, xs_hbm, idx_v, rows_v, lsem, ssem):
        wid = _sc_worker()
        pltpu.sync_copy(pos_hbm.at[wid], idx_v)
        base = wid * tok_w

        def load(i):
            b = i % SC_BUFFERS
            return pltpu.async_copy(rows_hbm.at[pl.ds(base + i * SC_CHUNK, SC_CHUNK)], rows_v.at[b], lsem.at[b])

        def scatter(i):
            b = i % SC_BUFFERS
            return [pltpu.async_copy(rows_v.at[b], xs_hbm.at[idx_v.at[i, k]], ssem.at[b]) for k in range(TOP_K)]

        loads = {i: load(i) for i in range(SC_BUFFERS - 1)}
        scat = {}
        for i in range(n_items):
            loads[i].wait()
            scat[i] = scatter(i)
            if i >= 1:
                for cp in scat[i - 1]:
                    cp.wait()
            if i + SC_BUFFERS - 1 < n_items:
                loads[i + SC_BUFFERS - 1] = load(i + SC_BUFFERS - 1)
        for cp in scat[n_items - 1]:
            cp.wait()

    return pl.kernel(
        body, mesh=_sc_mesh(),
        out_type=jax.ShapeDtypeStruct((n_slots, ROW_WORDS), jnp.int32),
        scratch_types=[pltpu.VMEM((n_items, TOP_K, SC_CHUNK), jnp.int32),
                       pltpu.VMEM((SC_BUFFERS, SC_CHUNK, ROW_WORDS), jnp.int32),
                       pltpu.SemaphoreType.DMA((SC_BUFFERS,)), pltpu.SemaphoreType.DMA((SC_BUFFERS,))],
        compiler_params=pltpu.CompilerParams(use_tc_tiling_on_sc=True),
        name="moe_dispatch",
    )(rows, pos)


def _collect_call(ys, pos, n):
    tok_w = n // SC_WORKERS
    n_chunks = tok_w // SC_CHUNK
    items = [(c, k) for c in range(n_chunks) for k in range(TOP_K)]

    def body(ys_hbm, pos_hbm, yt_hbm, idx_v, rows_v, gsem, wsem):
        wid = _sc_worker()
        pltpu.sync_copy(pos_hbm.at[wid], idx_v)
        base = wid * tok_w

        def gather(j):
            c, k = items[j]
            b = j % SC_BUFFERS
            return pltpu.async_copy(ys_hbm.at[idx_v.at[c, k]], rows_v.at[b], gsem.at[b])

        def write(j):
            c, k = items[j]
            b = j % SC_BUFFERS
            return pltpu.async_copy(rows_v.at[b], yt_hbm.at[k, pl.ds(base + c * SC_CHUNK, SC_CHUNK)], wsem.at[b])

        g = {j: gather(j) for j in range(SC_BUFFERS - 1)}
        w = {}
        for j in range(len(items)):
            g[j].wait()
            w[j] = write(j)
            if j >= 1:
                w[j - 1].wait()
            if j + SC_BUFFERS - 1 < len(items):
                g[j + SC_BUFFERS - 1] = gather(j + SC_BUFFERS - 1)
        w[len(items) - 1].wait()

    return pl.kernel(
        body, mesh=_sc_mesh(),
        out_type=jax.ShapeDtypeStruct((TOP_K, n, ROW_WORDS), jnp.int32),
        scratch_types=[pltpu.VMEM((n_chunks, TOP_K, SC_CHUNK), jnp.int32),
                       pltpu.VMEM((SC_BUFFERS, SC_CHUNK, ROW_WORDS), jnp.int32),
                       pltpu.SemaphoreType.DMA((SC_BUFFERS,)), pltpu.SemaphoreType.DMA((SC_BUFFERS,))],
        compiler_params=pltpu.CompilerParams(use_tc_tiling_on_sc=True),
        name="moe_collect",
    )(ys, pos)


def _expert_gemm_kernel(plan_ref, after_ref, xs_ref, wg_ref, wu_ref, wd_ref, ys_ref, wgu_scr, wd_scr):
    del after_ref
    i = pl.program_id(0)
    e = plan_ref[0, i]
    prev = plan_ref[0, jnp.maximum(i - 1, 0)]
    live = plan_ref[1, i]

    @pl.when((i == 0) | (e != prev))
    def _():
        wgu_scr[:, 0:D_EXPERT] = wg_ref[...].astype(BF16)
        wgu_scr[:, D_EXPERT:2 * D_EXPERT] = wu_ref[...].astype(BF16)
        wd_scr[...] = wd_ref[...].astype(BF16)

    def swiglu_rows(n_rows):
        lo, hi = _unpack_bf16_pairs(xs_ref[0:n_rows, :])
        gu = _dot(lo.astype(BF16), wgu_scr[0:ROW_WORDS, :]) + _dot(hi.astype(BF16), wgu_scr[ROW_WORDS:D_MODEL, :])
        h = _silu(gu[:, 0:D_EXPERT]) * gu[:, D_EXPERT:2 * D_EXPERT]
        y = _dot(h.astype(BF16), wd_scr[...])
        ys_ref[0:n_rows, :] = _pack_bf16_pairs_native(y)

    @pl.when((i < plan_ref[2, 0]) & (live > ROW_TILE // 2))
    def _():
        swiglu_rows(ROW_TILE)

    @pl.when((i < plan_ref[2, 0]) & (live <= ROW_TILE // 2))
    def _():
        swiglu_rows(ROW_TILE // 2)


def _expert_gemm_call(after, plan, xs, w_e_gate, w_e_up, w_e_down):
    n_slots = xs.shape[0]
    n_row_tiles = n_slots // ROW_TILE
    rows = pl.BlockSpec((ROW_TILE, ROW_WORDS), lambda i, plan: (jnp.minimum(i, plan[2, 0] - 1), 0))
    expert = lambda i, plan: (plan[0, i], 0, 0)
    return pl.pallas_call(
        _expert_gemm_kernel,
        grid_spec=pltpu.PrefetchScalarGridSpec(
            num_scalar_prefetch=1,
            grid=(n_row_tiles,),
            in_specs=[
                pl.BlockSpec(memory_space=pl.ANY),
                rows,
                pl.BlockSpec((None, D_MODEL, D_EXPERT), expert),
                pl.BlockSpec((None, D_MODEL, D_EXPERT), expert),
                pl.BlockSpec((None, D_EXPERT, D_MODEL), expert),
            ],
            out_specs=rows,
            scratch_shapes=[pltpu.VMEM((D_MODEL, 2 * D_EXPERT), BF16), pltpu.VMEM((D_EXPERT, D_MODEL), BF16)],
        ),
        out_shape=jax.ShapeDtypeStruct((n_slots, ROW_WORDS), jnp.int32),
        compiler_params=pltpu.CompilerParams(dimension_semantics=("arbitrary",), vmem_limit_bytes=VMEM_LIMIT),
        name="moe_experts",
    )(plan, after, xs, w_e_gate, w_e_up, w_e_down)


def _moe_out_kernel(yt_ref, w_ref, t_ref, x1_ref, mod_ref, post2_ref, wsg_ref, wsu_ref, wsd_ref, o_ref):
    lo, hi = _unpack_bf16_pairs(t_ref[...])
    lo = lo.astype(BF16)
    hi = hi.astype(BF16)
    g = _dot(lo, wsg_ref[0:ROW_WORDS, :]) + _dot(hi, wsg_ref[ROW_WORDS:D_MODEL, :])
    u = _dot(lo, wsu_ref[0:ROW_WORDS, :]) + _dot(hi, wsu_ref[ROW_WORDS:D_MODEL, :])
    shared = _dot((_silu(g) * u).astype(BF16), wsd_ref[...])
    acc_lo = shared[:, 0:ROW_WORDS]
    acc_hi = shared[:, ROW_WORDS:D_MODEL]
    for k in range(TOP_K):
        y_lo, y_hi = _unpack_bf16_pairs(yt_ref[k])
        wk = w_ref[:, k:k + 1]
        acc_lo = acc_lo + wk * y_lo
        acc_hi = acc_hi + wk * y_hi
    moe = jnp.concatenate([acc_lo, acc_hi], axis=-1)
    o_ref[...] = x1_ref[...] + mod_ref[5:6, :] * _rms_norm(moe, post2_ref[...])


def _moe_out_into_kernel(prev_ref, *refs):
    del prev_ref
    _moe_out_kernel(*refs)


def _moe_out_call(prev_out, n_total, b0, yt, w, t, x1, mod_x, post2_g, wsg_bf, wsu_bf, wsd_bf, seq):
    n = t.shape[0]
    tiles_per_seq = seq // TOUT
    first_tile = b0 * tiles_per_seq
    const = lambda i: (0, 0)
    in_specs = [
        pl.BlockSpec((TOP_K, TOUT, ROW_WORDS), lambda i: (0, i, 0)),
        pl.BlockSpec((TOUT, TOP_K), lambda i: (i, 0)),
        pl.BlockSpec((TOUT, ROW_WORDS), lambda i: (i, 0)),
        pl.BlockSpec((TOUT, D_MODEL), lambda i: (i, 0)),
        pl.BlockSpec((None, 6, D_MODEL), lambda i: (b0 + i // tiles_per_seq, 0, 0)),
        pl.BlockSpec((1, D_MODEL), const),
        pl.BlockSpec((D_MODEL, D_EXPERT), const),
        pl.BlockSpec((D_MODEL, D_EXPERT), const),
        pl.BlockSpec((D_EXPERT, D_MODEL), const),
    ]
    args = (yt, w, t, x1, mod_x, post2_g, wsg_bf, wsu_bf, wsd_bf)
    aliased = prev_out is not None
    return pl.pallas_call(
        _moe_out_into_kernel if aliased else _moe_out_kernel,
        grid=(n // TOUT,),
        in_specs=([pl.BlockSpec(memory_space=pl.ANY)] if aliased else []) + in_specs,
        out_specs=pl.BlockSpec((TOUT, D_MODEL), lambda i: (first_tile + i, 0)),
        out_shape=jax.ShapeDtypeStruct((n_total, D_MODEL), F32),
        input_output_aliases={0: 0} if aliased else {},
        compiler_params=pltpu.CompilerParams(dimension_semantics=("arbitrary",), vmem_limit_bytes=VMEM_LIMIT),
        name="moe_out",
    )(*(((prev_out,) if aliased else ()) + args))


def _moe_dispatch(after, hx2w, idx_t):
    n = hx2w.shape[0]
    n_slots = n * TOP_K + N_EXPERTS * ROW_TILE
    pos_t, plan = _positions_call(after, idx_t, n_slots // ROW_TILE)
    pos = pos_t.reshape(TOP_K, SC_WORKERS, n // (SC_WORKERS * SC_CHUNK), SC_CHUNK).transpose(1, 2, 0, 3)
    return _dispatch_call(hx2w, pos, n_slots), pos, plan


def kernel(x, c, ctx, c_ctx, w_mod, b_mod, pre1_g, post1_g, pre2_g, post2_g, w_in, w_dw, b_dw, ln_conv_g, ln_conv_b, w_conv_out, w_sc, b_sc, w_rg_a, b_rg_a, w_rg_x, b_rg_x, lru_lambda, w_rnn_out, w_out, w_router, router_bias, w_e_gate, w_e_up, w_e_down, w_s_gate, w_s_up, w_s_down):
    assert w_mod.shape[0] == 1, "single-layer block"
    bsz, seq, d = x.shape
    n = bsz * seq
    assert d == D_MODEL and seq % TM == 0 and seq % TOUT == 0 and bsz + 1 <= SUBLANES
    assert 0 < TAIL_BATCHES < bsz and seq % (SC_WORKERS * SC_CHUNK) == 0 and seq % POS_TILE == 0

    cc = jnp.zeros((SUBLANES, D_MODEL), F32).at[:bsz].set(c).at[bsz].set(c_ctx)
    mod = _mod_call(cc, w_mod[0], b_mod)
    mod_x = mod[:bsz].reshape(bsz, 6, D_MODEL)
    mod_c = mod[bsz].reshape(6, D_MODEL)

    w_in_bf = w_in[0].astype(BF16)
    wa_bf = w_rg_a[0].astype(BF16)
    wx_bf = w_rg_x[0].astype(BF16)

    h0 = _ctx_call(ctx, mod_c, pre1_g, w_in_bf, w_sc[0], b_sc[0], wa_bf, b_rg_a[0], wx_bf, b_rg_x[0],
                   lru_lambda[0])

    wco_bf, wro_bf, wout_bf = w_conv_out[0].astype(BF16), w_rnn_out[0].astype(BF16), w_out[0].astype(BF16)
    wsg_bf, wsu_bf, wsd_bf = w_s_gate[0].astype(BF16), w_s_up[0].astype(BF16), w_s_down[0].astype(BF16)
    w_router_t, rbias = w_router[0].T, router_bias.reshape(N_EXPERTS, 1)
    group_a = (0, bsz - TAIL_BATCHES)
    group_b = (bsz - TAIL_BATCHES, TAIL_BATCHES)
    none = jnp.zeros((TOP_K, LANES), jnp.int32)

    def fwd(grp, after):
        return _mixer_fwd_call(grp[0], grp[1], after, x, mod_x, pre1_g, w_in_bf, w_dw[0], b_dw, ln_conv_g,
                               ln_conv_b, wco_bf, w_sc[0], b_sc[0], wa_bf, b_rg_a[0], wx_bf, b_rg_x[0],
                               lru_lambda[0], h0)

    def bwd(grp, after, acts):
        gaya, gb, gbr, urnn, hf = acts
        x1, hx2w, idx_t, w_t = _mixer_bwd_call(
            grp[0], grp[1], after, urnn, hf, gbr, gaya, gb, x, mod_x, w_sc[0], b_sc[0], wa_bf, b_rg_a[0], wx_bf,
            b_rg_x[0], lru_lambda[0], h0, wro_bf, wout_bf, post1_g, pre2_g, w_router_t, rbias)
        return x1.reshape(grp[1] * seq, D_MODEL), hx2w.reshape(grp[1] * seq, ROW_WORDS), idx_t, w_t

    def experts(after, plan, xs):
        return _expert_gemm_call(after, plan, xs, w_e_gate[0], w_e_up[0], w_e_down[0])

    def finish(grp, prev_out, yt, w_t, hx2w, x1):
        return _moe_out_call(prev_out, n, grp[0], yt, w_t.T, hx2w, x1, mod_x, post2_g, wsg_bf, wsu_bf, wsd_bf, seq)

    x1_a, hx2w_a, idx_a, w_a = bwd(group_a, none, fwd(group_a, none))
    xs_a, pos_a, plan_a = _moe_dispatch(none, hx2w_a, idx_a)
    acts_b = fwd(group_b, idx_a)
    ys_a = experts(acts_b[4], plan_a, xs_a)
    yt_a = _collect_call(ys_a, pos_a, group_a[1] * seq)
    x1_b, hx2w_b, idx_b, w_b = bwd(group_b, ys_a, acts_b)
    xs_b, pos_b, plan_b = _moe_dispatch(yt_a, hx2w_b, idx_b)
    out = finish(group_a, None, yt_a, w_a, hx2w_a, x1_a)
    ys_b = experts(none, plan_b, xs_b)
    yt_b = _collect_call(ys_b, pos_b, group_b[1] * seq)
    out = finish(group_b, out, yt_b, w_b, hx2w_b, x1_b)
    return out.reshape(bsz, seq, D_MODEL)
```

```python
import jax
import jax.numpy as jnp
from jax import lax
from jax.experimental import pallas as pl
from jax.experimental.pallas import tpu as pltpu
from jax.experimental.pallas import tpu_sc as plsc

F32 = jnp.float32
BF16 = jnp.bfloat16

D_MODEL = 1024
GRID_W = 64
CONV_WIDTH = 31
CONV_HALF = CONV_WIDTH // 2
SHORT_CONV = 4
RNN_HEADS = 4
RNN_BLOCK = D_MODEL // RNN_HEADS
LRU_C = 8.0
N_EXPERTS = 64
N_GROUPS = 8
GROUP_SIZE = N_EXPERTS // N_GROUPS
TOPK_GROUPS = 4
TOP_K = 8
D_EXPERT = 256
ROUTED_SCALE = 2.5
EPS = 1e-6

SUBLANES = 8
TM = 512
ROWS_PER_TILE = TM // GRID_W
PAD = 16
ROW_STRIDE = GRID_W + PAD
UPAD_ROWS = ROWS_PER_TILE * ROW_STRIDE + PAD
SEG = TM // SUBLANES
HALO = (SHORT_CONV - 1) * SUBLANES
CONV_LANES = 256
LANES = 128
ROW_WORDS = D_MODEL // 2
ROW_TILE = 1024
POS_TILE = 512
TOUT = 256
TOKEN_GROUPS = 2
V7X_SC_CORES = 2
V7X_SC_SUBCORES = 16
SC_WORKERS = V7X_SC_CORES * V7X_SC_SUBCORES
SC_CHUNK = 64
SC_BUFFERS = 2
VMEM_LIMIT = 58 * 1024 * 1024


def _sigmoid(x):
    return 0.5 * (jnp.tanh(0.5 * x) + 1.0)


def _silu(x):
    return x * _sigmoid(x)


def _gelu_tanh(x):
    return 0.5 * x * (1.0 + jnp.tanh(0.7978845608028654 * (x + 0.044715 * (x * x * x))))


def _rms_norm(x, g):
    return x * lax.rsqrt(jnp.mean(x * x, axis=-1, keepdims=True) + EPS) * g


def _dot(a, b):
    return jnp.dot(a, b, preferred_element_type=F32)


def _pack_bf16_pairs(x):
    half = x.shape[-1] // 2
    lo = lax.bitcast_convert_type(x[:, :half].astype(BF16).astype(F32), jnp.uint32)
    hi = lax.bitcast_convert_type(x[:, half:].astype(BF16).astype(F32), jnp.uint32)
    return lax.bitcast_convert_type(hi | (lo >> 16), jnp.int32)


def _pack_bf16_pairs_native(x):
    half = x.shape[-1] // 2
    packed = pltpu.pack_elementwise([x[:, :half], x[:, half:]], packed_dtype=BF16)
    return lax.bitcast_convert_type(packed, jnp.int32)


def _unpack_bf16_pairs(words):
    u = lax.bitcast_convert_type(words, jnp.uint32)
    return (lax.bitcast_convert_type(u << 16, F32),
            lax.bitcast_convert_type(u & jnp.uint32(0xFFFF0000), F32))


def _log_decay(lam):
    return LRU_C * (jnp.minimum(lam, 0.0) - jnp.log1p(jnp.exp(-jnp.abs(lam))))


def _rglru_coeffs(v, wa_ref, ba, wx_ref, bx, c_lam, a_ref, b_ref):
    vb = v.astype(BF16)
    for h in range(RNN_HEADS):
        cs = slice(h * RNN_BLOCK, (h + 1) * RNN_BLOCK)
        vh = vb[:, cs]
        r = _sigmoid(_dot(vh, wa_ref[h]) + ba[:, cs])
        i = _sigmoid(_dot(vh, wx_ref[h]) + bx[:, cs])
        log_a = c_lam[:, cs] * r
        a = jnp.exp(log_a)
        a_ref[:, cs] = a
        b_ref[:, cs] = jnp.sqrt(jnp.tanh(-log_a) * (1.0 + a * a)) * (i * v[:, cs])


def _scan_tile(a_ref, b_ref, carry, n_rows, reverse):
    row = lax.broadcasted_iota(jnp.int32, (SUBLANES, D_MODEL), 0)
    n_groups = n_rows // SUBLANES

    def body(j, carry):
        g = (n_groups - 1 - j) if reverse else j
        off = pl.multiple_of(g * SUBLANES, SUBLANES)
        a = a_ref[pl.ds(off, SUBLANES), :]
        b = b_ref[pl.ds(off, SUBLANES), :]
        for s in (1, 2, 4):
            keep = (row < SUBLANES - s) if reverse else (row >= s)
            shift = (SUBLANES - s) if reverse else s
            a_sh = jnp.where(keep, pltpu.roll(a, shift, 0), 1.0)
            b_sh = jnp.where(keep, pltpu.roll(b, shift, 0), 0.0)
            b = a * b_sh + b
            a = a * a_sh
        h = a * carry + b
        b_ref[pl.ds(off, SUBLANES), :] = h
        last = h[0:1, :] if reverse else h[SUBLANES - 1:SUBLANES, :]
        return jnp.broadcast_to(last, (SUBLANES, D_MODEL))

    return lax.fori_loop(0, n_groups, body, carry, unroll=2)


def _segment_perm(to_natural):
    r = lax.broadcasted_iota(jnp.int32, (TM, TM), 0)
    c = lax.broadcasted_iota(jnp.int32, (TM, TM), 1)
    if to_natural:
        src = (r & (SEG - 1)) * SUBLANES + lax.shift_right_logical(r, SEG.bit_length() - 1)
    else:
        src = (r & (SUBLANES - 1)) * SEG + lax.shift_right_logical(r, SUBLANES.bit_length() - 1)
    return jnp.where(c == src, 1.0, 0.0).astype(BF16)


def _scan_segments(a_ref, b_ref, carry, reverse):
    row = lax.broadcasted_iota(jnp.int32, (SUBLANES, D_MODEL), 0)

    def body(s, hp):
        h, prod = hp
        off = pl.multiple_of(((SEG - 1 - s) if reverse else s) * SUBLANES, SUBLANES)
        a = a_ref[pl.ds(off, SUBLANES), :]
        h = a * h + b_ref[pl.ds(off, SUBLANES), :]
        prod = a * prod
        b_ref[pl.ds(off, SUBLANES), :] = h
        a_ref[pl.ds(off, SUBLANES), :] = prod
        return h, prod

    b, a = lax.fori_loop(0, SEG, body, (jnp.zeros((SUBLANES, D_MODEL), F32), jnp.ones((SUBLANES, D_MODEL), F32)),
                         unroll=4)
    for s in (1, 2, 4):
        keep = (row < SUBLANES - s) if reverse else (row >= s)
        shift = (SUBLANES - s) if reverse else s
        a_sh = jnp.where(keep, pltpu.roll(a, shift, 0), 1.0)
        b_sh = jnp.where(keep, pltpu.roll(b, shift, 0), 0.0)
        b = a * b_sh + b
        a = a * a_sh
    leave = a * carry + b
    if reverse:
        enter = jnp.where(row < SUBLANES - 1, pltpu.roll(leave, SUBLANES - 1, 0), carry)
        last = leave[0:1, :]
    else:
        enter = jnp.where(row >= 1, pltpu.roll(leave, 1, 0), carry)
        last = leave[SUBLANES - 1:SUBLANES, :]
    h = b_ref[...].reshape(SEG, SUBLANES, D_MODEL) + a_ref[...].reshape(SEG, SUBLANES, D_MODEL) * enter[None]
    b_ref[...] = h.reshape(TM, D_MODEL)
    return jnp.broadcast_to(last, (SUBLANES, D_MODEL))


def _short_conv_segments(uext_ref, wsc_ref, bsc_ref):
    v = bsc_ref[...] + wsc_ref[0:1, :] * uext_ref[0:TM, :]
    for k in range(1, SHORT_CONV):
        v = v + wsc_ref[k:k + 1, :] * uext_ref[k * SUBLANES:k * SUBLANES + TM, :]
    return v


def _mod_kernel(c_ref, w_ref, b_ref, o_ref):
    o_ref[...] = jnp.dot(_silu(c_ref[...]), w_ref[...], preferred_element_type=F32,
                         precision=lax.Precision.HIGHEST) + b_ref[...]


def _mod_call(cc, w_mod, b_mod):
    n_mod = w_mod.shape[1]
    return pl.pallas_call(
        _mod_kernel,
        grid=(n_mod // D_MODEL,),
        in_specs=[
            pl.BlockSpec((SUBLANES, D_MODEL), lambda j: (0, 0)),
            pl.BlockSpec((D_MODEL, D_MODEL), lambda j: (0, j)),
            pl.BlockSpec((1, D_MODEL), lambda j: (0, j)),
        ],
        out_specs=pl.BlockSpec((SUBLANES, D_MODEL), lambda j: (0, j)),
        out_shape=jax.ShapeDtypeStruct((SUBLANES, n_mod), F32),
        name="mod",
    )(cc, w_mod, b_mod)


def _ctx_kernel(ctx_ref, mod_ref, g_ref, w_ref, wsc_ref, bsc_ref, wa_ref, ba_ref, wx_ref, bx_ref,
                lam_ref, o_ref, uext_ref, a_ref, b_ref):
    n = ctx_ref.shape[0]
    hc = _rms_norm(ctx_ref[...], g_ref[...]) * (1.0 + mod_ref[1:2, :]) + mod_ref[0:1, :]
    u = _dot(hc.astype(BF16), w_ref[...])
    zeros8 = jnp.zeros((SUBLANES, D_MODEL), F32)
    uext_ref[0:SUBLANES, :] = zeros8
    uext_ref[SUBLANES:SUBLANES + n, :] = u
    uext_ref[SUBLANES + n:2 * SUBLANES + n, :] = zeros8
    for d in range(2):
        v = jnp.broadcast_to(bsc_ref[d:d + 1, :], (n, D_MODEL))
        for k in range(SHORT_CONV):
            start = SUBLANES + k - (SHORT_CONV - 1) * (1 - d)
            v = v + wsc_ref[d, k:k + 1, :] * uext_ref[start:start + n, :]
        _rglru_coeffs(v, wa_ref.at[d], ba_ref[d:d + 1, :], wx_ref.at[d], bx_ref[d:d + 1, :],
                      _log_decay(lam_ref[d:d + 1, :]), a_ref, b_ref)
        final = _scan_tile(a_ref, b_ref, zeros8, n, reverse=(d == 1))
        o_ref[d:d + 1, :] = final[0:1, :]


def _ctx_call(ctx, mod_c, pre1_g, w_in_bf, w_sc, b_sc, wa_bf, b_rg_a, wx_bf, b_rg_x, lam):
    bsz, n, _ = ctx.shape
    const2 = lambda b: (0, 0)
    const3 = lambda b: (0, 0, 0)
    const4 = lambda b: (0, 0, 0, 0)
    return pl.pallas_call(
        _ctx_kernel,
        grid=(bsz,),
        in_specs=[
            pl.BlockSpec((None, n, D_MODEL), lambda b: (b, 0, 0)),
            pl.BlockSpec((6, D_MODEL), const2),
            pl.BlockSpec((1, D_MODEL), const2),
            pl.BlockSpec((D_MODEL, D_MODEL), lambda b: (0, 2)),
            pl.BlockSpec((2, SHORT_CONV, D_MODEL), const3),
            pl.BlockSpec((2, D_MODEL), const2),
            pl.BlockSpec((2, RNN_HEADS, RNN_BLOCK, RNN_BLOCK), const4),
            pl.BlockSpec((2, D_MODEL), const2),
            pl.BlockSpec((2, RNN_HEADS, RNN_BLOCK, RNN_BLOCK), const4),
            pl.BlockSpec((2, D_MODEL), const2),
            pl.BlockSpec((2, D_MODEL), const2),
        ],
        out_specs=pl.BlockSpec((None, 2, D_MODEL), lambda b: (b, 0, 0)),
        out_shape=jax.ShapeDtypeStruct((bsz, 2, D_MODEL), F32),
        scratch_shapes=[
            pltpu.VMEM((n + 2 * SUBLANES, D_MODEL), F32),
            pltpu.VMEM((n, D_MODEL), F32),
            pltpu.VMEM((n, D_MODEL), F32),
        ],
        name="ctx",
    )(ctx, mod_c, pre1_g, w_in_bf, w_sc, b_sc, wa_bf, b_rg_a, wx_bf, b_rg_x, lam)


def _mixer_fwd_kernel(after_ref, x_ref, mod_ref, g_ref, win_ref, wdw_ref, bdw_ref, lng_ref, lnb_ref, wco_ref,
                      wsc_ref, bsc_ref, wa_ref, ba_ref, wx_ref, bx_ref, lam_ref, h0_ref,
                      gaya_ref, gb_ref, gbr_ref, urnn_ref, hf_ref,
                      upad_ref, ush_ref, wb_ref, cv_ref, uext_ref, a_ref, b_ref, carry_ref, halo_ref, perm_ref):
    del after_ref
    j = pl.program_id(1)

    @pl.when(j == 0)
    def _():
        carry_ref[...] = jnp.broadcast_to(h0_ref[0:1, :], (SUBLANES, D_MODEL))
        halo_ref[...] = jnp.zeros((SUBLANES, D_MODEL), F32)
        perm_ref[...] = _segment_perm(to_natural=False)
        zeros_pad = jnp.zeros((PAD, D_MODEL), F32)
        for r in range(ROWS_PER_TILE + 1):
            upad_ref[r * ROW_STRIDE:r * ROW_STRIDE + PAD, :] = zeros_pad
        for k in range(CONV_WIDTH):
            wb_ref[k] = jnp.broadcast_to(wdw_ref[k:k + 1, :], (SUBLANES, D_MODEL))

    hx = (_rms_norm(x_ref[...], g_ref[...]) * (1.0 + mod_ref[1:2, :]) + mod_ref[0:1, :]).astype(BF16)

    u = _dot(hx, win_ref[:, 0:D_MODEL]) * _sigmoid(_dot(hx, win_ref[:, D_MODEL:2 * D_MODEL]))
    for r in range(ROWS_PER_TILE):
        upad_ref[PAD + r * ROW_STRIDE:PAD + r * ROW_STRIDE + GRID_W, :] = u[r * GRID_W:(r + 1) * GRID_W, :]
    vregs_per_row = GRID_W // SUBLANES
    hxp = _dot(perm_ref[...], hx).astype(BF16)
    g_a = None
    for c in range(D_MODEL // CONV_LANES):
        if c == 0:
            gb_ref[...] = _sigmoid(_dot(hxp, win_ref[:, 5 * D_MODEL:6 * D_MODEL])).astype(BF16)
        elif c == 1:
            gbr_ref[...] = _gelu_tanh(_dot(hxp, win_ref[:, 3 * D_MODEL:4 * D_MODEL])).astype(BF16)
        elif c == 2:
            g_a = _sigmoid(_dot(hxp, win_ref[:, 4 * D_MODEL:5 * D_MODEL]))
        else:
            ur = _dot(hxp, win_ref[:, 2 * D_MODEL:3 * D_MODEL])
            urnn_ref[...] = ur.astype(BF16)
            uext_ref[HALO:HALO + TM, :] = ur
        cs = slice(c * CONV_LANES, (c + 1) * CONV_LANES)
        xpad = upad_ref[:, cs]
        for s in range(1, SUBLANES):
            ush_ref[s - 1] = pltpu.roll(xpad, UPAD_ROWS - s, 0)
        for r in range(ROWS_PER_TILE):
            acc = jnp.broadcast_to(bdw_ref[:, cs].reshape(1, 1, CONV_LANES), (vregs_per_row, SUBLANES, CONV_LANES))
            for k in range(CONV_WIDTH):
                q, s = divmod(r * ROW_STRIDE + PAD - CONV_HALF + k, SUBLANES)
                rows = slice(q * SUBLANES, q * SUBLANES + GRID_W)
                win = upad_ref[rows, cs] if s == 0 else ush_ref[s - 1, rows, :]
                acc = acc + wb_ref[k, :, cs] * win.reshape(vregs_per_row, SUBLANES, CONV_LANES)
            cv_ref[r * GRID_W:(r + 1) * GRID_W, cs] = acc.reshape(GRID_W, CONV_LANES)
    cv = cv_ref[...]
    cvc = cv - jnp.mean(cv, axis=-1, keepdims=True)
    cvn = cvc * lax.rsqrt(jnp.mean(cvc * cvc, axis=-1, keepdims=True) + EPS) * lng_ref[...] + lnb_ref[...]
    y_a = _dot(_dot(perm_ref[...], _silu(cvn).astype(BF16)).astype(BF16), wco_ref[...])

    gaya_ref[...] = (g_a * y_a).astype(BF16)

    row8 = lax.broadcasted_iota(jnp.int32, (SUBLANES, D_MODEL), 0)
    for jj in range(SHORT_CONV - 1):
        late = uext_ref[HALO + (SEG - (SHORT_CONV - 1) + jj) * SUBLANES:HALO + (SEG - (SHORT_CONV - 1) + jj + 1) * SUBLANES, :]
        prev_tile = jnp.broadcast_to(halo_ref[jj:jj + 1, :], (SUBLANES, D_MODEL))
        uext_ref[jj * SUBLANES:(jj + 1) * SUBLANES, :] = jnp.where(row8 == 0, prev_tile, pltpu.roll(late, 1, 0))
        halo_ref[jj:jj + 1, :] = late[SUBLANES - 1:SUBLANES, :]
    v = _short_conv_segments(uext_ref, wsc_ref, bsc_ref)
    _rglru_coeffs(v, wa_ref, ba_ref[...], wx_ref, bx_ref[...], _log_decay(lam_ref[...]), a_ref, b_ref)
    carry_ref[...] = _scan_segments(a_ref, b_ref, carry_ref[...], reverse=False)
    hf_ref[...] = b_ref[...].astype(BF16)


def _resident(shape):
    nd = len(shape)
    return pl.BlockSpec(shape, lambda b, j: (0,) * nd, pipeline_mode=pl.Buffered(1))


def _mixer_fwd_call(b0, bsz, after, x, mod_x, pre1_g, w_in_bf, w_dw, b_dw, ln_g, ln_b, wco_bf,
                    w_sc, b_sc, wa_bf, b_rg_a, wx_bf, b_rg_x, lam, h0):
    seq = x.shape[1]
    nt = seq // TM
    tile = pl.BlockSpec((None, TM, D_MODEL), lambda b, j: (b, j, 0))
    act = jax.ShapeDtypeStruct((bsz, seq, D_MODEL), BF16)
    head_w = pl.BlockSpec((None, RNN_HEADS, RNN_BLOCK, RNN_BLOCK), lambda b, j: (0, 0, 0, 0),
                          pipeline_mode=pl.Buffered(1))
    dir_row = pl.BlockSpec((None, 1, D_MODEL), lambda b, j: (0, 0, 0), pipeline_mode=pl.Buffered(1))
    return pl.pallas_call(
        _mixer_fwd_kernel,
        grid=(bsz, nt),
        in_specs=[
            pl.BlockSpec(memory_space=pl.ANY),
            pl.BlockSpec((None, TM, D_MODEL), lambda b, j: (b0 + b, j, 0)),
            pl.BlockSpec((None, 6, D_MODEL), lambda b, j: (b0 + b, 0, 0)),
            _resident((1, D_MODEL)),
            _resident((D_MODEL, 6 * D_MODEL)),
            _resident((CONV_WIDTH, D_MODEL)),
            _resident((1, D_MODEL)),
            _resident((1, D_MODEL)),
            _resident((1, D_MODEL)),
            _resident((D_MODEL, D_MODEL)),
            pl.BlockSpec((None, SHORT_CONV, D_MODEL), lambda b, j: (0, 0, 0), pipeline_mode=pl.Buffered(1)),
            dir_row, head_w, dir_row, head_w, dir_row, dir_row,
            pl.BlockSpec((None, 2, D_MODEL), lambda b, j: (b0 + b, 0, 0)),
        ],
        out_specs=[tile] * 5,
        out_shape=[act] * 5,
        scratch_shapes=[
            pltpu.VMEM((UPAD_ROWS, D_MODEL), F32),
            pltpu.VMEM((SUBLANES - 1, UPAD_ROWS, CONV_LANES), F32),
            pltpu.VMEM((CONV_WIDTH, SUBLANES, D_MODEL), F32),
            pltpu.VMEM((TM, D_MODEL), F32),
            pltpu.VMEM((TM + HALO, D_MODEL), F32),
            pltpu.VMEM((TM, D_MODEL), F32),
            pltpu.VMEM((TM, D_MODEL), F32),
            pltpu.VMEM((SUBLANES, D_MODEL), F32),
            pltpu.VMEM((SUBLANES, D_MODEL), F32),
            pltpu.VMEM((TM, TM), BF16),
        ],
        compiler_params=pltpu.CompilerParams(
            dimension_semantics=("arbitrary", "arbitrary"), vmem_limit_bytes=VMEM_LIMIT),
        name="mixer_fwd",
    )(after, x, mod_x, pre1_g, w_in_bf, w_dw, b_dw, ln_g, ln_b, wco_bf,
      w_sc, b_sc.reshape(2, 1, D_MODEL), wa_bf, b_rg_a.reshape(2, 1, D_MODEL), wx_bf,
      b_rg_x.reshape(2, 1, D_MODEL), lam.reshape(2, 1, D_MODEL), h0)


def _route(logits_t, bias):
    t = logits_t.shape[1]
    scores = _sigmoid(logits_t)
    sel = scores + bias
    neg_inf = jnp.float32(-jnp.inf)

    sel3 = sel.reshape(N_GROUPS, GROUP_SIZE, t)
    within = lax.broadcasted_iota(jnp.int32, sel3.shape, 1)
    m1 = jnp.max(sel3, axis=1, keepdims=True)
    first = jnp.min(jnp.where(sel3 == m1, within, GROUP_SIZE), axis=1, keepdims=True)
    m2 = jnp.max(jnp.where(within == first, neg_inf, sel3), axis=1, keepdims=True)
    gscore = (m1 + m2).reshape(N_GROUPS, t)

    gidx = lax.broadcasted_iota(jnp.int32, gscore.shape, 0)
    rank = jnp.zeros(gscore.shape, jnp.int32)
    for g in range(N_GROUPS):
        other = gscore[g:g + 1, :]
        beats = jnp.where(other > gscore, 1, jnp.where((other == gscore) & (gidx > g), 1, 0))
        rank = rank + beats
    gkeep = (rank < TOPK_GROUPS).reshape(N_GROUPS, 1, t)
    masked = jnp.where(gkeep, sel3, neg_inf).reshape(N_EXPERTS, t)

    eidx = lax.broadcasted_iota(jnp.int32, masked.shape, 0)
    picks, weights = [], []
    for _ in range(TOP_K):
        m = jnp.max(masked, axis=0, keepdims=True)
        first = jnp.min(jnp.where(masked == m, eidx, N_EXPERTS), axis=0, keepdims=True)
        pick = eidx == first
        picks.append(first)
        weights.append(jnp.sum(jnp.where(pick, scores, 0.0), axis=0, keepdims=True))
        masked = jnp.where(pick, neg_inf, masked)
    idx = jnp.concatenate(picks, axis=0)
    w = jnp.concatenate(weights, axis=0)
    return idx, ROUTED_SCALE * w / jnp.sum(w, axis=0, keepdims=True)


def _mixer_bwd_kernel(after_ref, urnn_ref, hf_ref, gbr_ref, gaya_ref, gb_ref, x_ref, mod_ref,
                      wsc_ref, bsc_ref, wa_ref, ba_ref, wx_ref, bx_ref, lam_ref, h0_ref,
                      wro_ref, wout_ref, post1_ref, pre2_ref, wrt_ref, rbias_ref,
                      x1_ref, hx2w_ref, idx_ref, w_ref,
                      uext_ref, a_ref, b_ref, carry_ref, halo_ref, perm_ref):
    del after_ref
    j = pl.program_id(1)

    @pl.when(j == 0)
    def _():
        carry_ref[...] = jnp.broadcast_to(h0_ref[1:2, :], (SUBLANES, D_MODEL))
        halo_ref[...] = jnp.zeros((SUBLANES, D_MODEL), F32)
        perm_ref[...] = _segment_perm(to_natural=True)

    uext_ref[0:TM, :] = urnn_ref[...].astype(F32)
    row8 = lax.broadcasted_iota(jnp.int32, (SUBLANES, D_MODEL), 0)
    for jj in range(SHORT_CONV - 1):
        early = uext_ref[jj * SUBLANES:(jj + 1) * SUBLANES, :]
        next_tile = jnp.broadcast_to(halo_ref[jj:jj + 1, :], (SUBLANES, D_MODEL))
        uext_ref[TM + jj * SUBLANES:TM + (jj + 1) * SUBLANES, :] = jnp.where(
            row8 == SUBLANES - 1, next_tile, pltpu.roll(early, SUBLANES - 1, 0))
        halo_ref[jj:jj + 1, :] = early[0:1, :]
    v = _short_conv_segments(uext_ref, wsc_ref, bsc_ref)
    _rglru_coeffs(v, wa_ref, ba_ref[...], wx_ref, bx_ref[...], _log_decay(lam_ref[...]), a_ref, b_ref)
    carry_ref[...] = _scan_segments(a_ref, b_ref, carry_ref[...], reverse=True)

    h_sum = hf_ref[...].astype(F32) + b_ref[...]
    y_b = _dot((gbr_ref[...].astype(F32) * h_sum).astype(BF16), wro_ref[...])
    mix = gaya_ref[...].astype(F32) + gb_ref[...].astype(F32) * y_b
    out = _dot(_dot(perm_ref[...], mix.astype(BF16)).astype(BF16), wout_ref[...])
    x1 = x_ref[...] + mod_ref[2:3, :] * _rms_norm(out, post1_ref[...])
    x1_ref[...] = x1

    hx2 = _rms_norm(x1, pre2_ref[...]) * (1.0 + mod_ref[4:5, :]) + mod_ref[3:4, :]
    hx2w_ref[...] = _pack_bf16_pairs(hx2)
    logits_t = lax.dot_general(wrt_ref[...], hx2, (((1,), (1,)), ((), ())),
                               preferred_element_type=F32, precision=lax.Precision.HIGHEST)
    idx, w = _route(logits_t, rbias_ref[...])
    idx_ref[...] = idx
    w_ref[...] = w


def _mixer_bwd_call(b0, bsz, after, urnn, hf, gbr, gaya, gb, x, mod_x, w_sc, b_sc, wa_bf, b_rg_a, wx_bf, b_rg_x, lam, h0,
                    wro_bf, wout_bf, post1_g, pre2_g, w_router_t, router_bias):
    seq = x.shape[1]
    nt = seq // TM
    rev = lambda b, j: (b, nt - 1 - j, 0)
    tile = pl.BlockSpec((None, TM, D_MODEL), rev)
    head_w = pl.BlockSpec((None, RNN_HEADS, RNN_BLOCK, RNN_BLOCK), lambda b, j: (1, 0, 0, 0),
                          pipeline_mode=pl.Buffered(1))
    dir_row = pl.BlockSpec((None, 1, D_MODEL), lambda b, j: (1, 0, 0), pipeline_mode=pl.Buffered(1))
    return pl.pallas_call(
        _mixer_bwd_kernel,
        grid=(bsz, nt),
        in_specs=[
            pl.BlockSpec(memory_space=pl.ANY),
            tile, tile, tile, tile, tile,
            pl.BlockSpec((None, TM, D_MODEL), lambda b, j: (b0 + b, nt - 1 - j, 0)),
            pl.BlockSpec((None, 6, D_MODEL), lambda b, j: (b0 + b, 0, 0)),
            pl.BlockSpec((None, SHORT_CONV, D_MODEL), lambda b, j: (1, 0, 0), pipeline_mode=pl.Buffered(1)),
            dir_row, head_w, dir_row, head_w, dir_row, dir_row,
            pl.BlockSpec((None, 2, D_MODEL), lambda b, j: (b0 + b, 0, 0)),
            _resident((D_MODEL, D_MODEL)),
            _resident((D_MODEL, D_MODEL)),
            _resident((1, D_MODEL)),
            _resident((1, D_MODEL)),
            _resident((N_EXPERTS, D_MODEL)),
            _resident((N_EXPERTS, 1)),
        ],
        out_specs=[
            pl.BlockSpec((None, TM, D_MODEL), rev),
            pl.BlockSpec((None, TM, ROW_WORDS), rev),
            pl.BlockSpec((TOP_K, TM), lambda b, j: (0, b * nt + nt - 1 - j)),
            pl.BlockSpec((TOP_K, TM), lambda b, j: (0, b * nt + nt - 1 - j)),
        ],
        out_shape=[
            jax.ShapeDtypeStruct((bsz, seq, D_MODEL), F32),
            jax.ShapeDtypeStruct((bsz, seq, ROW_WORDS), jnp.int32),
            jax.ShapeDtypeStruct((TOP_K, bsz * seq), jnp.int32),
            jax.ShapeDtypeStruct((TOP_K, bsz * seq), F32),
        ],
        scratch_shapes=[
            pltpu.VMEM((TM + HALO, D_MODEL), F32),
            pltpu.VMEM((TM, D_MODEL), F32),
            pltpu.VMEM((TM, D_MODEL), F32),
            pltpu.VMEM((SUBLANES, D_MODEL), F32),
            pltpu.VMEM((SUBLANES, D_MODEL), F32),
            pltpu.VMEM((TM, TM), BF16),
        ],
        compiler_params=pltpu.CompilerParams(
            dimension_semantics=("arbitrary", "arbitrary"), vmem_limit_bytes=VMEM_LIMIT),
        name="mixer_bwd",
    )(after, urnn, hf, gbr, gaya, gb, x, mod_x, w_sc, b_sc.reshape(2, 1, D_MODEL), wa_bf,
      b_rg_a.reshape(2, 1, D_MODEL), wx_bf, b_rg_x.reshape(2, 1, D_MODEL), lam.reshape(2, 1, D_MODEL), h0,
      wro_bf, wout_bf, post1_g, pre2_g, w_router_t, router_bias)


def _positions_kernel(after_ref, idx_ref, pos_ref, plan_ref):
    del after_ref
    n = idx_ref.shape[1]
    n_tiles = n // POS_TILE
    eidx = lax.broadcasted_iota(jnp.int32, (N_EXPERTS, POS_TILE), 0)

    def chosen(t):
        idx = idx_ref[:, pl.ds(pl.multiple_of(t * POS_TILE, POS_TILE), POS_TILE)]
        ch = jnp.zeros((N_EXPERTS, POS_TILE), F32)
        for k in range(TOP_K):
            ch = ch + jnp.where(eidx == idx[k:k + 1, :], 1.0, 0.0)
        return idx, ch

    def count_body(t, cnt):
        return cnt + jnp.sum(chosen(t)[1], axis=1, keepdims=True)

    cnt = lax.fori_loop(0, n_tiles, count_body, jnp.zeros((N_EXPERTS, 1), F32))
    padded = jnp.ceil(cnt * (1.0 / ROW_TILE)) * ROW_TILE
    r = lax.broadcasted_iota(jnp.int32, (N_EXPERTS, N_EXPERTS), 0)
    c = lax.broadcasted_iota(jnp.int32, (N_EXPERTS, N_EXPERTS), 1)
    off = jnp.dot(jnp.where(c < r, 1.0, 0.0), jnp.broadcast_to(padded, (N_EXPERTS, LANES)),
                  preferred_element_type=F32, precision=lax.Precision.HIGHEST)[:, 0:1]
    end = off + padded

    n_map = plan_ref.shape[1]
    tstart = lax.broadcasted_iota(jnp.int32, (N_EXPERTS, n_map), 1).astype(F32) * ROW_TILE
    te = jnp.minimum(jnp.sum(jnp.where(end <= tstart, 1, 0), axis=0, keepdims=True), N_EXPERTS - 1)
    emap = lax.broadcasted_iota(jnp.int32, (N_EXPERTS, n_map), 0)
    live_end = jnp.sum(jnp.where(emap == te, off + cnt, 0.0), axis=0, keepdims=True)
    total = jnp.sum(padded, axis=0, keepdims=True)
    plan_ref[0:1, :] = te
    plan_ref[1:2, :] = jnp.clip(live_end - tstart[0:1, :], 0.0, ROW_TILE).astype(jnp.int32)
    plan_ref[2:3, :] = jnp.broadcast_to(total * (1.0 / ROW_TILE), (1, n_map)).astype(jnp.int32)

    row = lax.broadcasted_iota(jnp.int32, (POS_TILE, POS_TILE), 0)
    col = lax.broadcasted_iota(jnp.int32, (POS_TILE, POS_TILE), 1)
    before = jnp.where(row < col, 1.0, 0.0).astype(BF16)

    def pos_body(t, carry):
        idx, ch = chosen(t)
        base = _dot(ch.astype(BF16), before) + (carry + off)
        rows = [jnp.sum(jnp.where(eidx == idx[k:k + 1, :], base, 0.0), axis=0, keepdims=True)
                for k in range(TOP_K)]
        pos_ref[:, pl.ds(pl.multiple_of(t * POS_TILE, POS_TILE), POS_TILE)] = (
            jnp.concatenate(rows, axis=0).astype(jnp.int32))
        return carry + jnp.sum(ch, axis=1, keepdims=True)

    lax.fori_loop(0, n_tiles, pos_body, jnp.zeros((N_EXPERTS, 1), F32))


def _positions_call(after, idx_t, n_row_tiles):
    n = idx_t.shape[1]
    n_map = -(-n_row_tiles // LANES) * LANES
    return pl.pallas_call(
        _positions_kernel,
        in_specs=[pl.BlockSpec(memory_space=pl.ANY), pl.BlockSpec(memory_space=pltpu.VMEM)],
        out_shape=[
            jax.ShapeDtypeStruct((TOP_K, n), jnp.int32),
            jax.ShapeDtypeStruct((3, n_map), jnp.int32),
        ],
        name="moe_positions",
    )(after, idx_t)


def _sc_mesh():
    return plsc.VectorSubcoreMesh(core_axis_name="c", subcore_axis_name="s",
                                  num_cores=V7X_SC_CORES, num_subcores=V7X_SC_SUBCORES)


def _sc_worker():
    return lax.axis_index("s") * V7X_SC_CORES + lax.axis_index("c")


def _dispatch_call(rows, pos, n_slots):
    n = rows.shape[0]
    tok_w = n // SC_WORKERS
    n_items = tok_w // SC_CHUNK

    def body(rows_hbm, pos_hbm, xs_hbm, idx_v, rows_v, lsem, ssem):
        wid = _sc_worker()
        pltpu.sync_copy(pos_hbm.at[wid], idx_v)
        base = wid * tok_w

        def load(i):
            b = i % SC_BUFFERS
            return pltpu.async_copy(rows_hbm.at[pl.ds(base + i * SC_CHUNK, SC_CHUNK)], rows_v.at[b], lsem.at[b])

        def scatter(i):
            b = i % SC_BUFFERS
            return [pltpu.async_copy(rows_v.at[b], xs_hbm.at[idx_v.at[i, k]], ssem.at[b]) for k in range(TOP_K)]

        loads = {i: load(i) for i in range(SC_BUFFERS - 1)}
        scat = {}
        for i in range(n_items):
            loads[i].wait()
            scat[i] = scatter(i)
            if i >= 1:
                for cp in scat[i - 1]:
                    cp.wait()
            if i + SC_BUFFERS - 1 < n_items:
                loads[i + SC_BUFFERS - 1] = load(i + SC_BUFFERS - 1)
        for cp in scat[n_items - 1]:
            cp.wait()

    return pl.kernel(
        body, mesh=_sc_mesh(),
        out_type=jax.ShapeDtypeStruct((n_slots, ROW_WORDS), jnp.int32),
        scratch_types=[pltpu.VMEM((n_items, TOP_K, SC_CHUNK), jnp.int32),
                       pltpu.VMEM((SC_BUFFERS, SC_CHUNK, ROW_WORDS), jnp.int32),
                       pltpu.SemaphoreType.DMA((SC_BUFFERS,)), pltpu.SemaphoreType.DMA((SC_BUFFERS,))],
        compiler_params=pltpu.CompilerParams(use_tc_tiling_on_sc=True),
        name="moe_dispatch",
    )(rows, pos)


def _collect_call(ys, pos, n):
    tok_w = n // SC_WORKERS
    n_chunks = tok_w // SC_CHUNK
    items = [(c, k) for c in range(n_chunks) for k in range(TOP_K)]

    def body(ys_hbm, pos_hbm, yt_hbm, idx_v, rows_v, gsem, wsem):
        wid = _sc_worker()
        pltpu.sync_copy(pos_hbm.at[wid], idx_v)
        base = wid * tok_w

        def gather(j):
            c, k = items[j]
            b = j % SC_BUFFERS
            return pltpu.async_copy(ys_hbm.at[idx_v.at[c, k]], rows_v.at[b], gsem.at[b])

        def write(j):
            c, k = items[j]
            b = j % SC_BUFFERS
            return pltpu.async_copy(rows_v.at[b], yt_hbm.at[k, pl.ds(base + c * SC_CHUNK, SC_CHUNK)], wsem.at[b])

        g = {j: gather(j) for j in range(SC_BUFFERS - 1)}
        w = {}
        for j in range(len(items)):
            g[j].wait()
            w[j] = write(j)
            if j >= 1:
                w[j - 1].wait()
            if j + SC_BUFFERS - 1 < len(items):
                g[j + SC_BUFFERS - 1] = gather(j + SC_BUFFERS - 1)
        w[len(items) - 1].wait()

    return pl.kernel(
        body, mesh=_sc_mesh(),
        out_type=jax.ShapeDtypeStruct((TOP_K, n, ROW_WORDS), jnp.int32),
        scratch_types=[pltpu.VMEM((n_chunks, TOP_K, SC_CHUNK), jnp.int32),
                       pltpu.VMEM((SC_BUFFERS, SC_CHUNK, ROW_WORDS), jnp.int32),
                       pltpu.SemaphoreType.DMA((SC_BUFFERS,)), pltpu.SemaphoreType.DMA((SC_BUFFERS,))],
        compiler_params=pltpu.CompilerParams(use_tc_tiling_on_sc=True),
        name="moe_collect",
    )(ys, pos)


def _expert_gemm_kernel(plan_ref, after_ref, xs_ref, wg_ref, wu_ref, wd_ref, ys_ref, wgu_scr, wd_scr):
    del after_ref
    i = pl.program_id(0)
    e = plan_ref[0, i]
    prev = plan_ref[0, jnp.maximum(i - 1, 0)]
    live = plan_ref[1, i]

    @pl.when((i == 0) | (e != prev))
    def _():
        wgu_scr[:, 0:D_EXPERT] = wg_ref[...].astype(BF16)
        wgu_scr[:, D_EXPERT:2 * D_EXPERT] = wu_ref[...].astype(BF16)
        wd_scr[...] = wd_ref[...].astype(BF16)

    def swiglu_rows(n_rows):
        lo, hi = _unpack_bf16_pairs(xs_ref[0:n_rows, :])
        gu = _dot(lo.astype(BF16), wgu_scr[0:ROW_WORDS, :]) + _dot(hi.astype(BF16), wgu_scr[ROW_WORDS:D_MODEL, :])
        h = _silu(gu[:, 0:D_EXPERT]) * gu[:, D_EXPERT:2 * D_EXPERT]
        y = _dot(h.astype(BF16), wd_scr[...])
        ys_ref[0:n_rows, :] = _pack_bf16_pairs_native(y)

    @pl.when((i < plan_ref[2, 0]) & (live > ROW_TILE // 2))
    def _():
        swiglu_rows(ROW_TILE)

    @pl.when((i < plan_ref[2, 0]) & (live <= ROW_TILE // 2))
    def _():
        swiglu_rows(ROW_TILE // 2)


def _expert_gemm_call(after, plan, xs, w_e_gate, w_e_up, w_e_down):
    n_slots = xs.shape[0]
    n_row_tiles = n_slots // ROW_TILE
    rows = pl.BlockSpec((ROW_TILE, ROW_WORDS), lambda i, plan: (jnp.minimum(i, plan[2, 0] - 1), 0))
    expert = lambda i, plan: (plan[0, i], 0, 0)
    return pl.pallas_call(
        _expert_gemm_kernel,
        grid_spec=pltpu.PrefetchScalarGridSpec(
            num_scalar_prefetch=1,
            grid=(n_row_tiles,),
            in_specs=[
                pl.BlockSpec(memory_space=pl.ANY),
                rows,
                pl.BlockSpec((None, D_MODEL, D_EXPERT), expert),
                pl.BlockSpec((None, D_MODEL, D_EXPERT), expert),
                pl.BlockSpec((None, D_EXPERT, D_MODEL), expert),
            ],
            out_specs=rows,
            scratch_shapes=[pltpu.VMEM((D_MODEL, 2 * D_EXPERT), BF16), pltpu.VMEM((D_EXPERT, D_MODEL), BF16)],
        ),
        out_shape=jax.ShapeDtypeStruct((n_slots, ROW_WORDS), jnp.int32),
        compiler_params=pltpu.CompilerParams(dimension_semantics=("arbitrary",), vmem_limit_bytes=VMEM_LIMIT),
        name="moe_experts",
    )(plan, after, xs, w_e_gate, w_e_up, w_e_down)


def _moe_out_kernel(yt_ref, w_ref, t_ref, x1_ref, mod_ref, post2_ref, wsg_ref, wsu_ref, wsd_ref, o_ref):
    lo, hi = _unpack_bf16_pairs(t_ref[...])
    lo = lo.astype(BF16)
    hi = hi.astype(BF16)
    g = _dot(lo, wsg_ref[0:ROW_WORDS, :]) + _dot(hi, wsg_ref[ROW_WORDS:D_MODEL, :])
    u = _dot(lo, wsu_ref[0:ROW_WORDS, :]) + _dot(hi, wsu_ref[ROW_WORDS:D_MODEL, :])
    shared = _dot((_silu(g) * u).astype(BF16), wsd_ref[...])
    acc_lo = shared[:, 0:ROW_WORDS]
    acc_hi = shared[:, ROW_WORDS:D_MODEL]
    for k in range(TOP_K):
        y_lo, y_hi = _unpack_bf16_pairs(yt_ref[k])
        wk = w_ref[:, k:k + 1]
        acc_lo = acc_lo + wk * y_lo
        acc_hi = acc_hi + wk * y_hi
    moe = jnp.concatenate([acc_lo, acc_hi], axis=-1)
    o_ref[...] = x1_ref[...] + mod_ref[5:6, :] * _rms_norm(moe, post2_ref[...])


def _moe_out_into_kernel(prev_ref, *refs):
    del prev_ref
    _moe_out_kernel(*refs)


def _moe_out_call(prev_out, n_total, b0, yt, w, t, x1, mod_x, post2_g, wsg_bf, wsu_bf, wsd_bf, seq):
    n = t.shape[0]
    tiles_per_seq = seq // TOUT
    first_tile = b0 * tiles_per_seq
    const = lambda i: (0, 0)
    in_specs = [
        pl.BlockSpec((TOP_K, TOUT, ROW_WORDS), lambda i: (0, i, 0)),
        pl.BlockSpec((TOUT, TOP_K), lambda i: (i, 0)),
        pl.BlockSpec((TOUT, ROW_WORDS), lambda i: (i, 0)),
        pl.BlockSpec((TOUT, D_MODEL), lambda i: (i, 0)),
        pl.BlockSpec((None, 6, D_MODEL), lambda i: (b0 + i // tiles_per_seq, 0, 0)),
        pl.BlockSpec((1, D_MODEL), const),
        pl.BlockSpec((D_MODEL, D_EXPERT), const),
        pl.BlockSpec((D_MODEL, D_EXPERT), const),
        pl.BlockSpec((D_EXPERT, D_MODEL), const),
    ]
    args = (yt, w, t, x1, mod_x, post2_g, wsg_bf, wsu_bf, wsd_bf)
    aliased = prev_out is not None
    return pl.pallas_call(
        _moe_out_into_kernel if aliased else _moe_out_kernel,
        grid=(n // TOUT,),
        in_specs=([pl.BlockSpec(memory_space=pl.ANY)] if aliased else []) + in_specs,
        out_specs=pl.BlockSpec((TOUT, D_MODEL), lambda i: (first_tile + i, 0)),
        out_shape=jax.ShapeDtypeStruct((n_total, D_MODEL), F32),
        input_output_aliases={0: 0} if aliased else {},
        compiler_params=pltpu.CompilerParams(dimension_semantics=("arbitrary",), vmem_limit_bytes=VMEM_LIMIT),
        name="moe_out",
    )(*(((prev_out,) if aliased else ()) + args))


def _moe_dispatch(after, hx2w, idx_t):
    n = hx2w.shape[0]
    n_slots = n * TOP_K + N_EXPERTS * ROW_TILE
    pos_t, plan = _positions_call(after, idx_t, n_slots // ROW_TILE)
    pos = pos_t.reshape(TOP_K, SC_WORKERS, n // (SC_WORKERS * SC_CHUNK), SC_CHUNK).transpose(1, 2, 0, 3)
    return _dispatch_call(hx2w, pos, n_slots), pos, plan


def kernel(x, c, ctx, c_ctx, w_mod, b_mod, pre1_g, post1_g, pre2_g, post2_g, w_in, w_dw, b_dw, ln_conv_g, ln_conv_b, w_conv_out, w_sc, b_sc, w_rg_a, b_rg_a, w_rg_x, b_rg_x, lru_lambda, w_rnn_out, w_out, w_router, router_bias, w_e_gate, w_e_up, w_e_down, w_s_gate, w_s_up, w_s_down):
    assert w_mod.shape[0] == 1, "single-layer block"
    bsz, seq, d = x.shape
    n = bsz * seq
    assert d == D_MODEL and seq % TM == 0 and seq % TOUT == 0 and bsz + 1 <= SUBLANES
    assert TOKEN_GROUPS == 2 and bsz % TOKEN_GROUPS == 0 and (n // TOKEN_GROUPS) % (SC_WORKERS * SC_CHUNK) == 0
    assert (n // TOKEN_GROUPS) % POS_TILE == 0

    cc = jnp.zeros((SUBLANES, D_MODEL), F32).at[:bsz].set(c).at[bsz].set(c_ctx)
    mod = _mod_call(cc, w_mod[0], b_mod)
    mod_x = mod[:bsz].reshape(bsz, 6, D_MODEL)
    mod_c = mod[bsz].reshape(6, D_MODEL)

    w_in_bf = w_in[0].astype(BF16)
    wa_bf = w_rg_a[0].astype(BF16)
    wx_bf = w_rg_x[0].astype(BF16)

    h0 = _ctx_call(ctx, mod_c, pre1_g, w_in_bf, w_sc[0], b_sc[0], wa_bf, b_rg_a[0], wx_bf, b_rg_x[0],
                   lru_lambda[0])

    wco_bf, wro_bf, wout_bf = w_conv_out[0].astype(BF16), w_rnn_out[0].astype(BF16), w_out[0].astype(BF16)
    wsg_bf, wsu_bf, wsd_bf = w_s_gate[0].astype(BF16), w_s_up[0].astype(BF16), w_s_down[0].astype(BF16)
    w_router_t, rbias = w_router[0].T, router_bias.reshape(N_EXPERTS, 1)
    gsz = bsz // TOKEN_GROUPS
    gn = gsz * seq
    none = jnp.zeros((TOP_K, LANES), jnp.int32)

    def fwd(g, after):
        return _mixer_fwd_call(g * gsz, gsz, after, x, mod_x, pre1_g, w_in_bf, w_dw[0], b_dw, ln_conv_g,
                               ln_conv_b, wco_bf, w_sc[0], b_sc[0], wa_bf, b_rg_a[0], wx_bf, b_rg_x[0],
                               lru_lambda[0], h0)

    def bwd(g, after, acts):
        gaya, gb, gbr, urnn, hf = acts
        x1, hx2w, idx_t, w_t = _mixer_bwd_call(
            g * gsz, gsz, after, urnn, hf, gbr, gaya, gb, x, mod_x, w_sc[0], b_sc[0], wa_bf, b_rg_a[0], wx_bf,
            b_rg_x[0], lru_lambda[0], h0, wro_bf, wout_bf, post1_g, pre2_g, w_router_t, rbias)
        return x1.reshape(gn, D_MODEL), hx2w.reshape(gn, ROW_WORDS), idx_t, w_t

    def experts(after, plan, xs):
        return _expert_gemm_call(after, plan, xs, w_e_gate[0], w_e_up[0], w_e_down[0])

    def finish(g, prev_out, yt, w_t, hx2w, x1):
        return _moe_out_call(prev_out, n, g * gsz, yt, w_t.T, hx2w, x1, mod_x, post2_g, wsg_bf, wsu_bf, wsd_bf, seq)

    x1_a, hx2w_a, idx_a, w_a = bwd(0, none, fwd(0, none))
    xs_a, pos_a, plan_a = _moe_dispatch(none, hx2w_a, idx_a)
    acts_b = fwd(1, idx_a)
    ys_a = experts(acts_b[4], plan_a, xs_a)
    yt_a = _collect_call(ys_a, pos_a, gn)
    x1_b, hx2w_b, idx_b, w_b = bwd(1, ys_a, acts_b)
    xs_b, pos_b, plan_b = _moe_dispatch(yt_a, hx2w_b, idx_b)
    out = finish(0, None, yt_a, w_a, hx2w_a, x1_a)
    ys_b = experts(none, plan_b, xs_b)
    yt_b = _collect_call(ys_b, pos_b, gn)
    out = finish(1, out, yt_b, w_b, hx2w_b, x1_b)
    return out.reshape(bsz, seq, D_MODEL)
```

```python
import jax
import jax.numpy as jnp
from jax import lax
from jax.experimental import pallas as pl
from jax.experimental.pallas import tpu as pltpu
from jax.experimental.pallas import tpu_sc as plsc

F32 = jnp.float32
BF16 = jnp.bfloat16

D_MODEL = 1024
GRID_W = 64
CONV_WIDTH = 31
CONV_HALF = CONV_WIDTH // 2
SHORT_CONV = 4
RNN_HEADS = 4
RNN_BLOCK = D_MODEL // RNN_HEADS
LRU_C = 8.0
N_EXPERTS = 64
N_GROUPS = 8
GROUP_SIZE = N_EXPERTS // N_GROUPS
TOPK_GROUPS = 4
TOP_K = 8
D_EXPERT = 256
ROUTED_SCALE = 2.5
EPS = 1e-6

SUBLANES = 8
TM = 512
ROWS_PER_TILE = TM // GRID_W
PAD = 16
ROW_STRIDE = GRID_W + PAD
UPAD_ROWS = ROWS_PER_TILE * ROW_STRIDE + PAD
SEG = TM // SUBLANES
HALO = (SHORT_CONV - 1) * SUBLANES
CONV_LANES = 256
LANES = 128
ROW_WORDS = D_MODEL // 2
ROW_TILE = 1024
POS_TILE = 512
TOUT = 512
TOKEN_GROUPS = 2
V7X_SC_CORES = 2
V7X_SC_SUBCORES = 16
SC_WORKERS = V7X_SC_CORES * V7X_SC_SUBCORES
SC_CHUNK = 64
SC_BUFFERS = 2
VMEM_LIMIT = 58 * 1024 * 1024


def _sigmoid(x):
    return 0.5 * (jnp.tanh(0.5 * x) + 1.0)


def _silu(x):
    return x * _sigmoid(x)


def _gelu_tanh(x):
    return 0.5 * x * (1.0 + jnp.tanh(0.7978845608028654 * (x + 0.044715 * (x * x * x))))


def _rms_norm(x, g):
    return x * lax.rsqrt(jnp.mean(x * x, axis=-1, keepdims=True) + EPS) * g


def _dot(a, b):
    return jnp.dot(a, b, preferred_element_type=F32)


def _pack_bf16_pairs(x):
    half = x.shape[-1] // 2
    lo = lax.bitcast_convert_type(x[:, :half].astype(BF16).astype(F32), jnp.uint32)
    hi = lax.bitcast_convert_type(x[:, half:].astype(BF16).astype(F32), jnp.uint32)
    return lax.bitcast_convert_type(hi | (lo >> 16), jnp.int32)


def _pack_bf16_pairs_native(x):
    half = x.shape[-1] // 2
    packed = pltpu.pack_elementwise([x[:, :half], x[:, half:]], packed_dtype=BF16)
    return lax.bitcast_convert_type(packed, jnp.int32)


def _unpack_bf16_pairs(words):
    u = lax.bitcast_convert_type(words, jnp.uint32)
    return (lax.bitcast_convert_type(u << 16, F32),
            lax.bitcast_convert_type(u & jnp.uint32(0xFFFF0000), F32))


def _log_decay(lam):
    return LRU_C * (jnp.minimum(lam, 0.0) - jnp.log1p(jnp.exp(-jnp.abs(lam))))


def _rglru_coeffs(v, wa_ref, ba, wx_ref, bx, c_lam, a_ref, b_ref):
    vb = v.astype(BF16)
    for h in range(RNN_HEADS):
        cs = slice(h * RNN_BLOCK, (h + 1) * RNN_BLOCK)
        vh = vb[:, cs]
        r = _sigmoid(_dot(vh, wa_ref[h]) + ba[:, cs])
        i = _sigmoid(_dot(vh, wx_ref[h]) + bx[:, cs])
        log_a = c_lam[:, cs] * r
        a = jnp.exp(log_a)
        a_ref[:, cs] = a
        b_ref[:, cs] = jnp.sqrt(jnp.tanh(-log_a) * (1.0 + a * a)) * (i * v[:, cs])


def _scan_tile(a_ref, b_ref, carry, n_rows, reverse):
    row = lax.broadcasted_iota(jnp.int32, (SUBLANES, D_MODEL), 0)
    n_groups = n_rows // SUBLANES

    def body(j, carry):
        g = (n_groups - 1 - j) if reverse else j
        off = pl.multiple_of(g * SUBLANES, SUBLANES)
        a = a_ref[pl.ds(off, SUBLANES), :]
        b = b_ref[pl.ds(off, SUBLANES), :]
        for s in (1, 2, 4):
            keep = (row < SUBLANES - s) if reverse else (row >= s)
            shift = (SUBLANES - s) if reverse else s
            a_sh = jnp.where(keep, pltpu.roll(a, shift, 0), 1.0)
            b_sh = jnp.where(keep, pltpu.roll(b, shift, 0), 0.0)
            b = a * b_sh + b
            a = a * a_sh
        h = a * carry + b
        b_ref[pl.ds(off, SUBLANES), :] = h
        last = h[0:1, :] if reverse else h[SUBLANES - 1:SUBLANES, :]
        return jnp.broadcast_to(last, (SUBLANES, D_MODEL))

    return lax.fori_loop(0, n_groups, body, carry, unroll=2)


def _segment_perm(to_natural):
    r = lax.broadcasted_iota(jnp.int32, (TM, TM), 0)
    c = lax.broadcasted_iota(jnp.int32, (TM, TM), 1)
    if to_natural:
        src = (r & (SEG - 1)) * SUBLANES + lax.shift_right_logical(r, SEG.bit_length() - 1)
    else:
        src = (r & (SUBLANES - 1)) * SEG + lax.shift_right_logical(r, SUBLANES.bit_length() - 1)
    return jnp.where(c == src, 1.0, 0.0).astype(BF16)


def _scan_segments(a_ref, b_ref, carry, reverse):
    row = lax.broadcasted_iota(jnp.int32, (SUBLANES, D_MODEL), 0)

    def body(s, hp):
        h, prod = hp
        off = pl.multiple_of(((SEG - 1 - s) if reverse else s) * SUBLANES, SUBLANES)
        a = a_ref[pl.ds(off, SUBLANES), :]
        h = a * h + b_ref[pl.ds(off, SUBLANES), :]
        prod = a * prod
        b_ref[pl.ds(off, SUBLANES), :] = h
        a_ref[pl.ds(off, SUBLANES), :] = prod
        return h, prod

    b, a = lax.fori_loop(0, SEG, body, (jnp.zeros((SUBLANES, D_MODEL), F32), jnp.ones((SUBLANES, D_MODEL), F32)),
                         unroll=4)
    for s in (1, 2, 4):
        keep = (row < SUBLANES - s) if reverse else (row >= s)
        shift = (SUBLANES - s) if reverse else s
        a_sh = jnp.where(keep, pltpu.roll(a, shift, 0), 1.0)
        b_sh = jnp.where(keep, pltpu.roll(b, shift, 0), 0.0)
        b = a * b_sh + b
        a = a * a_sh
    leave = a * carry + b
    if reverse:
        enter = jnp.where(row < SUBLANES - 1, pltpu.roll(leave, SUBLANES - 1, 0), carry)
        last = leave[0:1, :]
    else:
        enter = jnp.where(row >= 1, pltpu.roll(leave, 1, 0), carry)
        last = leave[SUBLANES - 1:SUBLANES, :]
    h = b_ref[...].reshape(SEG, SUBLANES, D_MODEL) + a_ref[...].reshape(SEG, SUBLANES, D_MODEL) * enter[None]
    b_ref[...] = h.reshape(TM, D_MODEL)
    return jnp.broadcast_to(last, (SUBLANES, D_MODEL))


def _short_conv_segments(uext_ref, wsc_ref, bsc_ref):
    v = bsc_ref[...] + wsc_ref[0:1, :] * uext_ref[0:TM, :]
    for k in range(1, SHORT_CONV):
        v = v + wsc_ref[k:k + 1, :] * uext_ref[k * SUBLANES:k * SUBLANES + TM, :]
    return v


def _mod_kernel(c_ref, w_ref, b_ref, o_ref):
    o_ref[...] = jnp.dot(_silu(c_ref[...]), w_ref[...], preferred_element_type=F32,
                         precision=lax.Precision.HIGHEST) + b_ref[...]


def _mod_call(cc, w_mod, b_mod):
    n_mod = w_mod.shape[1]
    return pl.pallas_call(
        _mod_kernel,
        grid=(n_mod // D_MODEL,),
        in_specs=[
            pl.BlockSpec((SUBLANES, D_MODEL), lambda j: (0, 0)),
            pl.BlockSpec((D_MODEL, D_MODEL), lambda j: (0, j)),
            pl.BlockSpec((1, D_MODEL), lambda j: (0, j)),
        ],
        out_specs=pl.BlockSpec((SUBLANES, D_MODEL), lambda j: (0, j)),
        out_shape=jax.ShapeDtypeStruct((SUBLANES, n_mod), F32),
        name="mod",
    )(cc, w_mod, b_mod)


def _ctx_kernel(ctx_ref, mod_ref, g_ref, w_ref, wsc_ref, bsc_ref, wa_ref, ba_ref, wx_ref, bx_ref,
                lam_ref, o_ref, uext_ref, a_ref, b_ref):
    n = ctx_ref.shape[0]
    hc = _rms_norm(ctx_ref[...], g_ref[...]) * (1.0 + mod_ref[1:2, :]) + mod_ref[0:1, :]
    u = _dot(hc.astype(BF16), w_ref[...])
    zeros8 = jnp.zeros((SUBLANES, D_MODEL), F32)
    uext_ref[0:SUBLANES, :] = zeros8
    uext_ref[SUBLANES:SUBLANES + n, :] = u
    uext_ref[SUBLANES + n:2 * SUBLANES + n, :] = zeros8
    for d in range(2):
        v = jnp.broadcast_to(bsc_ref[d:d + 1, :], (n, D_MODEL))
        for k in range(SHORT_CONV):
            start = SUBLANES + k - (SHORT_CONV - 1) * (1 - d)
            v = v + wsc_ref[d, k:k + 1, :] * uext_ref[start:start + n, :]
        _rglru_coeffs(v, wa_ref.at[d], ba_ref[d:d + 1, :], wx_ref.at[d], bx_ref[d:d + 1, :],
                      _log_decay(lam_ref[d:d + 1, :]), a_ref, b_ref)
        final = _scan_tile(a_ref, b_ref, zeros8, n, reverse=(d == 1))
        o_ref[d:d + 1, :] = final[0:1, :]


def _ctx_call(ctx, mod_c, pre1_g, w_in_bf, w_sc, b_sc, wa_bf, b_rg_a, wx_bf, b_rg_x, lam):
    bsz, n, _ = ctx.shape
    const2 = lambda b: (0, 0)
    const3 = lambda b: (0, 0, 0)
    const4 = lambda b: (0, 0, 0, 0)
    return pl.pallas_call(
        _ctx_kernel,
        grid=(bsz,),
        in_specs=[
            pl.BlockSpec((None, n, D_MODEL), lambda b: (b, 0, 0)),
            pl.BlockSpec((6, D_MODEL), const2),
            pl.BlockSpec((1, D_MODEL), const2),
            pl.BlockSpec((D_MODEL, D_MODEL), lambda b: (0, 2)),
            pl.BlockSpec((2, SHORT_CONV, D_MODEL), const3),
            pl.BlockSpec((2, D_MODEL), const2),
            pl.BlockSpec((2, RNN_HEADS, RNN_BLOCK, RNN_BLOCK), const4),
            pl.BlockSpec((2, D_MODEL), const2),
            pl.BlockSpec((2, RNN_HEADS, RNN_BLOCK, RNN_BLOCK), const4),
            pl.BlockSpec((2, D_MODEL), const2),
            pl.BlockSpec((2, D_MODEL), const2),
        ],
        out_specs=pl.BlockSpec((None, 2, D_MODEL), lambda b: (b, 0, 0)),
        out_shape=jax.ShapeDtypeStruct((bsz, 2, D_MODEL), F32),
        scratch_shapes=[
            pltpu.VMEM((n + 2 * SUBLANES, D_MODEL), F32),
            pltpu.VMEM((n, D_MODEL), F32),
            pltpu.VMEM((n, D_MODEL), F32),
        ],
        name="ctx",
    )(ctx, mod_c, pre1_g, w_in_bf, w_sc, b_sc, wa_bf, b_rg_a, wx_bf, b_rg_x, lam)


def _mixer_fwd_kernel(after_ref, x_ref, mod_ref, g_ref, win_ref, wdw_ref, bdw_ref, lng_ref, lnb_ref, wco_ref,
                      wsc_ref, bsc_ref, wa_ref, ba_ref, wx_ref, bx_ref, lam_ref, h0_ref,
                      gaya_ref, gb_ref, gbr_ref, urnn_ref, hf_ref,
                      upad_ref, ush_ref, wb_ref, cv_ref, uext_ref, a_ref, b_ref, carry_ref, halo_ref, perm_ref):
    del after_ref
    j = pl.program_id(1)

    @pl.when(j == 0)
    def _():
        carry_ref[...] = jnp.broadcast_to(h0_ref[0:1, :], (SUBLANES, D_MODEL))
        halo_ref[...] = jnp.zeros((SUBLANES, D_MODEL), F32)
        perm_ref[...] = _segment_perm(to_natural=False)
        zeros_pad = jnp.zeros((PAD, D_MODEL), F32)
        for r in range(ROWS_PER_TILE + 1):
            upad_ref[r * ROW_STRIDE:r * ROW_STRIDE + PAD, :] = zeros_pad
        for k in range(CONV_WIDTH):
            wb_ref[k] = jnp.broadcast_to(wdw_ref[k:k + 1, :], (SUBLANES, D_MODEL))

    hx = (_rms_norm(x_ref[...], g_ref[...]) * (1.0 + mod_ref[1:2, :]) + mod_ref[0:1, :]).astype(BF16)

    u = _dot(hx, win_ref[:, 0:D_MODEL]) * _sigmoid(_dot(hx, win_ref[:, D_MODEL:2 * D_MODEL]))
    for r in range(ROWS_PER_TILE):
        upad_ref[PAD + r * ROW_STRIDE:PAD + r * ROW_STRIDE + GRID_W, :] = u[r * GRID_W:(r + 1) * GRID_W, :]
    vregs_per_row = GRID_W // SUBLANES
    hxp = _dot(perm_ref[...], hx).astype(BF16)
    g_a = None
    for c in range(D_MODEL // CONV_LANES):
        if c == 0:
            gb_ref[...] = _sigmoid(_dot(hxp, win_ref[:, 5 * D_MODEL:6 * D_MODEL])).astype(BF16)
        elif c == 1:
            gbr_ref[...] = _gelu_tanh(_dot(hxp, win_ref[:, 3 * D_MODEL:4 * D_MODEL])).astype(BF16)
        elif c == 2:
            g_a = _sigmoid(_dot(hxp, win_ref[:, 4 * D_MODEL:5 * D_MODEL]))
        else:
            ur = _dot(hxp, win_ref[:, 2 * D_MODEL:3 * D_MODEL])
            urnn_ref[...] = ur.astype(BF16)
            uext_ref[HALO:HALO + TM, :] = ur
        cs = slice(c * CONV_LANES, (c + 1) * CONV_LANES)
        xpad = upad_ref[:, cs]
        for s in range(1, SUBLANES):
            ush_ref[s - 1] = pltpu.roll(xpad, UPAD_ROWS - s, 0)
        for r in range(ROWS_PER_TILE):
            acc = jnp.broadcast_to(bdw_ref[:, cs].reshape(1, 1, CONV_LANES), (vregs_per_row, SUBLANES, CONV_LANES))
            for k in range(CONV_WIDTH):
                q, s = divmod(r * ROW_STRIDE + PAD - CONV_HALF + k, SUBLANES)
                rows = slice(q * SUBLANES, q * SUBLANES + GRID_W)
                win = upad_ref[rows, cs] if s == 0 else ush_ref[s - 1, rows, :]
                acc = acc + wb_ref[k, :, cs] * win.reshape(vregs_per_row, SUBLANES, CONV_LANES)
            cv_ref[r * GRID_W:(r + 1) * GRID_W, cs] = acc.reshape(GRID_W, CONV_LANES)
    cv = cv_ref[...]
    cvc = cv - jnp.mean(cv, axis=-1, keepdims=True)
    cvn = cvc * lax.rsqrt(jnp.mean(cvc * cvc, axis=-1, keepdims=True) + EPS) * lng_ref[...] + lnb_ref[...]
    y_a = _dot(_dot(perm_ref[...], _silu(cvn).astype(BF16)).astype(BF16), wco_ref[...])

    gaya_ref[...] = (g_a * y_a).astype(BF16)

    row8 = lax.broadcasted_iota(jnp.int32, (SUBLANES, D_MODEL), 0)
    for jj in range(SHORT_CONV - 1):
        late = uext_ref[HALO + (SEG - (SHORT_CONV - 1) + jj) * SUBLANES:HALO + (SEG - (SHORT_CONV - 1) + jj + 1) * SUBLANES, :]
        prev_tile = jnp.broadcast_to(halo_ref[jj:jj + 1, :], (SUBLANES, D_MODEL))
        uext_ref[jj * SUBLANES:(jj + 1) * SUBLANES, :] = jnp.where(row8 == 0, prev_tile, pltpu.roll(late, 1, 0))
        halo_ref[jj:jj + 1, :] = late[SUBLANES - 1:SUBLANES, :]
    v = _short_conv_segments(uext_ref, wsc_ref, bsc_ref)
    _rglru_coeffs(v, wa_ref, ba_ref[...], wx_ref, bx_ref[...], _log_decay(lam_ref[...]), a_ref, b_ref)
    carry_ref[...] = _scan_segments(a_ref, b_ref, carry_ref[...], reverse=False)
    hf_ref[...] = b_ref[...].astype(BF16)


def _resident(shape):
    nd = len(shape)
    return pl.BlockSpec(shape, lambda b, j: (0,) * nd, pipeline_mode=pl.Buffered(1))


def _mixer_fwd_call(b0, bsz, after, x, mod_x, pre1_g, w_in_bf, w_dw, b_dw, ln_g, ln_b, wco_bf,
                    w_sc, b_sc, wa_bf, b_rg_a, wx_bf, b_rg_x, lam, h0):
    seq = x.shape[1]
    nt = seq // TM
    tile = pl.BlockSpec((None, TM, D_MODEL), lambda b, j: (b, j, 0))
    act = jax.ShapeDtypeStruct((bsz, seq, D_MODEL), BF16)
    head_w = pl.BlockSpec((None, RNN_HEADS, RNN_BLOCK, RNN_BLOCK), lambda b, j: (0, 0, 0, 0),
                          pipeline_mode=pl.Buffered(1))
    dir_row = pl.BlockSpec((None, 1, D_MODEL), lambda b, j: (0, 0, 0), pipeline_mode=pl.Buffered(1))
    return pl.pallas_call(
        _mixer_fwd_kernel,
        grid=(bsz, nt),
        in_specs=[
            pl.BlockSpec(memory_space=pl.ANY),
            pl.BlockSpec((None, TM, D_MODEL), lambda b, j: (b0 + b, j, 0)),
            pl.BlockSpec((None, 6, D_MODEL), lambda b, j: (b0 + b, 0, 0)),
            _resident((1, D_MODEL)),
            _resident((D_MODEL, 6 * D_MODEL)),
            _resident((CONV_WIDTH, D_MODEL)),
            _resident((1, D_MODEL)),
            _resident((1, D_MODEL)),
            _resident((1, D_MODEL)),
            _resident((D_MODEL, D_MODEL)),
            pl.BlockSpec((None, SHORT_CONV, D_MODEL), lambda b, j: (0, 0, 0), pipeline_mode=pl.Buffered(1)),
            dir_row, head_w, dir_row, head_w, dir_row, dir_row,
            pl.BlockSpec((None, 2, D_MODEL), lambda b, j: (b0 + b, 0, 0)),
        ],
        out_specs=[tile] * 5,
        out_shape=[act] * 5,
        scratch_shapes=[
            pltpu.VMEM((UPAD_ROWS, D_MODEL), F32),
            pltpu.VMEM((SUBLANES - 1, UPAD_ROWS, CONV_LANES), F32),
            pltpu.VMEM((CONV_WIDTH, SUBLANES, D_MODEL), F32),
            pltpu.VMEM((TM, D_MODEL), F32),
            pltpu.VMEM((TM + HALO, D_MODEL), F32),
            pltpu.VMEM((TM, D_MODEL), F32),
            pltpu.VMEM((TM, D_MODEL), F32),
            pltpu.VMEM((SUBLANES, D_MODEL), F32),
            pltpu.VMEM((SUBLANES, D_MODEL), F32),
            pltpu.VMEM((TM, TM), BF16),
        ],
        compiler_params=pltpu.CompilerParams(
            dimension_semantics=("arbitrary", "arbitrary"), vmem_limit_bytes=VMEM_LIMIT),
        name="mixer_fwd",
    )(after, x, mod_x, pre1_g, w_in_bf, w_dw, b_dw, ln_g, ln_b, wco_bf,
      w_sc, b_sc.reshape(2, 1, D_MODEL), wa_bf, b_rg_a.reshape(2, 1, D_MODEL), wx_bf,
      b_rg_x.reshape(2, 1, D_MODEL), lam.reshape(2, 1, D_MODEL), h0)


def _route(logits_t, bias):
    t = logits_t.shape[1]
    scores = _sigmoid(logits_t)
    sel = scores + bias
    neg_inf = jnp.float32(-jnp.inf)

    sel3 = sel.reshape(N_GROUPS, GROUP_SIZE, t)
    within = lax.broadcasted_iota(jnp.int32, sel3.shape, 1)
    m1 = jnp.max(sel3, axis=1, keepdims=True)
    first = jnp.min(jnp.where(sel3 == m1, within, GROUP_SIZE), axis=1, keepdims=True)
    m2 = jnp.max(jnp.where(within == first, neg_inf, sel3), axis=1, keepdims=True)
    gscore = (m1 + m2).reshape(N_GROUPS, t)

    gidx = lax.broadcasted_iota(jnp.int32, gscore.shape, 0)
    rank = jnp.zeros(gscore.shape, jnp.int32)
    for g in range(N_GROUPS):
        other = gscore[g:g + 1, :]
        beats = jnp.where(other > gscore, 1, jnp.where((other == gscore) & (gidx > g), 1, 0))
        rank = rank + beats
    gkeep = (rank < TOPK_GROUPS).reshape(N_GROUPS, 1, t)
    masked = jnp.where(gkeep, sel3, neg_inf).reshape(N_EXPERTS, t)

    eidx = lax.broadcasted_iota(jnp.int32, masked.shape, 0)
    picks, weights = [], []
    for _ in range(TOP_K):
        m = jnp.max(masked, axis=0, keepdims=True)
        first = jnp.min(jnp.where(masked == m, eidx, N_EXPERTS), axis=0, keepdims=True)
        pick = eidx == first
        picks.append(first)
        weights.append(jnp.sum(jnp.where(pick, scores, 0.0), axis=0, keepdims=True))
        masked = jnp.where(pick, neg_inf, masked)
    idx = jnp.concatenate(picks, axis=0)
    w = jnp.concatenate(weights, axis=0)
    return idx, ROUTED_SCALE * w / jnp.sum(w, axis=0, keepdims=True)


def _mixer_bwd_kernel(after_ref, urnn_ref, hf_ref, gbr_ref, gaya_ref, gb_ref, x_ref, mod_ref,
                      wsc_ref, bsc_ref, wa_ref, ba_ref, wx_ref, bx_ref, lam_ref, h0_ref,
                      wro_ref, wout_ref, post1_ref, pre2_ref, wrt_ref, rbias_ref,
                      x1_ref, hx2w_ref, idx_ref, w_ref,
                      uext_ref, a_ref, b_ref, carry_ref, halo_ref, perm_ref):
    del after_ref
    j = pl.program_id(1)

    @pl.when(j == 0)
    def _():
        carry_ref[...] = jnp.broadcast_to(h0_ref[1:2, :], (SUBLANES, D_MODEL))
        halo_ref[...] = jnp.zeros((SUBLANES, D_MODEL), F32)
        perm_ref[...] = _segment_perm(to_natural=True)

    uext_ref[0:TM, :] = urnn_ref[...].astype(F32)
    row8 = lax.broadcasted_iota(jnp.int32, (SUBLANES, D_MODEL), 0)
    for jj in range(SHORT_CONV - 1):
        early = uext_ref[jj * SUBLANES:(jj + 1) * SUBLANES, :]
        next_tile = jnp.broadcast_to(halo_ref[jj:jj + 1, :], (SUBLANES, D_MODEL))
        uext_ref[TM + jj * SUBLANES:TM + (jj + 1) * SUBLANES, :] = jnp.where(
            row8 == SUBLANES - 1, next_tile, pltpu.roll(early, SUBLANES - 1, 0))
        halo_ref[jj:jj + 1, :] = early[0:1, :]
    v = _short_conv_segments(uext_ref, wsc_ref, bsc_ref)
    _rglru_coeffs(v, wa_ref, ba_ref[...], wx_ref, bx_ref[...], _log_decay(lam_ref[...]), a_ref, b_ref)
    carry_ref[...] = _scan_segments(a_ref, b_ref, carry_ref[...], reverse=True)

    h_sum = hf_ref[...].astype(F32) + b_ref[...]
    y_b = _dot((gbr_ref[...].astype(F32) * h_sum).astype(BF16), wro_ref[...])
    mix = gaya_ref[...].astype(F32) + gb_ref[...].astype(F32) * y_b
    out = _dot(_dot(perm_ref[...], mix.astype(BF16)).astype(BF16), wout_ref[...])
    x1 = x_ref[...] + mod_ref[2:3, :] * _rms_norm(out, post1_ref[...])
    x1_ref[...] = x1

    hx2 = _rms_norm(x1, pre2_ref[...]) * (1.0 + mod_ref[4:5, :]) + mod_ref[3:4, :]
    hx2w_ref[...] = _pack_bf16_pairs(hx2)
    logits_t = lax.dot_general(wrt_ref[...], hx2, (((1,), (1,)), ((), ())),
                               preferred_element_type=F32, precision=lax.Precision.HIGHEST)
    idx, w = _route(logits_t, rbias_ref[...])
    idx_ref[...] = idx
    w_ref[...] = w


def _mixer_bwd_call(b0, bsz, after, urnn, hf, gbr, gaya, gb, x, mod_x, w_sc, b_sc, wa_bf, b_rg_a, wx_bf, b_rg_x, lam, h0,
                    wro_bf, wout_bf, post1_g, pre2_g, w_router_t, router_bias):
    seq = x.shape[1]
    nt = seq // TM
    rev = lambda b, j: (b, nt - 1 - j, 0)
    tile = pl.BlockSpec((None, TM, D_MODEL), rev)
    head_w = pl.BlockSpec((None, RNN_HEADS, RNN_BLOCK, RNN_BLOCK), lambda b, j: (1, 0, 0, 0),
                          pipeline_mode=pl.Buffered(1))
    dir_row = pl.BlockSpec((None, 1, D_MODEL), lambda b, j: (1, 0, 0), pipeline_mode=pl.Buffered(1))
    return pl.pallas_call(
        _mixer_bwd_kernel,
        grid=(bsz, nt),
        in_specs=[
            pl.BlockSpec(memory_space=pl.ANY),
            tile, tile, tile, tile, tile,
            pl.BlockSpec((None, TM, D_MODEL), lambda b, j: (b0 + b, nt - 1 - j, 0)),
            pl.BlockSpec((None, 6, D_MODEL), lambda b, j: (b0 + b, 0, 0)),
            pl.BlockSpec((None, SHORT_CONV, D_MODEL), lambda b, j: (1, 0, 0), pipeline_mode=pl.Buffered(1)),
            dir_row, head_w, dir_row, head_w, dir_row, dir_row,
            pl.BlockSpec((None, 2, D_MODEL), lambda b, j: (b0 + b, 0, 0)),
            _resident((D_MODEL, D_MODEL)),
            _resident((D_MODEL, D_MODEL)),
            _resident((1, D_MODEL)),
            _resident((1, D_MODEL)),
            _resident((N_EXPERTS, D_MODEL)),
            _resident((N_EXPERTS, 1)),
        ],
        out_specs=[
            pl.BlockSpec((None, TM, D_MODEL), rev),
            pl.BlockSpec((None, TM, ROW_WORDS), rev),
            pl.BlockSpec((TOP_K, TM), lambda b, j: (0, b * nt + nt - 1 - j)),
            pl.BlockSpec((TOP_K, TM), lambda b, j: (0, b * nt + nt - 1 - j)),
        ],
        out_shape=[
            jax.ShapeDtypeStruct((bsz, seq, D_MODEL), F32),
            jax.ShapeDtypeStruct((bsz, seq, ROW_WORDS), jnp.int32),
            jax.ShapeDtypeStruct((TOP_K, bsz * seq), jnp.int32),
            jax.ShapeDtypeStruct((TOP_K, bsz * seq), F32),
        ],
        scratch_shapes=[
            pltpu.VMEM((TM + HALO, D_MODEL), F32),
            pltpu.VMEM((TM, D_MODEL), F32),
            pltpu.VMEM((TM, D_MODEL), F32),
            pltpu.VMEM((SUBLANES, D_MODEL), F32),
            pltpu.VMEM((SUBLANES, D_MODEL), F32),
            pltpu.VMEM((TM, TM), BF16),
        ],
        compiler_params=pltpu.CompilerParams(
            dimension_semantics=("arbitrary", "arbitrary"), vmem_limit_bytes=VMEM_LIMIT),
        name="mixer_bwd",
    )(after, urnn, hf, gbr, gaya, gb, x, mod_x, w_sc, b_sc.reshape(2, 1, D_MODEL), wa_bf,
      b_rg_a.reshape(2, 1, D_MODEL), wx_bf, b_rg_x.reshape(2, 1, D_MODEL), lam.reshape(2, 1, D_MODEL), h0,
      wro_bf, wout_bf, post1_g, pre2_g, w_router_t, router_bias)


def _positions_kernel(after_ref, idx_ref, pos_ref, plan_ref):
    del after_ref
    n = idx_ref.shape[1]
    n_tiles = n // POS_TILE
    eidx = lax.broadcasted_iota(jnp.int32, (N_EXPERTS, POS_TILE), 0)

    def chosen(t):
        idx = idx_ref[:, pl.ds(pl.multiple_of(t * POS_TILE, POS_TILE), POS_TILE)]
        ch = jnp.zeros((N_EXPERTS, POS_TILE), F32)
        for k in range(TOP_K):
            ch = ch + jnp.where(eidx == idx[k:k + 1, :], 1.0, 0.0)
        return idx, ch

    def count_body(t, cnt):
        return cnt + jnp.sum(chosen(t)[1], axis=1, keepdims=True)

    cnt = lax.fori_loop(0, n_tiles, count_body, jnp.zeros((N_EXPERTS, 1), F32))
    padded = jnp.ceil(cnt * (1.0 / ROW_TILE)) * ROW_TILE
    r = lax.broadcasted_iota(jnp.int32, (N_EXPERTS, N_EXPERTS), 0)
    c = lax.broadcasted_iota(jnp.int32, (N_EXPERTS, N_EXPERTS), 1)
    off = jnp.dot(jnp.where(c < r, 1.0, 0.0), jnp.broadcast_to(padded, (N_EXPERTS, LANES)),
                  preferred_element_type=F32, precision=lax.Precision.HIGHEST)[:, 0:1]
    end = off + padded

    n_map = plan_ref.shape[1]
    tstart = lax.broadcasted_iota(jnp.int32, (N_EXPERTS, n_map), 1).astype(F32) * ROW_TILE
    te = jnp.minimum(jnp.sum(jnp.where(end <= tstart, 1, 0), axis=0, keepdims=True), N_EXPERTS - 1)
    emap = lax.broadcasted_iota(jnp.int32, (N_EXPERTS, n_map), 0)
    live_end = jnp.sum(jnp.where(emap == te, off + cnt, 0.0), axis=0, keepdims=True)
    total = jnp.sum(padded, axis=0, keepdims=True)
    plan_ref[0:1, :] = te
    plan_ref[1:2, :] = jnp.clip(live_end - tstart[0:1, :], 0.0, ROW_TILE).astype(jnp.int32)
    plan_ref[2:3, :] = jnp.broadcast_to(total * (1.0 / ROW_TILE), (1, n_map)).astype(jnp.int32)

    row = lax.broadcasted_iota(jnp.int32, (POS_TILE, POS_TILE), 0)
    col = lax.broadcasted_iota(jnp.int32, (POS_TILE, POS_TILE), 1)
    before = jnp.where(row < col, 1.0, 0.0).astype(BF16)

    def pos_body(t, carry):
        idx, ch = chosen(t)
        base = _dot(ch.astype(BF16), before) + (carry + off)
        rows = [jnp.sum(jnp.where(eidx == idx[k:k + 1, :], base, 0.0), axis=0, keepdims=True)
                for k in range(TOP_K)]
        pos_ref[:, pl.ds(pl.multiple_of(t * POS_TILE, POS_TILE), POS_TILE)] = (
            jnp.concatenate(rows, axis=0).astype(jnp.int32))
        return carry + jnp.sum(ch, axis=1, keepdims=True)

    lax.fori_loop(0, n_tiles, pos_body, jnp.zeros((N_EXPERTS, 1), F32))


def _positions_call(after, idx_t, n_row_tiles):
    n = idx_t.shape[1]
    n_map = -(-n_row_tiles // LANES) * LANES
    return pl.pallas_call(
        _positions_kernel,
        in_specs=[pl.BlockSpec(memory_space=pl.ANY), pl.BlockSpec(memory_space=pltpu.VMEM)],
        out_shape=[
            jax.ShapeDtypeStruct((TOP_K, n), jnp.int32),
            jax.ShapeDtypeStruct((3, n_map), jnp.int32),
        ],
        name="moe_positions",
    )(after, idx_t)


def _sc_mesh():
    return plsc.VectorSubcoreMesh(core_axis_name="c", subcore_axis_name="s",
                                  num_cores=V7X_SC_CORES, num_subcores=V7X_SC_SUBCORES)


def _sc_worker():
    return lax.axis_index("s") * V7X_SC_CORES + lax.axis_index("c")


def _dispatch_call(rows, pos, n_slots):
    n = rows.shape[0]
    tok_w = n // SC_WORKERS
    n_items = tok_w // SC_CHUNK

    def body(rows_hbm, pos_hbm, xs_hbm, idx_v, rows_v, lsem, ssem):
        wid = _sc_worker()
        pltpu.sync_copy(pos_hbm.at[wid], idx_v)
        base = wid * tok_w

        def load(i):
            b = i % SC_BUFFERS
            return pltpu.async_copy(rows_hbm.at[pl.ds(base + i * SC_CHUNK, SC_CHUNK)], rows_v.at[b], lsem.at[b])

        def scatter(i):
            b = i % SC_BUFFERS
            return [pltpu.async_copy(rows_v.at[b], xs_hbm.at[idx_v.at[i, k]], ssem.at[b]) for k in range(TOP_K)]

        loads = {i: load(i) for i in range(SC_BUFFERS - 1)}
        scat = {}
        for i in range(n_items):
            loads[i].wait()
            scat[i] = scatter(i)
            if i >= 1:
                for cp in scat[i - 1]:
                    cp.wait()
            if i + SC_BUFFERS - 1 < n_items:
                loads[i + SC_BUFFERS - 1] = load(i + SC_BUFFERS - 1)
        for cp in scat[n_items - 1]:
            cp.wait()

    return pl.kernel(
        body, mesh=_sc_mesh(),
        out_type=jax.ShapeDtypeStruct((n_slots, ROW_WORDS), jnp.int32),
        scratch_types=[pltpu.VMEM((n_items, TOP_K, SC_CHUNK), jnp.int32),
                       pltpu.VMEM((SC_BUFFERS, SC_CHUNK, ROW_WORDS), jnp.int32),
                       pltpu.SemaphoreType.DMA((SC_BUFFERS,)), pltpu.SemaphoreType.DMA((SC_BUFFERS,))],
        compiler_params=pltpu.CompilerParams(use_tc_tiling_on_sc=True),
        name="moe_dispatch",
    )(rows, pos)


def _collect_call(ys, pos, n):
    tok_w = n // SC_WORKERS
    n_chunks = tok_w // SC_CHUNK
    items = [(c, k) for c in range(n_chunks) for k in range(TOP_K)]

    def body(ys_hbm, pos_hbm, yt_hbm, idx_v, rows_v, gsem, wsem):
        wid = _sc_worker()
        pltpu.sync_copy(pos_hbm.at[wid], idx_v)
        base = wid * tok_w

        def gather(j):
            c, k = items[j]
            b = j % SC_BUFFERS
            return pltpu.async_copy(ys_hbm.at[idx_v.at[c, k]], rows_v.at[b], gsem.at[b])

        def write(j):
            c, k = items[j]
            b = j % SC_BUFFERS
            return pltpu.async_copy(rows_v.at[b], yt_hbm.at[k, pl.ds(base + c * SC_CHUNK, SC_CHUNK)], wsem.at[b])

        g = {j: gather(j) for j in range(SC_BUFFERS - 1)}
        w = {}
        for j in range(len(items)):
            g[j].wait()
            w[j] = write(j)
            if j >= 1:
                w[j - 1].wait()
            if j + SC_BUFFERS - 1 < len(items):
                g[j + SC_BUFFERS - 1] = gather(j + SC_BUFFERS - 1)
        w[len(items) - 1].wait()

    return pl.kernel(
        body, mesh=_sc_mesh(),
        out_type=jax.ShapeDtypeStruct((TOP_K, n, ROW_WORDS), jnp.int32),
        scratch_types=[pltpu.VMEM((n_chunks, TOP_K, SC_CHUNK), jnp.int32),
                       pltpu.VMEM((SC_BUFFERS, SC_CHUNK, ROW_WORDS), jnp.int32),
                       pltpu.SemaphoreType.DMA((SC_BUFFERS,)), pltpu.SemaphoreType.DMA((SC_BUFFERS,))],
        compiler_params=pltpu.CompilerParams(use_tc_tiling_on_sc=True),
        name="moe_collect",
    )(ys, pos)


def _expert_gemm_kernel(plan_ref, after_ref, xs_ref, wg_ref, wu_ref, wd_ref, ys_ref, wgu_scr, wd_scr):
    del after_ref
    i = pl.program_id(0)
    e = plan_ref[0, i]
    prev = plan_ref[0, jnp.maximum(i - 1, 0)]
    live = plan_ref[1, i]

    @pl.when((i == 0) | (e != prev))
    def _():
        wgu_scr[:, 0:D_EXPERT] = wg_ref[...].astype(BF16)
        wgu_scr[:, D_EXPERT:2 * D_EXPERT] = wu_ref[...].astype(BF16)
        wd_scr[...] = wd_ref[...].astype(BF16)

    def swiglu_rows(n_rows):
        lo, hi = _unpack_bf16_pairs(xs_ref[0:n_rows, :])
        gu = _dot(lo.astype(BF16), wgu_scr[0:ROW_WORDS, :]) + _dot(hi.astype(BF16), wgu_scr[ROW_WORDS:D_MODEL, :])
        h = _silu(gu[:, 0:D_EXPERT]) * gu[:, D_EXPERT:2 * D_EXPERT]
        y = _dot(h.astype(BF16), wd_scr[...])
        ys_ref[0:n_rows, :] = _pack_bf16_pairs_native(y)

    @pl.when((i < plan_ref[2, 0]) & (live > ROW_TILE // 2))
    def _():
        swiglu_rows(ROW_TILE)

    @pl.when((i < plan_ref[2, 0]) & (live <= ROW_TILE // 2))
    def _():
        swiglu_rows(ROW_TILE // 2)


def _expert_gemm_call(after, plan, xs, w_e_gate, w_e_up, w_e_down):
    n_slots = xs.shape[0]
    n_row_tiles = n_slots // ROW_TILE
    rows = pl.BlockSpec((ROW_TILE, ROW_WORDS), lambda i, plan: (jnp.minimum(i, plan[2, 0] - 1), 0))
    expert = lambda i, plan: (plan[0, i], 0, 0)
    return pl.pallas_call(
        _expert_gemm_kernel,
        grid_spec=pltpu.PrefetchScalarGridSpec(
            num_scalar_prefetch=1,
            grid=(n_row_tiles,),
            in_specs=[
                pl.BlockSpec(memory_space=pl.ANY),
                rows,
                pl.BlockSpec((None, D_MODEL, D_EXPERT), expert),
                pl.BlockSpec((None, D_MODEL, D_EXPERT), expert),
                pl.BlockSpec((None, D_EXPERT, D_MODEL), expert),
            ],
            out_specs=rows,
            scratch_shapes=[pltpu.VMEM((D_MODEL, 2 * D_EXPERT), BF16), pltpu.VMEM((D_EXPERT, D_MODEL), BF16)],
        ),
        out_shape=jax.ShapeDtypeStruct((n_slots, ROW_WORDS), jnp.int32),
        compiler_params=pltpu.CompilerParams(dimension_semantics=("arbitrary",), vmem_limit_bytes=VMEM_LIMIT),
        name="moe_experts",
    )(plan, after, xs, w_e_gate, w_e_up, w_e_down)


def _moe_out_kernel(yt_ref, w_ref, t_ref, x1_ref, mod_ref, post2_ref, wsg_ref, wsu_ref, wsd_ref, o_ref):
    lo, hi = _unpack_bf16_pairs(t_ref[...])
    lo = lo.astype(BF16)
    hi = hi.astype(BF16)
    g = _dot(lo, wsg_ref[0:ROW_WORDS, :]) + _dot(hi, wsg_ref[ROW_WORDS:D_MODEL, :])
    u = _dot(lo, wsu_ref[0:ROW_WORDS, :]) + _dot(hi, wsu_ref[ROW_WORDS:D_MODEL, :])
    shared = _dot((_silu(g) * u).astype(BF16), wsd_ref[...])
    acc_lo = shared[:, 0:ROW_WORDS]
    acc_hi = shared[:, ROW_WORDS:D_MODEL]
    for k in range(TOP_K):
        y_lo, y_hi = _unpack_bf16_pairs(yt_ref[k])
        wk = w_ref[:, k:k + 1]
        acc_lo = acc_lo + wk * y_lo
        acc_hi = acc_hi + wk * y_hi
    moe = jnp.concatenate([acc_lo, acc_hi], axis=-1)
    o_ref[...] = x1_ref[...] + mod_ref[5:6, :] * _rms_norm(moe, post2_ref[...])


def _moe_out_into_kernel(prev_ref, *refs):
    del prev_ref
    _moe_out_kernel(*refs)


def _moe_out_call(prev_out, n_total, b0, yt, w, t, x1, mod_x, post2_g, wsg_bf, wsu_bf, wsd_bf, seq):
    n = t.shape[0]
    tiles_per_seq = seq // TOUT
    first_tile = b0 * tiles_per_seq
    const = lambda i: (0, 0)
    in_specs = [
        pl.BlockSpec((TOP_K, TOUT, ROW_WORDS), lambda i: (0, i, 0)),
        pl.BlockSpec((TOUT, TOP_K), lambda i: (i, 0)),
        pl.BlockSpec((TOUT, ROW_WORDS), lambda i: (i, 0)),
        pl.BlockSpec((TOUT, D_MODEL), lambda i: (i, 0)),
        pl.BlockSpec((None, 6, D_MODEL), lambda i: (b0 + i // tiles_per_seq, 0, 0)),
        pl.BlockSpec((1, D_MODEL), const),
        pl.BlockSpec((D_MODEL, D_EXPERT), const),
        pl.BlockSpec((D_MODEL, D_EXPERT), const),
        pl.BlockSpec((D_EXPERT, D_MODEL), const),
    ]
    args = (yt, w, t, x1, mod_x, post2_g, wsg_bf, wsu_bf, wsd_bf)
    aliased = prev_out is not None
    return pl.pallas_call(
        _moe_out_into_kernel if aliased else _moe_out_kernel,
        grid=(n // TOUT,),
        in_specs=([pl.BlockSpec(memory_space=pl.ANY)] if aliased else []) + in_specs,
        out_specs=pl.BlockSpec((TOUT, D_MODEL), lambda i: (first_tile + i, 0)),
        out_shape=jax.ShapeDtypeStruct((n_total, D_MODEL), F32),
        input_output_aliases={0: 0} if aliased else {},
        compiler_params=pltpu.CompilerParams(dimension_semantics=("arbitrary",), vmem_limit_bytes=VMEM_LIMIT),
        name="moe_out",
    )(*(((prev_out,) if aliased else ()) + args))


def _moe_dispatch(after, hx2w, idx_t):
    n = hx2w.shape[0]
    n_slots = n * TOP_K + N_EXPERTS * ROW_TILE
    pos_t, plan = _positions_call(after, idx_t, n_slots // ROW_TILE)
    pos = pos_t.reshape(TOP_K, SC_WORKERS, n // (SC_WORKERS * SC_CHUNK), SC_CHUNK).transpose(1, 2, 0, 3)
    return _dispatch_call(hx2w, pos, n_slots), pos, plan


def kernel(x, c, ctx, c_ctx, w_mod, b_mod, pre1_g, post1_g, pre2_g, post2_g, w_in, w_dw, b_dw, ln_conv_g, ln_conv_b, w_conv_out, w_sc, b_sc, w_rg_a, b_rg_a, w_rg_x, b_rg_x, lru_lambda, w_rnn_out, w_out, w_router, router_bias, w_e_gate, w_e_up, w_e_down, w_s_gate, w_s_up, w_s_down):
    assert w_mod.shape[0] == 1, "single-layer block"
    bsz, seq, d = x.shape
    n = bsz * seq
    assert d == D_MODEL and seq % TM == 0 and seq % TOUT == 0 and bsz + 1 <= SUBLANES
    assert TOKEN_GROUPS == 2 and bsz % TOKEN_GROUPS == 0 and (n // TOKEN_GROUPS) % (SC_WORKERS * SC_CHUNK) == 0
    assert (n // TOKEN_GROUPS) % POS_TILE == 0

    cc = jnp.zeros((SUBLANES, D_MODEL), F32).at[:bsz].set(c).at[bsz].set(c_ctx)
    mod = _mod_call(cc, w_mod[0], b_mod)
    mod_x = mod[:bsz].reshape(bsz, 6, D_MODEL)
    mod_c = mod[bsz].reshape(6, D_MODEL)

    w_in_bf = w_in[0].astype(BF16)
    wa_bf = w_rg_a[0].astype(BF16)
    wx_bf = w_rg_x[0].astype(BF16)

    h0 = _ctx_call(ctx, mod_c, pre1_g, w_in_bf, w_sc[0], b_sc[0], wa_bf, b_rg_a[0], wx_bf, b_rg_x[0],
                   lru_lambda[0])

    wco_bf, wro_bf, wout_bf = w_conv_out[0].astype(BF16), w_rnn_out[0].astype(BF16), w_out[0].astype(BF16)
    wsg_bf, wsu_bf, wsd_bf = w_s_gate[0].astype(BF16), w_s_up[0].astype(BF16), w_s_down[0].astype(BF16)
    w_router_t, rbias = w_router[0].T, router_bias.reshape(N_EXPERTS, 1)
    gsz = bsz // TOKEN_GROUPS
    gn = gsz * seq
    none = jnp.zeros((TOP_K, LANES), jnp.int32)

    def fwd(g, after):
        return _mixer_fwd_call(g * gsz, gsz, after, x, mod_x, pre1_g, w_in_bf, w_dw[0], b_dw, ln_conv_g,
                               ln_conv_b, wco_bf, w_sc[0], b_sc[0], wa_bf, b_rg_a[0], wx_bf, b_rg_x[0],
                               lru_lambda[0], h0)

    def bwd(g, after, acts):
        gaya, gb, gbr, urnn, hf = acts
        x1, hx2w, idx_t, w_t = _mixer_bwd_call(
            g * gsz, gsz, after, urnn, hf, gbr, gaya, gb, x, mod_x, w_sc[0], b_sc[0], wa_bf, b_rg_a[0], wx_bf,
            b_rg_x[0], lru_lambda[0], h0, wro_bf, wout_bf, post1_g, pre2_g, w_router_t, rbias)
        return x1.reshape(gn, D_MODEL), hx2w.reshape(gn, ROW_WORDS), idx_t, w_t

    def experts(after, plan, xs):
        return _expert_gemm_call(after, plan, xs, w_e_gate[0], w_e_up[0], w_e_down[0])

    def finish(g, prev_out, yt, w_t, hx2w, x1):
        return _moe_out_call(prev_out, n, g * gsz, yt, w_t.T, hx2w, x1, mod_x, post2_g, wsg_bf, wsu_bf, wsd_bf, seq)

    x1_a, hx2w_a, idx_a, w_a = bwd(0, none, fwd(0, none))
    xs_a, pos_a, plan_a = _moe_dispatch(none, hx2w_a, idx_a)
    acts_b = fwd(1, idx_a)
    ys_a = experts(acts_b[4], plan_a, xs_a)
    yt_a = _collect_call(ys_a, pos_a, gn)
    x1_b, hx2w_b, idx_b, w_b = bwd(1, ys_a, acts_b)
    xs_b, pos_b, plan_b = _moe_dispatch(yt_a, hx2w_b, idx_b)
    out = finish(0, None, yt_a, w_a, hx2w_a, x1_a)
    ys_b = experts(none, plan_b, xs_b)
    yt_b = _collect_call(ys_b, pos_b, gn)
    out = finish(1, out, yt_b, w_b, hx2w_b, x1_b)
    return out.reshape(bsz, seq, D_MODEL)
```

```python
import jax
import jax.numpy as jnp
from jax import lax
from jax.experimental import pallas as pl
from jax.experimental.pallas import tpu as pltpu
from jax.experimental.pallas import tpu_sc as plsc

F32 = jnp.float32
BF16 = jnp.bfloat16

D_MODEL = 1024
GRID_W = 64
CONV_WIDTH = 31
CONV_HALF = CONV_WIDTH // 2
SHORT_CONV = 4
RNN_HEADS = 4
RNN_BLOCK = D_MODEL // RNN_HEADS
LRU_C = 8.0
N_EXPERTS = 64
N_GROUPS = 8
GROUP_SIZE = N_EXPERTS // N_GROUPS
TOPK_GROUPS = 4
TOP_K = 8
D_EXPERT = 256
ROUTED_SCALE = 2.5
EPS = 1e-6

SUBLANES = 8
TM = 512
ROWS_PER_TILE = TM // GRID_W
PAD = 16
ROW_STRIDE = GRID_W + PAD
UPAD_ROWS = ROWS_PER_TILE * ROW_STRIDE + PAD
SEG = TM // SUBLANES
HALO = (SHORT_CONV - 1) * SUBLANES
CONV_LANES = 256
LANES = 128
ROW_WORDS = D_MODEL // 2
ROW_TILE = 1024
POS_TILE = 512
TOUT = 512
TOKEN_GROUPS = 2
V7X_SC_CORES = 2
V7X_SC_SUBCORES = 16
SC_WORKERS = V7X_SC_CORES * V7X_SC_SUBCORES
SC_CHUNK = 64
SC_BUFFERS = 2
VMEM_LIMIT = 58 * 1024 * 1024


def _sigmoid(x):
    return 0.5 * (jnp.tanh(0.5 * x) + 1.0)


def _silu(x):
    return x * _sigmoid(x)


def _gelu_tanh(x):
    return 0.5 * x * (1.0 + jnp.tanh(0.7978845608028654 * (x + 0.044715 * (x * x * x))))


def _rms_norm(x, g):
    return x * lax.rsqrt(jnp.mean(x * x, axis=-1, keepdims=True) + EPS) * g


def _dot(a, b):
    return jnp.dot(a, b, preferred_element_type=F32)


def _pack_bf16_pairs(x):
    half = x.shape[-1] // 2
    lo = lax.bitcast_convert_type(x[:, :half].astype(BF16).astype(F32), jnp.uint32)
    hi = lax.bitcast_convert_type(x[:, half:].astype(BF16).astype(F32), jnp.uint32)
    return lax.bitcast_convert_type(hi | (lo >> 16), jnp.int32)


def _pack_bf16_pairs_native(x):
    half = x.shape[-1] // 2
    packed = pltpu.pack_elementwise([x[:, :half], x[:, half:]], packed_dtype=BF16)
    return lax.bitcast_convert_type(packed, jnp.int32)


def _unpack_bf16_pairs(words):
    u = lax.bitcast_convert_type(words, jnp.uint32)
    return (lax.bitcast_convert_type(u << 16, F32),
            lax.bitcast_convert_type(u & jnp.uint32(0xFFFF0000), F32))


def _log_decay(lam):
    return LRU_C * (jnp.minimum(lam, 0.0) - jnp.log1p(jnp.exp(-jnp.abs(lam))))


def _rglru_coeffs(v, wa_ref, ba, wx_ref, bx, c_lam, a_ref, b_ref):
    vb = v.astype(BF16)
    for h in range(RNN_HEADS):
        cs = slice(h * RNN_BLOCK, (h + 1) * RNN_BLOCK)
        vh = vb[:, cs]
        r = _sigmoid(_dot(vh, wa_ref[h]) + ba[:, cs])
        i = _sigmoid(_dot(vh, wx_ref[h]) + bx[:, cs])
        log_a = c_lam[:, cs] * r
        a = jnp.exp(log_a)
        a_ref[:, cs] = a
        b_ref[:, cs] = jnp.sqrt(jnp.tanh(-log_a) * (1.0 + a * a)) * (i * v[:, cs])


def _scan_tile(a_ref, b_ref, carry, n_rows, reverse):
    row = lax.broadcasted_iota(jnp.int32, (SUBLANES, D_MODEL), 0)
    n_groups = n_rows // SUBLANES

    def body(j, carry):
        g = (n_groups - 1 - j) if reverse else j
        off = pl.multiple_of(g * SUBLANES, SUBLANES)
        a = a_ref[pl.ds(off, SUBLANES), :]
        b = b_ref[pl.ds(off, SUBLANES), :]
        for s in (1, 2, 4):
            keep = (row < SUBLANES - s) if reverse else (row >= s)
            shift = (SUBLANES - s) if reverse else s
            a_sh = jnp.where(keep, pltpu.roll(a, shift, 0), 1.0)
            b_sh = jnp.where(keep, pltpu.roll(b, shift, 0), 0.0)
            b = a * b_sh + b
            a = a * a_sh
        h = a * carry + b
        b_ref[pl.ds(off, SUBLANES), :] = h
        last = h[0:1, :] if reverse else h[SUBLANES - 1:SUBLANES, :]
        return jnp.broadcast_to(last, (SUBLANES, D_MODEL))

    return lax.fori_loop(0, n_groups, body, carry, unroll=2)


def _segment_perm(to_natural):
    r = lax.broadcasted_iota(jnp.int32, (TM, TM), 0)
    c = lax.broadcasted_iota(jnp.int32, (TM, TM), 1)
    if to_natural:
        src = (r & (SEG - 1)) * SUBLANES + lax.shift_right_logical(r, SEG.bit_length() - 1)
    else:
        src = (r & (SUBLANES - 1)) * SEG + lax.shift_right_logical(r, SUBLANES.bit_length() - 1)
    return jnp.where(c == src, 1.0, 0.0).astype(BF16)


def _scan_segments(a_ref, b_ref, carry, reverse):
    row = lax.broadcasted_iota(jnp.int32, (SUBLANES, D_MODEL), 0)

    def body(s, hp):
        h, prod = hp
        off = pl.multiple_of(((SEG - 1 - s) if reverse else s) * SUBLANES, SUBLANES)
        a = a_ref[pl.ds(off, SUBLANES), :]
        h = a * h + b_ref[pl.ds(off, SUBLANES), :]
        prod = a * prod
        b_ref[pl.ds(off, SUBLANES), :] = h
        a_ref[pl.ds(off, SUBLANES), :] = prod
        return h, prod

    b, a = lax.fori_loop(0, SEG, body, (jnp.zeros((SUBLANES, D_MODEL), F32), jnp.ones((SUBLANES, D_MODEL), F32)),
                         unroll=4)
    for s in (1, 2, 4):
        keep = (row < SUBLANES - s) if reverse else (row >= s)
        shift = (SUBLANES - s) if reverse else s
        a_sh = jnp.where(keep, pltpu.roll(a, shift, 0), 1.0)
        b_sh = jnp.where(keep, pltpu.roll(b, shift, 0), 0.0)
        b = a * b_sh + b
        a = a * a_sh
    leave = a * carry + b
    if reverse:
        enter = jnp.where(row < SUBLANES - 1, pltpu.roll(leave, SUBLANES - 1, 0), carry)
        last = leave[0:1, :]
    else:
        enter = jnp.where(row >= 1, pltpu.roll(leave, 1, 0), carry)
        last = leave[SUBLANES - 1:SUBLANES, :]
    h = b_ref[...].reshape(SEG, SUBLANES, D_MODEL) + a_ref[...].reshape(SEG, SUBLANES, D_MODEL) * enter[None]
    b_ref[...] = h.reshape(TM, D_MODEL)
    return jnp.broadcast_to(last, (SUBLANES, D_MODEL))


def _short_conv_segments(uext_ref, wsc_ref, bsc_ref):
    v = bsc_ref[...] + wsc_ref[0:1, :] * uext_ref[0:TM, :]
    for k in range(1, SHORT_CONV):
        v = v + wsc_ref[k:k + 1, :] * uext_ref[k * SUBLANES:k * SUBLANES + TM, :]
    return v


def _mod_kernel(c_ref, w_ref, b_ref, o_ref):
    o_ref[...] = jnp.dot(_silu(c_ref[...]), w_ref[...], preferred_element_type=F32,
                         precision=lax.Precision.HIGHEST) + b_ref[...]


def _mod_call(cc, w_mod, b_mod):
    n_mod = w_mod.shape[1]
    return pl.pallas_call(
        _mod_kernel,
        grid=(n_mod // D_MODEL,),
        in_specs=[
            pl.BlockSpec((SUBLANES, D_MODEL), lambda j: (0, 0)),
            pl.BlockSpec((D_MODEL, D_MODEL), lambda j: (0, j)),
            pl.BlockSpec((1, D_MODEL), lambda j: (0, j)),
        ],
        out_specs=pl.BlockSpec((SUBLANES, D_MODEL), lambda j: (0, j)),
        out_shape=jax.ShapeDtypeStruct((SUBLANES, n_mod), F32),
        name="mod",
    )(cc, w_mod, b_mod)


def _ctx_kernel(ctx_ref, mod_ref, g_ref, w_ref, wsc_ref, bsc_ref, wa_ref, ba_ref, wx_ref, bx_ref,
                lam_ref, o_ref, uext_ref, a_ref, b_ref):
    n = ctx_ref.shape[0]
    hc = _rms_norm(ctx_ref[...], g_ref[...]) * (1.0 + mod_ref[1:2, :]) + mod_ref[0:1, :]
    u = _dot(hc.astype(BF16), w_ref[...])
    zeros8 = jnp.zeros((SUBLANES, D_MODEL), F32)
    uext_ref[0:SUBLANES, :] = zeros8
    uext_ref[SUBLANES:SUBLANES + n, :] = u
    uext_ref[SUBLANES + n:2 * SUBLANES + n, :] = zeros8
    for d in range(2):
        v = jnp.broadcast_to(bsc_ref[d:d + 1, :], (n, D_MODEL))
        for k in range(SHORT_CONV):
            start = SUBLANES + k - (SHORT_CONV - 1) * (1 - d)
            v = v + wsc_ref[d, k:k + 1, :] * uext_ref[start:start + n, :]
        _rglru_coeffs(v, wa_ref.at[d], ba_ref[d:d + 1, :], wx_ref.at[d], bx_ref[d:d + 1, :],
                      _log_decay(lam_ref[d:d + 1, :]), a_ref, b_ref)
        final = _scan_tile(a_ref, b_ref, zeros8, n, reverse=(d == 1))
        o_ref[d:d + 1, :] = final[0:1, :]


def _ctx_call(ctx, mod_c, pre1_g, w_in_bf, w_sc, b_sc, wa_bf, b_rg_a, wx_bf, b_rg_x, lam):
    bsz, n, _ = ctx.shape
    const2 = lambda b: (0, 0)
    const3 = lambda b: (0, 0, 0)
    const4 = lambda b: (0, 0, 0, 0)
    return pl.pallas_call(
        _ctx_kernel,
        grid=(bsz,),
        in_specs=[
            pl.BlockSpec((None, n, D_MODEL), lambda b: (b, 0, 0)),
            pl.BlockSpec((6, D_MODEL), const2),
            pl.BlockSpec((1, D_MODEL), const2),
            pl.BlockSpec((D_MODEL, D_MODEL), lambda b: (0, 2)),
            pl.BlockSpec((2, SHORT_CONV, D_MODEL), const3),
            pl.BlockSpec((2, D_MODEL), const2),
            pl.BlockSpec((2, RNN_HEADS, RNN_BLOCK, RNN_BLOCK), const4),
            pl.BlockSpec((2, D_MODEL), const2),
            pl.BlockSpec((2, RNN_HEADS, RNN_BLOCK, RNN_BLOCK), const4),
            pl.BlockSpec((2, D_MODEL), const2),
            pl.BlockSpec((2, D_MODEL), const2),
        ],
        out_specs=pl.BlockSpec((None, 2, D_MODEL), lambda b: (b, 0, 0)),
        out_shape=jax.ShapeDtypeStruct((bsz, 2, D_MODEL), F32),
        scratch_shapes=[
            pltpu.VMEM((n + 2 * SUBLANES, D_MODEL), F32),
            pltpu.VMEM((n, D_MODEL), F32),
            pltpu.VMEM((n, D_MODEL), F32),
        ],
        name="ctx",
    )(ctx, mod_c, pre1_g, w_in_bf, w_sc, b_sc, wa_bf, b_rg_a, wx_bf, b_rg_x, lam)


def _mixer_fwd_kernel(after_ref, x_ref, mod_ref, g_ref, win_ref, wdw_ref, bdw_ref, lng_ref, lnb_ref, wco_ref,
                      wsc_ref, bsc_ref, wa_ref, ba_ref, wx_ref, bx_ref, lam_ref, h0_ref,
                      gaya_ref, gb_ref, gbr_ref, urnn_ref, hf_ref,
                      upad_ref, ush_ref, wb_ref, cv_ref, uext_ref, a_ref, b_ref, carry_ref, halo_ref, perm_ref):
    del after_ref
    j = pl.program_id(1)

    @pl.when(j == 0)
    def _():
        carry_ref[...] = jnp.broadcast_to(h0_ref[0:1, :], (SUBLANES, D_MODEL))
        halo_ref[...] = jnp.zeros((SUBLANES, D_MODEL), F32)
        perm_ref[...] = _segment_perm(to_natural=False)
        zeros_pad = jnp.zeros((PAD, D_MODEL), F32)
        for r in range(ROWS_PER_TILE + 1):
            upad_ref[r * ROW_STRIDE:r * ROW_STRIDE + PAD, :] = zeros_pad
        for k in range(CONV_WIDTH):
            wb_ref[k] = jnp.broadcast_to(wdw_ref[k:k + 1, :], (SUBLANES, D_MODEL))

    hx = (_rms_norm(x_ref[...], g_ref[...]) * (1.0 + mod_ref[1:2, :]) + mod_ref[0:1, :]).astype(BF16)

    u = _dot(hx, win_ref[:, 0:D_MODEL]) * _sigmoid(_dot(hx, win_ref[:, D_MODEL:2 * D_MODEL]))
    for r in range(ROWS_PER_TILE):
        upad_ref[PAD + r * ROW_STRIDE:PAD + r * ROW_STRIDE + GRID_W, :] = u[r * GRID_W:(r + 1) * GRID_W, :]
    vregs_per_row = GRID_W // SUBLANES
    hxp = _dot(perm_ref[...], hx).astype(BF16)
    g_a = None
    for c in range(D_MODEL // CONV_LANES):
        if c == 0:
            gb_ref[...] = _sigmoid(_dot(hxp, win_ref[:, 5 * D_MODEL:6 * D_MODEL])).astype(BF16)
        elif c == 1:
            gbr_ref[...] = _gelu_tanh(_dot(hxp, win_ref[:, 3 * D_MODEL:4 * D_MODEL])).astype(BF16)
        elif c == 2:
            g_a = _sigmoid(_dot(hxp, win_ref[:, 4 * D_MODEL:5 * D_MODEL]))
        else:
            ur = _dot(hxp, win_ref[:, 2 * D_MODEL:3 * D_MODEL])
            urnn_ref[...] = ur.astype(BF16)
            uext_ref[HALO:HALO + TM, :] = ur
        cs = slice(c * CONV_LANES, (c + 1) * CONV_LANES)
        xpad = upad_ref[:, cs]
        for s in range(1, SUBLANES):
            ush_ref[s - 1] = pltpu.roll(xpad, UPAD_ROWS - s, 0)
        for r in range(ROWS_PER_TILE):
            acc = jnp.broadcast_to(bdw_ref[:, cs].reshape(1, 1, CONV_LANES), (vregs_per_row, SUBLANES, CONV_LANES))
            for k in range(CONV_WIDTH):
                q, s = divmod(r * ROW_STRIDE + PAD - CONV_HALF + k, SUBLANES)
                rows = slice(q * SUBLANES, q * SUBLANES + GRID_W)
                win = upad_ref[rows, cs] if s == 0 else ush_ref[s - 1, rows, :]
                acc = acc + wb_ref[k, :, cs] * win.reshape(vregs_per_row, SUBLANES, CONV_LANES)
            cv_ref[r * GRID_W:(r + 1) * GRID_W, cs] = acc.reshape(GRID_W, CONV_LANES)
    cv = cv_ref[...]
    cvc = cv - jnp.mean(cv, axis=-1, keepdims=True)
    cvn = cvc * lax.rsqrt(jnp.mean(cvc * cvc, axis=-1, keepdims=True) + EPS) * lng_ref[...] + lnb_ref[...]
    y_a = _dot(_dot(perm_ref[...], _silu(cvn).astype(BF16)).astype(BF16), wco_ref[...])

    gaya_ref[...] = (g_a * y_a).astype(BF16)

    row8 = lax.broadcasted_iota(jnp.int32, (SUBLANES, D_MODEL), 0)
    for jj in range(SHORT_CONV - 1):
        late = uext_ref[HALO + (SEG - (SHORT_CONV - 1) + jj) * SUBLANES:HALO + (SEG - (SHORT_CONV - 1) + jj + 1) * SUBLANES, :]
        prev_tile = jnp.broadcast_to(halo_ref[jj:jj + 1, :], (SUBLANES, D_MODEL))
        uext_ref[jj * SUBLANES:(jj + 1) * SUBLANES, :] = jnp.where(row8 == 0, prev_tile, pltpu.roll(late, 1, 0))
        halo_ref[jj:jj + 1, :] = late[SUBLANES - 1:SUBLANES, :]
    v = _short_conv_segments(uext_ref, wsc_ref, bsc_ref)
    _rglru_coeffs(v, wa_ref, ba_ref[...], wx_ref, bx_ref[...], _log_decay(lam_ref[...]), a_ref, b_ref)
    carry_ref[...] = _scan_segments(a_ref, b_ref, carry_ref[...], reverse=False)
    hf_ref[...] = b_ref[...].astype(BF16)


def _resident(shape):
    nd = len(shape)
    return pl.BlockSpec(shape, lambda b, j: (0,) * nd, pipeline_mode=pl.Buffered(1))


def _mixer_fwd_call(b0, bsz, after, x, mod_x, pre1_g, w_in_bf, w_dw, b_dw, ln_g, ln_b, wco_bf,
                    w_sc, b_sc, wa_bf, b_rg_a, wx_bf, b_rg_x, lam, h0):
    seq = x.shape[1]
    nt = seq // TM
    tile = pl.BlockSpec((None, TM, D_MODEL), lambda b, j: (b, j, 0))
    act = jax.ShapeDtypeStruct((bsz, seq, D_MODEL), BF16)
    head_w = pl.BlockSpec((None, RNN_HEADS, RNN_BLOCK, RNN_BLOCK), lambda b, j: (0, 0, 0, 0),
                          pipeline_mode=pl.Buffered(1))
    dir_row = pl.BlockSpec((None, 1, D_MODEL), lambda b, j: (0, 0, 0), pipeline_mode=pl.Buffered(1))
    return pl.pallas_call(
        _mixer_fwd_kernel,
        grid=(bsz, nt),
        in_specs=[
            pl.BlockSpec(memory_space=pl.ANY),
            pl.BlockSpec((None, TM, D_MODEL), lambda b, j: (b0 + b, j, 0)),
            pl.BlockSpec((None, 6, D_MODEL), lambda b, j: (b0 + b, 0, 0)),
            _resident((1, D_MODEL)),
            _resident((D_MODEL, 6 * D_MODEL)),
            _resident((CONV_WIDTH, D_MODEL)),
            _resident((1, D_MODEL)),
            _resident((1, D_MODEL)),
            _resident((1, D_MODEL)),
            _resident((D_MODEL, D_MODEL)),
            pl.BlockSpec((None, SHORT_CONV, D_MODEL), lambda b, j: (0, 0, 0), pipeline_mode=pl.Buffered(1)),
            dir_row, head_w, dir_row, head_w, dir_row, dir_row,
            pl.BlockSpec((None, 2, D_MODEL), lambda b, j: (b0 + b, 0, 0)),
        ],
        out_specs=[tile] * 5,
        out_shape=[act] * 5,
        scratch_shapes=[
            pltpu.VMEM((UPAD_ROWS, D_MODEL), F32),
            pltpu.VMEM((SUBLANES - 1, UPAD_ROWS, CONV_LANES), F32),
            pltpu.VMEM((CONV_WIDTH, SUBLANES, D_MODEL), F32),
            pltpu.VMEM((TM, D_MODEL), F32),
            pltpu.VMEM((TM + HALO, D_MODEL), F32),
            pltpu.VMEM((TM, D_MODEL), F32),
            pltpu.VMEM((TM, D_MODEL), F32),
            pltpu.VMEM((SUBLANES, D_MODEL), F32),
            pltpu.VMEM((SUBLANES, D_MODEL), F32),
            pltpu.VMEM((TM, TM), BF16),
        ],
        compiler_params=pltpu.CompilerParams(
            dimension_semantics=("arbitrary", "arbitrary"), vmem_limit_bytes=VMEM_LIMIT),
        name="mixer_fwd",
    )(after, x, mod_x, pre1_g, w_in_bf, w_dw, b_dw, ln_g, ln_b, wco_bf,
      w_sc, b_sc.reshape(2, 1, D_MODEL), wa_bf, b_rg_a.reshape(2, 1, D_MODEL), wx_bf,
      b_rg_x.reshape(2, 1, D_MODEL), lam.reshape(2, 1, D_MODEL), h0)


def _route(logits_t, bias):
    t = logits_t.shape[1]
    scores = _sigmoid(logits_t)
    sel = scores + bias
    neg_inf = jnp.float32(-jnp.inf)

    sel3 = sel.reshape(N_GROUPS, GROUP_SIZE, t)
    within = lax.broadcasted_iota(jnp.int32, sel3.shape, 1)
    m1 = jnp.max(sel3, axis=1, keepdims=True)
    first = jnp.min(jnp.where(sel3 == m1, within, GROUP_SIZE), axis=1, keepdims=True)
    m2 = jnp.max(jnp.where(within == first, neg_inf, sel3), axis=1, keepdims=True)
    gscore = (m1 + m2).reshape(N_GROUPS, t)

    gidx = lax.broadcasted_iota(jnp.int32, gscore.shape, 0)
    rank = jnp.zeros(gscore.shape, jnp.int32)
    for g in range(N_GROUPS):
        other = gscore[g:g + 1, :]
        beats = jnp.where(other > gscore, 1, jnp.where((other == gscore) & (gidx > g), 1, 0))
        rank = rank + beats
    gkeep = (rank < TOPK_GROUPS).reshape(N_GROUPS, 1, t)
    masked = jnp.where(gkeep, sel3, neg_inf).reshape(N_EXPERTS, t)

    eidx = lax.broadcasted_iota(jnp.int32, masked.shape, 0)
    picks, weights = [], []
    for _ in range(TOP_K):
        m = jnp.max(masked, axis=0, keepdims=True)
        first = jnp.min(jnp.where(masked == m, eidx, N_EXPERTS), axis=0, keepdims=True)
        pick = eidx == first
        picks.append(first)
        weights.append(jnp.sum(jnp.where(pick, scores, 0.0), axis=0, keepdims=True))
        masked = jnp.where(pick, neg_inf, masked)
    idx = jnp.concatenate(picks, axis=0)
    w = jnp.concatenate(weights, axis=0)
    return idx, ROUTED_SCALE * w / jnp.sum(w, axis=0, keepdims=True)


def _mixer_bwd_kernel(after_ref, urnn_ref, hf_ref, gbr_ref, gaya_ref, gb_ref, x_ref, mod_ref,
                      wsc_ref, bsc_ref, wa_ref, ba_ref, wx_ref, bx_ref, lam_ref, h0_ref,
                      wro_ref, wout_ref, post1_ref, pre2_ref, wrt_ref, rbias_ref,
                      x1_ref, hx2w_ref, idx_ref, w_ref,
                      uext_ref, a_ref, b_ref, carry_ref, halo_ref, perm_ref):
    del after_ref
    j = pl.program_id(1)

    @pl.when(j == 0)
    def _():
        carry_ref[...] = jnp.broadcast_to(h0_ref[1:2, :], (SUBLANES, D_MODEL))
        halo_ref[...] = jnp.zeros((SUBLANES, D_MODEL), F32)
        perm_ref[...] = _segment_perm(to_natural=True)

    uext_ref[0:TM, :] = urnn_ref[...].astype(F32)
    row8 = lax.broadcasted_iota(jnp.int32, (SUBLANES, D_MODEL), 0)
    for jj in range(SHORT_CONV - 1):
        early = uext_ref[jj * SUBLANES:(jj + 1) * SUBLANES, :]
        next_tile = jnp.broadcast_to(halo_ref[jj:jj + 1, :], (SUBLANES, D_MODEL))
        uext_ref[TM + jj * SUBLANES:TM + (jj + 1) * SUBLANES, :] = jnp.where(
            row8 == SUBLANES - 1, next_tile, pltpu.roll(early, SUBLANES - 1, 0))
        halo_ref[jj:jj + 1, :] = early[0:1, :]
    v = _short_conv_segments(uext_ref, wsc_ref, bsc_ref)
    _rglru_coeffs(v, wa_ref, ba_ref[...], wx_ref, bx_ref[...], _log_decay(lam_ref[...]), a_ref, b_ref)
    carry_ref[...] = _scan_segments(a_ref, b_ref, carry_ref[...], reverse=True)

    h_sum = hf_ref[...].astype(F32) + b_ref[...]
    y_b = _dot((gbr_ref[...].astype(F32) * h_sum).astype(BF16), wro_ref[...])
    mix = gaya_ref[...].astype(F32) + gb_ref[...].astype(F32) * y_b
    out = _dot(_dot(perm_ref[...], mix.astype(BF16)).astype(BF16), wout_ref[...])
    x1 = x_ref[...] + mod_ref[2:3, :] * _rms_norm(out, post1_ref[...])
    x1_ref[...] = x1

    hx2 = _rms_norm(x1, pre2_ref[...]) * (1.0 + mod_ref[4:5, :]) + mod_ref[3:4, :]
    hx2w_ref[...] = _pack_bf16_pairs(hx2)
    logits_t = lax.dot_general(wrt_ref[...], hx2, (((1,), (1,)), ((), ())),
                               preferred_element_type=F32, precision=lax.Precision.HIGHEST)
    idx, w = _route(logits_t, rbias_ref[...])
    idx_ref[...] = idx
    w_ref[...] = w


def _mixer_bwd_call(b0, bsz, after, urnn, hf, gbr, gaya, gb, x, mod_x, w_sc, b_sc, wa_bf, b_rg_a, wx_bf, b_rg_x, lam, h0,
                    wro_bf, wout_bf, post1_g, pre2_g, w_router_t, router_bias):
    seq = x.shape[1]
    nt = seq // TM
    rev = lambda b, j: (b, nt - 1 - j, 0)
    tile = pl.BlockSpec((None, TM, D_MODEL), rev)
    head_w = pl.BlockSpec((None, RNN_HEADS, RNN_BLOCK, RNN_BLOCK), lambda b, j: (1, 0, 0, 0),
                          pipeline_mode=pl.Buffered(1))
    dir_row = pl.BlockSpec((None, 1, D_MODEL), lambda b, j: (1, 0, 0), pipeline_mode=pl.Buffered(1))
    return pl.pallas_call(
        _mixer_bwd_kernel,
        grid=(bsz, nt),
        in_specs=[
            pl.BlockSpec(memory_space=pl.ANY),
            tile, tile, tile, tile, tile,
            pl.BlockSpec((None, TM, D_MODEL), lambda b, j: (b0 + b, nt - 1 - j, 0)),
            pl.BlockSpec((None, 6, D_MODEL), lambda b, j: (b0 + b, 0, 0)),
            pl.BlockSpec((None, SHORT_CONV, D_MODEL), lambda b, j: (1, 0, 0), pipeline_mode=pl.Buffered(1)),
            dir_row, head_w, dir_row, head_w, dir_row, dir_row,
            pl.BlockSpec((None, 2, D_MODEL), lambda b, j: (b0 + b, 0, 0)),
            _resident((D_MODEL, D_MODEL)),
            _resident((D_MODEL, D_MODEL)),
            _resident((1, D_MODEL)),
            _resident((1, D_MODEL)),
            _resident((N_EXPERTS, D_MODEL)),
            _resident((N_EXPERTS, 1)),
        ],
        out_specs=[
            pl.BlockSpec((None, TM, D_MODEL), rev),
            pl.BlockSpec((None, TM, ROW_WORDS), rev),
            pl.BlockSpec((TOP_K, TM), lambda b, j: (0, b * nt + nt - 1 - j)),
            pl.BlockSpec((TOP_K, TM), lambda b, j: (0, b * nt + nt - 1 - j)),
        ],
        out_shape=[
            jax.ShapeDtypeStruct((bsz, seq, D_MODEL), F32),
            jax.ShapeDtypeStruct((bsz, seq, ROW_WORDS), jnp.int32),
            jax.ShapeDtypeStruct((TOP_K, bsz * seq), jnp.int32),
            jax.ShapeDtypeStruct((TOP_K, bsz * seq), F32),
        ],
        scratch_shapes=[
            pltpu.VMEM((TM + HALO, D_MODEL), F32),
            pltpu.VMEM((TM, D_MODEL), F32),
            pltpu.VMEM((TM, D_MODEL), F32),
            pltpu.VMEM((SUBLANES, D_MODEL), F32),
            pltpu.VMEM((SUBLANES, D_MODEL), F32),
            pltpu.VMEM((TM, TM), BF16),
        ],
        compiler_params=pltpu.CompilerParams(
            dimension_semantics=("arbitrary", "arbitrary"), vmem_limit_bytes=VMEM_LIMIT),
        name="mixer_bwd",
    )(after, urnn, hf, gbr, gaya, gb, x, mod_x, w_sc, b_sc.reshape(2, 1, D_MODEL), wa_bf,
      b_rg_a.reshape(2, 1, D_MODEL), wx_bf, b_rg_x.reshape(2, 1, D_MODEL), lam.reshape(2, 1, D_MODEL), h0,
      wro_bf, wout_bf, post1_g, pre2_g, w_router_t, router_bias)


def _positions_kernel(after_ref, idx_ref, pos_ref, plan_ref):
    del after_ref
    n = idx_ref.shape[1]
    n_tiles = n // POS_TILE
    eidx = lax.broadcasted_iota(jnp.int32, (N_EXPERTS, POS_TILE), 0)

    def chosen(t):
        idx = idx_ref[:, pl.ds(pl.multiple_of(t * POS_TILE, POS_TILE), POS_TILE)]
        ch = jnp.zeros((N_EXPERTS, POS_TILE), F32)
        for k in range(TOP_K):
            ch = ch + jnp.where(eidx == idx[k:k + 1, :], 1.0, 0.0)
        return idx, ch

    def count_body(t, cnt):
        return cnt + jnp.sum(chosen(t)[1], axis=1, keepdims=True)

    cnt = lax.fori_loop(0, n_tiles, count_body, jnp.zeros((N_EXPERTS, 1), F32))
    padded = jnp.ceil(cnt * (1.0 / ROW_TILE)) * ROW_TILE
    r = lax.broadcasted_iota(jnp.int32, (N_EXPERTS, N_EXPERTS), 0)
    c = lax.broadcasted_iota(jnp.int32, (N_EXPERTS, N_EXPERTS), 1)
    off = jnp.dot(jnp.where(c < r, 1.0, 0.0), jnp.broadcast_to(padded, (N_EXPERTS, LANES)),
                  preferred_element_type=F32, precision=lax.Precision.HIGHEST)[:, 0:1]
    end = off + padded

    n_map = plan_ref.shape[1]
    tstart = lax.broadcasted_iota(jnp.int32, (N_EXPERTS, n_map), 1).astype(F32) * ROW_TILE
    te = jnp.minimum(jnp.sum(jnp.where(end <= tstart, 1, 0), axis=0, keepdims=True), N_EXPERTS - 1)
    emap = lax.broadcasted_iota(jnp.int32, (N_EXPERTS, n_map), 0)
    live_end = jnp.sum(jnp.where(emap == te, off + cnt, 0.0), axis=0, keepdims=True)
    total = jnp.sum(padded, axis=0, keepdims=True)
    plan_ref[0:1, :] = te
    plan_ref[1:2, :] = jnp.clip(live_end - tstart[0:1, :], 0.0, ROW_TILE).astype(jnp.int32)
    plan_ref[2:3, :] = jnp.broadcast_to(total * (1.0 / ROW_TILE), (1, n_map)).astype(jnp.int32)

    row = lax.broadcasted_iota(jnp.int32, (POS_TILE, POS_TILE), 0)
    col = lax.broadcasted_iota(jnp.int32, (POS_TILE, POS_TILE), 1)
    before = jnp.where(row < col, 1.0, 0.0).astype(BF16)

    def pos_body(t, carry):
        idx, ch = chosen(t)
        base = _dot(ch.astype(BF16), before) + (carry + off)
        rows = [jnp.sum(jnp.where(eidx == idx[k:k + 1, :], base, 0.0), axis=0, keepdims=True)
                for k in range(TOP_K)]
        pos_ref[:, pl.ds(pl.multiple_of(t * POS_TILE, POS_TILE), POS_TILE)] = (
            jnp.concatenate(rows, axis=0).astype(jnp.int32))
        return carry + jnp.sum(ch, axis=1, keepdims=True)

    lax.fori_loop(0, n_tiles, pos_body, jnp.zeros((N_EXPERTS, 1), F32))


def _positions_call(after, idx_t, n_row_tiles):
    n = idx_t.shape[1]
    n_map = -(-n_row_tiles // LANES) * LANES
    return pl.pallas_call(
        _positions_kernel,
        in_specs=[pl.BlockSpec(memory_space=pl.ANY), pl.BlockSpec(memory_space=pltpu.VMEM)],
        out_shape=[
            jax.ShapeDtypeStruct((TOP_K, n), jnp.int32),
            jax.ShapeDtypeStruct((3, n_map), jnp.int32),
        ],
        name="moe_positions",
    )(after, idx_t)


def _sc_mesh():
    return plsc.VectorSubcoreMesh(core_axis_name="c", subcore_axis_name="s",
                                  num_cores=V7X_SC_CORES, num_subcores=V7X_SC_SUBCORES)


def _sc_worker():
    return lax.axis_index("s") * V7X_SC_CORES + lax.axis_index("c")


def _dispatch_call(rows, pos, n_slots):
    n = rows.shape[0]
    tok_w = n // SC_WORKERS
    n_items = tok_w // SC_CHUNK

    def body(rows_hbm, pos_hbm, xs_hbm, idx_v, rows_v, lsem, ssem):
        wid = _sc_worker()
        pltpu.sync_copy(pos_hbm.at[wid], idx_v)
        base = wid * tok_w

        def load(i):
            b = i % SC_BUFFERS
            return pltpu.async_copy(rows_hbm.at[pl.ds(base + i * SC_CHUNK, SC_CHUNK)], rows_v.at[b], lsem.at[b])

        def scatter(i):
            b = i % SC_BUFFERS
            return [pltpu.async_copy(rows_v.at[b], xs_hbm.at[idx_v.at[i, k]], ssem.at[b]) for k in range(TOP_K)]

        loads = {i: load(i) for i in range(SC_BUFFERS - 1)}
        scat = {}
        for i in range(n_items):
            loads[i].wait()
            scat[i] = scatter(i)
            if i >= 1:
                for cp in scat[i - 1]:
                    cp.wait()
            if i + SC_BUFFERS - 1 < n_items:
                loads[i + SC_BUFFERS - 1] = load(i + SC_BUFFERS - 1)
        for cp in scat[n_items - 1]:
            cp.wait()

    return pl.kernel(
        body, mesh=_sc_mesh(),
        out_type=jax.ShapeDtypeStruct((n_slots, ROW_WORDS), jnp.int32),
        scratch_types=[pltpu.VMEM((n_items, TOP_K, SC_CHUNK), jnp.int32),
                       pltpu.VMEM((SC_BUFFERS, SC_CHUNK, ROW_WORDS), jnp.int32),
                       pltpu.SemaphoreType.DMA((SC_BUFFERS,)), pltpu.SemaphoreType.DMA((SC_BUFFERS,))],
        compiler_params=pltpu.CompilerParams(use_tc_tiling_on_sc=True),
        name="moe_dispatch",
    )(rows, pos)


def _collect_call(ys, pos, n):
    tok_w = n // SC_WORKERS
    n_chunks = tok_w // SC_CHUNK
    items = [(c, k) for c in range(n_chunks) for k in range(TOP_K)]

    def body(ys_hbm, pos_hbm, yt_hbm, idx_v, rows_v, gsem, wsem):
        wid = _sc_worker()
        pltpu.sync_copy(pos_hbm.at[wid], idx_v)
        base = wid * tok_w

        def gather(j):
            c, k = items[j]
            b = j % SC_BUFFERS
            return pltpu.async_copy(ys_hbm.at[idx_v.at[c, k]], rows_v.at[b], gsem.at[b])

        def write(j):
            c, k = items[j]
            b = j % SC_BUFFERS
            return pltpu.async_copy(rows_v.at[b], yt_hbm.at[k, pl.ds(base + c * SC_CHUNK, SC_CHUNK)], wsem.at[b])

        g = {j: gather(j) for j in range(SC_BUFFERS - 1)}
        w = {}
        for j in range(len(items)):
            g[j].wait()
            w[j] = write(j)
            if j >= 1:
                w[j - 1].wait()
            if j + SC_BUFFERS - 1 < len(items):
                g[j + SC_BUFFERS - 1] = gather(j + SC_BUFFERS - 1)
        w[len(items) - 1].wait()

    return pl.kernel(
        body, mesh=_sc_mesh(),
        out_type=jax.ShapeDtypeStruct((TOP_K, n, ROW_WORDS), jnp.int32),
        scratch_types=[pltpu.VMEM((n_chunks, TOP_K, SC_CHUNK), jnp.int32),
                       pltpu.VMEM((SC_BUFFERS, SC_CHUNK, ROW_WORDS), jnp.int32),
                       pltpu.SemaphoreType.DMA((SC_BUFFERS,)), pltpu.SemaphoreType.DMA((SC_BUFFERS,))],
        compiler_params=pltpu.CompilerParams(use_tc_tiling_on_sc=True),
        name="moe_collect",
    )(ys, pos)


def _expert_gemm_kernel(plan_ref, after_ref, xs_ref, wg_ref, wu_ref, wd_ref, ys_ref, wgu_scr, wd_scr):
    del after_ref
    i = pl.program_id(0)
    e = plan_ref[0, i]
    prev = plan_ref[0, jnp.maximum(i - 1, 0)]
    live = plan_ref[1, i]

    @pl.when((i == 0) | (e != prev))
    def _():
        wgu_scr[:, 0:D_EXPERT] = wg_ref[...].astype(BF16)
        wgu_scr[:, D_EXPERT:2 * D_EXPERT] = wu_ref[...].astype(BF16)
        wd_scr[...] = wd_ref[...].astype(BF16)

    def swiglu_rows(n_rows):
        lo, hi = _unpack_bf16_pairs(xs_ref[0:n_rows, :])
        gu = _dot(lo.astype(BF16), wgu_scr[0:ROW_WORDS, :]) + _dot(hi.astype(BF16), wgu_scr[ROW_WORDS:D_MODEL, :])
        h = _silu(gu[:, 0:D_EXPERT]) * gu[:, D_EXPERT:2 * D_EXPERT]
        y = _dot(h.astype(BF16), wd_scr[...])
        ys_ref[0:n_rows, :] = _pack_bf16_pairs_native(y)

    @pl.when((i < plan_ref[2, 0]) & (live > ROW_TILE // 2))
    def _():
        swiglu_rows(ROW_TILE)

    @pl.when((i < plan_ref[2, 0]) & (live <= ROW_TILE // 2))
    def _():
        swiglu_rows(ROW_TILE // 2)


def _expert_gemm_call(after, plan, xs, w_e_gate, w_e_up, w_e_down):
    n_slots = xs.shape[0]
    n_row_tiles = n_slots // ROW_TILE
    rows = pl.BlockSpec((ROW_TILE, ROW_WORDS), lambda i, plan: (jnp.minimum(i, plan[2, 0] - 1), 0))
    expert = lambda i, plan: (plan[0, i], 0, 0)
    return pl.pallas_call(
        _expert_gemm_kernel,
        grid_spec=pltpu.PrefetchScalarGridSpec(
            num_scalar_prefetch=1,
            grid=(n_row_tiles,),
            in_specs=[
                pl.BlockSpec(memory_space=pl.ANY),
                rows,
                pl.BlockSpec((None, D_MODEL, D_EXPERT), expert),
                pl.BlockSpec((None, D_MODEL, D_EXPERT), expert),
                pl.BlockSpec((None, D_EXPERT, D_MODEL), expert),
            ],
            out_specs=rows,
            scratch_shapes=[pltpu.VMEM((D_MODEL, 2 * D_EXPERT), BF16), pltpu.VMEM((D_EXPERT, D_MODEL), BF16)],
        ),
        out_shape=jax.ShapeDtypeStruct((n_slots, ROW_WORDS), jnp.int32),
        compiler_params=pltpu.CompilerParams(dimension_semantics=("arbitrary",), vmem_limit_bytes=VMEM_LIMIT),
        name="moe_experts",
    )(plan, after, xs, w_e_gate, w_e_up, w_e_down)


def _moe_out_kernel(yt_ref, w_ref, t_ref, x1_ref, mod_ref, post2_ref, wsg_ref, wsu_ref, wsd_ref, o_ref, done_ref):
    done_ref[...] = jnp.zeros(done_ref.shape, done_ref.dtype)
    lo, hi = _unpack_bf16_pairs(t_ref[...])
    lo = lo.astype(BF16)
    hi = hi.astype(BF16)
    g = _dot(lo, wsg_ref[0:ROW_WORDS, :]) + _dot(hi, wsg_ref[ROW_WORDS:D_MODEL, :])
    u = _dot(lo, wsu_ref[0:ROW_WORDS, :]) + _dot(hi, wsu_ref[ROW_WORDS:D_MODEL, :])
    shared = _dot((_silu(g) * u).astype(BF16), wsd_ref[...])
    acc_lo = shared[:, 0:ROW_WORDS]
    acc_hi = shared[:, ROW_WORDS:D_MODEL]
    for k in range(TOP_K):
        y_lo, y_hi = _unpack_bf16_pairs(yt_ref[k])
        wk = w_ref[:, k:k + 1]
        acc_lo = acc_lo + wk * y_lo
        acc_hi = acc_hi + wk * y_hi
    moe = jnp.concatenate([acc_lo, acc_hi], axis=-1)
    o_ref[...] = x1_ref[...] + mod_ref[5:6, :] * _rms_norm(moe, post2_ref[...])


def _moe_out_into_kernel(prev_ref, *refs):
    del prev_ref
    _moe_out_kernel(*refs)


def _moe_out_call(prev_out, n_total, b0, yt, w, t, x1, mod_x, post2_g, wsg_bf, wsu_bf, wsd_bf, seq):
    n = t.shape[0]
    tiles_per_seq = seq // TOUT
    first_tile = b0 * tiles_per_seq
    const = lambda i: (0, 0)
    in_specs = [
        pl.BlockSpec((TOP_K, TOUT, ROW_WORDS), lambda i: (0, i, 0)),
        pl.BlockSpec((TOUT, TOP_K), lambda i: (i, 0)),
        pl.BlockSpec((TOUT, ROW_WORDS), lambda i: (i, 0)),
        pl.BlockSpec((TOUT, D_MODEL), lambda i: (i, 0)),
        pl.BlockSpec((None, 6, D_MODEL), lambda i: (b0 + i // tiles_per_seq, 0, 0)),
        pl.BlockSpec((1, D_MODEL), const),
        pl.BlockSpec((D_MODEL, D_EXPERT), const),
        pl.BlockSpec((D_MODEL, D_EXPERT), const),
        pl.BlockSpec((D_EXPERT, D_MODEL), const),
    ]
    args = (yt, w, t, x1, mod_x, post2_g, wsg_bf, wsu_bf, wsd_bf)
    aliased = prev_out is not None
    return pl.pallas_call(
        _moe_out_into_kernel if aliased else _moe_out_kernel,
        grid=(n // TOUT,),
        in_specs=([pl.BlockSpec(memory_space=pl.ANY)] if aliased else []) + in_specs,
        out_specs=[pl.BlockSpec((TOUT, D_MODEL), lambda i: (first_tile + i, 0)),
                   pl.BlockSpec((SUBLANES, LANES), const)],
        out_shape=[jax.ShapeDtypeStruct((n_total, D_MODEL), F32), jax.ShapeDtypeStruct((SUBLANES, LANES), F32)],
        input_output_aliases={0: 0} if aliased else {},
        compiler_params=pltpu.CompilerParams(dimension_semantics=("arbitrary",), vmem_limit_bytes=VMEM_LIMIT),
        name="moe_out",
    )(*(((prev_out,) if aliased else ()) + args))


def _moe_dispatch(after, hx2w, idx_t):
    n = hx2w.shape[0]
    n_slots = n * TOP_K + N_EXPERTS * ROW_TILE
    pos_t, plan = _positions_call(after, idx_t, n_slots // ROW_TILE)
    pos = pos_t.reshape(TOP_K, SC_WORKERS, n // (SC_WORKERS * SC_CHUNK), SC_CHUNK).transpose(1, 2, 0, 3)
    return _dispatch_call(hx2w, pos, n_slots), pos, plan


def kernel(x, c, ctx, c_ctx, w_mod, b_mod, pre1_g, post1_g, pre2_g, post2_g, w_in, w_dw, b_dw, ln_conv_g, ln_conv_b, w_conv_out, w_sc, b_sc, w_rg_a, b_rg_a, w_rg_x, b_rg_x, lru_lambda, w_rnn_out, w_out, w_router, router_bias, w_e_gate, w_e_up, w_e_down, w_s_gate, w_s_up, w_s_down):
    assert w_mod.shape[0] == 1, "single-layer block"
    bsz, seq, d = x.shape
    n = bsz * seq
    assert d == D_MODEL and seq % TM == 0 and seq % TOUT == 0 and bsz + 1 <= SUBLANES
    assert TOKEN_GROUPS == 2 and bsz % TOKEN_GROUPS == 0 and (n // TOKEN_GROUPS) % (SC_WORKERS * SC_CHUNK) == 0
    assert (n // TOKEN_GROUPS) % POS_TILE == 0

    cc = jnp.zeros((SUBLANES, D_MODEL), F32).at[:bsz].set(c).at[bsz].set(c_ctx)
    mod = _mod_call(cc, w_mod[0], b_mod)
    mod_x = mod[:bsz].reshape(bsz, 6, D_MODEL)
    mod_c = mod[bsz].reshape(6, D_MODEL)

    w_in_bf = w_in[0].astype(BF16)
    wa_bf = w_rg_a[0].astype(BF16)
    wx_bf = w_rg_x[0].astype(BF16)

    h0 = _ctx_call(ctx, mod_c, pre1_g, w_in_bf, w_sc[0], b_sc[0], wa_bf, b_rg_a[0], wx_bf, b_rg_x[0],
                   lru_lambda[0])

    wco_bf, wro_bf, wout_bf = w_conv_out[0].astype(BF16), w_rnn_out[0].astype(BF16), w_out[0].astype(BF16)
    wsg_bf, wsu_bf, wsd_bf = w_s_gate[0].astype(BF16), w_s_up[0].astype(BF16), w_s_down[0].astype(BF16)
    w_router_t, rbias = w_router[0].T, router_bias.reshape(N_EXPERTS, 1)
    gsz = bsz // TOKEN_GROUPS
    gn = gsz * seq
    none = jnp.zeros((TOP_K, LANES), jnp.int32)

    def fwd(g, after):
        return _mixer_fwd_call(g * gsz, gsz, after, x, mod_x, pre1_g, w_in_bf, w_dw[0], b_dw, ln_conv_g,
                               ln_conv_b, wco_bf, w_sc[0], b_sc[0], wa_bf, b_rg_a[0], wx_bf, b_rg_x[0],
                               lru_lambda[0], h0)

    def bwd(g, after, acts):
        gaya, gb, gbr, urnn, hf = acts
        x1, hx2w, idx_t, w_t = _mixer_bwd_call(
            g * gsz, gsz, after, urnn, hf, gbr, gaya, gb, x, mod_x, w_sc[0], b_sc[0], wa_bf, b_rg_a[0], wx_bf,
            b_rg_x[0], lru_lambda[0], h0, wro_bf, wout_bf, post1_g, pre2_g, w_router_t, rbias)
        return x1.reshape(gn, D_MODEL), hx2w.reshape(gn, ROW_WORDS), idx_t, w_t

    def experts(after, plan, xs):
        return _expert_gemm_call(after, plan, xs, w_e_gate[0], w_e_up[0], w_e_down[0])

    def finish(g, prev_out, yt, w_t, hx2w, x1):
        return _moe_out_call(prev_out, n, g * gsz, yt, w_t.T, hx2w, x1, mod_x, post2_g, wsg_bf, wsu_bf, wsd_bf, seq)

    x1_a, hx2w_a, idx_a, w_a = bwd(0, none, fwd(0, none))
    xs_a, pos_a, plan_a = _moe_dispatch(none, hx2w_a, idx_a)
    acts_b = fwd(1, idx_a)
    ys_a = experts(acts_b[4], plan_a, xs_a)
    yt_a = _collect_call(ys_a, pos_a, gn)
    x1_b, hx2w_b, idx_b, w_b = bwd(1, ys_a, acts_b)
    xs_b, pos_b, plan_b = _moe_dispatch(yt_a, hx2w_b, idx_b)
    out, done_a = finish(0, None, yt_a, w_a, hx2w_a, x1_a)
    ys_b = experts(done_a, plan_b, xs_b)
    yt_b = _collect_call(ys_b, pos_b, gn)
    out, _ = finish(1, out, yt_b, w_b, hx2w_b, x1_b)
    return out.reshape(bsz, seq, D_MODEL)
```

```python
import jax
import jax.numpy as jnp
from jax import lax
from jax.experimental import pallas as pl
from jax.experimental.pallas import tpu as pltpu
from jax.experimental.pallas import tpu_sc as plsc

F32 = jnp.float32
BF16 = jnp.bfloat16

D_MODEL = 1024
GRID_W = 64
CONV_WIDTH = 31
CONV_HALF = CONV_WIDTH // 2
SHORT_CONV = 4
RNN_HEADS = 4
RNN_BLOCK = D_MODEL // RNN_HEADS
LRU_C = 8.0
N_EXPERTS = 64
N_GROUPS = 8
GROUP_SIZE = N_EXPERTS // N_GROUPS
TOPK_GROUPS = 4
TOP_K = 8
D_EXPERT = 256
ROUTED_SCALE = 2.5
EPS = 1e-6

SUBLANES = 8
TM = 512
ROWS_PER_TILE = TM // GRID_W
PAD = 16
ROW_STRIDE = GRID_W + PAD
UPAD_ROWS = ROWS_PER_TILE * ROW_STRIDE + PAD
SEG = TM // SUBLANES
HALO = (SHORT_CONV - 1) * SUBLANES
CONV_LANES = 256
LANES = 128
ROW_WORDS = D_MODEL // 2
ROW_TILE = 1024
POS_TILE = 512
TOUT = 512
TOKEN_GROUPS = 2
V7X_SC_CORES = 2
V7X_SC_SUBCORES = 16
SC_WORKERS = V7X_SC_CORES * V7X_SC_SUBCORES
SC_CHUNK = 64
SC_BUFFERS = 2
VMEM_LIMIT = 58 * 1024 * 1024


def _sigmoid(x):
    return 0.5 * (jnp.tanh(0.5 * x) + 1.0)


def _silu(x):
    return x * _sigmoid(x)


def _gelu_tanh(x):
    return 0.5 * x * (1.0 + jnp.tanh(0.7978845608028654 * (x + 0.044715 * (x * x * x))))


def _rms_norm(x, g):
    return x * lax.rsqrt(jnp.mean(x * x, axis=-1, keepdims=True) + EPS) * g


def _dot(a, b):
    return jnp.dot(a, b, preferred_element_type=F32)


def _pack_bf16_pairs(x):
    half = x.shape[-1] // 2
    lo = lax.bitcast_convert_type(x[:, :half].astype(BF16).astype(F32), jnp.uint32)
    hi = lax.bitcast_convert_type(x[:, half:].astype(BF16).astype(F32), jnp.uint32)
    return lax.bitcast_convert_type(hi | (lo >> 16), jnp.int32)


def _pack_bf16_pairs_native(x):
    half = x.shape[-1] // 2
    packed = pltpu.pack_elementwise([x[:, :half], x[:, half:]], packed_dtype=BF16)
    return lax.bitcast_convert_type(packed, jnp.int32)


def _unpack_bf16_pairs(words):
    u = lax.bitcast_convert_type(words, jnp.uint32)
    return (lax.bitcast_convert_type(u << 16, F32),
            lax.bitcast_convert_type(u & jnp.uint32(0xFFFF0000), F32))


def _log_decay(lam):
    return LRU_C * (jnp.minimum(lam, 0.0) - jnp.log1p(jnp.exp(-jnp.abs(lam))))


def _rglru_coeffs(v, wa_ref, ba, wx_ref, bx, c_lam, a_ref, b_ref):
    vb = v.astype(BF16)
    for h in range(RNN_HEADS):
        cs = slice(h * RNN_BLOCK, (h + 1) * RNN_BLOCK)
        vh = vb[:, cs]
        r = _sigmoid(_dot(vh, wa_ref[h]) + ba[:, cs])
        i = _sigmoid(_dot(vh, wx_ref[h]) + bx[:, cs])
        log_a = c_lam[:, cs] * r
        a = jnp.exp(log_a)
        a_ref[:, cs] = a
        b_ref[:, cs] = jnp.sqrt(jnp.tanh(-log_a) * (1.0 + a * a)) * (i * v[:, cs])


def _scan_tile(a_ref, b_ref, carry, n_rows, reverse):
    row = lax.broadcasted_iota(jnp.int32, (SUBLANES, D_MODEL), 0)
    n_groups = n_rows // SUBLANES

    def body(j, carry):
        g = (n_groups - 1 - j) if reverse else j
        off = pl.multiple_of(g * SUBLANES, SUBLANES)
        a = a_ref[pl.ds(off, SUBLANES), :]
        b = b_ref[pl.ds(off, SUBLANES), :]
        for s in (1, 2, 4):
            keep = (row < SUBLANES - s) if reverse else (row >= s)
            shift = (SUBLANES - s) if reverse else s
            a_sh = jnp.where(keep, pltpu.roll(a, shift, 0), 1.0)
            b_sh = jnp.where(keep, pltpu.roll(b, shift, 0), 0.0)
            b = a * b_sh + b
            a = a * a_sh
        h = a * carry + b
        b_ref[pl.ds(off, SUBLANES), :] = h
        last = h[0:1, :] if reverse else h[SUBLANES - 1:SUBLANES, :]
        return jnp.broadcast_to(last, (SUBLANES, D_MODEL))

    return lax.fori_loop(0, n_groups, body, carry, unroll=2)


def _segment_perm(to_natural):
    r = lax.broadcasted_iota(jnp.int32, (TM, TM), 0)
    c = lax.broadcasted_iota(jnp.int32, (TM, TM), 1)
    if to_natural:
        src = (r & (SEG - 1)) * SUBLANES + lax.shift_right_logical(r, SEG.bit_length() - 1)
    else:
        src = (r & (SUBLANES - 1)) * SEG + lax.shift_right_logical(r, SUBLANES.bit_length() - 1)
    return jnp.where(c == src, 1.0, 0.0).astype(BF16)


def _scan_segments(a_ref, b_ref, carry, reverse):
    row = lax.broadcasted_iota(jnp.int32, (SUBLANES, D_MODEL), 0)

    def body(s, hp):
        h, prod = hp
        off = pl.multiple_of(((SEG - 1 - s) if reverse else s) * SUBLANES, SUBLANES)
        a = a_ref[pl.ds(off, SUBLANES), :]
        h = a * h + b_ref[pl.ds(off, SUBLANES), :]
        prod = a * prod
        b_ref[pl.ds(off, SUBLANES), :] = h
        a_ref[pl.ds(off, SUBLANES), :] = prod
        return h, prod

    b, a = lax.fori_loop(0, SEG, body, (jnp.zeros((SUBLANES, D_MODEL), F32), jnp.ones((SUBLANES, D_MODEL), F32)),
                         unroll=4)
    for s in (1, 2, 4):
        keep = (row < SUBLANES - s) if reverse else (row >= s)
        shift = (SUBLANES - s) if reverse else s
        a_sh = jnp.where(keep, pltpu.roll(a, shift, 0), 1.0)
        b_sh = jnp.where(keep, pltpu.roll(b, shift, 0), 0.0)
        b = a * b_sh + b
        a = a * a_sh
    leave = a * carry + b
    if reverse:
        enter = jnp.where(row < SUBLANES - 1, pltpu.roll(leave, SUBLANES - 1, 0), carry)
        last = leave[0:1, :]
    else:
        enter = jnp.where(row >= 1, pltpu.roll(leave, 1, 0), carry)
        last = leave[SUBLANES - 1:SUBLANES, :]
    h = b_ref[...].reshape(SEG, SUBLANES, D_MODEL) + a_ref[...].reshape(SEG, SUBLANES, D_MODEL) * enter[None]
    b_ref[...] = h.reshape(TM, D_MODEL)
    return jnp.broadcast_to(last, (SUBLANES, D_MODEL))


def _short_conv_segments(uext_ref, wsc_ref, bsc_ref):
    v = bsc_ref[...] + wsc_ref[0:1, :] * uext_ref[0:TM, :]
    for k in range(1, SHORT_CONV):
        v = v + wsc_ref[k:k + 1, :] * uext_ref[k * SUBLANES:k * SUBLANES + TM, :]
    return v


def _mod_kernel(c_ref, w_ref, b_ref, o_ref):
    o_ref[...] = jnp.dot(_silu(c_ref[...]), w_ref[...], preferred_element_type=F32,
                         precision=lax.Precision.HIGHEST) + b_ref[...]


def _mod_call(cc, w_mod, b_mod):
    n_mod = w_mod.shape[1]
    return pl.pallas_call(
        _mod_kernel,
        grid=(n_mod // D_MODEL,),
        in_specs=[
            pl.BlockSpec((SUBLANES, D_MODEL), lambda j: (0, 0)),
            pl.BlockSpec((D_MODEL, D_MODEL), lambda j: (0, j)),
            pl.BlockSpec((1, D_MODEL), lambda j: (0, j)),
        ],
        out_specs=pl.BlockSpec((SUBLANES, D_MODEL), lambda j: (0, j)),
        out_shape=jax.ShapeDtypeStruct((SUBLANES, n_mod), F32),
        name="mod",
    )(cc, w_mod, b_mod)


def _ctx_kernel(ctx_ref, mod_ref, g_ref, w_ref, wsc_ref, bsc_ref, wa_ref, ba_ref, wx_ref, bx_ref,
                lam_ref, o_ref, uext_ref, a_ref, b_ref):
    n = ctx_ref.shape[0]
    hc = _rms_norm(ctx_ref[...], g_ref[...]) * (1.0 + mod_ref[1:2, :]) + mod_ref[0:1, :]
    u = _dot(hc.astype(BF16), w_ref[...])
    zeros8 = jnp.zeros((SUBLANES, D_MODEL), F32)
    uext_ref[0:SUBLANES, :] = zeros8
    uext_ref[SUBLANES:SUBLANES + n, :] = u
    uext_ref[SUBLANES + n:2 * SUBLANES + n, :] = zeros8
    for d in range(2):
        v = jnp.broadcast_to(bsc_ref[d:d + 1, :], (n, D_MODEL))
        for k in range(SHORT_CONV):
            start = SUBLANES + k - (SHORT_CONV - 1) * (1 - d)
            v = v + wsc_ref[d, k:k + 1, :] * uext_ref[start:start + n, :]
        _rglru_coeffs(v, wa_ref.at[d], ba_ref[d:d + 1, :], wx_ref.at[d], bx_ref[d:d + 1, :],
                      _log_decay(lam_ref[d:d + 1, :]), a_ref, b_ref)
        final = _scan_tile(a_ref, b_ref, zeros8, n, reverse=(d == 1))
        o_ref[d:d + 1, :] = final[0:1, :]


def _ctx_call(ctx, mod_c, pre1_g, w_in_bf, w_sc, b_sc, wa_bf, b_rg_a, wx_bf, b_rg_x, lam):
    bsz, n, _ = ctx.shape
    const2 = lambda b: (0, 0)
    const3 = lambda b: (0, 0, 0)
    const4 = lambda b: (0, 0, 0, 0)
    return pl.pallas_call(
        _ctx_kernel,
        grid=(bsz,),
        in_specs=[
            pl.BlockSpec((None, n, D_MODEL), lambda b: (b, 0, 0)),
            pl.BlockSpec((6, D_MODEL), const2),
            pl.BlockSpec((1, D_MODEL), const2),
            pl.BlockSpec((D_MODEL, D_MODEL), lambda b: (0, 2)),
            pl.BlockSpec((2, SHORT_CONV, D_MODEL), const3),
            pl.BlockSpec((2, D_MODEL), const2),
            pl.BlockSpec((2, RNN_HEADS, RNN_BLOCK, RNN_BLOCK), const4),
            pl.BlockSpec((2, D_MODEL), const2),
            pl.BlockSpec((2, RNN_HEADS, RNN_BLOCK, RNN_BLOCK), const4),
            pl.BlockSpec((2, D_MODEL), const2),
            pl.BlockSpec((2, D_MODEL), const2),
        ],
        out_specs=pl.BlockSpec((None, 2, D_MODEL), lambda b: (b, 0, 0)),
        out_shape=jax.ShapeDtypeStruct((bsz, 2, D_MODEL), F32),
        scratch_shapes=[
            pltpu.VMEM((n + 2 * SUBLANES, D_MODEL), F32),
            pltpu.VMEM((n, D_MODEL), F32),
            pltpu.VMEM((n, D_MODEL), F32),
        ],
        name="ctx",
    )(ctx, mod_c, pre1_g, w_in_bf, w_sc, b_sc, wa_bf, b_rg_a, wx_bf, b_rg_x, lam)


def _mixer_fwd_kernel(after_ref, x_ref, mod_ref, g_ref, win_ref, wdw_ref, bdw_ref, lng_ref, lnb_ref, wco_ref,
                      wsc_ref, bsc_ref, wa_ref, ba_ref, wx_ref, bx_ref, lam_ref, h0_ref,
                      gaya_ref, gb_ref, gbr_ref, urnn_ref, hf_ref,
                      upad_ref, ush_ref, wb_ref, cv_ref, uext_ref, a_ref, b_ref, carry_ref, halo_ref, perm_ref):
    del after_ref
    j = pl.program_id(1)

    @pl.when(j == 0)
    def _():
        carry_ref[...] = jnp.broadcast_to(h0_ref[0:1, :], (SUBLANES, D_MODEL))
        halo_ref[...] = jnp.zeros((SUBLANES, D_MODEL), F32)
        perm_ref[...] = _segment_perm(to_natural=False)
        zeros_pad = jnp.zeros((PAD, D_MODEL), F32)
        for r in range(ROWS_PER_TILE + 1):
            upad_ref[r * ROW_STRIDE:r * ROW_STRIDE + PAD, :] = zeros_pad
        for k in range(CONV_WIDTH):
            wb_ref[k] = jnp.broadcast_to(wdw_ref[k:k + 1, :], (SUBLANES, D_MODEL))

    hx = (_rms_norm(x_ref[...], g_ref[...]) * (1.0 + mod_ref[1:2, :]) + mod_ref[0:1, :]).astype(BF16)

    for c in range(D_MODEL // CONV_LANES):
        cs = slice(c * CONV_LANES, (c + 1) * CONV_LANES)
        u = (_dot(hx, win_ref[:, c * CONV_LANES:(c + 1) * CONV_LANES])
             * _sigmoid(_dot(hx, win_ref[:, D_MODEL + c * CONV_LANES:D_MODEL + (c + 1) * CONV_LANES])))
        for r in range(ROWS_PER_TILE):
            upad_ref[PAD + r * ROW_STRIDE:PAD + r * ROW_STRIDE + GRID_W, cs] = u[r * GRID_W:(r + 1) * GRID_W, :]
    vregs_per_row = GRID_W // SUBLANES
    hxp = _dot(perm_ref[...], hx).astype(BF16)
    for c in range(D_MODEL // CONV_LANES):
        for cb in range(D_MODEL // CONV_LANES):
            cols = slice(cb * CONV_LANES, (cb + 1) * CONV_LANES)
            w0 = (5, 3, 4, 2)[c] * D_MODEL + cb * CONV_LANES
            y = _dot(hxp, win_ref[:, w0:w0 + CONV_LANES])
            if c == 0:
                gb_ref[:, cols] = _sigmoid(y).astype(BF16)
            elif c == 1:
                gbr_ref[:, cols] = _gelu_tanh(y).astype(BF16)
            elif c == 2:
                a_ref[:, cols] = _sigmoid(y)
            else:
                urnn_ref[:, cols] = y.astype(BF16)
                uext_ref[HALO:HALO + TM, cols] = y
        cs = slice(c * CONV_LANES, (c + 1) * CONV_LANES)
        xpad = upad_ref[:, cs]
        for s in range(1, SUBLANES):
            ush_ref[s - 1] = pltpu.roll(xpad, UPAD_ROWS - s, 0)
        for r in range(ROWS_PER_TILE):
            acc = jnp.broadcast_to(bdw_ref[:, cs].reshape(1, 1, CONV_LANES), (vregs_per_row, SUBLANES, CONV_LANES))
            for k in range(CONV_WIDTH):
                q, s = divmod(r * ROW_STRIDE + PAD - CONV_HALF + k, SUBLANES)
                rows = slice(q * SUBLANES, q * SUBLANES + GRID_W)
                win = upad_ref[rows, cs] if s == 0 else ush_ref[s - 1, rows, :]
                acc = acc + wb_ref[k, :, cs] * win.reshape(vregs_per_row, SUBLANES, CONV_LANES)
            cv_ref[r * GRID_W:(r + 1) * GRID_W, cs] = acc.reshape(GRID_W, CONV_LANES)
    cv = cv_ref[...]
    cvc = cv - jnp.mean(cv, axis=-1, keepdims=True)
    cvn = cvc * lax.rsqrt(jnp.mean(cvc * cvc, axis=-1, keepdims=True) + EPS) * lng_ref[...] + lnb_ref[...]
    y_a = _dot(_dot(perm_ref[...], _silu(cvn).astype(BF16)).astype(BF16), wco_ref[...])

    gaya_ref[...] = (a_ref[...] * y_a).astype(BF16)

    row8 = lax.broadcasted_iota(jnp.int32, (SUBLANES, D_MODEL), 0)
    for jj in range(SHORT_CONV - 1):
        late = uext_ref[HALO + (SEG - (SHORT_CONV - 1) + jj) * SUBLANES:HALO + (SEG - (SHORT_CONV - 1) + jj + 1) * SUBLANES, :]
        prev_tile = jnp.broadcast_to(halo_ref[jj:jj + 1, :], (SUBLANES, D_MODEL))
        uext_ref[jj * SUBLANES:(jj + 1) * SUBLANES, :] = jnp.where(row8 == 0, prev_tile, pltpu.roll(late, 1, 0))
        halo_ref[jj:jj + 1, :] = late[SUBLANES - 1:SUBLANES, :]
    v = _short_conv_segments(uext_ref, wsc_ref, bsc_ref)
    _rglru_coeffs(v, wa_ref, ba_ref[...], wx_ref, bx_ref[...], _log_decay(lam_ref[...]), a_ref, b_ref)
    carry_ref[...] = _scan_segments(a_ref, b_ref, carry_ref[...], reverse=False)
    hf_ref[...] = b_ref[...].astype(BF16)


def _resident(shape):
    nd = len(shape)
    return pl.BlockSpec(shape, lambda b, j: (0,) * nd, pipeline_mode=pl.Buffered(1))


def _mixer_fwd_call(b0, bsz, after, x, mod_x, pre1_g, w_in_bf, w_dw, b_dw, ln_g, ln_b, wco_bf,
                    w_sc, b_sc, wa_bf, b_rg_a, wx_bf, b_rg_x, lam, h0):
    seq = x.shape[1]
    nt = seq // TM
    tile = pl.BlockSpec((None, TM, D_MODEL), lambda b, j: (b, j, 0))
    act = jax.ShapeDtypeStruct((bsz, seq, D_MODEL), BF16)
    head_w = pl.BlockSpec((None, RNN_HEADS, RNN_BLOCK, RNN_BLOCK), lambda b, j: (0, 0, 0, 0),
                          pipeline_mode=pl.Buffered(1))
    dir_row = pl.BlockSpec((None, 1, D_MODEL), lambda b, j: (0, 0, 0), pipeline_mode=pl.Buffered(1))
    return pl.pallas_call(
        _mixer_fwd_kernel,
        grid=(bsz, nt),
        in_specs=[
            pl.BlockSpec(memory_space=pl.ANY),
            pl.BlockSpec((None, TM, D_MODEL), lambda b, j: (b0 + b, j, 0)),
            pl.BlockSpec((None, 6, D_MODEL), lambda b, j: (b0 + b, 0, 0)),
            _resident((1, D_MODEL)),
            _resident((D_MODEL, 6 * D_MODEL)),
            _resident((CONV_WIDTH, D_MODEL)),
            _resident((1, D_MODEL)),
            _resident((1, D_MODEL)),
            _resident((1, D_MODEL)),
            _resident((D_MODEL, D_MODEL)),
            pl.BlockSpec((None, SHORT_CONV, D_MODEL), lambda b, j: (0, 0, 0), pipeline_mode=pl.Buffered(1)),
            dir_row, head_w, dir_row, head_w, dir_row, dir_row,
            pl.BlockSpec((None, 2, D_MODEL), lambda b, j: (b0 + b, 0, 0)),
        ],
        out_specs=[tile] * 5,
        out_shape=[act] * 5,
        scratch_shapes=[
            pltpu.VMEM((UPAD_ROWS, D_MODEL), F32),
            pltpu.VMEM((SUBLANES - 1, UPAD_ROWS, CONV_LANES), F32),
            pltpu.VMEM((CONV_WIDTH, SUBLANES, D_MODEL), F32),
            pltpu.VMEM((TM, D_MODEL), F32),
            pltpu.VMEM((TM + HALO, D_MODEL), F32),
            pltpu.VMEM((TM, D_MODEL), F32),
            pltpu.VMEM((TM, D_MODEL), F32),
            pltpu.VMEM((SUBLANES, D_MODEL), F32),
            pltpu.VMEM((SUBLANES, D_MODEL), F32),
            pltpu.VMEM((TM, TM), BF16),
        ],
        compiler_params=pltpu.CompilerParams(
            dimension_semantics=("arbitrary", "arbitrary"), vmem_limit_bytes=VMEM_LIMIT),
        name="mixer_fwd",
    )(after, x, mod_x, pre1_g, w_in_bf, w_dw, b_dw, ln_g, ln_b, wco_bf,
      w_sc, b_sc.reshape(2, 1, D_MODEL), wa_bf, b_rg_a.reshape(2, 1, D_MODEL), wx_bf,
      b_rg_x.reshape(2, 1, D_MODEL), lam.reshape(2, 1, D_MODEL), h0)


def _route(logits_t, bias):
    t = logits_t.shape[1]
    scores = _sigmoid(logits_t)
    sel = scores + bias
    neg_inf = jnp.float32(-jnp.inf)

    sel3 = sel.reshape(N_GROUPS, GROUP_SIZE, t)
    within = lax.broadcasted_iota(jnp.int32, sel3.shape, 1)
    m1 = jnp.max(sel3, axis=1, keepdims=True)
    first = jnp.min(jnp.where(sel3 == m1, within, GROUP_SIZE), axis=1, keepdims=True)
    m2 = jnp.max(jnp.where(within == first, neg_inf, sel3), axis=1, keepdims=True)
    gscore = (m1 + m2).reshape(N_GROUPS, t)

    gidx = lax.broadcasted_iota(jnp.int32, gscore.shape, 0)
    rank = jnp.zeros(gscore.shape, jnp.int32)
    for g in range(N_GROUPS):
        other = gscore[g:g + 1, :]
        beats = jnp.where(other > gscore, 1, jnp.where((other == gscore) & (gidx > g), 1, 0))
        rank = rank + beats
    gkeep = (rank < TOPK_GROUPS).reshape(N_GROUPS, 1, t)
    masked = jnp.where(gkeep, sel3, neg_inf).reshape(N_EXPERTS, t)

    eidx = lax.broadcasted_iota(jnp.int32, masked.shape, 0)
    picks, weights = [], []
    for _ in range(TOP_K):
        m = jnp.max(masked, axis=0, keepdims=True)
        first = jnp.min(jnp.where(masked == m, eidx, N_EXPERTS), axis=0, keepdims=True)
        pick = eidx == first
        picks.append(first)
        weights.append(jnp.sum(jnp.where(pick, scores, 0.0), axis=0, keepdims=True))
        masked = jnp.where(pick, neg_inf, masked)
    idx = jnp.concatenate(picks, axis=0)
    w = jnp.concatenate(weights, axis=0)
    return idx, ROUTED_SCALE * w / jnp.sum(w, axis=0, keepdims=True)


def _mixer_bwd_kernel(after_ref, urnn_ref, hf_ref, gbr_ref, gaya_ref, gb_ref, x_ref, mod_ref,
                      wsc_ref, bsc_ref, wa_ref, ba_ref, wx_ref, bx_ref, lam_ref, h0_ref,
                      wro_ref, wout_ref, post1_ref, pre2_ref, wrt_ref, rbias_ref,
                      x1_ref, hx2w_ref, idx_ref, w_ref,
                      uext_ref, a_ref, b_ref, carry_ref, halo_ref, perm_ref):
    del after_ref
    j = pl.program_id(1)

    @pl.when(j == 0)
    def _():
        carry_ref[...] = jnp.broadcast_to(h0_ref[1:2, :], (SUBLANES, D_MODEL))
        halo_ref[...] = jnp.zeros((SUBLANES, D_MODEL), F32)
        perm_ref[...] = _segment_perm(to_natural=True)

    uext_ref[0:TM, :] = urnn_ref[...].astype(F32)
    row8 = lax.broadcasted_iota(jnp.int32, (SUBLANES, D_MODEL), 0)
    for jj in range(SHORT_CONV - 1):
        early = uext_ref[jj * SUBLANES:(jj + 1) * SUBLANES, :]
        next_tile = jnp.broadcast_to(halo_ref[jj:jj + 1, :], (SUBLANES, D_MODEL))
        uext_ref[TM + jj * SUBLANES:TM + (jj + 1) * SUBLANES, :] = jnp.where(
            row8 == SUBLANES - 1, next_tile, pltpu.roll(early, SUBLANES - 1, 0))
        halo_ref[jj:jj + 1, :] = early[0:1, :]
    v = _short_conv_segments(uext_ref, wsc_ref, bsc_ref)
    _rglru_coeffs(v, wa_ref, ba_ref[...], wx_ref, bx_ref[...], _log_decay(lam_ref[...]), a_ref, b_ref)
    carry_ref[...] = _scan_segments(a_ref, b_ref, carry_ref[...], reverse=True)

    h_sum = hf_ref[...].astype(F32) + b_ref[...]
    y_b = _dot((gbr_ref[...].astype(F32) * h_sum).astype(BF16), wro_ref[...])
    mix = gaya_ref[...].astype(F32) + gb_ref[...].astype(F32) * y_b
    out = _dot(_dot(perm_ref[...], mix.astype(BF16)).astype(BF16), wout_ref[...])
    x1 = x_ref[...] + mod_ref[2:3, :] * _rms_norm(out, post1_ref[...])
    x1_ref[...] = x1

    hx2 = _rms_norm(x1, pre2_ref[...]) * (1.0 + mod_ref[4:5, :]) + mod_ref[3:4, :]
    hx2w_ref[...] = _pack_bf16_pairs(hx2)
    logits_t = lax.dot_general(wrt_ref[...], hx2, (((1,), (1,)), ((), ())),
                               preferred_element_type=F32, precision=lax.Precision.HIGHEST)
    idx, w = _route(logits_t, rbias_ref[...])
    idx_ref[...] = idx
    w_ref[...] = w


def _mixer_bwd_call(b0, bsz, after, urnn, hf, gbr, gaya, gb, x, mod_x, w_sc, b_sc, wa_bf, b_rg_a, wx_bf, b_rg_x, lam, h0,
                    wro_bf, wout_bf, post1_g, pre2_g, w_router_t, router_bias):
    seq = x.shape[1]
    nt = seq // TM
    rev = lambda b, j: (b, nt - 1 - j, 0)
    tile = pl.BlockSpec((None, TM, D_MODEL), rev)
    head_w = pl.BlockSpec((None, RNN_HEADS, RNN_BLOCK, RNN_BLOCK), lambda b, j: (1, 0, 0, 0),
                          pipeline_mode=pl.Buffered(1))
    dir_row = pl.BlockSpec((None, 1, D_MODEL), lambda b, j: (1, 0, 0), pipeline_mode=pl.Buffered(1))
    return pl.pallas_call(
        _mixer_bwd_kernel,
        grid=(bsz, nt),
        in_specs=[
            pl.BlockSpec(memory_space=pl.ANY),
            tile, tile, tile, tile, tile,
            pl.BlockSpec((None, TM, D_MODEL), lambda b, j: (b0 + b, nt - 1 - j, 0)),
            pl.BlockSpec((None, 6, D_MODEL), lambda b, j: (b0 + b, 0, 0)),
            pl.BlockSpec((None, SHORT_CONV, D_MODEL), lambda b, j: (1, 0, 0), pipeline_mode=pl.Buffered(1)),
            dir_row, head_w, dir_row, head_w, dir_row, dir_row,
            pl.BlockSpec((None, 2, D_MODEL), lambda b, j: (b0 + b, 0, 0)),
            _resident((D_MODEL, D_MODEL)),
            _resident((D_MODEL, D_MODEL)),
            _resident((1, D_MODEL)),
            _resident((1, D_MODEL)),
            _resident((N_EXPERTS, D_MODEL)),
            _resident((N_EXPERTS, 1)),
        ],
        out_specs=[
            pl.BlockSpec((None, TM, D_MODEL), rev),
            pl.BlockSpec((None, TM, ROW_WORDS), rev),
            pl.BlockSpec((TOP_K, TM), lambda b, j: (0, b * nt + nt - 1 - j)),
            pl.BlockSpec((TOP_K, TM), lambda b, j: (0, b * nt + nt - 1 - j)),
        ],
        out_shape=[
            jax.ShapeDtypeStruct((bsz, seq, D_MODEL), F32),
            jax.ShapeDtypeStruct((bsz, seq, ROW_WORDS), jnp.int32),
            jax.ShapeDtypeStruct((TOP_K, bsz * seq), jnp.int32),
            jax.ShapeDtypeStruct((TOP_K, bsz * seq), F32),
        ],
        scratch_shapes=[
            pltpu.VMEM((TM + HALO, D_MODEL), F32),
            pltpu.VMEM((TM, D_MODEL), F32),
            pltpu.VMEM((TM, D_MODEL), F32),
            pltpu.VMEM((SUBLANES, D_MODEL), F32),
            pltpu.VMEM((SUBLANES, D_MODEL), F32),
            pltpu.VMEM((TM, TM), BF16),
        ],
        compiler_params=pltpu.CompilerParams(
            dimension_semantics=("arbitrary", "arbitrary"), vmem_limit_bytes=VMEM_LIMIT),
        name="mixer_bwd",
    )(after, urnn, hf, gbr, gaya, gb, x, mod_x, w_sc, b_sc.reshape(2, 1, D_MODEL), wa_bf,
      b_rg_a.reshape(2, 1, D_MODEL), wx_bf, b_rg_x.reshape(2, 1, D_MODEL), lam.reshape(2, 1, D_MODEL), h0,
      wro_bf, wout_bf, post1_g, pre2_g, w_router_t, router_bias)


def _positions_kernel(after_ref, idx_ref, pos_ref, plan_ref):
    del after_ref
    n = idx_ref.shape[1]
    n_tiles = n // POS_TILE
    eidx = lax.broadcasted_iota(jnp.int32, (N_EXPERTS, POS_TILE), 0)

    def chosen(t):
        idx = idx_ref[:, pl.ds(pl.multiple_of(t * POS_TILE, POS_TILE), POS_TILE)]
        ch = jnp.zeros((N_EXPERTS, POS_TILE), F32)
        for k in range(TOP_K):
            ch = ch + jnp.where(eidx == idx[k:k + 1, :], 1.0, 0.0)
        return idx, ch

    def count_body(t, cnt):
        return cnt + jnp.sum(chosen(t)[1], axis=1, keepdims=True)

    cnt = lax.fori_loop(0, n_tiles, count_body, jnp.zeros((N_EXPERTS, 1), F32))
    padded = jnp.ceil(cnt * (1.0 / ROW_TILE)) * ROW_TILE
    r = lax.broadcasted_iota(jnp.int32, (N_EXPERTS, N_EXPERTS), 0)
    c = lax.broadcasted_iota(jnp.int32, (N_EXPERTS, N_EXPERTS), 1)
    off = jnp.dot(jnp.where(c < r, 1.0, 0.0), jnp.broadcast_to(padded, (N_EXPERTS, LANES)),
                  preferred_element_type=F32, precision=lax.Precision.HIGHEST)[:, 0:1]
    end = off + padded

    n_map = plan_ref.shape[1]
    tstart = lax.broadcasted_iota(jnp.int32, (N_EXPERTS, n_map), 1).astype(F32) * ROW_TILE
    te = jnp.minimum(jnp.sum(jnp.where(end <= tstart, 1, 0), axis=0, keepdims=True), N_EXPERTS - 1)
    emap = lax.broadcasted_iota(jnp.int32, (N_EXPERTS, n_map), 0)
    live_end = jnp.sum(jnp.where(emap == te, off + cnt, 0.0), axis=0, keepdims=True)
    total = jnp.sum(padded, axis=0, keepdims=True)
    plan_ref[0:1, :] = te
    plan_ref[1:2, :] = jnp.clip(live_end - tstart[0:1, :], 0.0, ROW_TILE).astype(jnp.int32)
    plan_ref[2:3, :] = jnp.broadcast_to(total * (1.0 / ROW_TILE), (1, n_map)).astype(jnp.int32)

    row = lax.broadcasted_iota(jnp.int32, (POS_TILE, POS_TILE), 0)
    col = lax.broadcasted_iota(jnp.int32, (POS_TILE, POS_TILE), 1)
    before = jnp.where(row < col, 1.0, 0.0).astype(BF16)

    def pos_body(t, carry):
        idx, ch = chosen(t)
        base = _dot(ch.astype(BF16), before) + (carry + off)
        rows = [jnp.sum(jnp.where(eidx == idx[k:k + 1, :], base, 0.0), axis=0, keepdims=True)
                for k in range(TOP_K)]
        pos_ref[:, pl.ds(pl.multiple_of(t * POS_TILE, POS_TILE), POS_TILE)] = (
            jnp.concatenate(rows, axis=0).astype(jnp.int32))
        return carry + jnp.sum(ch, axis=1, keepdims=True)

    lax.fori_loop(0, n_tiles, pos_body, jnp.zeros((N_EXPERTS, 1), F32))


def _positions_call(after, idx_t, n_row_tiles):
    n = idx_t.shape[1]
    n_map = -(-n_row_tiles // LANES) * LANES
    return pl.pallas_call(
        _positions_kernel,
        in_specs=[pl.BlockSpec(memory_space=pl.ANY), pl.BlockSpec(memory_space=pltpu.VMEM)],
        out_shape=[
            jax.ShapeDtypeStruct((TOP_K, n), jnp.int32),
            jax.ShapeDtypeStruct((3, n_map), jnp.int32),
        ],
        name="moe_positions",
    )(after, idx_t)


def _sc_mesh():
    return plsc.VectorSubcoreMesh(core_axis_name="c", subcore_axis_name="s",
                                  num_cores=V7X_SC_CORES, num_subcores=V7X_SC_SUBCORES)


def _sc_worker():
    return lax.axis_index("s") * V7X_SC_CORES + lax.axis_index("c")


def _dispatch_call(rows, pos, n_slots):
    n = rows.shape[0]
    tok_w = n // SC_WORKERS
    n_items = tok_w // SC_CHUNK

    def body(rows_hbm, pos_hbm, xs_hbm, idx_v, rows_v, lsem, ssem):
        wid = _sc_worker()
        pltpu.sync_copy(pos_hbm.at[wid], idx_v)
        base = wid * tok_w

        def load(i):
            b = i % SC_BUFFERS
            return pltpu.async_copy(rows_hbm.at[pl.ds(base + i * SC_CHUNK, SC_CHUNK)], rows_v.at[b], lsem.at[b])

        def scatter(i):
            b = i % SC_BUFFERS
            return [pltpu.async_copy(rows_v.at[b], xs_hbm.at[idx_v.at[i, k]], ssem.at[b]) for k in range(TOP_K)]

        loads = {i: load(i) for i in range(SC_BUFFERS - 1)}
        scat = {}
        for i in range(n_items):
            loads[i].wait()
            scat[i] = scatter(i)
            if i >= 1:
                for cp in scat[i - 1]:
                    cp.wait()
            if i + SC_BUFFERS - 1 < n_items:
                loads[i + SC_BUFFERS - 1] = load(i + SC_BUFFERS - 1)
        for cp in scat[n_items - 1]:
            cp.wait()

    return pl.kernel(
        body, mesh=_sc_mesh(),
        out_type=jax.ShapeDtypeStruct((n_slots, ROW_WORDS), jnp.int32),
        scratch_types=[pltpu.VMEM((n_items, TOP_K, SC_CHUNK), jnp.int32),
                       pltpu.VMEM((SC_BUFFERS, SC_CHUNK, ROW_WORDS), jnp.int32),
                       pltpu.SemaphoreType.DMA((SC_BUFFERS,)), pltpu.SemaphoreType.DMA((SC_BUFFERS,))],
        compiler_params=pltpu.CompilerParams(use_tc_tiling_on_sc=True),
        name="moe_dispatch",
    )(rows, pos)


def _collect_call(ys, pos, n):
    tok_w = n // SC_WORKERS
    n_chunks = tok_w // SC_CHUNK
    items = [(c, k) for c in range(n_chunks) for k in range(TOP_K)]

    def body(ys_hbm, pos_hbm, yt_hbm, idx_v, rows_v, gsem, wsem):
        wid = _sc_worker()
        pltpu.sync_copy(pos_hbm.at[wid], idx_v)
        base = wid * tok_w

        def gather(j):
            c, k = items[j]
            b = j % SC_BUFFERS
            return pltpu.async_copy(ys_hbm.at[idx_v.at[c, k]], rows_v.at[b], gsem.at[b])

        def write(j):
            c, k = items[j]
            b = j % SC_BUFFERS
            return pltpu.async_copy(rows_v.at[b], yt_hbm.at[k, pl.ds(base + c * SC_CHUNK, SC_CHUNK)], wsem.at[b])

        g = {j: gather(j) for j in range(SC_BUFFERS - 1)}
        w = {}
        for j in range(len(items)):
            g[j].wait()
            w[j] = write(j)
            if j >= 1:
                w[j - 1].wait()
            if j + SC_BUFFERS - 1 < len(items):
                g[j + SC_BUFFERS - 1] = gather(j + SC_BUFFERS - 1)
        w[len(items) - 1].wait()

    return pl.kernel(
        body, mesh=_sc_mesh(),
        out_type=jax.ShapeDtypeStruct((TOP_K, n, ROW_WORDS), jnp.int32),
        scratch_types=[pltpu.VMEM((n_chunks, TOP_K, SC_CHUNK), jnp.int32),
                       pltpu.VMEM((SC_BUFFERS, SC_CHUNK, ROW_WORDS), jnp.int32),
                       pltpu.SemaphoreType.DMA((SC_BUFFERS,)), pltpu.SemaphoreType.DMA((SC_BUFFERS,))],
        compiler_params=pltpu.CompilerParams(use_tc_tiling_on_sc=True),
        name="moe_collect",
    )(ys, pos)


def _expert_gemm_kernel(plan_ref, after_ref, xs_ref, wg_ref, wu_ref, wd_ref, ys_ref, wgu_scr, wd_scr):
    del after_ref
    i = pl.program_id(0)
    e = plan_ref[0, i]
    prev = plan_ref[0, jnp.maximum(i - 1, 0)]
    live = plan_ref[1, i]

    @pl.when((i == 0) | (e != prev))
    def _():
        wgu_scr[:, 0:D_EXPERT] = wg_ref[...].astype(BF16)
        wgu_scr[:, D_EXPERT:2 * D_EXPERT] = wu_ref[...].astype(BF16)
        wd_scr[...] = wd_ref[...].astype(BF16)

    def swiglu_rows(n_rows):
        lo, hi = _unpack_bf16_pairs(xs_ref[0:n_rows, :])
        gu = _dot(lo.astype(BF16), wgu_scr[0:ROW_WORDS, :]) + _dot(hi.astype(BF16), wgu_scr[ROW_WORDS:D_MODEL, :])
        h = _silu(gu[:, 0:D_EXPERT]) * gu[:, D_EXPERT:2 * D_EXPERT]
        y = _dot(h.astype(BF16), wd_scr[...])
        ys_ref[0:n_rows, :] = _pack_bf16_pairs_native(y)

    @pl.when((i < plan_ref[2, 0]) & (live > ROW_TILE // 2))
    def _():
        swiglu_rows(ROW_TILE)

    @pl.when((i < plan_ref[2, 0]) & (live <= ROW_TILE // 2))
    def _():
        swiglu_rows(ROW_TILE // 2)


def _expert_gemm_call(after, plan, xs, w_e_gate, w_e_up, w_e_down):
    n_slots = xs.shape[0]
    n_row_tiles = n_slots // ROW_TILE
    rows = pl.BlockSpec((ROW_TILE, ROW_WORDS), lambda i, plan: (jnp.minimum(i, plan[2, 0] - 1), 0))
    expert = lambda i, plan: (plan[0, i], 0, 0)
    return pl.pallas_call(
        _expert_gemm_kernel,
        grid_spec=pltpu.PrefetchScalarGridSpec(
            num_scalar_prefetch=1,
            grid=(n_row_tiles,),
            in_specs=[
                pl.BlockSpec(memory_space=pl.ANY),
                rows,
                pl.BlockSpec((None, D_MODEL, D_EXPERT), expert),
                pl.BlockSpec((None, D_MODEL, D_EXPERT), expert),
                pl.BlockSpec((None, D_EXPERT, D_MODEL), expert),
            ],
            out_specs=rows,
            scratch_shapes=[pltpu.VMEM((D_MODEL, 2 * D_EXPERT), BF16), pltpu.VMEM((D_EXPERT, D_MODEL), BF16)],
        ),
        out_shape=jax.ShapeDtypeStruct((n_slots, ROW_WORDS), jnp.int32),
        compiler_params=pltpu.CompilerParams(dimension_semantics=("arbitrary",), vmem_limit_bytes=VMEM_LIMIT),
        name="moe_experts",
    )(plan, after, xs, w_e_gate, w_e_up, w_e_down)


def _moe_out_kernel(yt_ref, w_ref, t_ref, x1_ref, mod_ref, post2_ref, wsg_ref, wsu_ref, wsd_ref, o_ref):
    lo, hi = _unpack_bf16_pairs(t_ref[...])
    lo = lo.astype(BF16)
    hi = hi.astype(BF16)
    g = _dot(lo, wsg_ref[0:ROW_WORDS, :]) + _dot(hi, wsg_ref[ROW_WORDS:D_MODEL, :])
    u = _dot(lo, wsu_ref[0:ROW_WORDS, :]) + _dot(hi, wsu_ref[ROW_WORDS:D_MODEL, :])
    shared = _dot((_silu(g) * u).astype(BF16), wsd_ref[...])
    acc_lo = shared[:, 0:ROW_WORDS]
    acc_hi = shared[:, ROW_WORDS:D_MODEL]
    for k in range(TOP_K):
        y_lo, y_hi = _unpack_bf16_pairs(yt_ref[k])
        wk = w_ref[:, k:k + 1]
        acc_lo = acc_lo + wk * y_lo
        acc_hi = acc_hi + wk * y_hi
    moe = jnp.concatenate([acc_lo, acc_hi], axis=-1)
    o_ref[...] = x1_ref[...] + mod_ref[5:6, :] * _rms_norm(moe, post2_ref[...])


def _moe_out_into_kernel(prev_ref, *refs):
    del prev_ref
    _moe_out_kernel(*refs)


def _moe_out_call(prev_out, n_total, b0, yt, w, t, x1, mod_x, post2_g, wsg_bf, wsu_bf, wsd_bf, seq):
    n = t.shape[0]
    tiles_per_seq = seq // TOUT
    first_tile = b0 * tiles_per_seq
    const = lambda i: (0, 0)
    in_specs = [
        pl.BlockSpec((TOP_K, TOUT, ROW_WORDS), lambda i: (0, i, 0)),
        pl.BlockSpec((TOUT, TOP_K), lambda i: (i, 0)),
        pl.BlockSpec((TOUT, ROW_WORDS), lambda i: (i, 0)),
        pl.BlockSpec((TOUT, D_MODEL), lambda i: (i, 0)),
        pl.BlockSpec((None, 6, D_MODEL), lambda i: (b0 + i // tiles_per_seq, 0, 0)),
        pl.BlockSpec((1, D_MODEL), const),
        pl.BlockSpec((D_MODEL, D_EXPERT), const),
        pl.BlockSpec((D_MODEL, D_EXPERT), const),
        pl.BlockSpec((D_EXPERT, D_MODEL), const),
    ]
    args = (yt, w, t, x1, mod_x, post2_g, wsg_bf, wsu_bf, wsd_bf)
    aliased = prev_out is not None
    return pl.pallas_call(
        _moe_out_into_kernel if aliased else _moe_out_kernel,
        grid=(n // TOUT,),
        in_specs=([pl.BlockSpec(memory_space=pl.ANY)] if aliased else []) + in_specs,
        out_specs=pl.BlockSpec((TOUT, D_MODEL), lambda i: (first_tile + i, 0)),
        out_shape=jax.ShapeDtypeStruct((n_total, D_MODEL), F32),
        input_output_aliases={0: 0} if aliased else {},
        compiler_params=pltpu.CompilerParams(dimension_semantics=("arbitrary",), vmem_limit_bytes=VMEM_LIMIT),
        name="moe_out",
    )(*(((prev_out,) if aliased else ()) + args))


def _moe_dispatch(after, hx2w, idx_t):
    n = hx2w.shape[0]
    n_slots = n * TOP_K + N_EXPERTS * ROW_TILE
    pos_t, plan = _positions_call(after, idx_t, n_slots // ROW_TILE)
    pos = pos_t.reshape(TOP_K, SC_WORKERS, n // (SC_WORKERS * SC_CHUNK), SC_CHUNK).transpose(1, 2, 0, 3)
    return _dispatch_call(hx2w, pos, n_slots), pos, plan


def kernel(x, c, ctx, c_ctx, w_mod, b_mod, pre1_g, post1_g, pre2_g, post2_g, w_in, w_dw, b_dw, ln_conv_g, ln_conv_b, w_conv_out, w_sc, b_sc, w_rg_a, b_rg_a, w_rg_x, b_rg_x, lru_lambda, w_rnn_out, w_out, w_router, router_bias, w_e_gate, w_e_up, w_e_down, w_s_gate, w_s_up, w_s_down):
    assert w_mod.shape[0] == 1, "single-layer block"
    bsz, seq, d = x.shape
    n = bsz * seq
    assert d == D_MODEL and seq % TM == 0 and seq % TOUT == 0 and bsz + 1 <= SUBLANES
    assert TOKEN_GROUPS == 2 and bsz % TOKEN_GROUPS == 0 and (n // TOKEN_GROUPS) % (SC_WORKERS * SC_CHUNK) == 0
    assert (n // TOKEN_GROUPS) % POS_TILE == 0

    cc = jnp.zeros((SUBLANES, D_MODEL), F32).at[:bsz].set(c).at[bsz].set(c_ctx)
    mod = _mod_call(cc, w_mod[0], b_mod)
    mod_x = mod[:bsz].reshape(bsz, 6, D_MODEL)
    mod_c = mod[bsz].reshape(6, D_MODEL)

    w_in_bf = w_in[0].astype(BF16)
    wa_bf = w_rg_a[0].astype(BF16)
    wx_bf = w_rg_x[0].astype(BF16)

    h0 = _ctx_call(ctx, mod_c, pre1_g, w_in_bf, w_sc[0], b_sc[0], wa_bf, b_rg_a[0], wx_bf, b_rg_x[0],
                   lru_lambda[0])

    wco_bf, wro_bf, wout_bf = w_conv_out[0].astype(BF16), w_rnn_out[0].astype(BF16), w_out[0].astype(BF16)
    wsg_bf, wsu_bf, wsd_bf = w_s_gate[0].astype(BF16), w_s_up[0].astype(BF16), w_s_down[0].astype(BF16)
    w_router_t, rbias = w_router[0].T, router_bias.reshape(N_EXPERTS, 1)
    gsz = bsz // TOKEN_GROUPS
    gn = gsz * seq
    none = jnp.zeros((TOP_K, LANES), jnp.int32)

    def fwd(g, after):
        return _mixer_fwd_call(g * gsz, gsz, after, x, mod_x, pre1_g, w_in_bf, w_dw[0], b_dw, ln_conv_g,
                               ln_conv_b, wco_bf, w_sc[0], b_sc[0], wa_bf, b_rg_a[0], wx_bf, b_rg_x[0],
                               lru_lambda[0], h0)

    def bwd(g, after, acts):
        gaya, gb, gbr, urnn, hf = acts
        x1, hx2w, idx_t, w_t = _mixer_bwd_call(
            g * gsz, gsz, after, urnn, hf, gbr, gaya, gb, x, mod_x, w_sc[0], b_sc[0], wa_bf, b_rg_a[0], wx_bf,
            b_rg_x[0], lru_lambda[0], h0, wro_bf, wout_bf, post1_g, pre2_g, w_router_t, rbias)
        return x1.reshape(gn, D_MODEL), hx2w.reshape(gn, ROW_WORDS), idx_t, w_t

    def experts(after, plan, xs):
        return _expert_gemm_call(after, plan, xs, w_e_gate[0], w_e_up[0], w_e_down[0])

    def finish(g, prev_out, yt, w_t, hx2w, x1):
        return _moe_out_call(prev_out, n, g * gsz, yt, w_t.T, hx2w, x1, mod_x, post2_g, wsg_bf, wsu_bf, wsd_bf, seq)

    x1_a, hx2w_a, idx_a, w_a = bwd(0, none, fwd(0, none))
    xs_a, pos_a, plan_a = _moe_dispatch(none, hx2w_a, idx_a)
    acts_b = fwd(1, idx_a)
    ys_a = experts(acts_b[4], plan_a, xs_a)
    yt_a = _collect_call(ys_a, pos_a, gn)
    x1_b, hx2w_b, idx_b, w_b = bwd(1, ys_a, acts_b)
    xs_b, pos_b, plan_b = _moe_dispatch(yt_a, hx2w_b, idx_b)
    out = finish(0, None, yt_a, w_a, hx2w_a, x1_a)
    ys_b = experts(none, plan_b, xs_b)
    yt_b = _collect_call(ys_b, pos_b, gn)
    out = finish(1, out, yt_b, w_b, hx2w_b, x1_b)
    return out.reshape(bsz, seq, D_MODEL)
```

```python
import jax
import jax.numpy as jnp
from jax import lax
from jax.experimental import pallas as pl
from jax.experimental.pallas import tpu as pltpu
from jax.experimental.pallas import tpu_sc as plsc

F32 = jnp.float32
BF16 = jnp.bfloat16

D_MODEL = 1024
GRID_W = 64
CONV_WIDTH = 31
CONV_HALF = CONV_WIDTH // 2
SHORT_CONV = 4
RNN_HEADS = 4
RNN_BLOCK = D_MODEL // RNN_HEADS
LRU_C = 8.0
N_EXPERTS = 64
N_GROUPS = 8
GROUP_SIZE = N_EXPERTS // N_GROUPS
TOPK_GROUPS = 4
TOP_K = 8
D_EXPERT = 256
ROUTED_SCALE = 2.5
EPS = 1e-6

SUBLANES = 8
TM = 512
ROWS_PER_TILE = TM // GRID_W
PAD = 16
ROW_STRIDE = GRID_W + PAD
UPAD_ROWS = ROWS_PER_TILE * ROW_STRIDE + PAD
SEG = TM // SUBLANES
HALO = (SHORT_CONV - 1) * SUBLANES
CONV_LANES = 256
LANES = 128
ROW_WORDS = D_MODEL // 2
ROW_TILE = 1024
POS_TILE = 512
TOUT = 512
TOKEN_GROUPS = 2
V7X_SC_CORES = 2
V7X_SC_SUBCORES = 16
SC_WORKERS = V7X_SC_CORES * V7X_SC_SUBCORES
SC_CHUNK = 64
SC_BUFFERS = 2
VMEM_LIMIT = 58 * 1024 * 1024


def _sigmoid(x):
    return 0.5 * (jnp.tanh(0.5 * x) + 1.0)


def _silu(x):
    return x * _sigmoid(x)


def _gelu_tanh(x):
    return 0.5 * x * (1.0 + jnp.tanh(0.7978845608028654 * (x + 0.044715 * (x * x * x))))


def _rms_norm(x, g):
    return x * lax.rsqrt(jnp.mean(x * x, axis=-1, keepdims=True) + EPS) * g


def _dot(a, b):
    return jnp.dot(a, b, preferred_element_type=F32)


def _dot_nt_split(a, b):
    a_hi, b_hi = a.astype(BF16), b.astype(BF16)
    a_lo = (a - a_hi.astype(F32)).astype(BF16)
    b_lo = (b - b_hi.astype(F32)).astype(BF16)
    nt = (((1,), (1,)), ((), ()))
    return (lax.dot_general(a_hi, b_hi, nt, preferred_element_type=F32)
            + (lax.dot_general(a_hi, b_lo, nt, preferred_element_type=F32)
               + lax.dot_general(a_lo, b_hi, nt, preferred_element_type=F32)))


def _pack_bf16_pairs(x):
    half = x.shape[-1] // 2
    lo = lax.bitcast_convert_type(x[:, :half].astype(BF16).astype(F32), jnp.uint32)
    hi = lax.bitcast_convert_type(x[:, half:].astype(BF16).astype(F32), jnp.uint32)
    return lax.bitcast_convert_type(hi | (lo >> 16), jnp.int32)


def _pack_bf16_pairs_native(x):
    half = x.shape[-1] // 2
    packed = pltpu.pack_elementwise([x[:, :half], x[:, half:]], packed_dtype=BF16)
    return lax.bitcast_convert_type(packed, jnp.int32)


def _unpack_bf16_pairs(words):
    u = lax.bitcast_convert_type(words, jnp.uint32)
    return (lax.bitcast_convert_type(u << 16, F32),
            lax.bitcast_convert_type(u & jnp.uint32(0xFFFF0000), F32))


def _log_decay(lam):
    return LRU_C * (jnp.minimum(lam, 0.0) - jnp.log1p(jnp.exp(-jnp.abs(lam))))


def _rglru_coeffs(v, wa_ref, ba, wx_ref, bx, c_lam, a_ref, b_ref):
    vb = v.astype(BF16)
    for h in range(RNN_HEADS):
        cs = slice(h * RNN_BLOCK, (h + 1) * RNN_BLOCK)
        vh = vb[:, cs]
        r = _sigmoid(_dot(vh, wa_ref[h]) + ba[:, cs])
        i = _sigmoid(_dot(vh, wx_ref[h]) + bx[:, cs])
        log_a = c_lam[:, cs] * r
        a = jnp.exp(log_a)
        a_ref[:, cs] = a
        b_ref[:, cs] = jnp.sqrt(jnp.tanh(-log_a) * (1.0 + a * a)) * (i * v[:, cs])


def _scan_tile(a_ref, b_ref, carry, n_rows, reverse):
    row = lax.broadcasted_iota(jnp.int32, (SUBLANES, D_MODEL), 0)
    n_groups = n_rows // SUBLANES

    def body(j, carry):
        g = (n_groups - 1 - j) if reverse else j
        off = pl.multiple_of(g * SUBLANES, SUBLANES)
        a = a_ref[pl.ds(off, SUBLANES), :]
        b = b_ref[pl.ds(off, SUBLANES), :]
        for s in (1, 2, 4):
            keep = (row < SUBLANES - s) if reverse else (row >= s)
            shift = (SUBLANES - s) if reverse else s
            a_sh = jnp.where(keep, pltpu.roll(a, shift, 0), 1.0)
            b_sh = jnp.where(keep, pltpu.roll(b, shift, 0), 0.0)
            b = a * b_sh + b
            a = a * a_sh
        h = a * carry + b
        b_ref[pl.ds(off, SUBLANES), :] = h
        last = h[0:1, :] if reverse else h[SUBLANES - 1:SUBLANES, :]
        return jnp.broadcast_to(last, (SUBLANES, D_MODEL))

    return lax.fori_loop(0, n_groups, body, carry, unroll=2)


def _segment_perm(to_natural):
    r = lax.broadcasted_iota(jnp.int32, (TM, TM), 0)
    c = lax.broadcasted_iota(jnp.int32, (TM, TM), 1)
    if to_natural:
        src = (r & (SEG - 1)) * SUBLANES + lax.shift_right_logical(r, SEG.bit_length() - 1)
    else:
        src = (r & (SUBLANES - 1)) * SEG + lax.shift_right_logical(r, SUBLANES.bit_length() - 1)
    return jnp.where(c == src, 1.0, 0.0).astype(BF16)


def _scan_segments(a_ref, b_ref, carry, reverse):
    row = lax.broadcasted_iota(jnp.int32, (SUBLANES, D_MODEL), 0)

    def body(s, hp):
        h, prod = hp
        off = pl.multiple_of(((SEG - 1 - s) if reverse else s) * SUBLANES, SUBLANES)
        a = a_ref[pl.ds(off, SUBLANES), :]
        h = a * h + b_ref[pl.ds(off, SUBLANES), :]
        prod = a * prod
        b_ref[pl.ds(off, SUBLANES), :] = h
        a_ref[pl.ds(off, SUBLANES), :] = prod
        return h, prod

    b, a = lax.fori_loop(0, SEG, body, (jnp.zeros((SUBLANES, D_MODEL), F32), jnp.ones((SUBLANES, D_MODEL), F32)),
                         unroll=4)
    for s in (1, 2, 4):
        keep = (row < SUBLANES - s) if reverse else (row >= s)
        shift = (SUBLANES - s) if reverse else s
        a_sh = jnp.where(keep, pltpu.roll(a, shift, 0), 1.0)
        b_sh = jnp.where(keep, pltpu.roll(b, shift, 0), 0.0)
        b = a * b_sh + b
        a = a * a_sh
    leave = a * carry + b
    if reverse:
        enter = jnp.where(row < SUBLANES - 1, pltpu.roll(leave, SUBLANES - 1, 0), carry)
        last = leave[0:1, :]
    else:
        enter = jnp.where(row >= 1, pltpu.roll(leave, 1, 0), carry)
        last = leave[SUBLANES - 1:SUBLANES, :]
    h = b_ref[...].reshape(SEG, SUBLANES, D_MODEL) + a_ref[...].reshape(SEG, SUBLANES, D_MODEL) * enter[None]
    b_ref[...] = h.reshape(TM, D_MODEL)
    return jnp.broadcast_to(last, (SUBLANES, D_MODEL))


def _short_conv_segments(uext_ref, wsc_ref, bsc_ref):
    v = bsc_ref[...] + wsc_ref[0:1, :] * uext_ref[0:TM, :]
    for k in range(1, SHORT_CONV):
        v = v + wsc_ref[k:k + 1, :] * uext_ref[k * SUBLANES:k * SUBLANES + TM, :]
    return v


def _mod_kernel(c_ref, w_ref, b_ref, o_ref):
    o_ref[...] = jnp.dot(_silu(c_ref[...]), w_ref[...], preferred_element_type=F32,
                         precision=lax.Precision.HIGHEST) + b_ref[...]


def _mod_call(cc, w_mod, b_mod):
    n_mod = w_mod.shape[1]
    return pl.pallas_call(
        _mod_kernel,
        grid=(n_mod // D_MODEL,),
        in_specs=[
            pl.BlockSpec((SUBLANES, D_MODEL), lambda j: (0, 0)),
            pl.BlockSpec((D_MODEL, D_MODEL), lambda j: (0, j)),
            pl.BlockSpec((1, D_MODEL), lambda j: (0, j)),
        ],
        out_specs=pl.BlockSpec((SUBLANES, D_MODEL), lambda j: (0, j)),
        out_shape=jax.ShapeDtypeStruct((SUBLANES, n_mod), F32),
        name="mod",
    )(cc, w_mod, b_mod)


def _ctx_kernel(ctx_ref, mod_ref, g_ref, w_ref, wsc_ref, bsc_ref, wa_ref, ba_ref, wx_ref, bx_ref,
                lam_ref, o_ref, uext_ref, a_ref, b_ref):
    n = ctx_ref.shape[0]
    hc = _rms_norm(ctx_ref[...], g_ref[...]) * (1.0 + mod_ref[1:2, :]) + mod_ref[0:1, :]
    u = _dot(hc.astype(BF16), w_ref[...])
    zeros8 = jnp.zeros((SUBLANES, D_MODEL), F32)
    uext_ref[0:SUBLANES, :] = zeros8
    uext_ref[SUBLANES:SUBLANES + n, :] = u
    uext_ref[SUBLANES + n:2 * SUBLANES + n, :] = zeros8
    for d in range(2):
        v = jnp.broadcast_to(bsc_ref[d:d + 1, :], (n, D_MODEL))
        for k in range(SHORT_CONV):
            start = SUBLANES + k - (SHORT_CONV - 1) * (1 - d)
            v = v + wsc_ref[d, k:k + 1, :] * uext_ref[start:start + n, :]
        _rglru_coeffs(v, wa_ref.at[d], ba_ref[d:d + 1, :], wx_ref.at[d], bx_ref[d:d + 1, :],
                      _log_decay(lam_ref[d:d + 1, :]), a_ref, b_ref)
        final = _scan_tile(a_ref, b_ref, zeros8, n, reverse=(d == 1))
        o_ref[d:d + 1, :] = final[0:1, :]


def _ctx_call(ctx, mod_c, pre1_g, w_in_bf, w_sc, b_sc, wa_bf, b_rg_a, wx_bf, b_rg_x, lam):
    bsz, n, _ = ctx.shape
    const2 = lambda b: (0, 0)
    const3 = lambda b: (0, 0, 0)
    const4 = lambda b: (0, 0, 0, 0)
    return pl.pallas_call(
        _ctx_kernel,
        grid=(bsz,),
        in_specs=[
            pl.BlockSpec((None, n, D_MODEL), lambda b: (b, 0, 0)),
            pl.BlockSpec((6, D_MODEL), const2),
            pl.BlockSpec((1, D_MODEL), const2),
            pl.BlockSpec((D_MODEL, D_MODEL), lambda b: (0, 2)),
            pl.BlockSpec((2, SHORT_CONV, D_MODEL), const3),
            pl.BlockSpec((2, D_MODEL), const2),
            pl.BlockSpec((2, RNN_HEADS, RNN_BLOCK, RNN_BLOCK), const4),
            pl.BlockSpec((2, D_MODEL), const2),
            pl.BlockSpec((2, RNN_HEADS, RNN_BLOCK, RNN_BLOCK), const4),
            pl.BlockSpec((2, D_MODEL), const2),
            pl.BlockSpec((2, D_MODEL), const2),
        ],
        out_specs=pl.BlockSpec((None, 2, D_MODEL), lambda b: (b, 0, 0)),
        out_shape=jax.ShapeDtypeStruct((bsz, 2, D_MODEL), F32),
        scratch_shapes=[
            pltpu.VMEM((n + 2 * SUBLANES, D_MODEL), F32),
            pltpu.VMEM((n, D_MODEL), F32),
            pltpu.VMEM((n, D_MODEL), F32),
        ],
        name="ctx",
    )(ctx, mod_c, pre1_g, w_in_bf, w_sc, b_sc, wa_bf, b_rg_a, wx_bf, b_rg_x, lam)


def _mixer_fwd_kernel(after_ref, x_ref, mod_ref, g_ref, win_ref, wdw_ref, bdw_ref, lng_ref, lnb_ref, wco_ref,
                      wsc_ref, bsc_ref, wa_ref, ba_ref, wx_ref, bx_ref, lam_ref, h0_ref,
                      gaya_ref, gb_ref, gbr_ref, urnn_ref, hf_ref,
                      upad_ref, ush_ref, wb_ref, cv_ref, uext_ref, a_ref, b_ref, carry_ref, halo_ref, perm_ref):
    del after_ref
    j = pl.program_id(1)

    @pl.when(j == 0)
    def _():
        carry_ref[...] = jnp.broadcast_to(h0_ref[0:1, :], (SUBLANES, D_MODEL))
        halo_ref[...] = jnp.zeros((SUBLANES, D_MODEL), F32)
        perm_ref[...] = _segment_perm(to_natural=False)
        zeros_pad = jnp.zeros((PAD, D_MODEL), F32)
        for r in range(ROWS_PER_TILE + 1):
            upad_ref[r * ROW_STRIDE:r * ROW_STRIDE + PAD, :] = zeros_pad
        for k in range(CONV_WIDTH):
            wb_ref[k] = jnp.broadcast_to(wdw_ref[k:k + 1, :], (SUBLANES, D_MODEL))

    hx = (_rms_norm(x_ref[...], g_ref[...]) * (1.0 + mod_ref[1:2, :]) + mod_ref[0:1, :]).astype(BF16)

    for c in range(D_MODEL // CONV_LANES):
        cs = slice(c * CONV_LANES, (c + 1) * CONV_LANES)
        u = (_dot(hx, win_ref[:, c * CONV_LANES:(c + 1) * CONV_LANES])
             * _sigmoid(_dot(hx, win_ref[:, D_MODEL + c * CONV_LANES:D_MODEL + (c + 1) * CONV_LANES])))
        for r in range(ROWS_PER_TILE):
            upad_ref[PAD + r * ROW_STRIDE:PAD + r * ROW_STRIDE + GRID_W, cs] = u[r * GRID_W:(r + 1) * GRID_W, :]
    vregs_per_row = GRID_W // SUBLANES
    hxp = _dot(perm_ref[...], hx).astype(BF16)
    for c in range(D_MODEL // CONV_LANES):
        for cb in range(D_MODEL // CONV_LANES):
            cols = slice(cb * CONV_LANES, (cb + 1) * CONV_LANES)
            w0 = (5, 3, 4, 2)[c] * D_MODEL + cb * CONV_LANES
            y = _dot(hxp, win_ref[:, w0:w0 + CONV_LANES])
            if c == 0:
                gb_ref[:, cols] = _sigmoid(y).astype(BF16)
            elif c == 1:
                gbr_ref[:, cols] = _gelu_tanh(y).astype(BF16)
            elif c == 2:
                a_ref[:, cols] = _sigmoid(y)
            else:
                urnn_ref[:, cols] = y.astype(BF16)
                uext_ref[HALO:HALO + TM, cols] = y
        cs = slice(c * CONV_LANES, (c + 1) * CONV_LANES)
        xpad = upad_ref[:, cs]
        for s in range(1, SUBLANES):
            ush_ref[s - 1] = pltpu.roll(xpad, UPAD_ROWS - s, 0)
        for r in range(ROWS_PER_TILE):
            acc = jnp.broadcast_to(bdw_ref[:, cs].reshape(1, 1, CONV_LANES), (vregs_per_row, SUBLANES, CONV_LANES))
            for k in range(CONV_WIDTH):
                q, s = divmod(r * ROW_STRIDE + PAD - CONV_HALF + k, SUBLANES)
                rows = slice(q * SUBLANES, q * SUBLANES + GRID_W)
                win = upad_ref[rows, cs] if s == 0 else ush_ref[s - 1, rows, :]
                acc = acc + wb_ref[k, :, cs] * win.reshape(vregs_per_row, SUBLANES, CONV_LANES)
            cv_ref[r * GRID_W:(r + 1) * GRID_W, cs] = acc.reshape(GRID_W, CONV_LANES)
    cv = cv_ref[...]
    cvc = cv - jnp.mean(cv, axis=-1, keepdims=True)
    cvn = cvc * lax.rsqrt(jnp.mean(cvc * cvc, axis=-1, keepdims=True) + EPS) * lng_ref[...] + lnb_ref[...]
    y_a = _dot(_dot(perm_ref[...], _silu(cvn).astype(BF16)).astype(BF16), wco_ref[...])

    gaya_ref[...] = (a_ref[...] * y_a).astype(BF16)

    row8 = lax.broadcasted_iota(jnp.int32, (SUBLANES, D_MODEL), 0)
    for jj in range(SHORT_CONV - 1):
        late = uext_ref[HALO + (SEG - (SHORT_CONV - 1) + jj) * SUBLANES:HALO + (SEG - (SHORT_CONV - 1) + jj + 1) * SUBLANES, :]
        prev_tile = jnp.broadcast_to(halo_ref[jj:jj + 1, :], (SUBLANES, D_MODEL))
        uext_ref[jj * SUBLANES:(jj + 1) * SUBLANES, :] = jnp.where(row8 == 0, prev_tile, pltpu.roll(late, 1, 0))
        halo_ref[jj:jj + 1, :] = late[SUBLANES - 1:SUBLANES, :]
    v = _short_conv_segments(uext_ref, wsc_ref, bsc_ref)
    _rglru_coeffs(v, wa_ref, ba_ref[...], wx_ref, bx_ref[...], _log_decay(lam_ref[...]), a_ref, b_ref)
    carry_ref[...] = _scan_segments(a_ref, b_ref, carry_ref[...], reverse=False)
    hf_ref[...] = b_ref[...].astype(BF16)


def _resident(shape):
    nd = len(shape)
    return pl.BlockSpec(shape, lambda b, j: (0,) * nd, pipeline_mode=pl.Buffered(1))


def _mixer_fwd_call(b0, bsz, after, x, mod_x, pre1_g, w_in_bf, w_dw, b_dw, ln_g, ln_b, wco_bf,
                    w_sc, b_sc, wa_bf, b_rg_a, wx_bf, b_rg_x, lam, h0):
    seq = x.shape[1]
    nt = seq // TM
    tile = pl.BlockSpec((None, TM, D_MODEL), lambda b, j: (b, j, 0))
    act = jax.ShapeDtypeStruct((bsz, seq, D_MODEL), BF16)
    head_w = pl.BlockSpec((None, RNN_HEADS, RNN_BLOCK, RNN_BLOCK), lambda b, j: (0, 0, 0, 0),
                          pipeline_mode=pl.Buffered(1))
    dir_row = pl.BlockSpec((None, 1, D_MODEL), lambda b, j: (0, 0, 0), pipeline_mode=pl.Buffered(1))
    return pl.pallas_call(
        _mixer_fwd_kernel,
        grid=(bsz, nt),
        in_specs=[
            pl.BlockSpec(memory_space=pl.ANY),
            pl.BlockSpec((None, TM, D_MODEL), lambda b, j: (b0 + b, j, 0)),
            pl.BlockSpec((None, 6, D_MODEL), lambda b, j: (b0 + b, 0, 0)),
            _resident((1, D_MODEL)),
            _resident((D_MODEL, 6 * D_MODEL)),
            _resident((CONV_WIDTH, D_MODEL)),
            _resident((1, D_MODEL)),
            _resident((1, D_MODEL)),
            _resident((1, D_MODEL)),
            _resident((D_MODEL, D_MODEL)),
            pl.BlockSpec((None, SHORT_CONV, D_MODEL), lambda b, j: (0, 0, 0), pipeline_mode=pl.Buffered(1)),
            dir_row, head_w, dir_row, head_w, dir_row, dir_row,
            pl.BlockSpec((None, 2, D_MODEL), lambda b, j: (b0 + b, 0, 0)),
        ],
        out_specs=[tile] * 5,
        out_shape=[act] * 5,
        scratch_shapes=[
            pltpu.VMEM((UPAD_ROWS, D_MODEL), F32),
            pltpu.VMEM((SUBLANES - 1, UPAD_ROWS, CONV_LANES), F32),
            pltpu.VMEM((CONV_WIDTH, SUBLANES, D_MODEL), F32),
            pltpu.VMEM((TM, D_MODEL), F32),
            pltpu.VMEM((TM + HALO, D_MODEL), F32),
            pltpu.VMEM((TM, D_MODEL), F32),
            pltpu.VMEM((TM, D_MODEL), F32),
            pltpu.VMEM((SUBLANES, D_MODEL), F32),
            pltpu.VMEM((SUBLANES, D_MODEL), F32),
            pltpu.VMEM((TM, TM), BF16),
        ],
        compiler_params=pltpu.CompilerParams(
            dimension_semantics=("arbitrary", "arbitrary"), vmem_limit_bytes=VMEM_LIMIT),
        name="mixer_fwd",
    )(after, x, mod_x, pre1_g, w_in_bf, w_dw, b_dw, ln_g, ln_b, wco_bf,
      w_sc, b_sc.reshape(2, 1, D_MODEL), wa_bf, b_rg_a.reshape(2, 1, D_MODEL), wx_bf,
      b_rg_x.reshape(2, 1, D_MODEL), lam.reshape(2, 1, D_MODEL), h0)


def _route(logits_t, bias):
    t = logits_t.shape[1]
    scores = _sigmoid(logits_t)
    sel = scores + bias
    neg_inf = jnp.float32(-jnp.inf)

    sel3 = sel.reshape(N_GROUPS, GROUP_SIZE, t)
    within = lax.broadcasted_iota(jnp.int32, sel3.shape, 1)
    m1 = jnp.max(sel3, axis=1, keepdims=True)
    first = jnp.min(jnp.where(sel3 == m1, within, GROUP_SIZE), axis=1, keepdims=True)
    m2 = jnp.max(jnp.where(within == first, neg_inf, sel3), axis=1, keepdims=True)
    gscore = (m1 + m2).reshape(N_GROUPS, t)

    gidx = lax.broadcasted_iota(jnp.int32, gscore.shape, 0)
    rank = jnp.zeros(gscore.shape, jnp.int32)
    for g in range(N_GROUPS):
        other = gscore[g:g + 1, :]
        beats = jnp.where(other > gscore, 1, jnp.where((other == gscore) & (gidx > g), 1, 0))
        rank = rank + beats
    gkeep = (rank < TOPK_GROUPS).reshape(N_GROUPS, 1, t)
    masked = jnp.where(gkeep, sel3, neg_inf).reshape(N_EXPERTS, t)

    eidx = lax.broadcasted_iota(jnp.int32, masked.shape, 0)
    picks, weights = [], []
    for _ in range(TOP_K):
        m = jnp.max(masked, axis=0, keepdims=True)
        first = jnp.min(jnp.where(masked == m, eidx, N_EXPERTS), axis=0, keepdims=True)
        pick = eidx == first
        picks.append(first)
        weights.append(jnp.sum(jnp.where(pick, scores, 0.0), axis=0, keepdims=True))
        masked = jnp.where(pick, neg_inf, masked)
    idx = jnp.concatenate(picks, axis=0)
    w = jnp.concatenate(weights, axis=0)
    return idx, ROUTED_SCALE * w / jnp.sum(w, axis=0, keepdims=True)


def _mixer_bwd_kernel(after_ref, urnn_ref, hf_ref, gbr_ref, gaya_ref, gb_ref, x_ref, mod_ref,
                      wsc_ref, bsc_ref, wa_ref, ba_ref, wx_ref, bx_ref, lam_ref, h0_ref,
                      wro_ref, wout_ref, post1_ref, pre2_ref, wrt_ref, rbias_ref,
                      x1_ref, hx2w_ref, idx_ref, w_ref,
                      uext_ref, a_ref, b_ref, carry_ref, halo_ref, perm_ref):
    del after_ref
    j = pl.program_id(1)

    @pl.when(j == 0)
    def _():
        carry_ref[...] = jnp.broadcast_to(h0_ref[1:2, :], (SUBLANES, D_MODEL))
        halo_ref[...] = jnp.zeros((SUBLANES, D_MODEL), F32)
        perm_ref[...] = _segment_perm(to_natural=True)

    uext_ref[0:TM, :] = urnn_ref[...].astype(F32)
    row8 = lax.broadcasted_iota(jnp.int32, (SUBLANES, D_MODEL), 0)
    for jj in range(SHORT_CONV - 1):
        early = uext_ref[jj * SUBLANES:(jj + 1) * SUBLANES, :]
        next_tile = jnp.broadcast_to(halo_ref[jj:jj + 1, :], (SUBLANES, D_MODEL))
        uext_ref[TM + jj * SUBLANES:TM + (jj + 1) * SUBLANES, :] = jnp.where(
            row8 == SUBLANES - 1, next_tile, pltpu.roll(early, SUBLANES - 1, 0))
        halo_ref[jj:jj + 1, :] = early[0:1, :]
    v = _short_conv_segments(uext_ref, wsc_ref, bsc_ref)
    _rglru_coeffs(v, wa_ref, ba_ref[...], wx_ref, bx_ref[...], _log_decay(lam_ref[...]), a_ref, b_ref)
    carry_ref[...] = _scan_segments(a_ref, b_ref, carry_ref[...], reverse=True)

    h_sum = hf_ref[...].astype(F32) + b_ref[...]
    y_b = _dot((gbr_ref[...].astype(F32) * h_sum).astype(BF16), wro_ref[...])
    mix = gaya_ref[...].astype(F32) + gb_ref[...].astype(F32) * y_b
    out = _dot(_dot(perm_ref[...], mix.astype(BF16)).astype(BF16), wout_ref[...])
    x1 = x_ref[...] + mod_ref[2:3, :] * _rms_norm(out, post1_ref[...])
    x1_ref[...] = x1

    hx2 = _rms_norm(x1, pre2_ref[...]) * (1.0 + mod_ref[4:5, :]) + mod_ref[3:4, :]
    hx2w_ref[...] = _pack_bf16_pairs(hx2)
    logits_t = _dot_nt_split(wrt_ref[...], hx2)
    idx, w = _route(logits_t, rbias_ref[...])
    idx_ref[...] = idx
    w_ref[...] = w


def _mixer_bwd_call(b0, bsz, after, urnn, hf, gbr, gaya, gb, x, mod_x, w_sc, b_sc, wa_bf, b_rg_a, wx_bf, b_rg_x, lam, h0,
                    wro_bf, wout_bf, post1_g, pre2_g, w_router_t, router_bias):
    seq = x.shape[1]
    nt = seq // TM
    rev = lambda b, j: (b, nt - 1 - j, 0)
    tile = pl.BlockSpec((None, TM, D_MODEL), rev)
    head_w = pl.BlockSpec((None, RNN_HEADS, RNN_BLOCK, RNN_BLOCK), lambda b, j: (1, 0, 0, 0),
                          pipeline_mode=pl.Buffered(1))
    dir_row = pl.BlockSpec((None, 1, D_MODEL), lambda b, j: (1, 0, 0), pipeline_mode=pl.Buffered(1))
    return pl.pallas_call(
        _mixer_bwd_kernel,
        grid=(bsz, nt),
        in_specs=[
            pl.BlockSpec(memory_space=pl.ANY),
            tile, tile, tile, tile, tile,
            pl.BlockSpec((None, TM, D_MODEL), lambda b, j: (b0 + b, nt - 1 - j, 0)),
            pl.BlockSpec((None, 6, D_MODEL), lambda b, j: (b0 + b, 0, 0)),
            pl.BlockSpec((None, SHORT_CONV, D_MODEL), lambda b, j: (1, 0, 0), pipeline_mode=pl.Buffered(1)),
            dir_row, head_w, dir_row, head_w, dir_row, dir_row,
            pl.BlockSpec((None, 2, D_MODEL), lambda b, j: (b0 + b, 0, 0)),
            _resident((D_MODEL, D_MODEL)),
            _resident((D_MODEL, D_MODEL)),
            _resident((1, D_MODEL)),
            _resident((1, D_MODEL)),
            _resident((N_EXPERTS, D_MODEL)),
            _resident((N_EXPERTS, 1)),
        ],
        out_specs=[
            pl.BlockSpec((None, TM, D_MODEL), rev),
            pl.BlockSpec((None, TM, ROW_WORDS), rev),
            pl.BlockSpec((TOP_K, TM), lambda b, j: (0, b * nt + nt - 1 - j)),
            pl.BlockSpec((TOP_K, TM), lambda b, j: (0, b * nt + nt - 1 - j)),
        ],
        out_shape=[
            jax.ShapeDtypeStruct((bsz, seq, D_MODEL), F32),
            jax.ShapeDtypeStruct((bsz, seq, ROW_WORDS), jnp.int32),
            jax.ShapeDtypeStruct((TOP_K, bsz * seq), jnp.int32),
            jax.ShapeDtypeStruct((TOP_K, bsz * seq), F32),
        ],
        scratch_shapes=[
            pltpu.VMEM((TM + HALO, D_MODEL), F32),
            pltpu.VMEM((TM, D_MODEL), F32),
            pltpu.VMEM((TM, D_MODEL), F32),
            pltpu.VMEM((SUBLANES, D_MODEL), F32),
            pltpu.VMEM((SUBLANES, D_MODEL), F32),
            pltpu.VMEM((TM, TM), BF16),
        ],
        compiler_params=pltpu.CompilerParams(
            dimension_semantics=("arbitrary", "arbitrary"), vmem_limit_bytes=VMEM_LIMIT),
        name="mixer_bwd",
    )(after, urnn, hf, gbr, gaya, gb, x, mod_x, w_sc, b_sc.reshape(2, 1, D_MODEL), wa_bf,
      b_rg_a.reshape(2, 1, D_MODEL), wx_bf, b_rg_x.reshape(2, 1, D_MODEL), lam.reshape(2, 1, D_MODEL), h0,
      wro_bf, wout_bf, post1_g, pre2_g, w_router_t, router_bias)


def _positions_kernel(after_ref, idx_ref, pos_ref, plan_ref):
    del after_ref
    n = idx_ref.shape[1]
    n_tiles = n // POS_TILE
    eidx = lax.broadcasted_iota(jnp.int32, (N_EXPERTS, POS_TILE), 0)

    def chosen(t):
        idx = idx_ref[:, pl.ds(pl.multiple_of(t * POS_TILE, POS_TILE), POS_TILE)]
        ch = jnp.zeros((N_EXPERTS, POS_TILE), F32)
        for k in range(TOP_K):
            ch = ch + jnp.where(eidx == idx[k:k + 1, :], 1.0, 0.0)
        return idx, ch

    def count_body(t, cnt):
        return cnt + jnp.sum(chosen(t)[1], axis=1, keepdims=True)

    cnt = lax.fori_loop(0, n_tiles, count_body, jnp.zeros((N_EXPERTS, 1), F32))
    padded = jnp.ceil(cnt * (1.0 / ROW_TILE)) * ROW_TILE
    r = lax.broadcasted_iota(jnp.int32, (N_EXPERTS, N_EXPERTS), 0)
    c = lax.broadcasted_iota(jnp.int32, (N_EXPERTS, N_EXPERTS), 1)
    off = jnp.dot(jnp.where(c < r, 1.0, 0.0), jnp.broadcast_to(padded, (N_EXPERTS, LANES)),
                  preferred_element_type=F32, precision=lax.Precision.HIGHEST)[:, 0:1]
    end = off + padded

    n_map = plan_ref.shape[1]
    tstart = lax.broadcasted_iota(jnp.int32, (N_EXPERTS, n_map), 1).astype(F32) * ROW_TILE
    te = jnp.minimum(jnp.sum(jnp.where(end <= tstart, 1, 0), axis=0, keepdims=True), N_EXPERTS - 1)
    emap = lax.broadcasted_iota(jnp.int32, (N_EXPERTS, n_map), 0)
    live_end = jnp.sum(jnp.where(emap == te, off + cnt, 0.0), axis=0, keepdims=True)
    total = jnp.sum(padded, axis=0, keepdims=True)
    plan_ref[0:1, :] = te
    plan_ref[1:2, :] = jnp.clip(live_end - tstart[0:1, :], 0.0, ROW_TILE).astype(jnp.int32)
    plan_ref[2:3, :] = jnp.broadcast_to(total * (1.0 / ROW_TILE), (1, n_map)).astype(jnp.int32)

    row = lax.broadcasted_iota(jnp.int32, (POS_TILE, POS_TILE), 0)
    col = lax.broadcasted_iota(jnp.int32, (POS_TILE, POS_TILE), 1)
    before = jnp.where(row < col, 1.0, 0.0).astype(BF16)

    def pos_body(t, carry):
        idx, ch = chosen(t)
        base = _dot(ch.astype(BF16), before) + (carry + off)
        rows = [jnp.sum(jnp.where(eidx == idx[k:k + 1, :], base, 0.0), axis=0, keepdims=True)
                for k in range(TOP_K)]
        pos_ref[:, pl.ds(pl.multiple_of(t * POS_TILE, POS_TILE), POS_TILE)] = (
            jnp.concatenate(rows, axis=0).astype(jnp.int32))
        return carry + jnp.sum(ch, axis=1, keepdims=True)

    lax.fori_loop(0, n_tiles, pos_body, jnp.zeros((N_EXPERTS, 1), F32))


def _positions_call(after, idx_t, n_row_tiles):
    n = idx_t.shape[1]
    n_map = -(-n_row_tiles // LANES) * LANES
    return pl.pallas_call(
        _positions_kernel,
        in_specs=[pl.BlockSpec(memory_space=pl.ANY), pl.BlockSpec(memory_space=pltpu.VMEM)],
        out_shape=[
            jax.ShapeDtypeStruct((TOP_K, n), jnp.int32),
            jax.ShapeDtypeStruct((3, n_map), jnp.int32),
        ],
        name="moe_positions",
    )(after, idx_t)


def _sc_mesh():
    return plsc.VectorSubcoreMesh(core_axis_name="c", subcore_axis_name="s",
                                  num_cores=V7X_SC_CORES, num_subcores=V7X_SC_SUBCORES)


def _sc_worker():
    return lax.axis_index("s") * V7X_SC_CORES + lax.axis_index("c")


def _dispatch_call(rows, pos, n_slots):
    n = rows.shape[0]
    tok_w = n // SC_WORKERS
    n_items = tok_w // SC_CHUNK

    def body(rows_hbm, pos_hbm, xs_hbm, idx_v, rows_v, lsem, ssem):
        wid = _sc_worker()
        pltpu.sync_copy(pos_hbm.at[wid], idx_v)
        base = wid * tok_w

        def load(i):
            b = i % SC_BUFFERS
            return pltpu.async_copy(rows_hbm.at[pl.ds(base + i * SC_CHUNK, SC_CHUNK)], rows_v.at[b], lsem.at[b])

        def scatter(i):
            b = i % SC_BUFFERS
            return [pltpu.async_copy(rows_v.at[b], xs_hbm.at[idx_v.at[i, k]], ssem.at[b]) for k in range(TOP_K)]

        loads = {i: load(i) for i in range(SC_BUFFERS - 1)}
        scat = {}
        for i in range(n_items):
            loads[i].wait()
            scat[i] = scatter(i)
            if i >= 1:
                for cp in scat[i - 1]:
                    cp.wait()
            if i + SC_BUFFERS - 1 < n_items:
                loads[i + SC_BUFFERS - 1] = load(i + SC_BUFFERS - 1)
        for cp in scat[n_items - 1]:
            cp.wait()

    return pl.kernel(
        body, mesh=_sc_mesh(),
        out_type=jax.ShapeDtypeStruct((n_slots, ROW_WORDS), jnp.int32),
        scratch_types=[pltpu.VMEM((n_items, TOP_K, SC_CHUNK), jnp.int32),
                       pltpu.VMEM((SC_BUFFERS, SC_CHUNK, ROW_WORDS), jnp.int32),
                       pltpu.SemaphoreType.DMA((SC_BUFFERS,)), pltpu.SemaphoreType.DMA((SC_BUFFERS,))],
        compiler_params=pltpu.CompilerParams(use_tc_tiling_on_sc=True),
        name="moe_dispatch",
    )(rows, pos)


def _collect_call(ys, pos, n):
    tok_w = n // SC_WORKERS
    n_chunks = tok_w // SC_CHUNK
    items = [(c, k) for c in range(n_chunks) for k in range(TOP_K)]

    def body(ys_hbm, pos_hbm, yt_hbm, idx_v, rows_v, gsem, wsem):
        wid = _sc_worker()
        pltpu.sync_copy(pos_hbm.at[wid], idx_v)
        base = wid * tok_w

        def gather(j):
            c, k = items[j]
            b = j % SC_BUFFERS
            return pltpu.async_copy(ys_hbm.at[idx_v.at[c, k]], rows_v.at[b], gsem.at[b])

        def write(j):
            c, k = items[j]
            b = j % SC_BUFFERS
            return pltpu.async_copy(rows_v.at[b], yt_hbm.at[k, pl.ds(base + c * SC_CHUNK, SC_CHUNK)], wsem.at[b])

        g = {j: gather(j) for j in range(SC_BUFFERS - 1)}
        w = {}
        for j in range(len(items)):
            g[j].wait()
            w[j] = write(j)
            if j >= 1:
                w[j - 1].wait()
            if j + SC_BUFFERS - 1 < len(items):
                g[j + SC_BUFFERS - 1] = gather(j + SC_BUFFERS - 1)
        w[len(items) - 1].wait()

    return pl.kernel(
        body, mesh=_sc_mesh(),
        out_type=jax.ShapeDtypeStruct((TOP_K, n, ROW_WORDS), jnp.int32),
        scratch_types=[pltpu.VMEM((n_chunks, TOP_K, SC_CHUNK), jnp.int32),
                       pltpu.VMEM((SC_BUFFERS, SC_CHUNK, ROW_WORDS), jnp.int32),
                       pltpu.SemaphoreType.DMA((SC_BUFFERS,)), pltpu.SemaphoreType.DMA((SC_BUFFERS,))],
        compiler_params=pltpu.CompilerParams(use_tc_tiling_on_sc=True),
        name="moe_collect",
    )(ys, pos)


def _expert_gemm_kernel(plan_ref, after_ref, xs_ref, wg_ref, wu_ref, wd_ref, ys_ref, wgu_scr, wd_scr):
    del after_ref
    i = pl.program_id(0)
    e = plan_ref[0, i]
    prev = plan_ref[0, jnp.maximum(i - 1, 0)]
    live = plan_ref[1, i]

    @pl.when((i == 0) | (e != prev))
    def _():
        wgu_scr[:, 0:D_EXPERT] = wg_ref[...].astype(BF16)
        wgu_scr[:, D_EXPERT:2 * D_EXPERT] = wu_ref[...].astype(BF16)
        wd_scr[...] = wd_ref[...].astype(BF16)

    def swiglu_rows(n_rows):
        lo, hi = _unpack_bf16_pairs(xs_ref[0:n_rows, :])
        gu = _dot(lo.astype(BF16), wgu_scr[0:ROW_WORDS, :]) + _dot(hi.astype(BF16), wgu_scr[ROW_WORDS:D_MODEL, :])
        h = _silu(gu[:, 0:D_EXPERT]) * gu[:, D_EXPERT:2 * D_EXPERT]
        y = _dot(h.astype(BF16), wd_scr[...])
        ys_ref[0:n_rows, :] = _pack_bf16_pairs_native(y)

    @pl.when((i < plan_ref[2, 0]) & (live > ROW_TILE // 2))
    def _():
        swiglu_rows(ROW_TILE)

    @pl.when((i < plan_ref[2, 0]) & (live <= ROW_TILE // 2))
    def _():
        swiglu_rows(ROW_TILE // 2)


def _expert_gemm_call(after, plan, xs, w_e_gate, w_e_up, w_e_down):
    n_slots = xs.shape[0]
    n_row_tiles = n_slots // ROW_TILE
    rows = pl.BlockSpec((ROW_TILE, ROW_WORDS), lambda i, plan: (jnp.minimum(i, plan[2, 0] - 1), 0))
    expert = lambda i, plan: (plan[0, i], 0, 0)
    return pl.pallas_call(
        _expert_gemm_kernel,
        grid_spec=pltpu.PrefetchScalarGridSpec(
            num_scalar_prefetch=1,
            grid=(n_row_tiles,),
            in_specs=[
                pl.BlockSpec(memory_space=pl.ANY),
                rows,
                pl.BlockSpec((None, D_MODEL, D_EXPERT), expert),
                pl.BlockSpec((None, D_MODEL, D_EXPERT), expert),
                pl.BlockSpec((None, D_EXPERT, D_MODEL), expert),
            ],
            out_specs=rows,
            scratch_shapes=[pltpu.VMEM((D_MODEL, 2 * D_EXPERT), BF16), pltpu.VMEM((D_EXPERT, D_MODEL), BF16)],
        ),
        out_shape=jax.ShapeDtypeStruct((n_slots, ROW_WORDS), jnp.int32),
        compiler_params=pltpu.CompilerParams(dimension_semantics=("arbitrary",), vmem_limit_bytes=VMEM_LIMIT),
        name="moe_experts",
    )(plan, after, xs, w_e_gate, w_e_up, w_e_down)


def _moe_out_kernel(yt_ref, w_ref, t_ref, x1_ref, mod_ref, post2_ref, wsg_ref, wsu_ref, wsd_ref, o_ref):
    lo, hi = _unpack_bf16_pairs(t_ref[...])
    lo = lo.astype(BF16)
    hi = hi.astype(BF16)
    g = _dot(lo, wsg_ref[0:ROW_WORDS, :]) + _dot(hi, wsg_ref[ROW_WORDS:D_MODEL, :])
    u = _dot(lo, wsu_ref[0:ROW_WORDS, :]) + _dot(hi, wsu_ref[ROW_WORDS:D_MODEL, :])
    shared = _dot((_silu(g) * u).astype(BF16), wsd_ref[...])
    acc_lo = shared[:, 0:ROW_WORDS]
    acc_hi = shared[:, ROW_WORDS:D_MODEL]
    for k in range(TOP_K):
        y_lo, y_hi = _unpack_bf16_pairs(yt_ref[k])
        wk = w_ref[:, k:k + 1]
        acc_lo = acc_lo + wk * y_lo
        acc_hi = acc_hi + wk * y_hi
    moe = jnp.concatenate([acc_lo, acc_hi], axis=-1)
    o_ref[...] = x1_ref[...] + mod_ref[5:6, :] * _rms_norm(moe, post2_ref[...])


def _moe_out_into_kernel(prev_ref, *refs):
    del prev_ref
    _moe_out_kernel(*refs)


def _moe_out_call(prev_out, n_total, b0, yt, w, t, x1, mod_x, post2_g, wsg_bf, wsu_bf, wsd_bf, seq):
    n = t.shape[0]
    tiles_per_seq = seq // TOUT
    first_tile = b0 * tiles_per_seq
    const = lambda i: (0, 0)
    in_specs = [
        pl.BlockSpec((TOP_K, TOUT, ROW_WORDS), lambda i: (0, i, 0)),
        pl.BlockSpec((TOUT, TOP_K), lambda i: (i, 0)),
        pl.BlockSpec((TOUT, ROW_WORDS), lambda i: (i, 0)),
        pl.BlockSpec((TOUT, D_MODEL), lambda i: (i, 0)),
        pl.BlockSpec((None, 6, D_MODEL), lambda i: (b0 + i // tiles_per_seq, 0, 0)),
        pl.BlockSpec((1, D_MODEL), const),
        pl.BlockSpec((D_MODEL, D_EXPERT), const),
        pl.BlockSpec((D_MODEL, D_EXPERT), const),
        pl.BlockSpec((D_EXPERT, D_MODEL), const),
    ]
    args = (yt, w, t, x1, mod_x, post2_g, wsg_bf, wsu_bf, wsd_bf)
    aliased = prev_out is not None
    return pl.pallas_call(
        _moe_out_into_kernel if aliased else _moe_out_kernel,
        grid=(n // TOUT,),
        in_specs=([pl.BlockSpec(memory_space=pl.ANY)] if aliased else []) + in_specs,
        out_specs=pl.BlockSpec((TOUT, D_MODEL), lambda i: (first_tile + i, 0)),
        out_shape=jax.ShapeDtypeStruct((n_total, D_MODEL), F32),
        input_output_aliases={0: 0} if aliased else {},
        compiler_params=pltpu.CompilerParams(dimension_semantics=("arbitrary",), vmem_limit_bytes=VMEM_LIMIT),
        name="moe_out",
    )(*(((prev_out,) if aliased else ()) + args))


def _moe_dispatch(after, hx2w, idx_t):
    n = hx2w.shape[0]
    n_slots = n * TOP_K + N_EXPERTS * ROW_TILE
    pos_t, plan = _positions_call(after, idx_t, n_slots // ROW_TILE)
    pos = pos_t.reshape(TOP_K, SC_WORKERS, n // (SC_WORKERS * SC_CHUNK), SC_CHUNK).transpose(1, 2, 0, 3)
    return _dispatch_call(hx2w, pos, n_slots), pos, plan


def kernel(x, c, ctx, c_ctx, w_mod, b_mod, pre1_g, post1_g, pre2_g, post2_g, w_in, w_dw, b_dw, ln_conv_g, ln_conv_b, w_conv_out, w_sc, b_sc, w_rg_a, b_rg_a, w_rg_x, b_rg_x, lru_lambda, w_rnn_out, w_out, w_router, router_bias, w_e_gate, w_e_up, w_e_down, w_s_gate, w_s_up, w_s_down):
    assert w_mod.shape[0] == 1, "single-layer block"
    bsz, seq, d = x.shape
    n = bsz * seq
    assert d == D_MODEL and seq % TM == 0 and seq % TOUT == 0 and bsz + 1 <= SUBLANES
    assert TOKEN_GROUPS == 2 and bsz % TOKEN_GROUPS == 0 and (n // TOKEN_GROUPS) % (SC_WORKERS * SC_CHUNK) == 0
    assert (n // TOKEN_GROUPS) % POS_TILE == 0

    cc = jnp.zeros((SUBLANES, D_MODEL), F32).at[:bsz].set(c).at[bsz].set(c_ctx)
    mod = _mod_call(cc, w_mod[0], b_mod)
    mod_x = mod[:bsz].reshape(bsz, 6, D_MODEL)
    mod_c = mod[bsz].reshape(6, D_MODEL)

    w_in_bf = w_in[0].astype(BF16)
    wa_bf = w_rg_a[0].astype(BF16)
    wx_bf = w_rg_x[0].astype(BF16)

    h0 = _ctx_call(ctx, mod_c, pre1_g, w_in_bf, w_sc[0], b_sc[0], wa_bf, b_rg_a[0], wx_bf, b_rg_x[0],
                   lru_lambda[0])

    wco_bf, wro_bf, wout_bf = w_conv_out[0].astype(BF16), w_rnn_out[0].astype(BF16), w_out[0].astype(BF16)
    wsg_bf, wsu_bf, wsd_bf = w_s_gate[0].astype(BF16), w_s_up[0].astype(BF16), w_s_down[0].astype(BF16)
    w_router_t, rbias = w_router[0].T, router_bias.reshape(N_EXPERTS, 1)
    gsz = bsz // TOKEN_GROUPS
    gn = gsz * seq
    none = jnp.zeros((TOP_K, LANES), jnp.int32)

    def fwd(g, after):
        return _mixer_fwd_call(g * gsz, gsz, after, x, mod_x, pre1_g, w_in_bf, w_dw[0], b_dw, ln_conv_g,
                               ln_conv_b, wco_bf, w_sc[0], b_sc[0], wa_bf, b_rg_a[0], wx_bf, b_rg_x[0],
                               lru_lambda[0], h0)

    def bwd(g, after, acts):
        gaya, gb, gbr, urnn, hf = acts
        x1, hx2w, idx_t, w_t = _mixer_bwd_call(
            g * gsz, gsz, after, urnn, hf, gbr, gaya, gb, x, mod_x, w_sc[0], b_sc[0], wa_bf, b_rg_a[0], wx_bf,
            b_rg_x[0], lru_lambda[0], h0, wro_bf, wout_bf, post1_g, pre2_g, w_router_t, rbias)
        return x1.reshape(gn, D_MODEL), hx2w.reshape(gn, ROW_WORDS), idx_t, w_t

    def experts(after, plan, xs):
        return _expert_gemm_call(after, plan, xs, w_e_gate[0], w_e_up[0], w_e_down[0])

    def finish(g, prev_out, yt, w_t, hx2w, x1):
        return _moe_out_call(prev_out, n, g * gsz, yt, w_t.T, hx2w, x1, mod_x, post2_g, wsg_bf, wsu_bf, wsd_bf, seq)

    x1_a, hx2w_a, idx_a, w_a = bwd(0, none, fwd(0, none))
    xs_a, pos_a, plan_a = _moe_dispatch(none, hx2w_a, idx_a)
    acts_b = fwd(1, idx_a)
    ys_a = experts(acts_b[4], plan_a, xs_a)
    yt_a = _collect_call(ys_a, pos_a, gn)
    x1_b, hx2w_b, idx_b, w_b = bwd(1, ys_a, acts_b)
    xs_b, pos_b, plan_b = _moe_dispatch(yt_a, hx2w_b, idx_b)
    out = finish(0, None, yt_a, w_a, hx2w_a, x1_a)
    ys_b = experts(none, plan_b, xs_b)
    yt_b = _collect_call(ys_b, pos_b, gn)
    out = finish(1, out, yt_b, w_b, hx2w_b, x1_b)
    return out.reshape(bsz, seq, D_MODEL)
```

```python
import jax
import jax.numpy as jnp
from jax import lax
from jax.experimental import pallas as pl
from jax.experimental.pallas import tpu as pltpu
from jax.experimental.pallas import tpu_sc as plsc

F32 = jnp.float32
BF16 = jnp.bfloat16

D_MODEL = 1024
GRID_W = 64
CONV_WIDTH = 31
CONV_HALF = CONV_WIDTH // 2
SHORT_CONV = 4
RNN_HEADS = 4
RNN_BLOCK = D_MODEL // RNN_HEADS
LRU_C = 8.0
N_EXPERTS = 64
N_GROUPS = 8
GROUP_SIZE = N_EXPERTS // N_GROUPS
TOPK_GROUPS = 4
TOP_K = 8
D_EXPERT = 256
ROUTED_SCALE = 2.5
EPS = 1e-6

SUBLANES = 8
TM = 512
ROWS_PER_TILE = TM // GRID_W
PAD = 16
ROW_STRIDE = GRID_W + PAD
UPAD_ROWS = ROWS_PER_TILE * ROW_STRIDE + PAD
SEG = TM // SUBLANES
HALO = (SHORT_CONV - 1) * SUBLANES
CONV_LANES = 256
LANES = 128
ROW_WORDS = D_MODEL // 2
ROW_TILE = 1024
POS_TILE = 512
TOUT = 512
TOKEN_GROUPS = 2
V7X_SC_CORES = 2
V7X_SC_SUBCORES = 16
SC_WORKERS = V7X_SC_CORES * V7X_SC_SUBCORES
SC_CHUNK = 64
SC_BUFFERS = 2
VMEM_LIMIT = 58 * 1024 * 1024


def _sigmoid(x):
    return 0.5 * (jnp.tanh(0.5 * x) + 1.0)


def _silu(x):
    return x * _sigmoid(x)


def _gelu_tanh(x):
    return 0.5 * x * (1.0 + jnp.tanh(0.7978845608028654 * (x + 0.044715 * (x * x * x))))


def _rms_norm(x, g):
    return x * lax.rsqrt(jnp.mean(x * x, axis=-1, keepdims=True) + EPS) * g


def _dot(a, b):
    return jnp.dot(a, b, preferred_element_type=F32)


def _dot_nt_split(a, b):
    a_hi, b_hi = a.astype(BF16), b.astype(BF16)
    a_lo = (a - a_hi.astype(F32)).astype(BF16)
    b_lo = (b - b_hi.astype(F32)).astype(BF16)
    nt = (((1,), (1,)), ((), ()))
    return (lax.dot_general(a_hi, b_hi, nt, preferred_element_type=F32)
            + (lax.dot_general(a_hi, b_lo, nt, preferred_element_type=F32)
               + lax.dot_general(a_lo, b_hi, nt, preferred_element_type=F32)))


def _pack_bf16_pairs(x):
    half = x.shape[-1] // 2
    lo = lax.bitcast_convert_type(x[:, :half].astype(BF16).astype(F32), jnp.uint32)
    hi = lax.bitcast_convert_type(x[:, half:].astype(BF16).astype(F32), jnp.uint32)
    return lax.bitcast_convert_type(hi | (lo >> 16), jnp.int32)


def _pack_bf16_pairs_native(x):
    half = x.shape[-1] // 2
    packed = pltpu.pack_elementwise([x[:, :half], x[:, half:]], packed_dtype=BF16)
    return lax.bitcast_convert_type(packed, jnp.int32)


def _unpack_bf16_pairs(words):
    u = lax.bitcast_convert_type(words, jnp.uint32)
    return (lax.bitcast_convert_type(u << 16, F32),
            lax.bitcast_convert_type(u & jnp.uint32(0xFFFF0000), F32))


def _log_decay(lam):
    return LRU_C * (jnp.minimum(lam, 0.0) - jnp.log1p(jnp.exp(-jnp.abs(lam))))


def _rglru_coeffs(v, wa_ref, ba, wx_ref, bx, c_lam, a_ref, b_ref):
    vb = v.astype(BF16)
    for h in range(RNN_HEADS):
        cs = slice(h * RNN_BLOCK, (h + 1) * RNN_BLOCK)
        vh = vb[:, cs]
        r = _sigmoid(_dot(vh, wa_ref[h]) + ba[:, cs])
        i = _sigmoid(_dot(vh, wx_ref[h]) + bx[:, cs])
        log_a = c_lam[:, cs] * r
        a = jnp.exp(log_a)
        a_ref[:, cs] = a
        b_ref[:, cs] = jnp.sqrt(jnp.tanh(-log_a) * (1.0 + a * a)) * (i * v[:, cs])


def _scan_tile(a_ref, b_ref, carry, n_rows, reverse):
    row = lax.broadcasted_iota(jnp.int32, (SUBLANES, D_MODEL), 0)
    n_groups = n_rows // SUBLANES

    def body(j, carry):
        g = (n_groups - 1 - j) if reverse else j
        off = pl.multiple_of(g * SUBLANES, SUBLANES)
        a = a_ref[pl.ds(off, SUBLANES), :]
        b = b_ref[pl.ds(off, SUBLANES), :]
        for s in (1, 2, 4):
            keep = (row < SUBLANES - s) if reverse else (row >= s)
            shift = (SUBLANES - s) if reverse else s
            a_sh = jnp.where(keep, pltpu.roll(a, shift, 0), 1.0)
            b_sh = jnp.where(keep, pltpu.roll(b, shift, 0), 0.0)
            b = a * b_sh + b
            a = a * a_sh
        h = a * carry + b
        b_ref[pl.ds(off, SUBLANES), :] = h
        last = h[0:1, :] if reverse else h[SUBLANES - 1:SUBLANES, :]
        return jnp.broadcast_to(last, (SUBLANES, D_MODEL))

    return lax.fori_loop(0, n_groups, body, carry, unroll=2)


def _segment_perm(to_natural):
    r = lax.broadcasted_iota(jnp.int32, (TM, TM), 0)
    c = lax.broadcasted_iota(jnp.int32, (TM, TM), 1)
    if to_natural:
        src = (r & (SEG - 1)) * SUBLANES + lax.shift_right_logical(r, SEG.bit_length() - 1)
    else:
        src = (r & (SUBLANES - 1)) * SEG + lax.shift_right_logical(r, SUBLANES.bit_length() - 1)
    return jnp.where(c == src, 1.0, 0.0).astype(BF16)


def _scan_segments(a_ref, b_ref, carry, reverse):
    row = lax.broadcasted_iota(jnp.int32, (SUBLANES, D_MODEL), 0)

    def body(s, hp):
        h, prod = hp
        off = pl.multiple_of(((SEG - 1 - s) if reverse else s) * SUBLANES, SUBLANES)
        a = a_ref[pl.ds(off, SUBLANES), :]
        h = a * h + b_ref[pl.ds(off, SUBLANES), :]
        prod = a * prod
        b_ref[pl.ds(off, SUBLANES), :] = h
        a_ref[pl.ds(off, SUBLANES), :] = prod
        return h, prod

    b, a = lax.fori_loop(0, SEG, body, (jnp.zeros((SUBLANES, D_MODEL), F32), jnp.ones((SUBLANES, D_MODEL), F32)),
                         unroll=4)
    for s in (1, 2, 4):
        keep = (row < SUBLANES - s) if reverse else (row >= s)
        shift = (SUBLANES - s) if reverse else s
        a_sh = jnp.where(keep, pltpu.roll(a, shift, 0), 1.0)
        b_sh = jnp.where(keep, pltpu.roll(b, shift, 0), 0.0)
        b = a * b_sh + b
        a = a * a_sh
    leave = a * carry + b
    if reverse:
        enter = jnp.where(row < SUBLANES - 1, pltpu.roll(leave, SUBLANES - 1, 0), carry)
        last = leave[0:1, :]
    else:
        enter = jnp.where(row >= 1, pltpu.roll(leave, 1, 0), carry)
        last = leave[SUBLANES - 1:SUBLANES, :]
    h = b_ref[...].reshape(SEG, SUBLANES, D_MODEL) + a_ref[...].reshape(SEG, SUBLANES, D_MODEL) * enter[None]
    return h.reshape(TM, D_MODEL), jnp.broadcast_to(last, (SUBLANES, D_MODEL))


def _short_conv_segments(uext_ref, wsc_ref, bsc_ref):
    v = bsc_ref[...] + wsc_ref[0:1, :] * uext_ref[0:TM, :]
    for k in range(1, SHORT_CONV):
        v = v + wsc_ref[k:k + 1, :] * uext_ref[k * SUBLANES:k * SUBLANES + TM, :]
    return v


def _mod_kernel(c_ref, w_ref, b_ref, o_ref):
    o_ref[...] = jnp.dot(_silu(c_ref[...]), w_ref[...], preferred_element_type=F32,
                         precision=lax.Precision.HIGHEST) + b_ref[...]


def _mod_call(cc, w_mod, b_mod):
    n_mod = w_mod.shape[1]
    return pl.pallas_call(
        _mod_kernel,
        grid=(n_mod // D_MODEL,),
        in_specs=[
            pl.BlockSpec((SUBLANES, D_MODEL), lambda j: (0, 0)),
            pl.BlockSpec((D_MODEL, D_MODEL), lambda j: (0, j)),
            pl.BlockSpec((1, D_MODEL), lambda j: (0, j)),
        ],
        out_specs=pl.BlockSpec((SUBLANES, D_MODEL), lambda j: (0, j)),
        out_shape=jax.ShapeDtypeStruct((SUBLANES, n_mod), F32),
        name="mod",
    )(cc, w_mod, b_mod)


def _ctx_kernel(ctx_ref, mod_ref, g_ref, w_ref, wsc_ref, bsc_ref, wa_ref, ba_ref, wx_ref, bx_ref,
                lam_ref, o_ref, uext_ref, a_ref, b_ref):
    n = ctx_ref.shape[0]
    hc = _rms_norm(ctx_ref[...], g_ref[...]) * (1.0 + mod_ref[1:2, :]) + mod_ref[0:1, :]
    u = _dot(hc.astype(BF16), w_ref[...])
    zeros8 = jnp.zeros((SUBLANES, D_MODEL), F32)
    uext_ref[0:SUBLANES, :] = zeros8
    uext_ref[SUBLANES:SUBLANES + n, :] = u
    uext_ref[SUBLANES + n:2 * SUBLANES + n, :] = zeros8
    for d in range(2):
        v = jnp.broadcast_to(bsc_ref[d:d + 1, :], (n, D_MODEL))
        for k in range(SHORT_CONV):
            start = SUBLANES + k - (SHORT_CONV - 1) * (1 - d)
            v = v + wsc_ref[d, k:k + 1, :] * uext_ref[start:start + n, :]
        _rglru_coeffs(v, wa_ref.at[d], ba_ref[d:d + 1, :], wx_ref.at[d], bx_ref[d:d + 1, :],
                      _log_decay(lam_ref[d:d + 1, :]), a_ref, b_ref)
        final = _scan_tile(a_ref, b_ref, zeros8, n, reverse=(d == 1))
        o_ref[d:d + 1, :] = final[0:1, :]


def _ctx_call(ctx, mod_c, pre1_g, w_in_bf, w_sc, b_sc, wa_bf, b_rg_a, wx_bf, b_rg_x, lam):
    bsz, n, _ = ctx.shape
    const2 = lambda b: (0, 0)
    const3 = lambda b: (0, 0, 0)
    const4 = lambda b: (0, 0, 0, 0)
    return pl.pallas_call(
        _ctx_kernel,
        grid=(bsz,),
        in_specs=[
            pl.BlockSpec((None, n, D_MODEL), lambda b: (b, 0, 0)),
            pl.BlockSpec((6, D_MODEL), const2),
            pl.BlockSpec((1, D_MODEL), const2),
            pl.BlockSpec((D_MODEL, D_MODEL), lambda b: (0, 2)),
            pl.BlockSpec((2, SHORT_CONV, D_MODEL), const3),
            pl.BlockSpec((2, D_MODEL), const2),
            pl.BlockSpec((2, RNN_HEADS, RNN_BLOCK, RNN_BLOCK), const4),
            pl.BlockSpec((2, D_MODEL), const2),
            pl.BlockSpec((2, RNN_HEADS, RNN_BLOCK, RNN_BLOCK), const4),
            pl.BlockSpec((2, D_MODEL), const2),
            pl.BlockSpec((2, D_MODEL), const2),
        ],
        out_specs=pl.BlockSpec((None, 2, D_MODEL), lambda b: (b, 0, 0)),
        out_shape=jax.ShapeDtypeStruct((bsz, 2, D_MODEL), F32),
        scratch_shapes=[
            pltpu.VMEM((n + 2 * SUBLANES, D_MODEL), F32),
            pltpu.VMEM((n, D_MODEL), F32),
            pltpu.VMEM((n, D_MODEL), F32),
        ],
        name="ctx",
    )(ctx, mod_c, pre1_g, w_in_bf, w_sc, b_sc, wa_bf, b_rg_a, wx_bf, b_rg_x, lam)


def _mixer_fwd_kernel(after_ref, x_ref, mod_ref, g_ref, win_ref, wdw_ref, bdw_ref, lng_ref, lnb_ref, wco_ref,
                      wsc_ref, bsc_ref, wa_ref, ba_ref, wx_ref, bx_ref, lam_ref, h0_ref,
                      gaya_ref, gb_ref, gbr_ref, urnn_ref, hf_ref,
                      upad_ref, ush_ref, wb_ref, cv_ref, uext_ref, a_ref, b_ref, carry_ref, halo_ref, perm_ref):
    del after_ref
    j = pl.program_id(1)

    @pl.when(j == 0)
    def _():
        carry_ref[...] = jnp.broadcast_to(h0_ref[0:1, :], (SUBLANES, D_MODEL))
        halo_ref[...] = jnp.zeros((SUBLANES, D_MODEL), F32)
        perm_ref[...] = _segment_perm(to_natural=False)
        zeros_pad = jnp.zeros((PAD, D_MODEL), F32)
        for r in range(ROWS_PER_TILE + 1):
            upad_ref[r * ROW_STRIDE:r * ROW_STRIDE + PAD, :] = zeros_pad
        for k in range(CONV_WIDTH):
            wb_ref[k] = jnp.broadcast_to(wdw_ref[k:k + 1, :], (SUBLANES, D_MODEL))

    hx = (_rms_norm(x_ref[...], g_ref[...]) * (1.0 + mod_ref[1:2, :]) + mod_ref[0:1, :]).astype(BF16)

    for c in range(D_MODEL // CONV_LANES):
        cs = slice(c * CONV_LANES, (c + 1) * CONV_LANES)
        u = (_dot(hx, win_ref[:, c * CONV_LANES:(c + 1) * CONV_LANES])
             * _sigmoid(_dot(hx, win_ref[:, D_MODEL + c * CONV_LANES:D_MODEL + (c + 1) * CONV_LANES])))
        for r in range(ROWS_PER_TILE):
            upad_ref[PAD + r * ROW_STRIDE:PAD + r * ROW_STRIDE + GRID_W, cs] = u[r * GRID_W:(r + 1) * GRID_W, :]
    vregs_per_row = GRID_W // SUBLANES
    hxp = _dot(perm_ref[...], hx).astype(BF16)
    for c in range(D_MODEL // CONV_LANES):
        for cb in range(D_MODEL // CONV_LANES):
            cols = slice(cb * CONV_LANES, (cb + 1) * CONV_LANES)
            w0 = (5, 3, 4, 2)[c] * D_MODEL + cb * CONV_LANES
            y = _dot(hxp, win_ref[:, w0:w0 + CONV_LANES])
            if c == 0:
                gb_ref[:, cols] = _sigmoid(y).astype(BF16)
            elif c == 1:
                gbr_ref[:, cols] = _gelu_tanh(y).astype(BF16)
            elif c == 2:
                a_ref[:, cols] = _sigmoid(y)
            else:
                urnn_ref[:, cols] = y.astype(BF16)
                uext_ref[HALO:HALO + TM, cols] = y
        cs = slice(c * CONV_LANES, (c + 1) * CONV_LANES)
        xpad = upad_ref[:, cs]
        for s in range(1, SUBLANES):
            ush_ref[s - 1] = pltpu.roll(xpad, UPAD_ROWS - s, 0)
        for r in range(ROWS_PER_TILE):
            acc = jnp.broadcast_to(bdw_ref[:, cs].reshape(1, 1, CONV_LANES), (vregs_per_row, SUBLANES, CONV_LANES))
            for k in range(CONV_WIDTH):
                q, s = divmod(r * ROW_STRIDE + PAD - CONV_HALF + k, SUBLANES)
                rows = slice(q * SUBLANES, q * SUBLANES + GRID_W)
                win = upad_ref[rows, cs] if s == 0 else ush_ref[s - 1, rows, :]
                acc = acc + wb_ref[k, :, cs] * win.reshape(vregs_per_row, SUBLANES, CONV_LANES)
            cv_ref[r * GRID_W:(r + 1) * GRID_W, cs] = acc.reshape(GRID_W, CONV_LANES)
    cv = cv_ref[...]
    cvc = cv - jnp.mean(cv, axis=-1, keepdims=True)
    cvn = cvc * lax.rsqrt(jnp.mean(cvc * cvc, axis=-1, keepdims=True) + EPS) * lng_ref[...] + lnb_ref[...]
    y_a = _dot(_dot(perm_ref[...], _silu(cvn).astype(BF16)).astype(BF16), wco_ref[...])

    gaya_ref[...] = (a_ref[...] * y_a).astype(BF16)

    row8 = lax.broadcasted_iota(jnp.int32, (SUBLANES, D_MODEL), 0)
    for jj in range(SHORT_CONV - 1):
        late = uext_ref[HALO + (SEG - (SHORT_CONV - 1) + jj) * SUBLANES:HALO + (SEG - (SHORT_CONV - 1) + jj + 1) * SUBLANES, :]
        prev_tile = jnp.broadcast_to(halo_ref[jj:jj + 1, :], (SUBLANES, D_MODEL))
        uext_ref[jj * SUBLANES:(jj + 1) * SUBLANES, :] = jnp.where(row8 == 0, prev_tile, pltpu.roll(late, 1, 0))
        halo_ref[jj:jj + 1, :] = late[SUBLANES - 1:SUBLANES, :]
    v = _short_conv_segments(uext_ref, wsc_ref, bsc_ref)
    _rglru_coeffs(v, wa_ref, ba_ref[...], wx_ref, bx_ref[...], _log_decay(lam_ref[...]), a_ref, b_ref)
    h_f, carry_ref[...] = _scan_segments(a_ref, b_ref, carry_ref[...], reverse=False)
    hf_ref[...] = h_f.astype(BF16)


def _resident(shape):
    nd = len(shape)
    return pl.BlockSpec(shape, lambda b, j: (0,) * nd, pipeline_mode=pl.Buffered(1))


def _mixer_fwd_call(b0, bsz, after, x, mod_x, pre1_g, w_in_bf, w_dw, b_dw, ln_g, ln_b, wco_bf,
                    w_sc, b_sc, wa_bf, b_rg_a, wx_bf, b_rg_x, lam, h0):
    seq = x.shape[1]
    nt = seq // TM
    tile = pl.BlockSpec((None, TM, D_MODEL), lambda b, j: (b, j, 0))
    act = jax.ShapeDtypeStruct((bsz, seq, D_MODEL), BF16)
    head_w = pl.BlockSpec((None, RNN_HEADS, RNN_BLOCK, RNN_BLOCK), lambda b, j: (0, 0, 0, 0),
                          pipeline_mode=pl.Buffered(1))
    dir_row = pl.BlockSpec((None, 1, D_MODEL), lambda b, j: (0, 0, 0), pipeline_mode=pl.Buffered(1))
    return pl.pallas_call(
        _mixer_fwd_kernel,
        grid=(bsz, nt),
        in_specs=[
            pl.BlockSpec(memory_space=pl.ANY),
            pl.BlockSpec((None, TM, D_MODEL), lambda b, j: (b0 + b, j, 0)),
            pl.BlockSpec((None, 6, D_MODEL), lambda b, j: (b0 + b, 0, 0)),
            _resident((1, D_MODEL)),
            _resident((D_MODEL, 6 * D_MODEL)),
            _resident((CONV_WIDTH, D_MODEL)),
            _resident((1, D_MODEL)),
            _resident((1, D_MODEL)),
            _resident((1, D_MODEL)),
            _resident((D_MODEL, D_MODEL)),
            pl.BlockSpec((None, SHORT_CONV, D_MODEL), lambda b, j: (0, 0, 0), pipeline_mode=pl.Buffered(1)),
            dir_row, head_w, dir_row, head_w, dir_row, dir_row,
            pl.BlockSpec((None, 2, D_MODEL), lambda b, j: (b0 + b, 0, 0)),
        ],
        out_specs=[tile] * 5,
        out_shape=[act] * 5,
        scratch_shapes=[
            pltpu.VMEM((UPAD_ROWS, D_MODEL), F32),
            pltpu.VMEM((SUBLANES - 1, UPAD_ROWS, CONV_LANES), F32),
            pltpu.VMEM((CONV_WIDTH, SUBLANES, D_MODEL), F32),
            pltpu.VMEM((TM, D_MODEL), F32),
            pltpu.VMEM((TM + HALO, D_MODEL), F32),
            pltpu.VMEM((TM, D_MODEL), F32),
            pltpu.VMEM((TM, D_MODEL), F32),
            pltpu.VMEM((SUBLANES, D_MODEL), F32),
            pltpu.VMEM((SUBLANES, D_MODEL), F32),
            pltpu.VMEM((TM, TM), BF16),
        ],
        compiler_params=pltpu.CompilerParams(
            dimension_semantics=("arbitrary", "arbitrary"), vmem_limit_bytes=VMEM_LIMIT),
        name="mixer_fwd",
    )(after, x, mod_x, pre1_g, w_in_bf, w_dw, b_dw, ln_g, ln_b, wco_bf,
      w_sc, b_sc.reshape(2, 1, D_MODEL), wa_bf, b_rg_a.reshape(2, 1, D_MODEL), wx_bf,
      b_rg_x.reshape(2, 1, D_MODEL), lam.reshape(2, 1, D_MODEL), h0)


def _route(logits_t, bias):
    t = logits_t.shape[1]
    scores = _sigmoid(logits_t)
    sel = scores + bias
    neg_inf = jnp.float32(-jnp.inf)

    sel3 = sel.reshape(N_GROUPS, GROUP_SIZE, t)
    within = lax.broadcasted_iota(jnp.int32, sel3.shape, 1)
    m1 = jnp.max(sel3, axis=1, keepdims=True)
    first = jnp.min(jnp.where(sel3 == m1, within, GROUP_SIZE), axis=1, keepdims=True)
    m2 = jnp.max(jnp.where(within == first, neg_inf, sel3), axis=1, keepdims=True)
    gscore = (m1 + m2).reshape(N_GROUPS, t)

    gidx = lax.broadcasted_iota(jnp.int32, gscore.shape, 0)
    rank = jnp.zeros(gscore.shape, jnp.int32)
    for g in range(N_GROUPS):
        other = gscore[g:g + 1, :]
        beats = jnp.where(other > gscore, 1, jnp.where((other == gscore) & (gidx > g), 1, 0))
        rank = rank + beats
    gkeep = (rank < TOPK_GROUPS).reshape(N_GROUPS, 1, t)
    masked = jnp.where(gkeep, sel3, neg_inf).reshape(N_EXPERTS, t)

    eidx = lax.broadcasted_iota(jnp.int32, masked.shape, 0)
    picks, weights = [], []
    for _ in range(TOP_K):
        m = jnp.max(masked, axis=0, keepdims=True)
        first = jnp.min(jnp.where(masked == m, eidx, N_EXPERTS), axis=0, keepdims=True)
        pick = eidx == first
        picks.append(first)
        weights.append(jnp.sum(jnp.where(pick, scores, 0.0), axis=0, keepdims=True))
        masked = jnp.where(pick, neg_inf, masked)
    idx = jnp.concatenate(picks, axis=0)
    w = jnp.concatenate(weights, axis=0)
    return idx, ROUTED_SCALE * w / jnp.sum(w, axis=0, keepdims=True)


def _mixer_bwd_kernel(after_ref, urnn_ref, hf_ref, gbr_ref, gaya_ref, gb_ref, x_ref, mod_ref,
                      wsc_ref, bsc_ref, wa_ref, ba_ref, wx_ref, bx_ref, lam_ref, h0_ref,
                      wro_ref, wout_ref, post1_ref, pre2_ref, wrt_ref, rbias_ref,
                      x1_ref, hx2w_ref, idx_ref, w_ref,
                      uext_ref, a_ref, b_ref, carry_ref, halo_ref, perm_ref):
    del after_ref
    j = pl.program_id(1)

    @pl.when(j == 0)
    def _():
        carry_ref[...] = jnp.broadcast_to(h0_ref[1:2, :], (SUBLANES, D_MODEL))
        halo_ref[...] = jnp.zeros((SUBLANES, D_MODEL), F32)
        perm_ref[...] = _segment_perm(to_natural=True)

    uext_ref[0:TM, :] = urnn_ref[...].astype(F32)
    row8 = lax.broadcasted_iota(jnp.int32, (SUBLANES, D_MODEL), 0)
    for jj in range(SHORT_CONV - 1):
        early = uext_ref[jj * SUBLANES:(jj + 1) * SUBLANES, :]
        next_tile = jnp.broadcast_to(halo_ref[jj:jj + 1, :], (SUBLANES, D_MODEL))
        uext_ref[TM + jj * SUBLANES:TM + (jj + 1) * SUBLANES, :] = jnp.where(
            row8 == SUBLANES - 1, next_tile, pltpu.roll(early, SUBLANES - 1, 0))
        halo_ref[jj:jj + 1, :] = early[0:1, :]
    v = _short_conv_segments(uext_ref, wsc_ref, bsc_ref)
    _rglru_coeffs(v, wa_ref, ba_ref[...], wx_ref, bx_ref[...], _log_decay(lam_ref[...]), a_ref, b_ref)
    h_b, carry_ref[...] = _scan_segments(a_ref, b_ref, carry_ref[...], reverse=True)

    h_sum = hf_ref[...].astype(F32) + h_b
    y_b = _dot((gbr_ref[...].astype(F32) * h_sum).astype(BF16), wro_ref[...])
    mix = gaya_ref[...].astype(F32) + gb_ref[...].astype(F32) * y_b
    out = _dot(_dot(perm_ref[...], mix.astype(BF16)).astype(BF16), wout_ref[...])
    x1 = x_ref[...] + mod_ref[2:3, :] * _rms_norm(out, post1_ref[...])
    x1_ref[...] = x1

    hx2 = _rms_norm(x1, pre2_ref[...]) * (1.0 + mod_ref[4:5, :]) + mod_ref[3:4, :]
    hx2w_ref[...] = _pack_bf16_pairs(hx2)
    logits_t = _dot_nt_split(wrt_ref[...], hx2)
    idx, w = _route(logits_t, rbias_ref[...])
    idx_ref[...] = idx
    w_ref[...] = w


def _mixer_bwd_call(b0, bsz, after, urnn, hf, gbr, gaya, gb, x, mod_x, w_sc, b_sc, wa_bf, b_rg_a, wx_bf, b_rg_x, lam, h0,
                    wro_bf, wout_bf, post1_g, pre2_g, w_router_t, router_bias):
    seq = x.shape[1]
    nt = seq // TM
    rev = lambda b, j: (b, nt - 1 - j, 0)
    tile = pl.BlockSpec((None, TM, D_MODEL), rev)
    head_w = pl.BlockSpec((None, RNN_HEADS, RNN_BLOCK, RNN_BLOCK), lambda b, j: (1, 0, 0, 0),
                          pipeline_mode=pl.Buffered(1))
    dir_row = pl.BlockSpec((None, 1, D_MODEL), lambda b, j: (1, 0, 0), pipeline_mode=pl.Buffered(1))
    return pl.pallas_call(
        _mixer_bwd_kernel,
        grid=(bsz, nt),
        in_specs=[
            pl.BlockSpec(memory_space=pl.ANY),
            tile, tile, tile, tile, tile,
            pl.BlockSpec((None, TM, D_MODEL), lambda b, j: (b0 + b, nt - 1 - j, 0)),
            pl.BlockSpec((None, 6, D_MODEL), lambda b, j: (b0 + b, 0, 0)),
            pl.BlockSpec((None, SHORT_CONV, D_MODEL), lambda b, j: (1, 0, 0), pipeline_mode=pl.Buffered(1)),
            dir_row, head_w, dir_row, head_w, dir_row, dir_row,
            pl.BlockSpec((None, 2, D_MODEL), lambda b, j: (b0 + b, 0, 0)),
            _resident((D_MODEL, D_MODEL)),
            _resident((D_MODEL, D_MODEL)),
            _resident((1, D_MODEL)),
            _resident((1, D_MODEL)),
            _resident((N_EXPERTS, D_MODEL)),
            _resident((N_EXPERTS, 1)),
        ],
        out_specs=[
            pl.BlockSpec((None, TM, D_MODEL), rev),
            pl.BlockSpec((None, TM, ROW_WORDS), rev),
            pl.BlockSpec((TOP_K, TM), lambda b, j: (0, b * nt + nt - 1 - j)),
            pl.BlockSpec((TOP_K, TM), lambda b, j: (0, b * nt + nt - 1 - j)),
        ],
        out_shape=[
            jax.ShapeDtypeStruct((bsz, seq, D_MODEL), F32),
            jax.ShapeDtypeStruct((bsz, seq, ROW_WORDS), jnp.int32),
            jax.ShapeDtypeStruct((TOP_K, bsz * seq), jnp.int32),
            jax.ShapeDtypeStruct((TOP_K, bsz * seq), F32),
        ],
        scratch_shapes=[
            pltpu.VMEM((TM + HALO, D_MODEL), F32),
            pltpu.VMEM((TM, D_MODEL), F32),
            pltpu.VMEM((TM, D_MODEL), F32),
            pltpu.VMEM((SUBLANES, D_MODEL), F32),
            pltpu.VMEM((SUBLANES, D_MODEL), F32),
            pltpu.VMEM((TM, TM), BF16),
        ],
        compiler_params=pltpu.CompilerParams(
            dimension_semantics=("arbitrary", "arbitrary"), vmem_limit_bytes=VMEM_LIMIT),
        name="mixer_bwd",
    )(after, urnn, hf, gbr, gaya, gb, x, mod_x, w_sc, b_sc.reshape(2, 1, D_MODEL), wa_bf,
      b_rg_a.reshape(2, 1, D_MODEL), wx_bf, b_rg_x.reshape(2, 1, D_MODEL), lam.reshape(2, 1, D_MODEL), h0,
      wro_bf, wout_bf, post1_g, pre2_g, w_router_t, router_bias)


def _positions_kernel(after_ref, idx_ref, pos_ref, plan_ref):
    del after_ref
    n = idx_ref.shape[1]
    n_tiles = n // POS_TILE
    eidx = lax.broadcasted_iota(jnp.int32, (N_EXPERTS, POS_TILE), 0)

    def chosen(t):
        idx = idx_ref[:, pl.ds(pl.multiple_of(t * POS_TILE, POS_TILE), POS_TILE)]
        ch = jnp.zeros((N_EXPERTS, POS_TILE), F32)
        for k in range(TOP_K):
            ch = ch + jnp.where(eidx == idx[k:k + 1, :], 1.0, 0.0)
        return idx, ch

    def count_body(t, cnt):
        return cnt + jnp.sum(chosen(t)[1], axis=1, keepdims=True)

    cnt = lax.fori_loop(0, n_tiles, count_body, jnp.zeros((N_EXPERTS, 1), F32))
    padded = jnp.ceil(cnt * (1.0 / ROW_TILE)) * ROW_TILE
    r = lax.broadcasted_iota(jnp.int32, (N_EXPERTS, N_EXPERTS), 0)
    c = lax.broadcasted_iota(jnp.int32, (N_EXPERTS, N_EXPERTS), 1)
    off = jnp.dot(jnp.where(c < r, 1.0, 0.0), jnp.broadcast_to(padded, (N_EXPERTS, LANES)),
                  preferred_element_type=F32, precision=lax.Precision.HIGHEST)[:, 0:1]
    end = off + padded

    n_map = plan_ref.shape[1]
    tstart = lax.broadcasted_iota(jnp.int32, (N_EXPERTS, n_map), 1).astype(F32) * ROW_TILE
    te = jnp.minimum(jnp.sum(jnp.where(end <= tstart, 1, 0), axis=0, keepdims=True), N_EXPERTS - 1)
    emap = lax.broadcasted_iota(jnp.int32, (N_EXPERTS, n_map), 0)
    live_end = jnp.sum(jnp.where(emap == te, off + cnt, 0.0), axis=0, keepdims=True)
    total = jnp.sum(padded, axis=0, keepdims=True)
    plan_ref[0:1, :] = te
    plan_ref[1:2, :] = jnp.clip(live_end - tstart[0:1, :], 0.0, ROW_TILE).astype(jnp.int32)
    plan_ref[2:3, :] = jnp.broadcast_to(total * (1.0 / ROW_TILE), (1, n_map)).astype(jnp.int32)

    row = lax.broadcasted_iota(jnp.int32, (POS_TILE, POS_TILE), 0)
    col = lax.broadcasted_iota(jnp.int32, (POS_TILE, POS_TILE), 1)
    before = jnp.where(row < col, 1.0, 0.0).astype(BF16)

    def pos_body(t, carry):
        idx, ch = chosen(t)
        base = _dot(ch.astype(BF16), before) + (carry + off)
        rows = [jnp.sum(jnp.where(eidx == idx[k:k + 1, :], base, 0.0), axis=0, keepdims=True)
                for k in range(TOP_K)]
        pos_ref[:, pl.ds(pl.multiple_of(t * POS_TILE, POS_TILE), POS_TILE)] = (
            jnp.concatenate(rows, axis=0).astype(jnp.int32))
        return carry + jnp.sum(ch, axis=1, keepdims=True)

    lax.fori_loop(0, n_tiles, pos_body, jnp.zeros((N_EXPERTS, 1), F32))


def _positions_call(after, idx_t, n_row_tiles):
    n = idx_t.shape[1]
    n_map = -(-n_row_tiles // LANES) * LANES
    return pl.pallas_call(
        _positions_kernel,
        in_specs=[pl.BlockSpec(memory_space=pl.ANY), pl.BlockSpec(memory_space=pltpu.VMEM)],
        out_shape=[
            jax.ShapeDtypeStruct((TOP_K, n), jnp.int32),
            jax.ShapeDtypeStruct((3, n_map), jnp.int32),
        ],
        name="moe_positions",
    )(after, idx_t)


def _sc_mesh():
    return plsc.VectorSubcoreMesh(core_axis_name="c", subcore_axis_name="s",
                                  num_cores=V7X_SC_CORES, num_subcores=V7X_SC_SUBCORES)


def _sc_worker():
    return lax.axis_index("s") * V7X_SC_CORES + lax.axis_index("c")


def _dispatch_call(rows, pos, n_slots):
    n = rows.shape[0]
    tok_w = n // SC_WORKERS
    n_items = tok_w // SC_CHUNK

    def body(rows_hbm, pos_hbm, xs_hbm, idx_v, rows_v, lsem, ssem):
        wid = _sc_worker()
        pltpu.sync_copy(pos_hbm.at[wid], idx_v)
        base = wid * tok_w

        def load(i):
            b = i % SC_BUFFERS
            return pltpu.async_copy(rows_hbm.at[pl.ds(base + i * SC_CHUNK, SC_CHUNK)], rows_v.at[b], lsem.at[b])

        def scatter(i):
            b = i % SC_BUFFERS
            return [pltpu.async_copy(rows_v.at[b], xs_hbm.at[idx_v.at[i, k]], ssem.at[b]) for k in range(TOP_K)]

        loads = {i: load(i) for i in range(SC_BUFFERS - 1)}
        scat = {}
        for i in range(n_items):
            loads[i].wait()
            scat[i] = scatter(i)
            if i >= 1:
                for cp in scat[i - 1]:
                    cp.wait()
            if i + SC_BUFFERS - 1 < n_items:
                loads[i + SC_BUFFERS - 1] = load(i + SC_BUFFERS - 1)
        for cp in scat[n_items - 1]:
            cp.wait()

    return pl.kernel(
        body, mesh=_sc_mesh(),
        out_type=jax.ShapeDtypeStruct((n_slots, ROW_WORDS), jnp.int32),
        scratch_types=[pltpu.VMEM((n_items, TOP_K, SC_CHUNK), jnp.int32),
                       pltpu.VMEM((SC_BUFFERS, SC_CHUNK, ROW_WORDS), jnp.int32),
                       pltpu.SemaphoreType.DMA((SC_BUFFERS,)), pltpu.SemaphoreType.DMA((SC_BUFFERS,))],
        compiler_params=pltpu.CompilerParams(use_tc_tiling_on_sc=True),
        name="moe_dispatch",
    )(rows, pos)


def _collect_call(ys, pos, n):
    tok_w = n // SC_WORKERS
    n_chunks = tok_w // SC_CHUNK
    items = [(c, k) for c in range(n_chunks) for k in range(TOP_K)]

    def body(ys_hbm, pos_hbm, yt_hbm, idx_v, rows_v, gsem, wsem):
        wid = _sc_worker()
        pltpu.sync_copy(pos_hbm.at[wid], idx_v)
        base = wid * tok_w

        def gather(j):
            c, k = items[j]
            b = j % SC_BUFFERS
            return pltpu.async_copy(ys_hbm.at[idx_v.at[c, k]], rows_v.at[b], gsem.at[b])

        def write(j):
            c, k = items[j]
            b = j % SC_BUFFERS
            return pltpu.async_copy(rows_v.at[b], yt_hbm.at[k, pl.ds(base + c * SC_CHUNK, SC_CHUNK)], wsem.at[b])

        g = {j: gather(j) for j in range(SC_BUFFERS - 1)}
        w = {}
        for j in range(len(items)):
            g[j].wait()
            w[j] = write(j)
            if j >= 1:
                w[j - 1].wait()
            if j + SC_BUFFERS - 1 < len(items):
                g[j + SC_BUFFERS - 1] = gather(j + SC_BUFFERS - 1)
        w[len(items) - 1].wait()

    return pl.kernel(
        body, mesh=_sc_mesh(),
        out_type=jax.ShapeDtypeStruct((TOP_K, n, ROW_WORDS), jnp.int32),
        scratch_types=[pltpu.VMEM((n_chunks, TOP_K, SC_CHUNK), jnp.int32),
                       pltpu.VMEM((SC_BUFFERS, SC_CHUNK, ROW_WORDS), jnp.int32),
                       pltpu.SemaphoreType.DMA((SC_BUFFERS,)), pltpu.SemaphoreType.DMA((SC_BUFFERS,))],
        compiler_params=pltpu.CompilerParams(use_tc_tiling_on_sc=True),
        name="moe_collect",
    )(ys, pos)


def _expert_gemm_kernel(plan_ref, after_ref, xs_ref, wg_ref, wu_ref, wd_ref, ys_ref, wgu_scr, wd_scr):
    del after_ref
    i = pl.program_id(0)
    e = plan_ref[0, i]
    prev = plan_ref[0, jnp.maximum(i - 1, 0)]
    live = plan_ref[1, i]

    @pl.when((i == 0) | (e != prev))
    def _():
        wgu_scr[:, 0:D_EXPERT] = wg_ref[...].astype(BF16)
        wgu_scr[:, D_EXPERT:2 * D_EXPERT] = wu_ref[...].astype(BF16)
        wd_scr[...] = wd_ref[...].astype(BF16)

    def swiglu_rows(n_rows):
        lo, hi = _unpack_bf16_pairs(xs_ref[0:n_rows, :])
        gu = _dot(lo.astype(BF16), wgu_scr[0:ROW_WORDS, :]) + _dot(hi.astype(BF16), wgu_scr[ROW_WORDS:D_MODEL, :])
        h = _silu(gu[:, 0:D_EXPERT]) * gu[:, D_EXPERT:2 * D_EXPERT]
        y = _dot(h.astype(BF16), wd_scr[...])
        ys_ref[0:n_rows, :] = _pack_bf16_pairs_native(y)

    @pl.when((i < plan_ref[2, 0]) & (live > ROW_TILE // 2))
    def _():
        swiglu_rows(ROW_TILE)

    @pl.when((i < plan_ref[2, 0]) & (live <= ROW_TILE // 2))
    def _():
        swiglu_rows(ROW_TILE // 2)


def _expert_gemm_call(after, plan, xs, w_e_gate, w_e_up, w_e_down):
    n_slots = xs.shape[0]
    n_row_tiles = n_slots // ROW_TILE
    rows = pl.BlockSpec((ROW_TILE, ROW_WORDS), lambda i, plan: (jnp.minimum(i, plan[2, 0] - 1), 0))
    expert = lambda i, plan: (plan[0, i], 0, 0)
    return pl.pallas_call(
        _expert_gemm_kernel,
        grid_spec=pltpu.PrefetchScalarGridSpec(
            num_scalar_prefetch=1,
            grid=(n_row_tiles,),
            in_specs=[
                pl.BlockSpec(memory_space=pl.ANY),
                rows,
                pl.BlockSpec((None, D_MODEL, D_EXPERT), expert),
                pl.BlockSpec((None, D_MODEL, D_EXPERT), expert),
                pl.BlockSpec((None, D_EXPERT, D_MODEL), expert),
            ],
            out_specs=rows,
            scratch_shapes=[pltpu.VMEM((D_MODEL, 2 * D_EXPERT), BF16), pltpu.VMEM((D_EXPERT, D_MODEL), BF16)],
        ),
        out_shape=jax.ShapeDtypeStruct((n_slots, ROW_WORDS), jnp.int32),
        compiler_params=pltpu.CompilerParams(dimension_semantics=("arbitrary",), vmem_limit_bytes=VMEM_LIMIT),
        name="moe_experts",
    )(plan, after, xs, w_e_gate, w_e_up, w_e_down)


def _moe_out_kernel(yt_ref, w_ref, t_ref, x1_ref, mod_ref, post2_ref, wsg_ref, wsu_ref, wsd_ref, o_ref):
    lo, hi = _unpack_bf16_pairs(t_ref[...])
    lo = lo.astype(BF16)
    hi = hi.astype(BF16)
    g = _dot(lo, wsg_ref[0:ROW_WORDS, :]) + _dot(hi, wsg_ref[ROW_WORDS:D_MODEL, :])
    u = _dot(lo, wsu_ref[0:ROW_WORDS, :]) + _dot(hi, wsu_ref[ROW_WORDS:D_MODEL, :])
    shared = _dot((_silu(g) * u).astype(BF16), wsd_ref[...])
    acc_lo = shared[:, 0:ROW_WORDS]
    acc_hi = shared[:, ROW_WORDS:D_MODEL]
    for k in range(TOP_K):
        y_lo, y_hi = _unpack_bf16_pairs(yt_ref[k])
        wk = w_ref[:, k:k + 1]
        acc_lo = acc_lo + wk * y_lo
        acc_hi = acc_hi + wk * y_hi
    moe = jnp.concatenate([acc_lo, acc_hi], axis=-1)
    o_ref[...] = x1_ref[...] + mod_ref[5:6, :] * _rms_norm(moe, post2_ref[...])


def _moe_out_into_kernel(prev_ref, *refs):
    del prev_ref
    _moe_out_kernel(*refs)


def _moe_out_call(prev_out, n_total, b0, yt, w, t, x1, mod_x, post2_g, wsg_bf, wsu_bf, wsd_bf, seq):
    n = t.shape[0]
    tiles_per_seq = seq // TOUT
    first_tile = b0 * tiles_per_seq
    const = lambda i: (0, 0)
    in_specs = [
        pl.BlockSpec((TOP_K, TOUT, ROW_WORDS), lambda i: (0, i, 0)),
        pl.BlockSpec((TOUT, TOP_K), lambda i: (i, 0)),
        pl.BlockSpec((TOUT, ROW_WORDS), lambda i: (i, 0)),
        pl.BlockSpec((TOUT, D_MODEL), lambda i: (i, 0)),
        pl.BlockSpec((None, 6, D_MODEL), lambda i: (b0 + i // tiles_per_seq, 0, 0)),
        pl.BlockSpec((1, D_MODEL), const),
        pl.BlockSpec((D_MODEL, D_EXPERT), const),
        pl.BlockSpec((D_MODEL, D_EXPERT), const),
        pl.BlockSpec((D_EXPERT, D_MODEL), const),
    ]
    args = (yt, w, t, x1, mod_x, post2_g, wsg_bf, wsu_bf, wsd_bf)
    aliased = prev_out is not None
    return pl.pallas_call(
        _moe_out_into_kernel if aliased else _moe_out_kernel,
        grid=(n // TOUT,),
        in_specs=([pl.BlockSpec(memory_space=pl.ANY)] if aliased else []) + in_specs,
        out_specs=pl.BlockSpec((TOUT, D_MODEL), lambda i: (first_tile + i, 0)),
        out_shape=jax.ShapeDtypeStruct((n_total, D_MODEL), F32),
        input_output_aliases={0: 0} if aliased else {},
        compiler_params=pltpu.CompilerParams(dimension_semantics=("arbitrary",), vmem_limit_bytes=VMEM_LIMIT),
        name="moe_out",
    )(*(((prev_out,) if aliased else ()) + args))


def _moe_dispatch(after, hx2w, idx_t):
    n = hx2w.shape[0]
    n_slots = n * TOP_K + N_EXPERTS * ROW_TILE
    pos_t, plan = _positions_call(after, idx_t, n_slots // ROW_TILE)
    pos = pos_t.reshape(TOP_K, SC_WORKERS, n // (SC_WORKERS * SC_CHUNK), SC_CHUNK).transpose(1, 2, 0, 3)
    return _dispatch_call(hx2w, pos, n_slots), pos, plan


def kernel(x, c, ctx, c_ctx, w_mod, b_mod, pre1_g, post1_g, pre2_g, post2_g, w_in, w_dw, b_dw, ln_conv_g, ln_conv_b, w_conv_out, w_sc, b_sc, w_rg_a, b_rg_a, w_rg_x, b_rg_x, lru_lambda, w_rnn_out, w_out, w_router, router_bias, w_e_gate, w_e_up, w_e_down, w_s_gate, w_s_up, w_s_down):
    assert w_mod.shape[0] == 1, "single-layer block"
    bsz, seq, d = x.shape
    n = bsz * seq
    assert d == D_MODEL and seq % TM == 0 and seq % TOUT == 0 and bsz + 1 <= SUBLANES
    assert TOKEN_GROUPS == 2 and bsz % TOKEN_GROUPS == 0 and (n // TOKEN_GROUPS) % (SC_WORKERS * SC_CHUNK) == 0
    assert (n // TOKEN_GROUPS) % POS_TILE == 0

    cc = jnp.zeros((SUBLANES, D_MODEL), F32).at[:bsz].set(c).at[bsz].set(c_ctx)
    mod = _mod_call(cc, w_mod[0], b_mod)
    mod_x = mod[:bsz].reshape(bsz, 6, D_MODEL)
    mod_c = mod[bsz].reshape(6, D_MODEL)

    w_in_bf = w_in[0].astype(BF16)
    wa_bf = w_rg_a[0].astype(BF16)
    wx_bf = w_rg_x[0].astype(BF16)

    h0 = _ctx_call(ctx, mod_c, pre1_g, w_in_bf, w_sc[0], b_sc[0], wa_bf, b_rg_a[0], wx_bf, b_rg_x[0],
                   lru_lambda[0])

    wco_bf, wro_bf, wout_bf = w_conv_out[0].astype(BF16), w_rnn_out[0].astype(BF16), w_out[0].astype(BF16)
    wsg_bf, wsu_bf, wsd_bf = w_s_gate[0].astype(BF16), w_s_up[0].astype(BF16), w_s_down[0].astype(BF16)
    w_router_t, rbias = w_router[0].T, router_bias.reshape(N_EXPERTS, 1)
    gsz = bsz // TOKEN_GROUPS
    gn = gsz * seq
    none = jnp.zeros((TOP_K, LANES), jnp.int32)

    def fwd(g, after):
        return _mixer_fwd_call(g * gsz, gsz, after, x, mod_x, pre1_g, w_in_bf, w_dw[0], b_dw, ln_conv_g,
                               ln_conv_b, wco_bf, w_sc[0], b_sc[0], wa_bf, b_rg_a[0], wx_bf, b_rg_x[0],
                               lru_lambda[0], h0)

    def bwd(g, after, acts):
        gaya, gb, gbr, urnn, hf = acts
        x1, hx2w, idx_t, w_t = _mixer_bwd_call(
            g * gsz, gsz, after, urnn, hf, gbr, gaya, gb, x, mod_x, w_sc[0], b_sc[0], wa_bf, b_rg_a[0], wx_bf,
            b_rg_x[0], lru_lambda[0], h0, wro_bf, wout_bf, post1_g, pre2_g, w_router_t, rbias)
        return x1.reshape(gn, D_MODEL), hx2w.reshape(gn, ROW_WORDS), idx_t, w_t

    def experts(after, plan, xs):
        return _expert_gemm_call(after, plan, xs, w_e_gate[0], w_e_up[0], w_e_down[0])

    def finish(g, prev_out, yt, w_t, hx2w, x1):
        return _moe_out_call(prev_out, n, g * gsz, yt, w_t.T, hx2w, x1, mod_x, post2_g, wsg_bf, wsu_bf, wsd_bf, seq)

    x1_a, hx2w_a, idx_a, w_a = bwd(0, none, fwd(0, none))
    xs_a, pos_a, plan_a = _moe_dispatch(none, hx2w_a, idx_a)
    acts_b = fwd(1, idx_a)
    ys_a = experts(acts_b[4], plan_a, xs_a)
    yt_a = _collect_call(ys_a, pos_a, gn)
    x1_b, hx2w_b, idx_b, w_b = bwd(1, ys_a, acts_b)
    xs_b, pos_b, plan_b = _moe_dispatch(yt_a, hx2w_b, idx_b)
    out = finish(0, None, yt_a, w_a, hx2w_a, x1_a)
    ys_b = experts(none, plan_b, xs_b)
    yt_b = _collect_call(ys_b, pos_b, gn)
    out = finish(1, out, yt_b, w_b, hx2w_b, x1_b)
    return out.reshape(bsz, seq, D_MODEL)
```
